```python
import math
import jax, jax.numpy as jnp
from jax import lax
import numpy as np

D_MODEL = 1024
BATCH = 8
SEQ = 2048
DEPTH = 2

HEAD_DIM = 64
HEADS_PER_GROUP = 4
DILATION_PATTERNS = ((128, 1), (512, 4), (2048, 16))
N_DIL_GROUPS = len(DILATION_PATTERNS)
N_ATTN_HEADS = N_DIL_GROUPS * HEADS_PER_GROUP
ATTN_WIDTH = N_ATTN_HEADS * HEAD_DIM
ATTN_OUT_WIDTH = HEADS_PER_GROUP * HEAD_DIM
POOL_WINDOWS = (2, 4, 8, 16)
POOL_GROUP_WIDTH = 128
POOL_WIDTH = len(POOL_WINDOWS) * POOL_GROUP_WIDTH
N_BRANCHES = 2
IN_WIDTH = 3 * ATTN_WIDTH + POOL_WIDTH + N_BRANCHES * D_MODEL
N_BUCKETS = 32
MAX_DISTANCE = 1024
N_EXPERT_GROUPS = 8
EXPERTS_PER_GROUP = 8
N_EXPERTS = N_EXPERT_GROUPS * EXPERTS_PER_GROUP
TOP_K_IN_GROUP = 2
D_EXPERT = 512
MOE_BLOCK = 128
EPS = 1e-6
NEG_INF = -1e30

kernel_name = "hybrid_dilated_pool_hiermoe_encoder"


def rmsnorm(x, g):
    xf = x.astype(jnp.float32)
    r = lax.rsqrt(jnp.mean(xf * xf, axis=-1, keepdims=True) + EPS)
    return (xf * r).astype(x.dtype) * g


def t5_bucket(rel):
    nb = N_BUCKETS // 2
    ret = jnp.where(rel > 0, nb, 0)
    n = jnp.abs(rel)
    max_exact = nb // 2
    nf = jnp.maximum(n, max_exact).astype(jnp.float32)
    large = max_exact + (jnp.log(nf / max_exact) / math.log(MAX_DISTANCE / max_exact)
                         * (nb - max_exact)).astype(jnp.int32)
    large = jnp.minimum(large, nb - 1)
    return ret + jnp.where(n < max_exact, n, large)


def dilated_band_attention(q, k, v, bias_tab, dilation, n_side):
    B, S, H, Dh = q.shape
    L = S // dilation
    blk = n_side
    nb = -(-L // blk)
    Lp = nb * blk

    def to_sub(t):
        t = t.reshape(B, L, dilation, H, Dh).transpose(0, 2, 1, 3, 4)
        return jnp.pad(t, ((0, 0), (0, 0), (0, Lp - L), (0, 0), (0, 0)))

    def key_blocks(t):
        tp = jnp.pad(to_sub(t), ((0, 0), (0, 0), (blk, blk), (0, 0), (0, 0)))
        tp = tp.reshape(B, dilation, nb + 2, blk, H, Dh)
        return jnp.concatenate([tp[:, :, :-2], tp[:, :, 1:-1], tp[:, :, 2:]], axis=3)

    qs = to_sub(q).reshape(B, dilation, nb, blk, H, Dh)
    kb = key_blocks(k)
    vb = key_blocks(v)

    a = jnp.arange(blk)[:, None]
    c = jnp.arange(3 * blk)[None, :]
    rel = c - blk - a
    key_idx = jnp.arange(nb)[:, None, None] * blk - blk + c[None]
    valid = (jnp.abs(rel) <= n_side)[None] & (key_idx >= 0) & (key_idx < L)
    pos_bias = bias_tab[t5_bucket(rel * dilation)].transpose(2, 0, 1).astype(jnp.float32)

    scale = HEAD_DIM ** -0.5
    logits = jnp.einsum('bgnqhd,bgnkhd->bgnhqk', qs, kb,
                        preferred_element_type=jnp.float32) * scale + pos_bias
    logits = jnp.where(valid[:, None], logits, NEG_INF)
    m = jnp.max(logits, axis=-1, keepdims=True)
    p = jnp.exp(logits - m)
    den = jnp.sum(p, axis=-1, keepdims=True)
    o = jnp.einsum('bgnhqk,bgnkhd->bgnqhd', (p / den).astype(v.dtype), vb)
    lse = (m + jnp.log(den))[..., 0]

    o = o.reshape(B, dilation, Lp, H, Dh)[:, :, :L].transpose(0, 2, 1, 3, 4).reshape(B, S, H, Dh)
    lse = lse.transpose(0, 1, 2, 4, 3).reshape(B, dilation, Lp, H)[:, :, :L]
    lse = lse.transpose(0, 2, 1, 3).reshape(B, S, H)
    return o, lse


def dilated_attention_mixture(q, k, v, rel_bias):
    B, S = q.shape[:2]
    outs, lses = [], []
    for gi, (window, dilation) in enumerate(DILATION_PATTERNS):
        hs = slice(gi * HEADS_PER_GROUP, (gi + 1) * HEADS_PER_GROUP)
        n_side = window // (2 * dilation)
        o, lse = dilated_band_attention(q[:, :, hs], k[:, :, hs], v[:, :, hs],
                                        rel_bias[:, hs], dilation, n_side)
        outs.append(o)
        lses.append(lse)
    w = jax.nn.softmax(jnp.stack(lses, axis=0), axis=0)
    o = jnp.sum(w[..., None].astype(q.dtype) * jnp.stack(outs, axis=0), axis=0)
    return o.reshape(B, S, ATTN_OUT_WIDTH)


def multiscale_pool(u, pool_w, pool_scale):
    B, S, _ = u.shape
    uf = u.astype(jnp.float32)
    cs = jnp.concatenate([jnp.zeros((B, 1, POOL_WIDTH), jnp.float32),
                          jnp.cumsum(uf, axis=1)], axis=1)
    pos = jnp.arange(S)
    outs = []
    for gi, w in enumerate(POOL_WINDOWS):
        sl = slice(gi * POOL_GROUP_WIDTH, (gi + 1) * POOL_GROUP_WIDTH)
        lo = jnp.maximum(pos - w // 2, 0)
        hi = jnp.minimum(pos + w - w // 2, S)
        csg = cs[..., sl]
        mean = (csg[:, hi] - csg[:, lo]) / (hi - lo).astype(jnp.float32)[None, :, None]
        outs.append(mean - uf[..., sl])
    pooled = jnp.stack(outs, axis=2).astype(u.dtype)
    mixed = jnp.einsum('bsgc,gce->bsge', pooled, pool_w).reshape(B, S, POOL_WIDTH)
    return mixed * pool_scale


def hybrid_mixer(xn, w_in, rel_bias, pool_w, pool_scale, w_proj_attn, w_proj_pool, w_out):
    B, S, D = xn.shape
    proj = xn @ w_in
    q, k, v, u, gates = jnp.split(
        proj, [ATTN_WIDTH, 2 * ATTN_WIDTH, 3 * ATTN_WIDTH, 3 * ATTN_WIDTH + POOL_WIDTH], axis=-1)
    q = q.reshape(B, S, N_ATTN_HEADS, HEAD_DIM)
    k = k.reshape(B, S, N_ATTN_HEADS, HEAD_DIM)
    v = v.reshape(B, S, N_ATTN_HEADS, HEAD_DIM)
    y_attn = dilated_attention_mixture(q, k, v, rel_bias) @ w_proj_attn
    y_pool = multiscale_pool(u, pool_w, pool_scale) @ w_proj_pool
    g = jax.nn.sigmoid(gates.astype(jnp.float32)).astype(xn.dtype).reshape(B, S, N_BRANCHES, D)
    y = g[:, :, 0] * y_attn + g[:, :, 1] * y_pool
    return y @ w_out


def hierarchical_moe(xn, w_router_group, w_router_expert, w_gate_e, w_up_e, w_down_e):
    B, S, D = xn.shape
    T = B * S
    xt = xn.reshape(T, D)
    g_logits = (xt @ w_router_group).astype(jnp.float32)
    g_prob = jax.nn.softmax(g_logits, axis=-1)
    g_idx = jnp.argmax(g_logits, axis=-1)
    g_p = jnp.take_along_axis(g_prob, g_idx[:, None], axis=1)[:, 0]
    e_all = jnp.einsum('td,gde->tge', xt, w_router_expert).astype(jnp.float32)
    e_logits = jnp.take_along_axis(e_all, g_idx[:, None, None], axis=1)[:, 0]
    top_val, top_idx = lax.top_k(e_logits, TOP_K_IN_GROUP)
    comb_w = g_p[:, None] * jax.nn.softmax(top_val, axis=-1)
    expert_id = g_idx[:, None].astype(jnp.int32) * EXPERTS_PER_GROUP + top_idx.astype(jnp.int32)

    A = T * TOP_K_IN_GROUP
    flat_e = expert_id.reshape(A)
    flat_t = jnp.repeat(jnp.arange(T, dtype=jnp.int32), TOP_K_IN_GROUP)
    flat_w = comb_w.reshape(A)
    se, st, sw = lax.sort((flat_e, flat_t, flat_w), num_keys=1, is_stable=True)
    counts = jnp.bincount(flat_e, length=N_EXPERTS)
    offsets = jnp.cumsum(counts) - counts
    padded = (counts + MOE_BLOCK - 1) // MOE_BLOCK * MOE_BLOCK
    padded_end = jnp.cumsum(padded)
    padded_off = padded_end - padded
    dest = padded_off[se] + (jnp.arange(A, dtype=jnp.int32) - offsets[se])
    P = A + N_EXPERTS * MOE_BLOCK
    n_blk = P // MOE_BLOCK
    x_pad = jnp.zeros((P, D), xt.dtype).at[dest].set(xt[st])
    blk_start = jnp.arange(n_blk, dtype=jnp.int32) * MOE_BLOCK
    blk_expert = jnp.minimum(jnp.searchsorted(padded_end, blk_start, side='right'),
                             N_EXPERTS - 1).astype(jnp.int32)

    def expert_block(args):
        xb, e = args
        h = jax.nn.silu(xb @ w_gate_e[e]) * (xb @ w_up_e[e])
        return h @ w_down_e[e]

    y_pad = lax.map(expert_block, (x_pad.reshape(n_blk, MOE_BLOCK, D), blk_expert)).reshape(P, D)
    y = y_pad[dest] * sw[:, None].astype(xt.dtype)
    out = jax.ops.segment_sum(y, st, num_segments=T)
    return out.reshape(B, S, D)


def setup_inputs(seed: int = 0) -> dict:
    key = jax.random.key(seed)
    ks = jax.random.split(key, 16)
    f32 = jnp.float32
    nrm = lambda k, shape, fan_in: jax.random.normal(k, shape, f32) * (fan_in ** -0.5)
    return {
        "x": jax.random.normal(ks[0], (BATCH, SEQ, D_MODEL), f32),
        "rel_bias": jax.random.normal(ks[1], (N_BUCKETS, N_ATTN_HEADS), f32) * 0.3,
        "norm_mix_g": 1.0 + 0.02 * jax.random.normal(ks[2], (DEPTH, D_MODEL), f32),
        "w_in": nrm(ks[3], (DEPTH, D_MODEL, IN_WIDTH), D_MODEL),
        "pool_w": nrm(ks[4], (DEPTH, len(POOL_WINDOWS), POOL_GROUP_WIDTH, POOL_GROUP_WIDTH), POOL_GROUP_WIDTH),
        "pool_scale": 1.0 + 0.1 * jax.random.normal(ks[5], (DEPTH, POOL_WIDTH), f32),
        "w_proj_attn": nrm(ks[6], (DEPTH, ATTN_OUT_WIDTH, D_MODEL), ATTN_OUT_WIDTH),
        "w_proj_pool": nrm(ks[7], (DEPTH, POOL_WIDTH, D_MODEL), POOL_WIDTH),
        "w_out": nrm(ks[8], (DEPTH, D_MODEL, D_MODEL), D_MODEL),
        "norm_ffn_g": 1.0 + 0.02 * jax.random.normal(ks[9], (DEPTH, D_MODEL), f32),
        "w_router_group": nrm(ks[10], (DEPTH, D_MODEL, N_EXPERT_GROUPS), D_MODEL),
        "w_router_expert": nrm(ks[11], (DEPTH, N_EXPERT_GROUPS, D_MODEL, EXPERTS_PER_GROUP), D_MODEL),
        "w_gate_e": nrm(ks[12], (DEPTH, N_EXPERTS, D_MODEL, D_EXPERT), D_MODEL),
        "w_up_e": nrm(ks[13], (DEPTH, N_EXPERTS, D_MODEL, D_EXPERT), D_MODEL),
        "w_down_e": nrm(ks[14], (DEPTH, N_EXPERTS, D_EXPERT, D_MODEL), D_EXPERT),
        "norm_final_g": 1.0 + 0.02 * jax.random.normal(ks[15], (D_MODEL,), f32),
    }


def reference(x, rel_bias, norm_mix_g, w_in, pool_w, pool_scale, w_proj_attn, w_proj_pool,
              w_out, norm_ffn_g, w_router_group, w_router_expert, w_gate_e, w_up_e, w_down_e,
              norm_final_g):
    h = x
    for l in range(DEPTH):
        h = h + hybrid_mixer(rmsnorm(h, norm_mix_g[l]), w_in[l], rel_bias, pool_w[l],
                             pool_scale[l], w_proj_attn[l], w_proj_pool[l], w_out[l])
        h = h + hierarchical_moe(rmsnorm(h, norm_ffn_g[l]), w_router_group[l],
                                 w_router_expert[l], w_gate_e[l], w_up_e[l], w_down_e[l])
    return rmsnorm(h, norm_final_g)
```

```python
import functools
import math

import jax
import jax.numpy as jnp
from jax import lax
from jax.experimental import pallas as pl
from jax.experimental.pallas import tpu as pltpu

F32 = jnp.float32
BF16 = jnp.bfloat16

D_MODEL = 1024
BATCH = 8
SEQ = 2048
TOKENS = BATCH * SEQ
DEPTH = 2

HEAD_DIM = 64
HEADS_PER_GROUP = 4
GROUP_WIDTH = HEADS_PER_GROUP * HEAD_DIM
DILATION_PATTERNS = ((128, 1), (512, 4), (2048, 16))
N_GROUPS = len(DILATION_PATTERNS)
N_ATTN_HEADS = N_GROUPS * HEADS_PER_GROUP
ATTN_WIDTH = N_ATTN_HEADS * HEAD_DIM
QKV_WIDTH = 3 * ATTN_WIDTH
N_SIDE = 64
assert all(w // (2 * d) == N_SIDE for w, d in DILATION_PATTERNS)
POOL_WINDOWS = (2, 4, 8, 16)
POOL_GROUP_WIDTH = 128
POOL_WIDTH = len(POOL_WINDOWS) * POOL_GROUP_WIDTH
POOL_HALO = max(POOL_WINDOWS) // 2
N_BRANCHES = 2
GATE_WIDTH = N_BRANCHES * D_MODEL
IN_WIDTH = QKV_WIDTH + POOL_WIDTH + GATE_WIDTH
N_BUCKETS = 32
MAX_DISTANCE = 1024
N_EXPERT_GROUPS = 8
EXPERTS_PER_GROUP = 8
N_EXPERTS = N_EXPERT_GROUPS * EXPERTS_PER_GROUP
TOP_K = 2
D_EXPERT = 512
MOE_BLOCK = 128
N_ASSIGN = TOKENS * TOP_K
PAD_ROWS = N_ASSIGN + N_EXPERTS * MOE_BLOCK
N_MOE_BLOCKS = PAD_ROWS // MOE_BLOCK
EPS = 1e-6
NEG_INF = -1e30

LANES = 128
ROW_TILE = 512
Q_BLOCK = 128
K_BLOCK = Q_BLOCK + 2 * N_SIDE
ROUTER_LANES = 128
EXPERT_LANE0 = N_EXPERT_GROUPS
DMA_WINDOW = 16
VMEM_LIMIT = 56 * 1024 * 1024


def _cparams(*sem):
    return pltpu.CompilerParams(dimension_semantics=sem, vmem_limit_bytes=VMEM_LIMIT)


def _rms(h, g):
    r = lax.rsqrt(jnp.mean(h * h, axis=-1, keepdims=True) + EPS)
    return (h * r) * g


def _dot(a, b):
    return jnp.dot(a, b, preferred_element_type=F32)


def _proj_kernel(*refs, combine):
    if combine:
        (h_ref, y0_ref, y1_ref, rw_ref, g_ref, w_ref,
         hn_ref, qkv_ref, u_ref, gate_ref) = refs
        rw = rw_ref[...]
        h = h_ref[...] + (rw[:, 0:1] * y0_ref[...] + rw[:, 1:2] * y1_ref[...])
        hn_ref[...] = h
    else:
        h_ref, g_ref, w_ref, qkv_ref, u_ref, gate_ref = refs
        h = h_ref[...]
    xn = _rms(h, g_ref[...]).astype(BF16)
    qkv_ref[...] = _dot(xn, w_ref[:, 0:QKV_WIDTH]).astype(BF16)
    u_ref[...] = _dot(xn, w_ref[:, QKV_WIDTH:QKV_WIDTH + POOL_WIDTH])
    gates = _dot(xn, w_ref[:, QKV_WIDTH + POOL_WIDTH:IN_WIDTH])
    gate_ref[...] = jax.nn.sigmoid(gates).astype(BF16)


def _project(h, g, w_bf16, moe=None):
    row = lambda w: pl.BlockSpec((ROW_TILE, w), lambda i: (i, 0))
    full = lambda a: pl.BlockSpec(a.shape, lambda i: (0,) * a.ndim)
    outs = [jax.ShapeDtypeStruct((TOKENS, QKV_WIDTH), BF16),
            jax.ShapeDtypeStruct((TOKENS, POOL_WIDTH), F32),
            jax.ShapeDtypeStruct((TOKENS, GATE_WIDTH), BF16)]
    out_specs = [row(QKV_WIDTH), row(POOL_WIDTH), row(GATE_WIDTH)]
    if moe is None:
        args = (h, g, w_bf16)
        in_specs = [row(D_MODEL), full(g), full(w_bf16)]
    else:
        y0, y1, rw = moe
        args = (h, y0, y1, rw, g, w_bf16)
        in_specs = [row(D_MODEL), row(D_MODEL), row(D_MODEL), row(ROUTER_LANES), full(g), full(w_bf16)]
        outs = [jax.ShapeDtypeStruct((TOKENS, D_MODEL), F32)] + outs
        out_specs = [row(D_MODEL)] + out_specs
    res = pl.pallas_call(
        functools.partial(_proj_kernel, combine=moe is not None),
        grid=(TOKENS // ROW_TILE,),
        in_specs=in_specs, out_specs=out_specs, out_shape=outs,
        compiler_params=_cparams("parallel"),
        name="proj",
    )(*args)
    if moe is None:
        return (h,) + tuple(res)
    return tuple(res)


def _attn_kernel(q_ref, k_ref, v_ref, bias_ref, o_ref, kpad, vpad, *, sub_len):
    zpad = jnp.zeros((N_SIDE, GROUP_WIDTH), BF16)
    for pad_ref, src in ((kpad, k_ref), (vpad, v_ref)):
        pad_ref[0:N_SIDE, :] = zpad
        pad_ref[N_SIDE + sub_len:2 * N_SIDE + sub_len, :] = zpad
        pad_ref[N_SIDE:N_SIDE + sub_len, :] = src[0]

    head_of_lane = lax.broadcasted_iota(jnp.int32, (1, GROUP_WIDTH), 1) // HEAD_DIM
    qi = lax.broadcasted_iota(jnp.int32, (Q_BLOCK, K_BLOCK), 0)
    ki = lax.broadcasted_iota(jnp.int32, (Q_BLOCK, K_BLOCK), 1)
    band = jnp.abs(ki - N_SIDE - qi) <= N_SIDE

    def by_head(cols):
        out = cols[HEADS_PER_GROUP - 1]
        for h in range(HEADS_PER_GROUP - 2, -1, -1):
            out = jnp.where(head_of_lane == h, cols[h], out)
        return out

    def block(i, carry):
        r0 = pl.multiple_of(i * Q_BLOCK, Q_BLOCK)
        qb = q_ref[0, pl.ds(r0, Q_BLOCK), :]
        kw = kpad[pl.ds(r0, K_BLOCK), :]
        vw = vpad[pl.ds(r0, K_BLOCK), :]
        zero = jnp.zeros_like(qb)
        q_heads = jnp.concatenate(
            [jnp.where(head_of_lane == h, qb, zero) for h in range(HEADS_PER_GROUP)], axis=0)
        s = lax.dot_general(q_heads, kw, (((1,), (1,)), ((), ())), preferred_element_type=F32)
        s = s.reshape(HEADS_PER_GROUP, Q_BLOCK, K_BLOCK) + bias_ref[...]
        kpos = r0 - N_SIDE + ki
        valid = band & (kpos >= 0) & (kpos < sub_len)
        s = jnp.where(valid[None], s, NEG_INF)
        m = jnp.max(s, axis=-1, keepdims=True)
        p = jnp.exp(s - m)
        den = jnp.sum(p, axis=-1, keepdims=True)
        pb = p.astype(BF16)
        p_cat = jnp.concatenate([pb[h] for h in range(HEADS_PER_GROUP)], axis=1)
        zv = jnp.zeros_like(vw)
        v_heads = jnp.concatenate(
            [jnp.where(head_of_lane == h, vw, zv) for h in range(HEADS_PER_GROUP)], axis=0)
        o = _dot(p_cat, v_heads)
        lse = m + jnp.log(den)
        o_ref[0, pl.ds(r0, Q_BLOCK), 0:GROUP_WIDTH] = o / by_head([den[h] for h in range(HEADS_PER_GROUP)])
        o_ref[0, pl.ds(r0, Q_BLOCK), GROUP_WIDTH:2 * GROUP_WIDTH] = by_head(
            [lse[h] for h in range(HEADS_PER_GROUP)])
        return carry

    lax.fori_loop(0, sub_len // Q_BLOCK, block, 0)


def _t5_bucket(rel):
    nb = N_BUCKETS // 2
    ret = jnp.where(rel > 0, nb, 0)
    n = jnp.abs(rel)
    max_exact = nb // 2
    nf = jnp.maximum(n, max_exact).astype(F32)
    large = max_exact + (jnp.log(nf / max_exact) / math.log(MAX_DISTANCE / max_exact)
                         * (nb - max_exact)).astype(jnp.int32)
    large = jnp.minimum(large, nb - 1)
    return ret + jnp.where(n < max_exact, n, large)


def _band_bias(rel_bias, group, dilation):
    qi = jnp.arange(Q_BLOCK)[:, None]
    ki = jnp.arange(K_BLOCK)[None, :]
    bucket = _t5_bucket((ki - N_SIDE - qi) * dilation)
    tab = rel_bias[:, group * HEADS_PER_GROUP:(group + 1) * HEADS_PER_GROUP]
    return tab[bucket].transpose(2, 0, 1).astype(F32)


def _attention_group(qkv, rel_bias, group):
    _, dilation = DILATION_PATTERNS[group]
    sub_len = SEQ // dilation
    blocks_per_row = QKV_WIDTH // GROUP_WIDTH
    view = qkv.reshape(BATCH, sub_len, dilation * QKV_WIDTH)
    col = lambda part: (lambda b, r: (b, 0, r * blocks_per_row + part * N_GROUPS + group))
    qkv_spec = lambda part: pl.BlockSpec((1, sub_len, GROUP_WIDTH), col(part))
    bias = _band_bias(rel_bias, group, dilation)
    out = pl.pallas_call(
        functools.partial(_attn_kernel, sub_len=sub_len),
        grid=(BATCH, dilation),
        in_specs=[qkv_spec(0), qkv_spec(1), qkv_spec(2),
                  pl.BlockSpec(bias.shape, lambda b, r: (0, 0, 0))],
        out_specs=pl.BlockSpec((1, sub_len, 2 * GROUP_WIDTH), lambda b, r: (b, 0, r)),
        out_shape=jax.ShapeDtypeStruct((BATCH, sub_len, dilation * 2 * GROUP_WIDTH), F32),
        scratch_shapes=[pltpu.VMEM((sub_len + 2 * N_SIDE, GROUP_WIDTH), BF16),
                        pltpu.VMEM((sub_len + 2 * N_SIDE, GROUP_WIDTH), BF16)],
        compiler_params=_cparams("parallel", "parallel"),
        name=f"attn{group}",
    )(view, view, view, bias)
    return out.reshape(TOKENS, 2 * GROUP_WIDTH)


def _mixout_kernel(h_ref, a0_ref, a1_ref, a2_ref, u_ref, uprev_ref, unext_ref, gate_ref,
                   pw_ref, ps_ref, wpa_ref, wpp_ref, wo_ref, out_ref):
    tiles_per_seq = SEQ // ROW_TILE
    j = pl.program_id(0) % tiles_per_seq

    a_refs = (a0_ref, a1_ref, a2_ref)
    lse = [r[:, GROUP_WIDTH:2 * GROUP_WIDTH] for r in a_refs]
    m = jnp.maximum(jnp.maximum(lse[0], lse[1]), lse[2])
    e = [jnp.exp(l - m) for l in lse]
    num = e[0] * a0_ref[:, 0:GROUP_WIDTH] + e[1] * a1_ref[:, 0:GROUP_WIDTH] + e[2] * a2_ref[:, 0:GROUP_WIDTH]
    attn = num / (e[0] + e[1] + e[2])
    y_attn = _dot(attn.astype(BF16), wpa_ref[...])

    u = u_ref[...]
    prev = jnp.where(j == 0, 0.0, uprev_ref[0])
    nxt = jnp.where(j == tiles_per_seq - 1, 0.0, unext_ref[0])
    ext = jnp.concatenate([prev, u, nxt], axis=0)
    pos = j * ROW_TILE + lax.broadcasted_iota(jnp.int32, (ROW_TILE, 1), 0)
    mixed = []
    for gi, w in enumerate(POOL_WINDOWS):
        half = w // 2
        sl = slice(gi * POOL_GROUP_WIDTH, (gi + 1) * POOL_GROUP_WIDTH)
        eg = ext[:, sl]
        acc = eg[POOL_HALO - half:POOL_HALO - half + ROW_TILE]
        for s in range(-half + 1, half):
            acc = acc + eg[POOL_HALO + s:POOL_HALO + s + ROW_TILE]
        cnt = (jnp.minimum(pos + half, SEQ) - jnp.maximum(pos - half, 0)).astype(F32)
        pooled = acc / cnt - u[:, sl]
        mixed.append(_dot(pooled.astype(BF16), pw_ref[gi]) * ps_ref[:, sl])
    y_pool = _dot(jnp.concatenate(mixed, axis=1).astype(BF16), wpp_ref[...])

    g_attn = gate_ref[:, 0:D_MODEL].astype(F32)
    g_pool = gate_ref[:, D_MODEL:GATE_WIDTH].astype(F32)
    y = g_attn * y_attn + g_pool * y_pool
    out_ref[...] = h_ref[...] + _dot(y.astype(BF16), wo_ref[...])


def _mix_out(h, attn_outs, u, gates, pool_w, pool_scale, w_proj_attn, w_proj_pool, w_out):
    row = lambda w: pl.BlockSpec((ROW_TILE, w), lambda i: (i, 0))
    full = lambda a: pl.BlockSpec(a.shape, lambda i: (0,) * a.ndim)
    halo_blocks = ROW_TILE // POOL_HALO
    u3 = u.reshape(TOKENS // POOL_HALO, POOL_HALO, POOL_WIDTH)
    last = TOKENS // POOL_HALO - 1
    prev_spec = pl.BlockSpec((1, POOL_HALO, POOL_WIDTH),
                             lambda i: (jnp.maximum(i * halo_blocks - 1, 0), 0, 0))
    next_spec = pl.BlockSpec((1, POOL_HALO, POOL_WIDTH),
                             lambda i: (jnp.minimum((i + 1) * halo_blocks, last), 0, 0))
    weights = (pool_w, pool_scale, w_proj_attn, w_proj_pool, w_out)
    return pl.pallas_call(
        _mixout_kernel,
        grid=(TOKENS // ROW_TILE,),
        in_specs=[row(D_MODEL)] + [row(2 * GROUP_WIDTH)] * 3
                 + [row(POOL_WIDTH), prev_spec, next_spec, row(GATE_WIDTH)]
                 + [full(w) for w in weights],
        out_specs=row(D_MODEL),
        out_shape=jax.ShapeDtypeStruct((TOKENS, D_MODEL), F32),
        compiler_params=_cparams("parallel"),
        name="mixout",
    )(h, *attn_outs, u, u3, u3, gates, *weights)


def _router_kernel(h_ref, g_ref, wr_ref, xn_ref, ri_ref, rw_ref, cnt_ref, carry):
    @pl.when(pl.program_id(0) == 0)
    def _():
        carry[...] = jnp.zeros_like(carry)

    xn = _rms(h_ref[...], g_ref[...])
    xn_ref[...] = xn
    hi = xn.astype(BF16)
    lo = (xn - hi.astype(F32)).astype(BF16)
    both = _dot(hi, wr_ref[...])
    lg = both[:, 0:ROUTER_LANES] + (both[:, ROUTER_LANES:] + _dot(lo, wr_ref[:, 0:ROUTER_LANES]))

    lane = lax.broadcasted_iota(jnp.int32, (ROW_TILE, ROUTER_LANES), 1)
    lanef = lane.astype(F32)
    low = jnp.float32(-3.0e38)
    far = jnp.float32(ROUTER_LANES)
    first = lambda hit: jnp.min(jnp.where(hit, lanef, far), axis=-1, keepdims=True)

    is_group = lane < N_EXPERT_GROUPS
    gl = jnp.where(is_group, lg, low)
    gmax = jnp.max(gl, axis=-1, keepdims=True)
    gidx = first(gl == gmax).astype(jnp.int32)
    gden = jnp.sum(jnp.where(is_group, jnp.exp(gl - gmax), 0.0), axis=-1, keepdims=True)
    g_p = 1.0 / gden

    in_group = ((lane >= EXPERT_LANE0) & (lane < EXPERT_LANE0 + N_EXPERTS)
                & ((lane - EXPERT_LANE0) // EXPERTS_PER_GROUP == gidx))
    el = jnp.where(in_group, lg, low)
    t1 = jnp.max(el, axis=-1, keepdims=True)
    l1 = first(in_group & (el == t1))
    rest = in_group & (lanef != l1)
    el2 = jnp.where(rest, lg, low)
    t2 = jnp.max(el2, axis=-1, keepdims=True)
    l2 = first(rest & (el2 == t2))
    e2 = jnp.exp(t2 - t1)
    w1 = g_p * (1.0 / (1.0 + e2))
    w2 = g_p * (e2 / (1.0 + e2))

    hit1 = lanef == l1
    hit2 = lanef == l2
    onehot = (hit1 | hit2).astype(BF16)
    ri = lax.broadcasted_iota(jnp.int32, (ROW_TILE, ROW_TILE), 0)
    ci = lax.broadcasted_iota(jnp.int32, (ROW_TILE, ROW_TILE), 1)
    before = (ci < ri).astype(BF16)
    seen = _dot(before, onehot) + carry[...]
    r1 = jnp.sum(jnp.where(hit1, seen, 0.0), axis=-1, keepdims=True)
    r2 = jnp.sum(jnp.where(hit2, seen, 0.0), axis=-1, keepdims=True)
    carry[...] = carry[...] + jnp.sum(onehot.astype(F32), axis=0, keepdims=True)

    vals = (l1 - EXPERT_LANE0, l2 - EXPERT_LANE0, r1, r2)
    packed = jnp.zeros((ROW_TILE, ROUTER_LANES), F32)
    for k, v in enumerate(vals):
        packed = jnp.where(lane == k, v, packed)
    ri_ref[...] = packed.astype(jnp.int32)
    rw_ref[...] = jnp.where(lane == 0, w1, jnp.where(lane == 1, w2, 0.0))
    cnt_ref[...] = jnp.broadcast_to(carry[...], cnt_ref.shape)


def _route(h, g, wr_split):
    row = lambda w: pl.BlockSpec((ROW_TILE, w), lambda i: (i, 0))
    full = lambda a: pl.BlockSpec(a.shape, lambda i: (0,) * a.ndim)
    return pl.pallas_call(
        _router_kernel,
        grid=(TOKENS // ROW_TILE,),
        in_specs=[row(D_MODEL), full(g), full(wr_split)],
        out_specs=[row(D_MODEL), row(ROUTER_LANES), row(ROUTER_LANES),
                   pl.BlockSpec((8, ROUTER_LANES), lambda i: (0, 0))],
        out_shape=[jax.ShapeDtypeStruct((TOKENS, D_MODEL), F32),
                   jax.ShapeDtypeStruct((TOKENS, ROUTER_LANES), jnp.int32),
                   jax.ShapeDtypeStruct((TOKENS, ROUTER_LANES), F32),
                   jax.ShapeDtypeStruct((8, ROUTER_LANES), F32)],
        scratch_shapes=[pltpu.VMEM((1, ROUTER_LANES), F32)],
        compiler_params=_cparams("arbitrary"),
        name="router",
    )(h, g, wr_split)


def _row_copy(src_hbm, src_row, dst_hbm, dst_row, sem):
    return pltpu.make_async_copy(src_hbm.at[pl.ds(src_row, 1)], dst_hbm.at[pl.ds(dst_row, 1)], sem)


def _dispatch_kernel(dest_ref, x_hbm, init_hbm, xpad_hbm, sem):
    del init_hbm

    def copies(t):
        return [_row_copy(x_hbm, t, xpad_hbm, dest_ref[TOP_K * t + k], sem) for k in range(TOP_K)]

    def issue(t, carry):
        for cp in copies(t):
            cp.start()

        @pl.when(t >= DMA_WINDOW)
        def _():
            for cp in copies(t - DMA_WINDOW):
                cp.wait()
        return carry

    lax.fori_loop(0, TOKENS, issue, 0)

    def drain(t, carry):
        for cp in copies(t):
            cp.wait()
        return carry

    lax.fori_loop(TOKENS - DMA_WINDOW, TOKENS, drain, 0)


def _dispatch(dest_flat, xn):
    zeros = jnp.zeros((PAD_ROWS, D_MODEL), F32)
    return pl.pallas_call(
        _dispatch_kernel,
        in_specs=[pl.BlockSpec(memory_space=pltpu.SMEM),
                  pl.BlockSpec(memory_space=pl.ANY),
                  pl.BlockSpec(memory_space=pl.ANY)],
        out_specs=pl.BlockSpec(memory_space=pl.ANY),
        out_shape=jax.ShapeDtypeStruct((PAD_ROWS, D_MODEL), F32),
        scratch_shapes=[pltpu.SemaphoreType.DMA(())],
        input_output_aliases={2: 0},
        name="dispatch",
    )(dest_flat, xn, zeros)


def _combine_kernel(dest_ref, ypad_hbm, y0_hbm, y1_hbm, sem):
    outs = (y0_hbm, y1_hbm)

    def copies(t):
        return [_row_copy(ypad_hbm, dest_ref[TOP_K * t + k], outs[k], t, sem) for k in range(TOP_K)]

    def issue(t, carry):
        for cp in copies(t):
            cp.start()

        @pl.when(t >= DMA_WINDOW)
        def _():
            for cp in copies(t - DMA_WINDOW):
                cp.wait()
        return carry

    lax.fori_loop(0, TOKENS, issue, 0)

    def drain(t, carry):
        for cp in copies(t):
            cp.wait()
        return carry

    lax.fori_loop(TOKENS - DMA_WINDOW, TOKENS, drain, 0)


def _combine_rows(dest_flat, y_pad):
    shape = jax.ShapeDtypeStruct((TOKENS, D_MODEL), F32)
    return pl.pallas_call(
        _combine_kernel,
        in_specs=[pl.BlockSpec(memory_space=pltpu.SMEM),
                  pl.BlockSpec(memory_space=pl.ANY)],
        out_specs=[pl.BlockSpec(memory_space=pl.ANY)] * TOP_K,
        out_shape=[shape] * TOP_K,
        scratch_shapes=[pltpu.SemaphoreType.DMA(())],
        name="combine_rows",
    )(dest_flat, y_pad)


def _expert_kernel(be_ref, nact_ref, x_ref, wg_ref, wu_ref, wd_ref, y_ref, wg_b, wu_b, wd_b):
    i = pl.program_id(0)

    @pl.when(i < nact_ref[0])
    def _():
        prev = be_ref[jnp.maximum(i - 1, 0)]

        @pl.when((i == 0) | (be_ref[i] != prev))
        def _():
            wg_b[...] = wg_ref[0].astype(BF16)
            wu_b[...] = wu_ref[0].astype(BF16)
            wd_b[...] = wd_ref[0].astype(BF16)

        x = x_ref[...].astype(BF16)
        gate = _dot(x, wg_b[...])
        up = _dot(x, wu_b[...])
        hmid = (jax.nn.silu(gate) * up).astype(BF16)
        y_ref[...] = _dot(hmid, wd_b[...])


def _experts(blk_expert, n_active, x_pad, w_gate, w_up, w_down):
    blk = lambda i, be, na: (jnp.minimum(i, na[0] - 1), 0)
    wsel = lambda i, be, na: (be[jnp.minimum(i, na[0] - 1)], 0, 0)
    grid_spec = pltpu.PrefetchScalarGridSpec(
        num_scalar_prefetch=2,
        grid=(N_MOE_BLOCKS,),
        in_specs=[pl.BlockSpec((MOE_BLOCK, D_MODEL), blk),
                  pl.BlockSpec((1, D_MODEL, D_EXPERT), wsel),
                  pl.BlockSpec((1, D_MODEL, D_EXPERT), wsel),
                  pl.BlockSpec((1, D_EXPERT, D_MODEL), wsel)],
        out_specs=pl.BlockSpec((MOE_BLOCK, D_MODEL), blk),
        scratch_shapes=[pltpu.VMEM((D_MODEL, D_EXPERT), BF16),
                        pltpu.VMEM((D_MODEL, D_EXPERT), BF16),
                        pltpu.VMEM((D_EXPERT, D_MODEL), BF16)],
    )
    return pl.pallas_call(
        _expert_kernel,
        grid_spec=grid_spec,
        out_shape=jax.ShapeDtypeStruct((PAD_ROWS, D_MODEL), F32),
        compiler_params=_cparams("arbitrary"),
        name="experts",
    )(blk_expert, n_active, x_pad, w_gate, w_up, w_down)


def _final_kernel(h_ref, y0_ref, y1_ref, rw_ref, g_ref, out_ref):
    rw = rw_ref[...]
    h = h_ref[...] + (rw[:, 0:1] * y0_ref[...] + rw[:, 1:2] * y1_ref[...])
    out_ref[...] = _rms(h, g_ref[...])


def _final(h, y0, y1, rw, g):
    row = lambda w: pl.BlockSpec((ROW_TILE, w), lambda i: (i, 0))
    return pl.pallas_call(
        _final_kernel,
        grid=(TOKENS // ROW_TILE,),
        in_specs=[row(D_MODEL)] * 3 + [row(ROUTER_LANES), pl.BlockSpec(g.shape, lambda i: (0, 0))],
        out_specs=row(D_MODEL),
        out_shape=jax.ShapeDtypeStruct((TOKENS, D_MODEL), F32),
        compiler_params=_cparams("parallel"),
        name="final_norm",
    )(h, y0, y1, rw, g)


def _routing_tables(route_i, counts_f):
    counts = counts_f[0, EXPERT_LANE0:EXPERT_LANE0 + N_EXPERTS].astype(jnp.int32)
    padded = (counts + MOE_BLOCK - 1) // MOE_BLOCK * MOE_BLOCK
    padded_end = jnp.cumsum(padded)
    padded_off = padded_end - padded
    experts = route_i[:, 0:TOP_K]
    ranks = route_i[:, TOP_K:2 * TOP_K]
    dest = (padded_off[experts] + ranks).reshape(N_ASSIGN)
    blk_start = jnp.arange(N_MOE_BLOCKS, dtype=jnp.int32) * MOE_BLOCK
    blk_expert = jnp.minimum(jnp.searchsorted(padded_end, blk_start, side='right'),
                             N_EXPERTS - 1).astype(jnp.int32)
    n_active = (padded_end[-1:] // MOE_BLOCK).astype(jnp.int32)
    return dest, blk_expert, n_active


def _split_router_weights(w_router_group, w_router_expert):
    w_e = w_router_expert.transpose(1, 0, 2).reshape(D_MODEL, N_EXPERTS)
    w = jnp.concatenate([w_router_group, w_e], axis=1)
    w = jnp.pad(w, ((0, 0), (0, ROUTER_LANES - w.shape[1])))
    hi = w.astype(BF16)
    lo = (w - hi.astype(F32)).astype(BF16)
    return jnp.concatenate([hi, lo], axis=1)


def kernel(x, rel_bias, norm_mix_g, w_in, pool_w, pool_scale, w_proj_attn, w_proj_pool, w_out,
           norm_ffn_g, w_router_group, w_router_expert, w_gate_e, w_up_e, w_down_e, norm_final_g):
    h = x.reshape(TOKENS, D_MODEL)
    q_scale = jnp.concatenate([jnp.full((ATTN_WIDTH,), HEAD_DIM ** -0.5, F32),
                               jnp.ones((IN_WIDTH - ATTN_WIDTH,), F32)])
    moe = None
    for l in range(DEPTH):
        w_in_b = (w_in[l] * q_scale).astype(BF16)
        h, qkv, u, gates = _project(h, norm_mix_g[l][None], w_in_b, moe)
        attn_outs = [_attention_group(qkv, rel_bias, g) for g in range(N_GROUPS)]
        h = _mix_out(h, attn_outs, u, gates, pool_w[l].astype(BF16), pool_scale[l][None],
                     w_proj_attn[l].astype(BF16), w_proj_pool[l].astype(BF16), w_out[l].astype(BF16))
        xn, route_i, route_w, counts = _route(
            h, norm_ffn_g[l][None], _split_router_weights(w_router_group[l], w_router_expert[l]))
        dest, blk_expert, n_active = _routing_tables(route_i, counts)
        x_pad = _dispatch(dest, xn)
        y_pad = _experts(blk_expert, n_active, x_pad, w_gate_e[l], w_up_e[l], w_down_e[l])
        y0, y1 = _combine_rows(dest, y_pad)
        moe = (y0, y1, route_w)
    out = _final(h, moe[0], moe[1], moe[2], norm_final_g[None])
    return out.reshape(BATCH, SEQ, D_MODEL)
```

```python
import functools
import math

import jax
import jax.numpy as jnp
from jax import lax
from jax.experimental import pallas as pl
from jax.experimental.pallas import tpu as pltpu

F32 = jnp.float32
BF16 = jnp.bfloat16
I32 = jnp.int32

D_MODEL = 1024
BATCH = 8
SEQ = 2048
TOKENS = BATCH * SEQ
DEPTH = 2

HEAD_DIM = 64
HEADS_PER_GROUP = 4
GROUP_WIDTH = HEADS_PER_GROUP * HEAD_DIM
DILATION_PATTERNS = ((128, 1), (512, 4), (2048, 16))
N_GROUPS = len(DILATION_PATTERNS)
N_ATTN_HEADS = N_GROUPS * HEADS_PER_GROUP
ATTN_WIDTH = N_ATTN_HEADS * HEAD_DIM
QKV_WIDTH = 3 * ATTN_WIDTH
N_SIDE = 64
assert all(w // (2 * d) == N_SIDE for w, d in DILATION_PATTERNS)
POOL_WINDOWS = (2, 4, 8, 16)
POOL_GROUP_WIDTH = 128
POOL_WIDTH = len(POOL_WINDOWS) * POOL_GROUP_WIDTH
POOL_HALO = max(POOL_WINDOWS) // 2
N_BRANCHES = 2
GATE_WIDTH = N_BRANCHES * D_MODEL
IN_WIDTH = QKV_WIDTH + POOL_WIDTH + GATE_WIDTH
N_BUCKETS = 32
MAX_DISTANCE = 1024
N_EXPERT_GROUPS = 8
EXPERTS_PER_GROUP = 8
N_EXPERTS = N_EXPERT_GROUPS * EXPERTS_PER_GROUP
TOP_K = 2
D_EXPERT = 512
N_ASSIGN = TOKENS * TOP_K
EPS = 1e-6
NEG_INF = -1e30

LANES = 128
SUBLANES = 8
ROW_TILE = 512
N_TILES = TOKENS // ROW_TILE
Q_BLOCK = 128
K_BLOCK = Q_BLOCK + 2 * N_SIDE
ROUTER_LANES = 128
EXPERT_LANE0 = N_EXPERT_GROUPS
VMEM_LIMIT = 56 * 1024 * 1024

RUN_ALIGN = SUBLANES
MOE_BLOCK = 128
SLOT_CHUNK = 256
MAX_SLOTS = -(-(TOP_K * ROW_TILE + N_EXPERTS * (RUN_ALIGN - 1)) // SLOT_CHUNK) * SLOT_CHUNK
PAD_ROWS = (N_ASSIGN + N_TILES * N_EXPERTS * (RUN_ALIGN - 1)
            + N_EXPERTS * (MOE_BLOCK - RUN_ALIGN))
assert PAD_ROWS % MOE_BLOCK == 0
N_MOE_BLOCKS = PAD_ROWS // MOE_BLOCK


def _cparams(*sem):
    return pltpu.CompilerParams(dimension_semantics=sem, vmem_limit_bytes=VMEM_LIMIT)


def _rms(h, g):
    r = lax.rsqrt(jnp.mean(h * h, axis=-1, keepdims=True) + EPS)
    return (h * r) * g


def _dot(a, b):
    return jnp.dot(a, b, preferred_element_type=F32)


def _row_spec(width):
    return pl.BlockSpec((ROW_TILE, width), lambda i, *_: (i, 0))


def _full_spec(a):
    return pl.BlockSpec(a.shape, lambda i, *_: (0,) * a.ndim)


def _proj_kernel(h_ref, g_ref, w_ref, qkv_ref, u_ref, gate_ref):
    xn = _rms(h_ref[...], g_ref[...]).astype(BF16)
    qkv_ref[...] = _dot(xn, w_ref[:, 0:QKV_WIDTH]).astype(BF16)
    u_ref[...] = _dot(xn, w_ref[:, QKV_WIDTH:QKV_WIDTH + POOL_WIDTH])
    gates = _dot(xn, w_ref[:, QKV_WIDTH + POOL_WIDTH:IN_WIDTH])
    gate_ref[...] = jax.nn.sigmoid(gates).astype(BF16)


def _project(h, g, w_bf16):
    return pl.pallas_call(
        _proj_kernel,
        grid=(N_TILES,),
        in_specs=[_row_spec(D_MODEL), _full_spec(g), _full_spec(w_bf16)],
        out_specs=[_row_spec(QKV_WIDTH), _row_spec(POOL_WIDTH), _row_spec(GATE_WIDTH)],
        out_shape=[jax.ShapeDtypeStruct((TOKENS, QKV_WIDTH), BF16),
                   jax.ShapeDtypeStruct((TOKENS, POOL_WIDTH), F32),
                   jax.ShapeDtypeStruct((TOKENS, GATE_WIDTH), BF16)],
        compiler_params=_cparams("parallel"),
        name="proj",
    )(h, g, w_bf16)


def _attn_kernel(q_ref, k_ref, v_ref, bias_ref, o_ref, kpad, vpad, *, sub_len):
    zpad = jnp.zeros((N_SIDE, GROUP_WIDTH), BF16)
    for pad_ref, src in ((kpad, k_ref), (vpad, v_ref)):
        pad_ref[0:N_SIDE, :] = zpad
        pad_ref[N_SIDE + sub_len:2 * N_SIDE + sub_len, :] = zpad
        pad_ref[N_SIDE:N_SIDE + sub_len, :] = src[0]

    head_of_lane = lax.broadcasted_iota(I32, (1, GROUP_WIDTH), 1) // HEAD_DIM
    qi = lax.broadcasted_iota(I32, (Q_BLOCK, K_BLOCK), 0)
    ki = lax.broadcasted_iota(I32, (Q_BLOCK, K_BLOCK), 1)
    band = jnp.abs(ki - N_SIDE - qi) <= N_SIDE

    def by_head(cols):
        out = cols[HEADS_PER_GROUP - 1]
        for h in range(HEADS_PER_GROUP - 2, -1, -1):
            out = jnp.where(head_of_lane == h, cols[h], out)
        return out

    def block(i, carry):
        r0 = pl.multiple_of(i * Q_BLOCK, Q_BLOCK)
        qb = q_ref[0, pl.ds(r0, Q_BLOCK), :]
        kw = kpad[pl.ds(r0, K_BLOCK), :]
        vw = vpad[pl.ds(r0, K_BLOCK), :]
        zero = jnp.zeros_like(qb)
        q_heads = jnp.concatenate(
            [jnp.where(head_of_lane == h, qb, zero) for h in range(HEADS_PER_GROUP)], axis=0)
        s = lax.dot_general(q_heads, kw, (((1,), (1,)), ((), ())), preferred_element_type=F32)
        s = s.reshape(HEADS_PER_GROUP, Q_BLOCK, K_BLOCK) + bias_ref[...]
        kpos = r0 - N_SIDE + ki
        valid = band & (kpos >= 0) & (kpos < sub_len)
        s = jnp.where(valid[None], s, NEG_INF)
        m = jnp.max(s, axis=-1, keepdims=True)
        p = jnp.exp(s - m)
        den = jnp.sum(p, axis=-1, keepdims=True)
        pb = p.astype(BF16)
        p_cat = jnp.concatenate([pb[h] for h in range(HEADS_PER_GROUP)], axis=1)
        zv = jnp.zeros_like(vw)
        v_heads = jnp.concatenate(
            [jnp.where(head_of_lane == h, vw, zv) for h in range(HEADS_PER_GROUP)], axis=0)
        o = _dot(p_cat, v_heads)
        lse = m + jnp.log(den)
        o_ref[0, pl.ds(r0, Q_BLOCK), 0:GROUP_WIDTH] = o / by_head([den[h] for h in range(HEADS_PER_GROUP)])
        o_ref[0, pl.ds(r0, Q_BLOCK), GROUP_WIDTH:2 * GROUP_WIDTH] = by_head(
            [lse[h] for h in range(HEADS_PER_GROUP)])
        return carry

    lax.fori_loop(0, sub_len // Q_BLOCK, block, 0)


def _t5_bucket(rel):
    nb = N_BUCKETS // 2
    ret = jnp.where(rel > 0, nb, 0)
    n = jnp.abs(rel)
    max_exact = nb // 2
    nf = jnp.maximum(n, max_exact).astype(F32)
    large = max_exact + (jnp.log(nf / max_exact) / math.log(MAX_DISTANCE / max_exact)
                         * (nb - max_exact)).astype(I32)
    large = jnp.minimum(large, nb - 1)
    return ret + jnp.where(n < max_exact, n, large)


def _band_bias(rel_bias, group, dilation):
    qi = jnp.arange(Q_BLOCK)[:, None]
    ki = jnp.arange(K_BLOCK)[None, :]
    bucket = _t5_bucket((ki - N_SIDE - qi) * dilation)
    tab = rel_bias[:, group * HEADS_PER_GROUP:(group + 1) * HEADS_PER_GROUP]
    onehot = (bucket[:, :, None] == jnp.arange(N_BUCKETS)[None, None, :]).astype(F32)
    bias = jnp.einsum('qkb,bh->hqk', onehot, tab, precision=lax.Precision.HIGHEST)
    return bias.astype(F32)


def _attention_group(qkv, rel_bias, group):
    _, dilation = DILATION_PATTERNS[group]
    sub_len = SEQ // dilation
    blocks_per_row = QKV_WIDTH // GROUP_WIDTH
    view = qkv.reshape(BATCH, sub_len, dilation * QKV_WIDTH)
    col = lambda part: (lambda b, r: (b, 0, r * blocks_per_row + part * N_GROUPS + group))
    qkv_spec = lambda part: pl.BlockSpec((1, sub_len, GROUP_WIDTH), col(part))
    bias = _band_bias(rel_bias, group, dilation)
    out = pl.pallas_call(
        functools.partial(_attn_kernel, sub_len=sub_len),
        grid=(BATCH, dilation),
        in_specs=[qkv_spec(0), qkv_spec(1), qkv_spec(2),
                  pl.BlockSpec(bias.shape, lambda b, r: (0, 0, 0))],
        out_specs=pl.BlockSpec((1, sub_len, 2 * GROUP_WIDTH), lambda b, r: (b, 0, r)),
        out_shape=jax.ShapeDtypeStruct((BATCH, sub_len, dilation * 2 * GROUP_WIDTH), F32),
        scratch_shapes=[pltpu.VMEM((sub_len + 2 * N_SIDE, GROUP_WIDTH), BF16),
                        pltpu.VMEM((sub_len + 2 * N_SIDE, GROUP_WIDTH), BF16)],
        compiler_params=_cparams("parallel", "parallel"),
        name=f"attn{group}",
    )(view, view, view, bias)
    return out.reshape(TOKENS, 2 * GROUP_WIDTH)


def _mixout_kernel(h_ref, a0_ref, a1_ref, a2_ref, u_ref, uprev_ref, unext_ref, gate_ref,
                   pw_ref, ps_ref, wpa_ref, wpp_ref, wo_ref, out_ref):
    tiles_per_seq = SEQ // ROW_TILE
    j = pl.program_id(0) % tiles_per_seq

    a_refs = (a0_ref, a1_ref, a2_ref)
    lse = [r[:, GROUP_WIDTH:2 * GROUP_WIDTH] for r in a_refs]
    m = jnp.maximum(jnp.maximum(lse[0], lse[1]), lse[2])
    e = [jnp.exp(l - m) for l in lse]
    num = e[0] * a0_ref[:, 0:GROUP_WIDTH] + e[1] * a1_ref[:, 0:GROUP_WIDTH] + e[2] * a2_ref[:, 0:GROUP_WIDTH]
    attn = num / (e[0] + e[1] + e[2])
    y_attn = _dot(attn.astype(BF16), wpa_ref[...])

    u = u_ref[...]
    prev = jnp.where(j == 0, 0.0, uprev_ref[0])
    nxt = jnp.where(j == tiles_per_seq - 1, 0.0, unext_ref[0])
    ext = jnp.concatenate([prev, u, nxt], axis=0)
    pos = j * ROW_TILE + lax.broadcasted_iota(I32, (ROW_TILE, 1), 0)
    mixed = []
    for gi, w in enumerate(POOL_WINDOWS):
        half = w // 2
        sl = slice(gi * POOL_GROUP_WIDTH, (gi + 1) * POOL_GROUP_WIDTH)
        eg = ext[:, sl]
        acc = eg[POOL_HALO - half:POOL_HALO - half + ROW_TILE]
        for s in range(-half + 1, half):
            acc = acc + eg[POOL_HALO + s:POOL_HALO + s + ROW_TILE]
        cnt = (jnp.minimum(pos + half, SEQ) - jnp.maximum(pos - half, 0)).astype(F32)
        pooled = acc / cnt - u[:, sl]
        mixed.append(_dot(pooled.astype(BF16), pw_ref[gi]) * ps_ref[:, sl])
    y_pool = _dot(jnp.concatenate(mixed, axis=1).astype(BF16), wpp_ref[...])

    g_attn = gate_ref[:, 0:D_MODEL].astype(F32)
    g_pool = gate_ref[:, D_MODEL:GATE_WIDTH].astype(F32)
    y = g_attn * y_attn + g_pool * y_pool
    out_ref[...] = h_ref[...] + _dot(y.astype(BF16), wo_ref[...])


def _mix_out(h, attn_outs, u, gates, pool_w, pool_scale, w_proj_attn, w_proj_pool, w_out):
    halo_blocks = ROW_TILE // POOL_HALO
    u3 = u.reshape(TOKENS // POOL_HALO, POOL_HALO, POOL_WIDTH)
    last = TOKENS // POOL_HALO - 1
    prev_spec = pl.BlockSpec((1, POOL_HALO, POOL_WIDTH),
                             lambda i: (jnp.maximum(i * halo_blocks - 1, 0), 0, 0))
    next_spec = pl.BlockSpec((1, POOL_HALO, POOL_WIDTH),
                             lambda i: (jnp.minimum((i + 1) * halo_blocks, last), 0, 0))
    weights = (pool_w, pool_scale, w_proj_attn, w_proj_pool, w_out)
    return pl.pallas_call(
        _mixout_kernel,
        grid=(N_TILES,),
        in_specs=[_row_spec(D_MODEL)] + [_row_spec(2 * GROUP_WIDTH)] * 3
                 + [_row_spec(POOL_WIDTH), prev_spec, next_spec, _row_spec(GATE_WIDTH)]
                 + [_full_spec(w) for w in weights],
        out_specs=_row_spec(D_MODEL),
        out_shape=jax.ShapeDtypeStruct((TOKENS, D_MODEL), F32),
        compiler_params=_cparams("parallel"),
        name="mixout",
    )(h, *attn_outs, u, u3, u3, gates, *weights)


def _router_kernel(h_ref, g_ref, wr_ref, xn_ref, ri_ref, rw_ref, cnt_ref):
    xn = _rms(h_ref[...], g_ref[...])
    hi = xn.astype(BF16)
    xn_ref[...] = hi
    lo = (xn - hi.astype(F32)).astype(BF16)
    both = _dot(hi, wr_ref[...])
    lg = both[:, 0:ROUTER_LANES] + (both[:, ROUTER_LANES:] + _dot(lo, wr_ref[:, 0:ROUTER_LANES]))

    lane = lax.broadcasted_iota(I32, (ROW_TILE, ROUTER_LANES), 1)
    lanef = lane.astype(F32)
    low = jnp.float32(-3.0e38)
    far = jnp.float32(ROUTER_LANES)
    first = lambda hit: jnp.min(jnp.where(hit, lanef, far), axis=-1, keepdims=True)

    is_group = lane < N_EXPERT_GROUPS
    gl = jnp.where(is_group, lg, low)
    gmax = jnp.max(gl, axis=-1, keepdims=True)
    gidx = first(gl == gmax).astype(I32)
    gden = jnp.sum(jnp.where(is_group, jnp.exp(gl - gmax), 0.0), axis=-1, keepdims=True)
    g_p = 1.0 / gden

    in_group = ((lane >= EXPERT_LANE0) & (lane < EXPERT_LANE0 + N_EXPERTS)
                & ((lane - EXPERT_LANE0) // EXPERTS_PER_GROUP == gidx))
    el = jnp.where(in_group, lg, low)
    t1 = jnp.max(el, axis=-1, keepdims=True)
    l1 = first(in_group & (el == t1))
    rest = in_group & (lanef != l1)
    el2 = jnp.where(rest, lg, low)
    t2 = jnp.max(el2, axis=-1, keepdims=True)
    l2 = first(rest & (el2 == t2))
    e2 = jnp.exp(t2 - t1)
    w1 = g_p * (1.0 / (1.0 + e2))
    w2 = g_p * (e2 / (1.0 + e2))

    hit1 = lanef == l1
    hit2 = lanef == l2
    onehot = (hit1 | hit2).astype(BF16)
    ri = lax.broadcasted_iota(I32, (ROW_TILE, ROW_TILE), 0)
    ci = lax.broadcasted_iota(I32, (ROW_TILE, ROW_TILE), 1)
    before = (ci < ri).astype(BF16)
    seen = _dot(before, onehot)
    r1 = jnp.sum(jnp.where(hit1, seen, 0.0), axis=-1, keepdims=True)
    r2 = jnp.sum(jnp.where(hit2, seen, 0.0), axis=-1, keepdims=True)

    packed = jnp.zeros((ROW_TILE, ROUTER_LANES), F32)
    for k, v in enumerate((l1, l2, r1, r2)):
        packed = jnp.where(lane == k, v, packed)
    ri_ref[...] = packed.astype(I32)
    rw_ref[...] = jnp.where(lane == 0, w1, jnp.where(lane == 1, w2, 0.0))
    counts = jnp.sum(onehot.astype(F32), axis=0, keepdims=True)
    cnt_ref[0] = jnp.broadcast_to(counts, (SUBLANES, ROUTER_LANES))


def _route(h, g, wr_split):
    return pl.pallas_call(
        _router_kernel,
        grid=(N_TILES,),
        in_specs=[_row_spec(D_MODEL), _full_spec(g), _full_spec(wr_split)],
        out_specs=[_row_spec(D_MODEL), _row_spec(ROUTER_LANES), _row_spec(ROUTER_LANES),
                   pl.BlockSpec((1, SUBLANES, ROUTER_LANES), lambda i: (i, 0, 0))],
        out_shape=[jax.ShapeDtypeStruct((TOKENS, D_MODEL), BF16),
                   jax.ShapeDtypeStruct((TOKENS, ROUTER_LANES), I32),
                   jax.ShapeDtypeStruct((TOKENS, ROUTER_LANES), F32),
                   jax.ShapeDtypeStruct((N_TILES, SUBLANES, ROUTER_LANES), F32)],
        compiler_params=_cparams("parallel"),
        name="router",
    )(h, g, wr_split)


def _slots(ri_ref, off_ref):
    ri = ri_ref[...]
    lane = lax.broadcasted_iota(I32, (ROW_TILE, ROUTER_LANES), 1)
    off = off_ref[0, 0:1, :]
    pick = lambda k: jnp.sum(jnp.where(lane == ri[:, k:k + 1], off, 0.0), axis=-1, keepdims=True)
    return (pick(0) + ri[:, 2:3].astype(F32), pick(1) + ri[:, 3:4].astype(F32))


def _run_copy(tile, e, loff_s, c8_s, dst_s, buf_ref, slot, hbm_ref, sem, to_hbm):
    k = tile * N_EXPERTS + e
    n = pl.multiple_of(c8_s[k], RUN_ALIGN)
    vm = buf_ref.at[slot, pl.ds(pl.multiple_of(loff_s[k], RUN_ALIGN), n)]
    hb = hbm_ref.at[pl.ds(pl.multiple_of(dst_s[k], RUN_ALIGN), n)]
    src, dst = (vm, hb) if to_hbm else (hb, vm)
    return n, pltpu.make_async_copy(src, dst, sem.at[slot])


def _for_runs(fn):
    def body(e, carry):
        fn(e)
        return carry
    lax.fori_loop(0, N_EXPERTS, body, 0)


def _dispatch_kernel(loff_s, c8_s, dst_s, nslot_s, zdst_s, zcnt_s,
                     xn_ref, ri_ref, off_ref, xpad_hbm, sorted_buf, zero_buf, sem, zsem):
    i = pl.program_id(0)
    slot = i % 2
    last = pl.num_programs(0) - 1

    def zero_copy(e):
        n = pl.multiple_of(zcnt_s[e], RUN_ALIGN)
        dst = xpad_hbm.at[pl.ds(pl.multiple_of(zdst_s[e], RUN_ALIGN), n)]
        return n, pltpu.make_async_copy(zero_buf.at[pl.ds(0, n)], dst, zsem)

    def start(n, cp):
        @pl.when(n > 0)
        def _():
            cp.start()

    def wait(n, cp):
        @pl.when(n > 0)
        def _():
            cp.wait()

    @pl.when(i == 0)
    def _():
        zero_buf[...] = jnp.zeros_like(zero_buf)
        _for_runs(lambda e: start(*zero_copy(e)))

    s1, s2 = _slots(ri_ref, off_ref)
    eye = (lax.broadcasted_iota(I32, (ROW_TILE, ROW_TILE), 0)
           == lax.broadcasted_iota(I32, (ROW_TILE, ROW_TILE), 1))
    s1_row = jnp.sum(jnp.where(eye, s1, 0.0), axis=0, keepdims=True)
    s2_row = jnp.sum(jnp.where(eye, s2, 0.0), axis=0, keepdims=True)
    xn = xn_ref[...]
    for c in range(MAX_SLOTS // SLOT_CHUNK):
        @pl.when(c * SLOT_CHUNK < nslot_s[i])
        def _():
            srow = (c * SLOT_CHUNK
                    + lax.broadcasted_iota(I32, (SLOT_CHUNK, ROW_TILE), 0)).astype(F32)
            perm = ((srow == s1_row) | (srow == s2_row)).astype(BF16)
            sorted_buf[slot, c * SLOT_CHUNK:(c + 1) * SLOT_CHUNK, :] = _dot(perm, xn)

    copy = lambda tile, sl: (lambda e: _run_copy(tile, e, loff_s, c8_s, dst_s, sorted_buf, sl,
                                                 xpad_hbm, sem, True))
    mine = copy(i, slot)
    _for_runs(lambda e: start(*mine(e)))

    @pl.when(i > 0)
    def _():
        prev = copy(i - 1, 1 - slot)
        _for_runs(lambda e: wait(*prev(e)))

    @pl.when(i == last)
    def _():
        _for_runs(lambda e: wait(*mine(e)))
        _for_runs(lambda e: wait(*zero_copy(e)))


def _dispatch(tables, xn, route_i):
    grid_spec = pltpu.PrefetchScalarGridSpec(
        num_scalar_prefetch=6,
        grid=(N_TILES,),
        in_specs=[_row_spec(D_MODEL), _row_spec(ROUTER_LANES),
                  pl.BlockSpec((1, SUBLANES, ROUTER_LANES), lambda i, *_: (i, 0, 0)),
                  ],
        out_specs=pl.BlockSpec(memory_space=pl.ANY),
        scratch_shapes=[pltpu.VMEM((2, MAX_SLOTS, D_MODEL), F32),
                        pltpu.VMEM((MOE_BLOCK, D_MODEL), F32),
                        pltpu.SemaphoreType.DMA((2,)),
                        pltpu.SemaphoreType.DMA(())],
    )
    return pl.pallas_call(
        _dispatch_kernel,
        grid_spec=grid_spec,
        out_shape=jax.ShapeDtypeStruct((PAD_ROWS, D_MODEL), F32),
        compiler_params=_cparams("arbitrary"),
        name="dispatch",
    )(tables["loff"], tables["c8"], tables["dst"], tables["nslot"], tables["zdst"], tables["zcnt"],
      xn, route_i, tables["offrow"])


def _combine_kernel(loff_s, c8_s, dst_s, nslot_s,
                    h_ref, ri_ref, rw_ref, off_ref, g_ref, ypad_hbm, out_ref,
                    ybuf, acc, sem, *, final_norm):
    i = pl.program_id(0)
    slot = i % 2
    last = pl.num_programs(0) - 1
    fetch = lambda tile, sl: (lambda e: _run_copy(tile, e, loff_s, c8_s, dst_s, ybuf, sl,
                                                  ypad_hbm, sem, False))

    def start(n, cp):
        @pl.when(n > 0)
        def _():
            cp.start()

    def wait(n, cp):
        @pl.when(n > 0)
        def _():
            cp.wait()

    @pl.when(i == 0)
    def _():
        ybuf[...] = jnp.zeros_like(ybuf)
        first = fetch(i, slot)
        _for_runs(lambda e: start(*first(e)))

    @pl.when(i < last)
    def _():
        nxt = fetch(i + 1, 1 - slot)
        _for_runs(lambda e: start(*nxt(e)))

    mine = fetch(i, slot)
    _for_runs(lambda e: wait(*mine(e)))

    s1, s2 = _slots(ri_ref, off_ref)
    rw = rw_ref[...]
    w1 = rw[:, 0:1]
    w2 = rw[:, 1:2]
    acc[...] = jnp.zeros_like(acc)
    for c in range(MAX_SLOTS // SLOT_CHUNK):
        @pl.when(c * SLOT_CHUNK < nslot_s[i])
        def _():
            scol = (c * SLOT_CHUNK
                    + lax.broadcasted_iota(I32, (ROW_TILE, SLOT_CHUNK), 1)).astype(F32)
            yb = ybuf[slot, c * SLOT_CHUNK:(c + 1) * SLOT_CHUNK, :].astype(BF16)
            y1 = _dot((scol == s1).astype(BF16), yb)
            y2 = _dot((scol == s2).astype(BF16), yb)
            acc[...] += w1 * y1 + w2 * y2

    h = h_ref[...] + acc[...]
    out_ref[...] = _rms(h, g_ref[...]) if final_norm else h


def _combine(tables, h, route_i, route_w, y_pad, g, final_norm):
    grid_spec = pltpu.PrefetchScalarGridSpec(
        num_scalar_prefetch=4,
        grid=(N_TILES,),
        in_specs=[_row_spec(D_MODEL), _row_spec(ROUTER_LANES), _row_spec(ROUTER_LANES),
                  pl.BlockSpec((1, SUBLANES, ROUTER_LANES), lambda i, *_: (i, 0, 0)),
                  _full_spec(g),
                  pl.BlockSpec(memory_space=pl.ANY)],
        out_specs=_row_spec(D_MODEL),
        scratch_shapes=[pltpu.VMEM((2, MAX_SLOTS, D_MODEL), F32),
                        pltpu.VMEM((ROW_TILE, D_MODEL), F32),
                        pltpu.SemaphoreType.DMA((2,))],
    )
    return pl.pallas_call(
        functools.partial(_combine_kernel, final_norm=final_norm),
        grid_spec=grid_spec,
        out_shape=jax.ShapeDtypeStruct((TOKENS, D_MODEL), F32),
        compiler_params=_cparams("arbitrary"),
        name="combine",
    )(tables["loff"], tables["c8"], tables["dst"], tables["nslot"],
      h, route_i, route_w, tables["offrow"], g, y_pad)


def _expert_kernel(be_ref, nact_ref, x_ref, wg_ref, wu_ref, wd_ref, y_ref, wg_b, wu_b, wd_b):
    i = pl.program_id(0)

    @pl.when(i < nact_ref[0])
    def _():
        prev = be_ref[jnp.maximum(i - 1, 0)]

        @pl.when((i == 0) | (be_ref[i] != prev))
        def _():
            wg_b[...] = wg_ref[0, 0].astype(BF16)
            wu_b[...] = wu_ref[0, 0].astype(BF16)
            wd_b[...] = wd_ref[0, 0].astype(BF16)

        x = x_ref[...].astype(BF16)
        gate = _dot(x, wg_b[...])
        up = _dot(x, wu_b[...])
        hmid = (jax.nn.silu(gate) * up).astype(BF16)
        y_ref[...] = _dot(hmid, wd_b[...])


def _experts(layer, blk_expert, n_active, x_pad, w_gate, w_up, w_down):
    blk = lambda i, be, na: (jnp.minimum(i, na[0] - 1), 0)
    wsel = lambda i, be, na: (layer, be[jnp.minimum(i, na[0] - 1)], 0, 0)
    grid_spec = pltpu.PrefetchScalarGridSpec(
        num_scalar_prefetch=2,
        grid=(N_MOE_BLOCKS,),
        in_specs=[pl.BlockSpec((MOE_BLOCK, D_MODEL), blk),
                  pl.BlockSpec((1, 1, D_MODEL, D_EXPERT), wsel),
                  pl.BlockSpec((1, 1, D_MODEL, D_EXPERT), wsel),
                  pl.BlockSpec((1, 1, D_EXPERT, D_MODEL), wsel)],
        out_specs=pl.BlockSpec((MOE_BLOCK, D_MODEL), blk),
        scratch_shapes=[pltpu.VMEM((D_MODEL, D_EXPERT), BF16),
                        pltpu.VMEM((D_MODEL, D_EXPERT), BF16),
                        pltpu.VMEM((D_EXPERT, D_MODEL), BF16)],
    )
    return pl.pallas_call(
        _expert_kernel,
        grid_spec=grid_spec,
        out_shape=jax.ShapeDtypeStruct((PAD_ROWS, D_MODEL), F32),
        compiler_params=_cparams("arbitrary"),
        name="experts",
    )(blk_expert, n_active, x_pad, w_gate, w_up, w_down)


def _routing_tables(tile_counts):
    cnt = tile_counts[:, 0, EXPERT_LANE0:EXPERT_LANE0 + N_EXPERTS].astype(I32)
    c8 = (cnt + RUN_ALIGN - 1) // RUN_ALIGN * RUN_ALIGN
    loff = jnp.cumsum(c8, axis=1) - c8
    nslot = jnp.sum(c8, axis=1)
    tot = jnp.sum(c8, axis=0)
    padded = (tot + MOE_BLOCK - 1) // MOE_BLOCK * MOE_BLOCK
    end = jnp.cumsum(padded)
    base = end - padded
    dst = base[None, :] + jnp.cumsum(c8, axis=0) - c8
    blk_start = jnp.arange(N_MOE_BLOCKS, dtype=I32) * MOE_BLOCK
    blk_expert = jnp.minimum(jnp.sum((end[None, :] <= blk_start[:, None]).astype(I32), axis=1),
                             N_EXPERTS - 1).astype(I32)
    offrow = jnp.zeros((N_TILES, ROUTER_LANES), F32).at[:, EXPERT_LANE0:EXPERT_LANE0 + N_EXPERTS].set(
        loff.astype(F32))
    offrow = jnp.broadcast_to(offrow[:, None, :], (N_TILES, SUBLANES, ROUTER_LANES))
    return {
        "loff": loff.reshape(-1).astype(I32), "c8": c8.reshape(-1).astype(I32),
        "dst": dst.reshape(-1).astype(I32), "nslot": nslot.astype(I32),
        "zdst": (base + tot).astype(I32), "zcnt": (padded - tot).astype(I32),
        "offrow": offrow, "blk_expert": blk_expert,
        "n_active": (end[-1:] // MOE_BLOCK).astype(I32),
    }


def _split_router_weights(w_router_group, w_router_expert):
    w_e = w_router_expert.transpose(1, 0, 2).reshape(D_MODEL, N_EXPERTS)
    w = jnp.concatenate([w_router_group, w_e], axis=1)
    w = jnp.pad(w, ((0, 0), (0, ROUTER_LANES - w.shape[1])))
    hi = w.astype(BF16)
    lo = (w - hi.astype(F32)).astype(BF16)
    return jnp.concatenate([hi, lo], axis=1)


def kernel(x, rel_bias, norm_mix_g, w_in, pool_w, pool_scale, w_proj_attn, w_proj_pool, w_out,
           norm_ffn_g, w_router_group, w_router_expert, w_gate_e, w_up_e, w_down_e, norm_final_g):
    h = x.reshape(TOKENS, D_MODEL)
    q_scale = jnp.concatenate([jnp.full((ATTN_WIDTH,), HEAD_DIM ** -0.5, F32),
                               jnp.ones((IN_WIDTH - ATTN_WIDTH,), F32)])
    for l in range(DEPTH):
        w_in_b = (w_in[l] * q_scale).astype(BF16)
        qkv, u, gates = _project(h, norm_mix_g[l][None], w_in_b)
        attn_outs = [_attention_group(qkv, rel_bias, g) for g in range(N_GROUPS)]
        h = _mix_out(h, attn_outs, u, gates, pool_w[l].astype(BF16), pool_scale[l][None],
                     w_proj_attn[l].astype(BF16), w_proj_pool[l].astype(BF16), w_out[l].astype(BF16))
        xn, route_i, route_w, tile_counts = _route(
            h, norm_ffn_g[l][None], _split_router_weights(w_router_group[l], w_router_expert[l]))
        tables = _routing_tables(tile_counts)
        x_pad = _dispatch(tables, xn, route_i)
        y_pad = _experts(l, tables["blk_expert"], tables["n_active"], x_pad, w_gate_e, w_up_e, w_down_e)
        final = l == DEPTH - 1
        h = _combine(tables, h, route_i, route_w, y_pad, norm_final_g[None], final)
    return h.reshape(BATCH, SEQ, D_MODEL)
```

```python
import functools
import math

import jax
import jax.numpy as jnp
from jax import lax
from jax.experimental import pallas as pl
from jax.experimental.pallas import tpu as pltpu

F32 = jnp.float32
BF16 = jnp.bfloat16
I32 = jnp.int32
U32 = jnp.uint32

D_MODEL = 1024
BATCH = 8
SEQ = 2048
TOKENS = BATCH * SEQ
DEPTH = 2

HEAD_DIM = 64
HEADS_PER_GROUP = 4
GROUP_WIDTH = HEADS_PER_GROUP * HEAD_DIM
DILATION_PATTERNS = ((128, 1), (512, 4), (2048, 16))
N_GROUPS = len(DILATION_PATTERNS)
N_ATTN_HEADS = N_GROUPS * HEADS_PER_GROUP
ATTN_WIDTH = N_ATTN_HEADS * HEAD_DIM
QKV_WIDTH = 3 * ATTN_WIDTH
GROUP_QKV = 3 * GROUP_WIDTH
N_SIDE = 64
assert all(w // (2 * d) == N_SIDE for w, d in DILATION_PATTERNS)
POOL_WINDOWS = (2, 4, 8, 16)
POOL_GROUP_WIDTH = 128
POOL_WIDTH = len(POOL_WINDOWS) * POOL_GROUP_WIDTH
POOL_HALO = max(POOL_WINDOWS) // 2
N_BRANCHES = 2
GATE_WIDTH = N_BRANCHES * D_MODEL
IN_WIDTH = QKV_WIDTH + POOL_WIDTH + GATE_WIDTH
N_BUCKETS = 32
MAX_DISTANCE = 1024
N_EXPERT_GROUPS = 8
EXPERTS_PER_GROUP = 8
N_EXPERTS = N_EXPERT_GROUPS * EXPERTS_PER_GROUP
TOP_K = 2
D_EXPERT = 512
N_ASSIGN = TOKENS * TOP_K
EPS = 1e-6
NEG_INF = -1e30

LANES = 128
SUBLANES = 8
ROW_TILE = 512
N_TILES = TOKENS // ROW_TILE
TILES_PER_SEQ = SEQ // ROW_TILE
Q_BLOCK = 128
K_BLOCK = Q_BLOCK + 2 * N_SIDE
ATTN_SLABS = 2 * GROUP_WIDTH // LANES
ROUTER_LANES = 128
EXPERT_LANE0 = N_EXPERT_GROUPS
VMEM_LIMIT = 56 * 1024 * 1024

RUN_ALIGN = SUBLANES
MOE_BLOCK = 256
HALF = D_MODEL // 2
X_WORDS = HALF + LANES
SLOT_CHUNK = 256
MAX_SLOTS = -(-(TOP_K * ROW_TILE + N_EXPERTS * (RUN_ALIGN - 1)) // SLOT_CHUNK) * SLOT_CHUNK
PAD_ROWS = (N_ASSIGN + N_TILES * N_EXPERTS * (RUN_ALIGN - 1)
            + N_EXPERTS * (MOE_BLOCK - RUN_ALIGN))
assert PAD_ROWS % MOE_BLOCK == 0
N_MOE_BLOCKS = PAD_ROWS // MOE_BLOCK
HIGH_HALF = 0xFFFF0000


def _cparams(*sem):
    return pltpu.CompilerParams(dimension_semantics=sem, vmem_limit_bytes=VMEM_LIMIT)


def _rms(h, g):
    r = lax.rsqrt(jnp.mean(h * h, axis=-1, keepdims=True) + EPS)
    return (h * r) * g


def _dot(a, b):
    return jnp.dot(a, b, preferred_element_type=F32)


def _row_spec(width):
    return pl.BlockSpec((ROW_TILE, width), lambda i, *_: (i, 0))


def _full_spec(a):
    return pl.BlockSpec(a.shape, lambda i, *_: (0,) * a.ndim)


def _bits(x):
    return lax.bitcast_convert_type(x, U32)


def _pack_halves(a, b):
    return (_bits(a) & jnp.uint32(HIGH_HALF)) | (_bits(b) >> 16)


def _unpack_halves(words):
    hi = lax.bitcast_convert_type(words & jnp.uint32(HIGH_HALF), F32)
    lo = lax.bitcast_convert_type(words << 16, F32)
    return hi.astype(BF16), lo.astype(BF16)


def _proj_kernel(h_ref, g_ref, w_ref, q0_ref, q1_ref, q2_ref, u_ref, gate_ref, slabs):
    xn = _rms(h_ref[...], g_ref[...]).astype(BF16)
    n_slabs = GROUP_QKV // LANES
    for g, out_ref in enumerate((q0_ref, q1_ref, q2_ref)):
        dilation = DILATION_PATTERNS[g][1]
        res = _dot(xn, w_ref[:, g * GROUP_QKV:(g + 1) * GROUP_QKV])
        if dilation == 1:
            out_ref[0, 0] = res.astype(BF16)
            continue
        for s in range(n_slabs):
            slabs[s] = res[:, s * LANES:(s + 1) * LANES]
        n = ROW_TILE // dilation
        for r in range(dilation):
            rows = [slabs[s, pl.ds(r, n, stride=dilation), :] for s in range(n_slabs)]
            out_ref[0, r] = jnp.concatenate(rows, axis=1).astype(BF16)
    u_ref[...] = _dot(xn, w_ref[:, QKV_WIDTH:QKV_WIDTH + POOL_WIDTH])
    gates = _dot(xn, w_ref[:, QKV_WIDTH + POOL_WIDTH:IN_WIDTH])
    gate_ref[...] = jax.nn.sigmoid(gates).astype(BF16)


def _project(h, g, w_bf16):
    qkv_shapes, qkv_specs = [], []
    for _, d in DILATION_PATTERNS:
        qkv_shapes.append(jax.ShapeDtypeStruct((BATCH, d, SEQ // d, GROUP_QKV), BF16))
        qkv_specs.append(pl.BlockSpec((1, d, ROW_TILE // d, GROUP_QKV),
                                      lambda i: (i // TILES_PER_SEQ, 0, i % TILES_PER_SEQ, 0)))
    res = pl.pallas_call(
        _proj_kernel,
        grid=(N_TILES,),
        in_specs=[_row_spec(D_MODEL), _full_spec(g), _full_spec(w_bf16)],
        out_specs=qkv_specs + [_row_spec(POOL_WIDTH), _row_spec(GATE_WIDTH)],
        out_shape=qkv_shapes + [jax.ShapeDtypeStruct((TOKENS, POOL_WIDTH), F32),
                                jax.ShapeDtypeStruct((TOKENS, GATE_WIDTH), BF16)],
        scratch_shapes=[pltpu.VMEM((GROUP_QKV // LANES, ROW_TILE, LANES), F32)],
        compiler_params=_cparams("parallel"),
        name="proj",
    )(h, g, w_bf16)
    return res[:N_GROUPS], res[N_GROUPS], res[N_GROUPS + 1]


def _attn_kernel(qkv_ref, bias_ref, o_ref, kpad, vpad, *, dilation):
    sub_len = SEQ // dilation
    zpad = jnp.zeros((N_SIDE, GROUP_WIDTH), BF16)
    for pad_ref in (kpad, vpad):
        pad_ref[0:N_SIDE, :] = zpad
        pad_ref[N_SIDE + sub_len:2 * N_SIDE + sub_len, :] = zpad

    head_of_lane = lax.broadcasted_iota(I32, (1, GROUP_WIDTH), 1) // HEAD_DIM
    qi = lax.broadcasted_iota(I32, (Q_BLOCK, K_BLOCK), 0)
    ki = lax.broadcasted_iota(I32, (Q_BLOCK, K_BLOCK), 1)
    band = jnp.abs(ki - N_SIDE - qi) <= N_SIDE

    def by_head(cols):
        out = cols[HEADS_PER_GROUP - 1]
        for h in range(HEADS_PER_GROUP - 2, -1, -1):
            out = jnp.where(head_of_lane == h, cols[h], out)
        return out

    def residue(r, carry):
        kpad[N_SIDE:N_SIDE + sub_len, :] = qkv_ref[0, r, :, GROUP_WIDTH:2 * GROUP_WIDTH]
        vpad[N_SIDE:N_SIDE + sub_len, :] = qkv_ref[0, r, :, 2 * GROUP_WIDTH:3 * GROUP_WIDTH]

        def block(i, carry2):
            r0 = pl.multiple_of(i * Q_BLOCK, Q_BLOCK)
            qb = qkv_ref[0, r, pl.ds(r0, Q_BLOCK), 0:GROUP_WIDTH]
            kw = kpad[pl.ds(r0, K_BLOCK), :]
            vw = vpad[pl.ds(r0, K_BLOCK), :]
            zero = jnp.zeros_like(qb)
            q_heads = jnp.concatenate(
                [jnp.where(head_of_lane == h, qb, zero) for h in range(HEADS_PER_GROUP)], axis=0)
            s = lax.dot_general(q_heads, kw, (((1,), (1,)), ((), ())), preferred_element_type=F32)
            s = s.reshape(HEADS_PER_GROUP, Q_BLOCK, K_BLOCK) + bias_ref[...]
            kpos = r0 - N_SIDE + ki
            valid = band & (kpos >= 0) & (kpos < sub_len)
            s = jnp.where(valid[None], s, NEG_INF)
            m = jnp.max(s, axis=-1, keepdims=True)
            p = jnp.exp(s - m)
            den = jnp.sum(p, axis=-1, keepdims=True)
            pb = p.astype(BF16)
            p_cat = jnp.concatenate([pb[h] for h in range(HEADS_PER_GROUP)], axis=1)
            zv = jnp.zeros_like(vw)
            v_heads = jnp.concatenate(
                [jnp.where(head_of_lane == h, vw, zv) for h in range(HEADS_PER_GROUP)], axis=0)
            o = _dot(p_cat, v_heads)
            lse = m + jnp.log(den)
            o = o / by_head([den[h] for h in range(HEADS_PER_GROUP)])
            lse_lanes = by_head([lse[h] for h in range(HEADS_PER_GROUP)])
            if dilation == 1:
                rows = pl.ds(r0, Q_BLOCK)
            else:
                rows = pl.ds(r + dilation * r0, Q_BLOCK, stride=dilation)
            for half in range(GROUP_WIDTH // LANES):
                lanes = slice(half * LANES, (half + 1) * LANES)
                o_ref[0, half, rows, :] = o[:, lanes]
                o_ref[0, GROUP_WIDTH // LANES + half, rows, :] = lse_lanes[:, lanes]
            return carry2

        lax.fori_loop(0, sub_len // Q_BLOCK, block, 0)
        return carry

    lax.fori_loop(0, dilation, residue, 0)


def _t5_bucket(rel):
    nb = N_BUCKETS // 2
    ret = jnp.where(rel > 0, nb, 0)
    n = jnp.abs(rel)
    max_exact = nb // 2
    nf = jnp.maximum(n, max_exact).astype(F32)
    large = max_exact + (jnp.log(nf / max_exact) / math.log(MAX_DISTANCE / max_exact)
                         * (nb - max_exact)).astype(I32)
    large = jnp.minimum(large, nb - 1)
    return ret + jnp.where(n < max_exact, n, large)


def _band_bias(rel_bias, group, dilation):
    qi = jnp.arange(Q_BLOCK)[:, None]
    ki = jnp.arange(K_BLOCK)[None, :]
    bucket = _t5_bucket((ki - N_SIDE - qi) * dilation)
    tab = rel_bias[:, group * HEADS_PER_GROUP:(group + 1) * HEADS_PER_GROUP]
    onehot = (bucket[:, :, None] == jnp.arange(N_BUCKETS)[None, None, :]).astype(F32)
    bias = jnp.einsum('qkb,bh->hqk', onehot, tab, precision=lax.Precision.HIGHEST)
    return bias.astype(F32)


def _attention_group(qkv_g, rel_bias, group):
    _, dilation = DILATION_PATTERNS[group]
    sub_len = SEQ // dilation
    bias = _band_bias(rel_bias, group, dilation)
    return pl.pallas_call(
        functools.partial(_attn_kernel, dilation=dilation),
        grid=(BATCH,),
        in_specs=[pl.BlockSpec((1, dilation, sub_len, GROUP_QKV), lambda b: (b, 0, 0, 0)),
                  pl.BlockSpec(bias.shape, lambda b: (0, 0, 0))],
        out_specs=pl.BlockSpec((1, ATTN_SLABS, SEQ, LANES), lambda b: (b, 0, 0, 0)),
        out_shape=jax.ShapeDtypeStruct((BATCH, ATTN_SLABS, SEQ, LANES), F32),
        scratch_shapes=[pltpu.VMEM((sub_len + 2 * N_SIDE, GROUP_WIDTH), BF16),
                        pltpu.VMEM((sub_len + 2 * N_SIDE, GROUP_WIDTH), BF16)],
        compiler_params=_cparams("parallel"),
        name=f"attn{group}",
    )(qkv_g, bias)


def _mixout_kernel(h_ref, a0_ref, a1_ref, a2_ref, u_ref, uprev_ref, unext_ref, gate_ref,
                   pw_ref, ps_ref, wpa_ref, wpp_ref, wo_ref, out_ref):
    j = pl.program_id(0) % TILES_PER_SEQ

    a_refs = (a0_ref, a1_ref, a2_ref)
    halves = GROUP_WIDTH // LANES
    attn_halves = []
    for half in range(halves):
        lse = [r[0, halves + half] for r in a_refs]
        m = jnp.maximum(jnp.maximum(lse[0], lse[1]), lse[2])
        e = [jnp.exp(l - m) for l in lse]
        num = e[0] * a0_ref[0, half] + e[1] * a1_ref[0, half] + e[2] * a2_ref[0, half]
        attn_halves.append(num / (e[0] + e[1] + e[2]))
    attn = jnp.concatenate(attn_halves, axis=1)
    y_attn = _dot(attn.astype(BF16), wpa_ref[...])

    u = u_ref[...]
    prev = jnp.where(j == 0, 0.0, uprev_ref[0])
    nxt = jnp.where(j == TILES_PER_SEQ - 1, 0.0, unext_ref[0])
    ext = jnp.concatenate([prev, u, nxt], axis=0)
    pos = j * ROW_TILE + lax.broadcasted_iota(I32, (ROW_TILE, 1), 0)
    mixed = []
    for gi, w in enumerate(POOL_WINDOWS):
        half = w // 2
        sl = slice(gi * POOL_GROUP_WIDTH, (gi + 1) * POOL_GROUP_WIDTH)
        eg = ext[:, sl]
        acc = eg[POOL_HALO - half:POOL_HALO - half + ROW_TILE]
        for s in range(-half + 1, half):
            acc = acc + eg[POOL_HALO + s:POOL_HALO + s + ROW_TILE]
        cnt = (jnp.minimum(pos + half, SEQ) - jnp.maximum(pos - half, 0)).astype(F32)
        pooled = acc / cnt - u[:, sl]
        mixed.append(_dot(pooled.astype(BF16), pw_ref[gi]) * ps_ref[:, sl])
    y_pool = _dot(jnp.concatenate(mixed, axis=1).astype(BF16), wpp_ref[...])

    g_attn = gate_ref[:, 0:D_MODEL].astype(F32)
    g_pool = gate_ref[:, D_MODEL:GATE_WIDTH].astype(F32)
    y = g_attn * y_attn + g_pool * y_pool
    out_ref[...] = h_ref[...] + _dot(y.astype(BF16), wo_ref[...])


def _mix_out(h, attn_outs, u, gates, pool_w, pool_scale, w_proj_attn, w_proj_pool, w_out):
    halo_blocks = ROW_TILE // POOL_HALO
    u3 = u.reshape(TOKENS // POOL_HALO, POOL_HALO, POOL_WIDTH)
    last = TOKENS // POOL_HALO - 1
    prev_spec = pl.BlockSpec((1, POOL_HALO, POOL_WIDTH),
                             lambda i: (jnp.maximum(i * halo_blocks - 1, 0), 0, 0))
    next_spec = pl.BlockSpec((1, POOL_HALO, POOL_WIDTH),
                             lambda i: (jnp.minimum((i + 1) * halo_blocks, last), 0, 0))
    attn_spec = pl.BlockSpec((1, ATTN_SLABS, ROW_TILE, LANES),
                             lambda i: (i // TILES_PER_SEQ, 0, i % TILES_PER_SEQ, 0))
    weights = (pool_w, pool_scale, w_proj_attn, w_proj_pool, w_out)
    return pl.pallas_call(
        _mixout_kernel,
        grid=(N_TILES,),
        in_specs=[_row_spec(D_MODEL)] + [attn_spec] * N_GROUPS
                 + [_row_spec(POOL_WIDTH), prev_spec, next_spec, _row_spec(GATE_WIDTH)]
                 + [_full_spec(w) for w in weights],
        out_specs=_row_spec(D_MODEL),
        out_shape=jax.ShapeDtypeStruct((TOKENS, D_MODEL), F32),
        compiler_params=_cparams("parallel"),
        name="mixout",
    )(h, *attn_outs, u, u3, u3, gates, *weights)


def _router_kernel(h_ref, g_ref, wr_ref, xn_ref, ri_ref, rw_ref, cnt_ref):
    xn = _rms(h_ref[...], g_ref[...])
    hi = xn.astype(BF16)
    xn_ref[...] = hi
    lo = (xn - hi.astype(F32)).astype(BF16)
    both = _dot(hi, wr_ref[...])
    lg = both[:, 0:ROUTER_LANES] + (both[:, ROUTER_LANES:] + _dot(lo, wr_ref[:, 0:ROUTER_LANES]))

    lane = lax.broadcasted_iota(I32, (ROW_TILE, ROUTER_LANES), 1)
    lanef = lane.astype(F32)
    low = jnp.float32(-3.0e38)
    far = jnp.float32(ROUTER_LANES)
    first = lambda hit: jnp.min(jnp.where(hit, lanef, far), axis=-1, keepdims=True)

    is_group = lane < N_EXPERT_GROUPS
    gl = jnp.where(is_group, lg, low)
    gmax = jnp.max(gl, axis=-1, keepdims=True)
    gidx = first(gl == gmax).astype(I32)
    gden = jnp.sum(jnp.where(is_group, jnp.exp(gl - gmax), 0.0), axis=-1, keepdims=True)
    g_p = 1.0 / gden

    in_group = ((lane >= EXPERT_LANE0) & (lane < EXPERT_LANE0 + N_EXPERTS)
                & ((lane - EXPERT_LANE0) // EXPERTS_PER_GROUP == gidx))
    el = jnp.where(in_group, lg, low)
    t1 = jnp.max(el, axis=-1, keepdims=True)
    l1 = first(in_group & (el == t1))
    rest = in_group & (lanef != l1)
    el2 = jnp.where(rest, lg, low)
    t2 = jnp.max(el2, axis=-1, keepdims=True)
    l2 = first(rest & (el2 == t2))
    e2 = jnp.exp(t2 - t1)
    w1 = g_p * (1.0 / (1.0 + e2))
    w2 = g_p * (e2 / (1.0 + e2))

    hit1 = lanef == l1
    hit2 = lanef == l2
    onehot = (hit1 | hit2).astype(BF16)
    ri = lax.broadcasted_iota(I32, (ROW_TILE, ROW_TILE), 0)
    ci = lax.broadcasted_iota(I32, (ROW_TILE, ROW_TILE), 1)
    before = (ci < ri).astype(BF16)
    seen = _dot(before, onehot)
    r1 = jnp.sum(jnp.where(hit1, seen, 0.0), axis=-1, keepdims=True)
    r2 = jnp.sum(jnp.where(hit2, seen, 0.0), axis=-1, keepdims=True)

    packed = jnp.zeros((ROW_TILE, ROUTER_LANES), F32)
    for k, v in enumerate((l1, l2, r1, r2)):
        packed = jnp.where(lane == k, v, packed)
    ri_ref[...] = packed.astype(I32)
    rw_ref[...] = jnp.where(lane == 0, w1, jnp.where(lane == 1, w2, 0.0))
    counts = jnp.sum(onehot.astype(F32), axis=0, keepdims=True)
    cnt_ref[0] = jnp.broadcast_to(counts, (SUBLANES, ROUTER_LANES))


def _route(h, g, wr_split):
    return pl.pallas_call(
        _router_kernel,
        grid=(N_TILES,),
        in_specs=[_row_spec(D_MODEL), _full_spec(g), _full_spec(wr_split)],
        out_specs=[_row_spec(D_MODEL), _row_spec(ROUTER_LANES), _row_spec(ROUTER_LANES),
                   pl.BlockSpec((1, SUBLANES, ROUTER_LANES), lambda i: (i, 0, 0))],
        out_shape=[jax.ShapeDtypeStruct((TOKENS, D_MODEL), BF16),
                   jax.ShapeDtypeStruct((TOKENS, ROUTER_LANES), I32),
                   jax.ShapeDtypeStruct((TOKENS, ROUTER_LANES), F32),
                   jax.ShapeDtypeStruct((N_TILES, SUBLANES, ROUTER_LANES), F32)],
        compiler_params=_cparams("parallel"),
        name="router",
    )(h, g, wr_split)


def _slots(ri_ref, off_ref):
    ri = ri_ref[...]
    lane = lax.broadcasted_iota(I32, (ROW_TILE, ROUTER_LANES), 1)
    off = off_ref[0, 0:1, :]
    pick = lambda k: jnp.sum(jnp.where(lane == ri[:, k:k + 1], off, 0.0), axis=-1, keepdims=True)
    return (pick(0) + ri[:, 2:3].astype(F32), pick(1) + ri[:, 3:4].astype(F32))


def _as_rows(cols):
    eye = (lax.broadcasted_iota(I32, (ROW_TILE, ROW_TILE), 0)
           == lax.broadcasted_iota(I32, (ROW_TILE, ROW_TILE), 1))
    return [jnp.sum(jnp.where(eye, c, 0.0), axis=0, keepdims=True) for c in cols]


def _run_copy(tile, e, loff_s, c8_s, dst_s, buf_ref, slot, hbm_ref, sem, to_hbm):
    k = tile * N_EXPERTS + e
    n = pl.multiple_of(c8_s[k], RUN_ALIGN)
    vm = buf_ref.at[slot, pl.ds(pl.multiple_of(loff_s[k], RUN_ALIGN), n)]
    hb = hbm_ref.at[pl.ds(pl.multiple_of(dst_s[k], RUN_ALIGN), n)]
    src, dst = (vm, hb) if to_hbm else (hb, vm)
    return n, pltpu.make_async_copy(src, dst, sem.at[slot])


def _loop(lo, hi, fn):
    def body(e, carry):
        fn(e)
        return carry
    lax.fori_loop(lo, hi, body, 0)


def _start(n, cp):
    @pl.when(n > 0)
    def _():
        cp.start()


def _wait(n, cp):
    @pl.when(n > 0)
    def _():
        cp.wait()


def _dispatch_kernel(loff_s, c8_s, dst_s, nslot_s, zdst_s, zcnt_s, nact_s,
                     xn_ref, ri_ref, rw_ref, off_ref, xpad_hbm, sorted_buf, zero_buf, sem, zsem):
    i = pl.program_id(0)
    slot = i % 2
    last = pl.num_programs(0) - 1

    def zero_copy(e):
        n = pl.multiple_of(zcnt_s[e], RUN_ALIGN)
        dst = xpad_hbm.at[pl.ds(pl.multiple_of(zdst_s[e], RUN_ALIGN), n)]
        return n, pltpu.make_async_copy(zero_buf.at[pl.ds(0, n)], dst, zsem)

    def tail_copy(b):
        dst = xpad_hbm.at[pl.ds(pl.multiple_of(b * MOE_BLOCK, MOE_BLOCK), MOE_BLOCK)]
        return pltpu.make_async_copy(zero_buf, dst, zsem)

    @pl.when(i == 0)
    def _():
        zero_buf[...] = jnp.zeros_like(zero_buf)
        _loop(0, N_EXPERTS, lambda e: _start(*zero_copy(e)))
        _loop(nact_s[0], N_MOE_BLOCKS, lambda b: tail_copy(b).start())

    s1, s2 = _slots(ri_ref, off_ref)
    rw = rw_ref[...]
    s1_row, s2_row, w1_row, w2_row = _as_rows([s1, s2, rw[:, 0:1], rw[:, 1:2]])
    xn = xn_ref[...]
    lane = lax.broadcasted_iota(I32, (SLOT_CHUNK, LANES), 1)
    for c in range(MAX_SLOTS // SLOT_CHUNK):
        @pl.when(c * SLOT_CHUNK < nslot_s[i])
        def _():
            rows = slice(c * SLOT_CHUNK, (c + 1) * SLOT_CHUNK)
            srow = (c * SLOT_CHUNK
                    + lax.broadcasted_iota(I32, (SLOT_CHUNK, ROW_TILE), 0)).astype(F32)
            hit1 = srow == s1_row
            hit2 = srow == s2_row
            xs = _dot((hit1 | hit2).astype(BF16), xn)
            sorted_buf[slot, rows, 0:HALF] = _pack_halves(xs[:, 0:HALF], xs[:, HALF:D_MODEL])
            ws = jnp.sum(jnp.where(hit1, w1_row, 0.0) + jnp.where(hit2, w2_row, 0.0),
                         axis=-1, keepdims=True)
            sorted_buf[slot, rows, HALF:X_WORDS] = jnp.where(lane == 0, _bits(ws), jnp.uint32(0))

    copy = lambda tile, sl: (lambda e: _run_copy(tile, e, loff_s, c8_s, dst_s, sorted_buf, sl,
                                                 xpad_hbm, sem, True))
    mine = copy(i, slot)
    _loop(0, N_EXPERTS, lambda e: _start(*mine(e)))

    @pl.when(i > 0)
    def _():
        prev = copy(i - 1, 1 - slot)
        _loop(0, N_EXPERTS, lambda e: _wait(*prev(e)))

    @pl.when(i == last)
    def _():
        _loop(0, N_EXPERTS, lambda e: _wait(*mine(e)))
        _loop(0, N_EXPERTS, lambda e: _wait(*zero_copy(e)))
        _loop(nact_s[0], N_MOE_BLOCKS, lambda b: tail_copy(b).wait())


def _dispatch(tables, xn, route_i, route_w):
    tile_row = pl.BlockSpec((1, SUBLANES, ROUTER_LANES), lambda i, *_: (i, 0, 0))
    grid_spec = pltpu.PrefetchScalarGridSpec(
        num_scalar_prefetch=7,
        grid=(N_TILES,),
        in_specs=[_row_spec(D_MODEL), _row_spec(ROUTER_LANES), _row_spec(ROUTER_LANES), tile_row],
        out_specs=pl.BlockSpec(memory_space=pl.ANY),
        scratch_shapes=[pltpu.VMEM((2, MAX_SLOTS, X_WORDS), U32),
                        pltpu.VMEM((MOE_BLOCK, X_WORDS), U32),
                        pltpu.SemaphoreType.DMA((2,)),
                        pltpu.SemaphoreType.DMA(())],
    )
    return pl.pallas_call(
        _dispatch_kernel,
        grid_spec=grid_spec,
        out_shape=jax.ShapeDtypeStruct((PAD_ROWS, X_WORDS), U32),
        compiler_params=_cparams("arbitrary"),
        name="dispatch",
    )(tables["loff"], tables["c8"], tables["dst"], tables["nslot"], tables["zdst"], tables["zcnt"],
      tables["n_active"], xn, route_i, route_w, tables["offrow"])


def _combine_kernel(loff_s, c8_s, dst_s, nslot_s,
                    h_ref, ri_ref, off_ref, g_ref, ypad_hbm, out_ref,
                    ybuf, acc, sem, *, final_norm):
    i = pl.program_id(0)
    slot = i % 2
    last = pl.num_programs(0) - 1
    fetch = lambda tile, sl: (lambda e: _run_copy(tile, e, loff_s, c8_s, dst_s, ybuf, sl,
                                                  ypad_hbm, sem, False))

    @pl.when(i == 0)
    def _():
        ybuf[...] = jnp.zeros_like(ybuf)
        first = fetch(i, slot)
        _loop(0, N_EXPERTS, lambda e: _start(*first(e)))

    @pl.when(i < last)
    def _():
        nxt = fetch(i + 1, 1 - slot)
        _loop(0, N_EXPERTS, lambda e: _start(*nxt(e)))

    mine = fetch(i, slot)
    _loop(0, N_EXPERTS, lambda e: _wait(*mine(e)))

    s1, s2 = _slots(ri_ref, off_ref)
    acc[...] = jnp.zeros_like(acc)
    for c in range(MAX_SLOTS // SLOT_CHUNK):
        @pl.when(c * SLOT_CHUNK < nslot_s[i])
        def _():
            scol = (c * SLOT_CHUNK
                    + lax.broadcasted_iota(I32, (ROW_TILE, SLOT_CHUNK), 1)).astype(F32)
            pick = ((scol == s1) | (scol == s2)).astype(BF16)
            y_hi, y_lo = _unpack_halves(ybuf[slot, c * SLOT_CHUNK:(c + 1) * SLOT_CHUNK, :])
            acc[:, 0:HALF] += _dot(pick, y_hi)
            acc[:, HALF:D_MODEL] += _dot(pick, y_lo)

    h = h_ref[...] + acc[...]
    out_ref[...] = _rms(h, g_ref[...]) if final_norm else h


def _combine(tables, h, route_i, y_pad, g, final_norm):
    tile_row = pl.BlockSpec((1, SUBLANES, ROUTER_LANES), lambda i, *_: (i, 0, 0))
    grid_spec = pltpu.PrefetchScalarGridSpec(
        num_scalar_prefetch=4,
        grid=(N_TILES,),
        in_specs=[_row_spec(D_MODEL), _row_spec(ROUTER_LANES), tile_row, _full_spec(g),
                  pl.BlockSpec(memory_space=pl.ANY)],
        out_specs=_row_spec(D_MODEL),
        scratch_shapes=[pltpu.VMEM((2, MAX_SLOTS, HALF), U32),
                        pltpu.VMEM((ROW_TILE, D_MODEL), F32),
                        pltpu.SemaphoreType.DMA((2,))],
    )
    return pl.pallas_call(
        functools.partial(_combine_kernel, final_norm=final_norm),
        grid_spec=grid_spec,
        out_shape=jax.ShapeDtypeStruct((TOKENS, D_MODEL), F32),
        compiler_params=_cparams("arbitrary"),
        name="combine",
    )(tables["loff"], tables["c8"], tables["dst"], tables["nslot"],
      h, route_i, tables["offrow"], g, y_pad)


def _expert_kernel(be_ref, nact_ref, x_ref, wg_ref, wu_ref, wd_ref, y_ref, wg_b, wu_b, wd_b):
    i = pl.program_id(0)

    @pl.when(i >= nact_ref[0])
    def _():
        y_ref[...] = jnp.zeros_like(y_ref)

    @pl.when(i < nact_ref[0])
    def _():
        prev = be_ref[jnp.maximum(i - 1, 0)]

        @pl.when((i == 0) | (be_ref[i] != prev))
        def _():
            wg_b[...] = wg_ref[0, 0].astype(BF16)
            wu_b[...] = wu_ref[0, 0].astype(BF16)
            wd_b[...] = wd_ref[0, 0].astype(BF16)

        x_hi, x_lo = _unpack_halves(x_ref[:, 0:HALF])
        row_w = lax.bitcast_convert_type(x_ref[:, HALF:HALF + 1], F32)
        gate = _dot(x_hi, wg_b[0:HALF, :]) + _dot(x_lo, wg_b[HALF:D_MODEL, :])
        up = _dot(x_hi, wu_b[0:HALF, :]) + _dot(x_lo, wu_b[HALF:D_MODEL, :])
        hmid = (jax.nn.silu(gate) * up).astype(BF16)
        y = (_dot(hmid, wd_b[...]) * row_w).astype(BF16).astype(F32)
        y_ref[...] = _pack_halves(y[:, 0:HALF], y[:, HALF:D_MODEL])


def _experts(layer, blk_expert, n_active, x_pad, w_gate, w_up, w_down):
    live = lambda i, na: jnp.minimum(i, na[0] - 1)
    wsel = lambda i, be, na: (layer, be[live(i, na)], 0, 0)
    grid_spec = pltpu.PrefetchScalarGridSpec(
        num_scalar_prefetch=2,
        grid=(N_MOE_BLOCKS,),
        in_specs=[pl.BlockSpec((MOE_BLOCK, X_WORDS), lambda i, be, na: (live(i, na), 0)),
                  pl.BlockSpec((1, 1, D_MODEL, D_EXPERT), wsel),
                  pl.BlockSpec((1, 1, D_MODEL, D_EXPERT), wsel),
                  pl.BlockSpec((1, 1, D_EXPERT, D_MODEL), wsel)],
        out_specs=pl.BlockSpec((MOE_BLOCK, HALF), lambda i, be, na: (i, 0)),
        scratch_shapes=[pltpu.VMEM((D_MODEL, D_EXPERT), BF16),
                        pltpu.VMEM((D_MODEL, D_EXPERT), BF16),
                        pltpu.VMEM((D_EXPERT, D_MODEL), BF16)],
    )
    return pl.pallas_call(
        _expert_kernel,
        grid_spec=grid_spec,
        out_shape=jax.ShapeDtypeStruct((PAD_ROWS, HALF), U32),
        compiler_params=_cparams("arbitrary"),
        name="experts",
    )(blk_expert, n_active, x_pad, w_gate, w_up, w_down)


def _routing_tables(tile_counts):
    cnt = tile_counts[:, 0, EXPERT_LANE0:EXPERT_LANE0 + N_EXPERTS].astype(I32)
    c8 = (cnt + RUN_ALIGN - 1) // RUN_ALIGN * RUN_ALIGN
    loff = jnp.cumsum(c8, axis=1) - c8
    nslot = jnp.sum(c8, axis=1)
    tot = jnp.sum(c8, axis=0)
    padded = (tot + MOE_BLOCK - 1) // MOE_BLOCK * MOE_BLOCK
    end = jnp.cumsum(padded)
    base = end - padded
    dst = base[None, :] + jnp.cumsum(c8, axis=0) - c8
    blk_start = jnp.arange(N_MOE_BLOCKS, dtype=I32) * MOE_BLOCK
    blk_expert = jnp.minimum(jnp.sum((end[None, :] <= blk_start[:, None]).astype(I32), axis=1),
                             N_EXPERTS - 1).astype(I32)
    offrow = jnp.zeros((N_TILES, ROUTER_LANES), F32).at[:, EXPERT_LANE0:EXPERT_LANE0 + N_EXPERTS].set(
        loff.astype(F32))
    offrow = jnp.broadcast_to(offrow[:, None, :], (N_TILES, SUBLANES, ROUTER_LANES))
    return {
        "loff": loff.reshape(-1).astype(I32), "c8": c8.reshape(-1).astype(I32),
        "dst": dst.reshape(-1).astype(I32), "nslot": nslot.astype(I32),
        "zdst": (base + tot).astype(I32), "zcnt": (padded - tot).astype(I32),
        "offrow": offrow, "blk_expert": blk_expert,
        "n_active": (end[-1:] // MOE_BLOCK).astype(I32),
    }


def _split_router_weights(w_router_group, w_router_expert):
    w_e = w_router_expert.transpose(1, 0, 2).reshape(D_MODEL, N_EXPERTS)
    w = jnp.concatenate([w_router_group, w_e], axis=1)
    w = jnp.pad(w, ((0, 0), (0, ROUTER_LANES - w.shape[1])))
    hi = w.astype(BF16)
    lo = (w - hi.astype(F32)).astype(BF16)
    return jnp.concatenate([hi, lo], axis=1)


def _input_weights(w):
    cols = []
    for g in range(N_GROUPS):
        for part in range(3):
            c0 = part * ATTN_WIDTH + g * GROUP_WIDTH
            blk = w[:, c0:c0 + GROUP_WIDTH]
            cols.append(blk * HEAD_DIM ** -0.5 if part == 0 else blk)
    cols.append(w[:, QKV_WIDTH:])
    return jnp.concatenate(cols, axis=1).astype(BF16)


def kernel(x, rel_bias, norm_mix_g, w_in, pool_w, pool_scale, w_proj_attn, w_proj_pool, w_out,
           norm_ffn_g, w_router_group, w_router_expert, w_gate_e, w_up_e, w_down_e, norm_final_g):
    h = x.reshape(TOKENS, D_MODEL)
    for l in range(DEPTH):
        qkv, u, gates = _project(h, norm_mix_g[l][None], _input_weights(w_in[l]))
        attn_outs = [_attention_group(qkv[g], rel_bias, g) for g in range(N_GROUPS)]
        h = _mix_out(h, attn_outs, u, gates, pool_w[l].astype(BF16), pool_scale[l][None],
                     w_proj_attn[l].astype(BF16), w_proj_pool[l].astype(BF16), w_out[l].astype(BF16))
        xn, route_i, route_w, tile_counts = _route(
            h, norm_ffn_g[l][None], _split_router_weights(w_router_group[l], w_router_expert[l]))
        tables = _routing_tables(tile_counts)
        x_pad = _dispatch(tables, xn, route_i, route_w)
        y_pad = _experts(l, tables["blk_expert"], tables["n_active"], x_pad, w_gate_e, w_up_e, w_down_e)
        h = _combine(tables, h, route_i, y_pad, norm_final_g[None], l == DEPTH - 1)
    return h.reshape(BATCH, SEQ, D_MODEL)
```

```python
import functools
import math

import jax
import jax.numpy as jnp
from jax import lax
from jax.experimental import pallas as pl
from jax.experimental.pallas import tpu as pltpu

F32 = jnp.float32
BF16 = jnp.bfloat16
I32 = jnp.int32
U32 = jnp.uint32

D_MODEL = 1024
BATCH = 8
SEQ = 2048
TOKENS = BATCH * SEQ
DEPTH = 2

HEAD_DIM = 64
HEADS_PER_GROUP = 4
GROUP_WIDTH = HEADS_PER_GROUP * HEAD_DIM
DILATION_PATTERNS = ((128, 1), (512, 4), (2048, 16))
N_GROUPS = len(DILATION_PATTERNS)
N_ATTN_HEADS = N_GROUPS * HEADS_PER_GROUP
ATTN_WIDTH = N_ATTN_HEADS * HEAD_DIM
QKV_WIDTH = 3 * ATTN_WIDTH
GROUP_QKV = 3 * GROUP_WIDTH
N_SIDE = 64
assert all(w // (2 * d) == N_SIDE for w, d in DILATION_PATTERNS)
POOL_WINDOWS = (2, 4, 8, 16)
POOL_GROUP_WIDTH = 128
POOL_WIDTH = len(POOL_WINDOWS) * POOL_GROUP_WIDTH
POOL_HALO = max(POOL_WINDOWS) // 2
N_BRANCHES = 2
GATE_WIDTH = N_BRANCHES * D_MODEL
IN_WIDTH = QKV_WIDTH + POOL_WIDTH + GATE_WIDTH
N_BUCKETS = 32
MAX_DISTANCE = 1024
N_EXPERT_GROUPS = 8
EXPERTS_PER_GROUP = 8
N_EXPERTS = N_EXPERT_GROUPS * EXPERTS_PER_GROUP
TOP_K = 2
D_EXPERT = 512
N_ASSIGN = TOKENS * TOP_K
EPS = 1e-6
NEG_INF = -1e30

LANES = 128
SUBLANES = 8
ROW_TILE = 512
N_TILES = TOKENS // ROW_TILE
TILES_PER_SEQ = SEQ // ROW_TILE
Q_BLOCK = 128
K_BLOCK = Q_BLOCK + 2 * N_SIDE
ATTN_SLABS = 2 * GROUP_WIDTH // LANES
ROUTER_LANES = 128
EXPERT_LANE0 = N_EXPERT_GROUPS
VMEM_LIMIT = 56 * 1024 * 1024

RUN_ALIGN = SUBLANES
MOE_BLOCK = 256
HALF = D_MODEL // 2
X_WORDS = HALF + LANES
SLOT_CHUNK = 512
RUN_UNROLL = 4
MAX_SLOTS = -(-(TOP_K * ROW_TILE + N_EXPERTS * (RUN_ALIGN - 1)) // SLOT_CHUNK) * SLOT_CHUNK
PAD_ROWS = (N_ASSIGN + N_TILES * N_EXPERTS * (RUN_ALIGN - 1)
            + N_EXPERTS * (MOE_BLOCK - RUN_ALIGN))
assert PAD_ROWS % MOE_BLOCK == 0
N_MOE_BLOCKS = PAD_ROWS // MOE_BLOCK
HIGH_HALF = 0xFFFF0000


def _cparams(*sem):
    return pltpu.CompilerParams(dimension_semantics=sem, vmem_limit_bytes=VMEM_LIMIT)


def _rms(h, g):
    r = lax.rsqrt(jnp.mean(h * h, axis=-1, keepdims=True) + EPS)
    return (h * r) * g


def _dot(a, b):
    return jnp.dot(a, b, preferred_element_type=F32)


def _row_spec(width):
    return pl.BlockSpec((ROW_TILE, width), lambda i, *_: (i, 0))


def _full_spec(a):
    return pl.BlockSpec(a.shape, lambda i, *_: (0,) * a.ndim)


def _bits(x):
    return lax.bitcast_convert_type(x, U32)


def _pack_halves(a, b):
    return (_bits(a) & jnp.uint32(HIGH_HALF)) | (_bits(b) >> 16)


def _unpack_halves(words):
    hi = lax.bitcast_convert_type(words & jnp.uint32(HIGH_HALF), F32)
    lo = lax.bitcast_convert_type(words << 16, F32)
    return hi.astype(BF16), lo.astype(BF16)


def _proj_kernel(h_ref, g_ref, w_ref, q0_ref, q1_ref, q2_ref, u_ref, gate_ref, slabs):
    xn = _rms(h_ref[...], g_ref[...]).astype(BF16)
    n_slabs = GROUP_QKV // LANES
    for g, out_ref in enumerate((q0_ref, q1_ref, q2_ref)):
        dilation = DILATION_PATTERNS[g][1]
        res = _dot(xn, w_ref[:, g * GROUP_QKV:(g + 1) * GROUP_QKV])
        if dilation == 1:
            out_ref[0, 0] = res.astype(BF16)
            continue
        for s in range(n_slabs):
            slabs[s] = res[:, s * LANES:(s + 1) * LANES]
        n = ROW_TILE // dilation
        for r in range(dilation):
            rows = [slabs[s, pl.ds(r, n, stride=dilation), :] for s in range(n_slabs)]
            out_ref[0, r] = jnp.concatenate(rows, axis=1).astype(BF16)
    u_ref[...] = _dot(xn, w_ref[:, QKV_WIDTH:QKV_WIDTH + POOL_WIDTH])
    gates = _dot(xn, w_ref[:, QKV_WIDTH + POOL_WIDTH:IN_WIDTH])
    gate_ref[...] = jax.nn.sigmoid(gates).astype(BF16)


def _project(h, g, w_bf16):
    qkv_shapes, qkv_specs = [], []
    for _, d in DILATION_PATTERNS:
        qkv_shapes.append(jax.ShapeDtypeStruct((BATCH, d, SEQ // d, GROUP_QKV), BF16))
        qkv_specs.append(pl.BlockSpec((1, d, ROW_TILE // d, GROUP_QKV),
                                      lambda i: (i // TILES_PER_SEQ, 0, i % TILES_PER_SEQ, 0)))
    res = pl.pallas_call(
        _proj_kernel,
        grid=(N_TILES,),
        in_specs=[_row_spec(D_MODEL), _full_spec(g), _full_spec(w_bf16)],
        out_specs=qkv_specs + [_row_spec(POOL_WIDTH), _row_spec(GATE_WIDTH)],
        out_shape=qkv_shapes + [jax.ShapeDtypeStruct((TOKENS, POOL_WIDTH), F32),
                                jax.ShapeDtypeStruct((TOKENS, GATE_WIDTH), BF16)],
        scratch_shapes=[pltpu.VMEM((GROUP_QKV // LANES, ROW_TILE, LANES), F32)],
        compiler_params=_cparams("parallel"),
        name="proj",
    )(h, g, w_bf16)
    return res[:N_GROUPS], res[N_GROUPS], res[N_GROUPS + 1]


def _attn_kernel(qkv_ref, bias_ref, o_ref, kpad, vpad, *, dilation):
    sub_len = SEQ // dilation
    zpad = jnp.zeros((N_SIDE, GROUP_WIDTH), BF16)
    for pad_ref in (kpad, vpad):
        pad_ref[0:N_SIDE, :] = zpad
        pad_ref[N_SIDE + sub_len:2 * N_SIDE + sub_len, :] = zpad

    head_of_lane = lax.broadcasted_iota(I32, (1, GROUP_WIDTH), 1) // HEAD_DIM
    qi = lax.broadcasted_iota(I32, (Q_BLOCK, K_BLOCK), 0)
    ki = lax.broadcasted_iota(I32, (Q_BLOCK, K_BLOCK), 1)
    band = jnp.abs(ki - N_SIDE - qi) <= N_SIDE

    def by_head(cols):
        out = cols[HEADS_PER_GROUP - 1]
        for h in range(HEADS_PER_GROUP - 2, -1, -1):
            out = jnp.where(head_of_lane == h, cols[h], out)
        return out

    def residue(r, carry):
        kpad[N_SIDE:N_SIDE + sub_len, :] = qkv_ref[0, r, :, GROUP_WIDTH:2 * GROUP_WIDTH]
        vpad[N_SIDE:N_SIDE + sub_len, :] = qkv_ref[0, r, :, 2 * GROUP_WIDTH:3 * GROUP_WIDTH]

        def block(i, carry2):
            r0 = pl.multiple_of(i * Q_BLOCK, Q_BLOCK)
            qb = qkv_ref[0, r, pl.ds(r0, Q_BLOCK), 0:GROUP_WIDTH]
            kw = kpad[pl.ds(r0, K_BLOCK), :]
            vw = vpad[pl.ds(r0, K_BLOCK), :]
            zero = jnp.zeros_like(qb)
            q_heads = jnp.concatenate(
                [jnp.where(head_of_lane == h, qb, zero) for h in range(HEADS_PER_GROUP)], axis=0)
            s = lax.dot_general(q_heads, kw, (((1,), (1,)), ((), ())), preferred_element_type=F32)
            s = s.reshape(HEADS_PER_GROUP, Q_BLOCK, K_BLOCK) + bias_ref[...]
            kpos = r0 - N_SIDE + ki
            valid = band & (kpos >= 0) & (kpos < sub_len)
            s = jnp.where(valid[None], s, NEG_INF)
            m = jnp.max(s, axis=-1, keepdims=True)
            p = jnp.exp(s - m)
            den = jnp.sum(p, axis=-1, keepdims=True)
            pb = p.astype(BF16)
            p_cat = jnp.concatenate([pb[h] for h in range(HEADS_PER_GROUP)], axis=1)
            zv = jnp.zeros_like(vw)
            v_heads = jnp.concatenate(
                [jnp.where(head_of_lane == h, vw, zv) for h in range(HEADS_PER_GROUP)], axis=0)
            o = _dot(p_cat, v_heads)
            lse = m + jnp.log(den)
            o = o / by_head([den[h] for h in range(HEADS_PER_GROUP)])
            lse_lanes = by_head([lse[h] for h in range(HEADS_PER_GROUP)])
            if dilation == 1:
                rows = pl.ds(r0, Q_BLOCK)
            else:
                rows = pl.ds(r + dilation * r0, Q_BLOCK, stride=dilation)
            for half in range(GROUP_WIDTH // LANES):
                lanes = slice(half * LANES, (half + 1) * LANES)
                o_ref[0, half, rows, :] = o[:, lanes]
                o_ref[0, GROUP_WIDTH // LANES + half, rows, :] = lse_lanes[:, lanes]
            return carry2

        lax.fori_loop(0, sub_len // Q_BLOCK, block, 0)
        return carry

    lax.fori_loop(0, dilation, residue, 0)


def _t5_bucket(rel):
    nb = N_BUCKETS // 2
    ret = jnp.where(rel > 0, nb, 0)
    n = jnp.abs(rel)
    max_exact = nb // 2
    nf = jnp.maximum(n, max_exact).astype(F32)
    large = max_exact + (jnp.log(nf / max_exact) / math.log(MAX_DISTANCE / max_exact)
                         * (nb - max_exact)).astype(I32)
    large = jnp.minimum(large, nb - 1)
    return ret + jnp.where(n < max_exact, n, large)


def _band_bias(rel_bias, group, dilation):
    qi = jnp.arange(Q_BLOCK)[:, None]
    ki = jnp.arange(K_BLOCK)[None, :]
    bucket = _t5_bucket((ki - N_SIDE - qi) * dilation)
    tab = rel_bias[:, group * HEADS_PER_GROUP:(group + 1) * HEADS_PER_GROUP]
    onehot = (bucket[:, :, None] == jnp.arange(N_BUCKETS)[None, None, :]).astype(F32)
    bias = jnp.einsum('qkb,bh->hqk', onehot, tab, precision=lax.Precision.HIGHEST)
    return bias.astype(F32)


def _attention_group(qkv_g, rel_bias, group):
    _, dilation = DILATION_PATTERNS[group]
    sub_len = SEQ // dilation
    bias = _band_bias(rel_bias, group, dilation)
    return pl.pallas_call(
        functools.partial(_attn_kernel, dilation=dilation),
        grid=(BATCH,),
        in_specs=[pl.BlockSpec((1, dilation, sub_len, GROUP_QKV), lambda b: (b, 0, 0, 0)),
                  pl.BlockSpec(bias.shape, lambda b: (0, 0, 0))],
        out_specs=pl.BlockSpec((1, ATTN_SLABS, SEQ, LANES), lambda b: (b, 0, 0, 0)),
        out_shape=jax.ShapeDtypeStruct((BATCH, ATTN_SLABS, SEQ, LANES), F32),
        scratch_shapes=[pltpu.VMEM((sub_len + 2 * N_SIDE, GROUP_WIDTH), BF16),
                        pltpu.VMEM((sub_len + 2 * N_SIDE, GROUP_WIDTH), BF16)],
        compiler_params=_cparams("parallel"),
        name=f"attn{group}",
    )(qkv_g, bias)


def _mixout_kernel(h_ref, a0_ref, a1_ref, a2_ref, u_ref, uprev_ref, unext_ref, gate_ref,
                   pw_ref, ps_ref, wpa_ref, wpp_ref, wo_ref, out_ref):
    j = pl.program_id(0) % TILES_PER_SEQ

    a_refs = (a0_ref, a1_ref, a2_ref)
    halves = GROUP_WIDTH // LANES
    attn_halves = []
    for half in range(halves):
        lse = [r[0, halves + half] for r in a_refs]
        m = jnp.maximum(jnp.maximum(lse[0], lse[1]), lse[2])
        e = [jnp.exp(l - m) for l in lse]
        num = e[0] * a0_ref[0, half] + e[1] * a1_ref[0, half] + e[2] * a2_ref[0, half]
        attn_halves.append(num / (e[0] + e[1] + e[2]))
    attn = jnp.concatenate(attn_halves, axis=1)
    y_attn = _dot(attn.astype(BF16), wpa_ref[...])

    u = u_ref[...]
    prev = jnp.where(j == 0, 0.0, uprev_ref[0])
    nxt = jnp.where(j == TILES_PER_SEQ - 1, 0.0, unext_ref[0])
    ext = jnp.concatenate([prev, u, nxt], axis=0)
    pos = j * ROW_TILE + lax.broadcasted_iota(I32, (ROW_TILE, 1), 0)
    mixed = []
    for gi, w in enumerate(POOL_WINDOWS):
        half = w // 2
        sl = slice(gi * POOL_GROUP_WIDTH, (gi + 1) * POOL_GROUP_WIDTH)
        run = ext[:, sl]
        span = 1
        while 2 * span < w:
            run = run[:-span] + run[span:]
            span *= 2
        lo = POOL_HALO - half
        acc = run[lo:lo + ROW_TILE] + run[lo + half:lo + half + ROW_TILE]
        cnt = (jnp.minimum(pos + half, SEQ) - jnp.maximum(pos - half, 0)).astype(F32)
        pooled = acc / cnt - u[:, sl]
        mixed.append(_dot(pooled.astype(BF16), pw_ref[gi]) * ps_ref[:, sl])
    y_pool = _dot(jnp.concatenate(mixed, axis=1).astype(BF16), wpp_ref[...])

    y = (gate_ref[:, 0:D_MODEL] * y_attn.astype(BF16)
         + gate_ref[:, D_MODEL:GATE_WIDTH] * y_pool.astype(BF16))
    out_ref[...] = h_ref[...] + _dot(y, wo_ref[...])


def _mix_out(h, attn_outs, u, gates, pool_w, pool_scale, w_proj_attn, w_proj_pool, w_out):
    halo_blocks = ROW_TILE // POOL_HALO
    u3 = u.reshape(TOKENS // POOL_HALO, POOL_HALO, POOL_WIDTH)
    last = TOKENS // POOL_HALO - 1
    prev_spec = pl.BlockSpec((1, POOL_HALO, POOL_WIDTH),
                             lambda i: (jnp.maximum(i * halo_blocks - 1, 0), 0, 0))
    next_spec = pl.BlockSpec((1, POOL_HALO, POOL_WIDTH),
                             lambda i: (jnp.minimum((i + 1) * halo_blocks, last), 0, 0))
    attn_spec = pl.BlockSpec((1, ATTN_SLABS, ROW_TILE, LANES),
                             lambda i: (i // TILES_PER_SEQ, 0, i % TILES_PER_SEQ, 0))
    weights = (pool_w, pool_scale, w_proj_attn, w_proj_pool, w_out)
    return pl.pallas_call(
        _mixout_kernel,
        grid=(N_TILES,),
        in_specs=[_row_spec(D_MODEL)] + [attn_spec] * N_GROUPS
                 + [_row_spec(POOL_WIDTH), prev_spec, next_spec, _row_spec(GATE_WIDTH)]
                 + [_full_spec(w) for w in weights],
        out_specs=_row_spec(D_MODEL),
        out_shape=jax.ShapeDtypeStruct((TOKENS, D_MODEL), F32),
        compiler_params=_cparams("parallel"),
        name="mixout",
    )(h, *attn_outs, u, u3, u3, gates, *weights)


def _router_kernel(h_ref, g_ref, wr_ref, xn_ref, ri_ref, rw_ref, cnt_ref):
    xn = _rms(h_ref[...], g_ref[...])
    hi = xn.astype(BF16)
    xn_ref[...] = hi
    lo = (xn - hi.astype(F32)).astype(BF16)
    both = _dot(hi, wr_ref[...])
    lg = both[:, 0:ROUTER_LANES] + (both[:, ROUTER_LANES:] + _dot(lo, wr_ref[:, 0:ROUTER_LANES]))

    lane = lax.broadcasted_iota(I32, (ROW_TILE, ROUTER_LANES), 1)
    lanef = lane.astype(F32)
    low = jnp.float32(-3.0e38)
    far = jnp.float32(ROUTER_LANES)
    first = lambda hit: jnp.min(jnp.where(hit, lanef, far), axis=-1, keepdims=True)

    is_group = lane < N_EXPERT_GROUPS
    gl = jnp.where(is_group, lg, low)
    gmax = jnp.max(gl, axis=-1, keepdims=True)
    gidx = first(gl == gmax).astype(I32)
    gden = jnp.sum(jnp.where(is_group, jnp.exp(gl - gmax), 0.0), axis=-1, keepdims=True)
    g_p = 1.0 / gden

    in_group = ((lane >= EXPERT_LANE0) & (lane < EXPERT_LANE0 + N_EXPERTS)
                & ((lane - EXPERT_LANE0) // EXPERTS_PER_GROUP == gidx))
    el = jnp.where(in_group, lg, low)
    t1 = jnp.max(el, axis=-1, keepdims=True)
    l1 = first(in_group & (el == t1))
    rest = in_group & (lanef != l1)
    el2 = jnp.where(rest, lg, low)
    t2 = jnp.max(el2, axis=-1, keepdims=True)
    l2 = first(rest & (el2 == t2))
    e2 = jnp.exp(t2 - t1)
    w1 = g_p * (1.0 / (1.0 + e2))
    w2 = g_p * (e2 / (1.0 + e2))

    hit1 = lanef == l1
    hit2 = lanef == l2
    onehot = (hit1 | hit2).astype(BF16)
    ri = lax.broadcasted_iota(I32, (ROW_TILE, ROW_TILE), 0)
    ci = lax.broadcasted_iota(I32, (ROW_TILE, ROW_TILE), 1)
    before = (ci < ri).astype(BF16)
    seen = _dot(before, onehot)
    r1 = jnp.sum(jnp.where(hit1, seen, 0.0), axis=-1, keepdims=True)
    r2 = jnp.sum(jnp.where(hit2, seen, 0.0), axis=-1, keepdims=True)

    packed = jnp.zeros((ROW_TILE, ROUTER_LANES), F32)
    for k, v in enumerate((l1, l2, r1, r2)):
        packed = jnp.where(lane == k, v, packed)
    ri_ref[...] = packed.astype(I32)
    rw_ref[...] = jnp.where(lane == 0, w1, jnp.where(lane == 1, w2, 0.0))
    counts = jnp.sum(onehot.astype(F32), axis=0, keepdims=True)
    cnt_ref[0] = jnp.broadcast_to(counts, (SUBLANES, ROUTER_LANES))


def _route(h, g, wr_split):
    return pl.pallas_call(
        _router_kernel,
        grid=(N_TILES,),
        in_specs=[_row_spec(D_MODEL), _full_spec(g), _full_spec(wr_split)],
        out_specs=[_row_spec(D_MODEL), _row_spec(ROUTER_LANES), _row_spec(ROUTER_LANES),
                   pl.BlockSpec((1, SUBLANES, ROUTER_LANES), lambda i: (i, 0, 0))],
        out_shape=[jax.ShapeDtypeStruct((TOKENS, D_MODEL), BF16),
                   jax.ShapeDtypeStruct((TOKENS, ROUTER_LANES), I32),
                   jax.ShapeDtypeStruct((TOKENS, ROUTER_LANES), F32),
                   jax.ShapeDtypeStruct((N_TILES, SUBLANES, ROUTER_LANES), F32)],
        compiler_params=_cparams("parallel"),
        name="router",
    )(h, g, wr_split)


def _slots(ri_ref, off_ref):
    ri = ri_ref[...]
    lane = lax.broadcasted_iota(I32, (ROW_TILE, ROUTER_LANES), 1)
    off = off_ref[0, 0:1, :]
    pick = lambda k: jnp.sum(jnp.where(lane == ri[:, k:k + 1], off, 0.0), axis=-1, keepdims=True)
    return (pick(0) + ri[:, 2:3].astype(F32), pick(1) + ri[:, 3:4].astype(F32))


def _as_rows(cols):
    eye = (lax.broadcasted_iota(I32, (ROW_TILE, ROW_TILE), 0)
           == lax.broadcasted_iota(I32, (ROW_TILE, ROW_TILE), 1))
    return [jnp.sum(jnp.where(eye, c, 0.0), axis=0, keepdims=True) for c in cols]


def _run_copy(tile, e, loff_s, c8_s, dst_s, buf_ref, slot, hbm_ref, sem, to_hbm):
    k = tile * N_EXPERTS + e
    n = pl.multiple_of(c8_s[k], RUN_ALIGN)
    vm = buf_ref.at[slot, pl.ds(pl.multiple_of(loff_s[k], RUN_ALIGN), n)]
    hb = hbm_ref.at[pl.ds(pl.multiple_of(dst_s[k], RUN_ALIGN), n)]
    src, dst = (vm, hb) if to_hbm else (hb, vm)
    return n, pltpu.make_async_copy(src, dst, sem.at[slot])


def _loop(lo, hi, fn, unroll=1):
    def body(e, carry):
        fn(e)
        return carry
    lax.fori_loop(lo, hi, body, 0, unroll=unroll)


def _wait_rows(n, buf_ref, slot, hbm_ref, sem, to_hbm):
    vm = buf_ref.at[slot, pl.ds(0, n)]
    hb = hbm_ref.at[pl.ds(0, n)]
    src, dst = (vm, hb) if to_hbm else (hb, vm)
    pltpu.make_async_copy(src, dst, sem.at[slot]).wait()


def _start(n, cp):
    @pl.when(n > 0)
    def _():
        cp.start()


def _wait(n, cp):
    @pl.when(n > 0)
    def _():
        cp.wait()


def _dispatch_kernel(loff_s, c8_s, dst_s, nslot_s, zdst_s, zcnt_s, nact_s,
                     xn_ref, ri_ref, rw_ref, off_ref, xpad_hbm, sorted_buf, zero_buf, sem, zsem):
    i = pl.program_id(0)
    slot = i % 2
    last = pl.num_programs(0) - 1

    def zero_copy(e):
        n = pl.multiple_of(zcnt_s[e], RUN_ALIGN)
        dst = xpad_hbm.at[pl.ds(pl.multiple_of(zdst_s[e], RUN_ALIGN), n)]
        return n, pltpu.make_async_copy(zero_buf.at[pl.ds(0, n)], dst, zsem)

    def tail_copy(b):
        dst = xpad_hbm.at[pl.ds(pl.multiple_of(b * MOE_BLOCK, MOE_BLOCK), MOE_BLOCK)]
        return pltpu.make_async_copy(zero_buf, dst, zsem)

    @pl.when(i == 0)
    def _():
        zero_buf[...] = jnp.zeros_like(zero_buf)
        _loop(0, N_EXPERTS, lambda e: _start(*zero_copy(e)))
        _loop(nact_s[0], N_MOE_BLOCKS, lambda b: tail_copy(b).start())

    s1, s2 = _slots(ri_ref, off_ref)
    rw = rw_ref[...]
    s1_row, s2_row, w1_row, w2_row = _as_rows([s1, s2, rw[:, 0:1], rw[:, 1:2]])
    xn = xn_ref[...]
    lane = lax.broadcasted_iota(I32, (SLOT_CHUNK, LANES), 1)
    for c in range(MAX_SLOTS // SLOT_CHUNK):
        @pl.when(c * SLOT_CHUNK < nslot_s[i])
        def _():
            rows = slice(c * SLOT_CHUNK, (c + 1) * SLOT_CHUNK)
            srow = (c * SLOT_CHUNK
                    + lax.broadcasted_iota(I32, (SLOT_CHUNK, ROW_TILE), 0)).astype(F32)
            hit1 = srow == s1_row
            hit2 = srow == s2_row
            xs = _dot((hit1 | hit2).astype(BF16), xn)
            sorted_buf[slot, rows, 0:HALF] = _pack_halves(xs[:, 0:HALF], xs[:, HALF:D_MODEL])
            ws = jnp.sum(jnp.where(hit1, w1_row, 0.0) + jnp.where(hit2, w2_row, 0.0),
                         axis=-1, keepdims=True)
            sorted_buf[slot, rows, HALF:X_WORDS] = jnp.where(lane == 0, _bits(ws), jnp.uint32(0))

    copy = lambda tile, sl: (lambda e: _run_copy(tile, e, loff_s, c8_s, dst_s, sorted_buf, sl,
                                                 xpad_hbm, sem, True))
    mine = copy(i, slot)
    _loop(0, N_EXPERTS, lambda e: _start(*mine(e)), unroll=RUN_UNROLL)
    tile_rows = lambda t: pl.multiple_of(nslot_s[t], RUN_ALIGN)

    @pl.when(i > 0)
    def _():
        _wait_rows(tile_rows(i - 1), sorted_buf, 1 - slot, xpad_hbm, sem, True)

    @pl.when(i == last)
    def _():
        _wait_rows(tile_rows(i), sorted_buf, slot, xpad_hbm, sem, True)
        _loop(0, N_EXPERTS, lambda e: _wait(*zero_copy(e)))
        _loop(nact_s[0], N_MOE_BLOCKS, lambda b: tail_copy(b).wait())


def _dispatch(tables, xn, route_i, route_w):
    tile_row = pl.BlockSpec((1, SUBLANES, ROUTER_LANES), lambda i, *_: (i, 0, 0))
    grid_spec = pltpu.PrefetchScalarGridSpec(
        num_scalar_prefetch=7,
        grid=(N_TILES,),
        in_specs=[_row_spec(D_MODEL), _row_spec(ROUTER_LANES), _row_spec(ROUTER_LANES), tile_row],
        out_specs=pl.BlockSpec(memory_space=pl.ANY),
        scratch_shapes=[pltpu.VMEM((2, MAX_SLOTS, X_WORDS), U32),
                        pltpu.VMEM((MOE_BLOCK, X_WORDS), U32),
                        pltpu.SemaphoreType.DMA((2,)),
                        pltpu.SemaphoreType.DMA(())],
    )
    return pl.pallas_call(
        _dispatch_kernel,
        grid_spec=grid_spec,
        out_shape=jax.ShapeDtypeStruct((PAD_ROWS, X_WORDS), U32),
        compiler_params=_cparams("arbitrary"),
        name="dispatch",
    )(tables["loff"], tables["c8"], tables["dst"], tables["nslot"], tables["zdst"], tables["zcnt"],
      tables["n_active"], xn, route_i, route_w, tables["offrow"])


def _combine_kernel(loff_s, c8_s, dst_s, nslot_s,
                    h_ref, ri_ref, off_ref, g_ref, ypad_hbm, out_ref,
                    ybuf, sem, *, final_norm):
    i = pl.program_id(0)
    slot = i % 2
    last = pl.num_programs(0) - 1
    fetch = lambda tile, sl: (lambda e: _run_copy(tile, e, loff_s, c8_s, dst_s, ybuf, sl,
                                                  ypad_hbm, sem, False))

    @pl.when(i == 0)
    def _():
        ybuf[...] = jnp.zeros_like(ybuf)
        first = fetch(i, slot)
        _loop(0, N_EXPERTS, lambda e: _start(*first(e)), unroll=RUN_UNROLL)

    @pl.when(i < last)
    def _():
        nxt = fetch(i + 1, 1 - slot)
        _loop(0, N_EXPERTS, lambda e: _start(*nxt(e)), unroll=RUN_UNROLL)

    _wait_rows(pl.multiple_of(nslot_s[i], RUN_ALIGN), ybuf, slot, ypad_hbm, sem, False)

    s1, s2 = _slots(ri_ref, off_ref)
    scol = lax.broadcasted_iota(I32, (ROW_TILE, MAX_SLOTS), 1).astype(F32)
    pick = ((scol == s1) | (scol == s2)).astype(BF16)
    y_hi, y_lo = _unpack_halves(ybuf[slot])
    h = h_ref[...] + jnp.concatenate([_dot(pick, y_hi), _dot(pick, y_lo)], axis=1)
    out_ref[...] = _rms(h, g_ref[...]) if final_norm else h


def _combine(tables, h, route_i, y_pad, g, final_norm):
    tile_row = pl.BlockSpec((1, SUBLANES, ROUTER_LANES), lambda i, *_: (i, 0, 0))
    grid_spec = pltpu.PrefetchScalarGridSpec(
        num_scalar_prefetch=4,
        grid=(N_TILES,),
        in_specs=[_row_spec(D_MODEL), _row_spec(ROUTER_LANES), tile_row, _full_spec(g),
                  pl.BlockSpec(memory_space=pl.ANY)],
        out_specs=_row_spec(D_MODEL),
        scratch_shapes=[pltpu.VMEM((2, MAX_SLOTS, HALF), U32),
                        pltpu.SemaphoreType.DMA((2,))],
    )
    return pl.pallas_call(
        functools.partial(_combine_kernel, final_norm=final_norm),
        grid_spec=grid_spec,
        out_shape=jax.ShapeDtypeStruct((TOKENS, D_MODEL), F32),
        compiler_params=_cparams("arbitrary"),
        name="combine",
    )(tables["loff"], tables["c8"], tables["dst"], tables["nslot"],
      h, route_i, tables["offrow"], g, y_pad)


def _expert_kernel(be_ref, slot_ref, next_ref, nact_ref, x_ref, wg_hbm, wu_hbm, wd_hbm, y_ref,
                   wg_f, wu_f, wd_f, wg_b, wu_b, wd_b, sem, *, layer):
    i = pl.program_id(0)
    staged = ((wg_hbm, wg_f, wg_b), (wu_hbm, wu_f, wu_b), (wd_hbm, wd_f, wd_b))

    def fetch(e, slot):
        return [pltpu.make_async_copy(hbm.at[layer, e], f32.at[slot], sem.at[slot, k])
                for k, (hbm, f32, _) in enumerate(staged)]

    @pl.when(i >= nact_ref[0])
    def _():
        y_ref[...] = jnp.zeros_like(y_ref)

    @pl.when(i < nact_ref[0])
    def _():
        e = be_ref[i]
        slot = slot_ref[i]

        @pl.when((i == 0) | (e != be_ref[jnp.maximum(i - 1, 0)]))
        def _():
            @pl.when(i == 0)
            def _():
                for cp in fetch(e, slot):
                    cp.start()
            for cp in fetch(e, slot):
                cp.wait()
            nxt = next_ref[i]

            @pl.when(nxt >= 0)
            def _():
                for cp in fetch(nxt, 1 - slot):
                    cp.start()
            for _, f32, b16 in staged:
                b16[...] = f32[slot].astype(BF16)

        x_hi, x_lo = _unpack_halves(x_ref[:, 0:HALF])
        row_w = lax.bitcast_convert_type(x_ref[:, HALF:HALF + 1], F32)
        gate = _dot(x_hi, wg_b[0:HALF, :]) + _dot(x_lo, wg_b[HALF:D_MODEL, :])
        up = _dot(x_hi, wu_b[0:HALF, :]) + _dot(x_lo, wu_b[HALF:D_MODEL, :])
        hmid = (jax.nn.silu(gate) * up).astype(BF16)
        y = (_dot(hmid, wd_b[...]) * row_w).astype(BF16).astype(F32)
        y_ref[...] = _pack_halves(y[:, 0:HALF], y[:, HALF:D_MODEL])


def _experts(layer, tables, x_pad, w_gate, w_up, w_down):
    hbm = pl.BlockSpec(memory_space=pl.ANY)
    up_shape, down_shape = (D_MODEL, D_EXPERT), (D_EXPERT, D_MODEL)
    grid_spec = pltpu.PrefetchScalarGridSpec(
        num_scalar_prefetch=4,
        grid=(N_MOE_BLOCKS,),
        in_specs=[pl.BlockSpec((MOE_BLOCK, X_WORDS),
                               lambda i, be, sl, nx, na: (jnp.minimum(i, na[0] - 1), 0)),
                  hbm, hbm, hbm],
        out_specs=pl.BlockSpec((MOE_BLOCK, HALF), lambda i, *_: (i, 0)),
        scratch_shapes=[pltpu.VMEM((2,) + up_shape, F32), pltpu.VMEM((2,) + up_shape, F32),
                        pltpu.VMEM((2,) + down_shape, F32),
                        pltpu.VMEM(up_shape, BF16), pltpu.VMEM(up_shape, BF16),
                        pltpu.VMEM(down_shape, BF16),
                        pltpu.SemaphoreType.DMA((2, 3))],
    )
    return pl.pallas_call(
        functools.partial(_expert_kernel, layer=layer),
        grid_spec=grid_spec,
        out_shape=jax.ShapeDtypeStruct((PAD_ROWS, HALF), U32),
        compiler_params=_cparams("arbitrary"),
        name="experts",
    )(tables["blk_expert"], tables["blk_slot"], tables["blk_next"], tables["n_active"],
      x_pad, w_gate, w_up, w_down)


def _routing_tables(tile_counts):
    cnt = tile_counts[:, 0, EXPERT_LANE0:EXPERT_LANE0 + N_EXPERTS].astype(I32)
    c8 = (cnt + RUN_ALIGN - 1) // RUN_ALIGN * RUN_ALIGN
    loff = jnp.cumsum(c8, axis=1) - c8
    nslot = jnp.sum(c8, axis=1)
    tot = jnp.sum(c8, axis=0)
    padded = (tot + MOE_BLOCK - 1) // MOE_BLOCK * MOE_BLOCK
    end = jnp.cumsum(padded)
    base = end - padded
    dst = base[None, :] + jnp.cumsum(c8, axis=0) - c8
    blk_start = jnp.arange(N_MOE_BLOCKS, dtype=I32) * MOE_BLOCK
    blk_expert = jnp.minimum(jnp.sum((end[None, :] <= blk_start[:, None]).astype(I32), axis=1),
                             N_EXPERTS - 1).astype(I32)
    offrow = jnp.zeros((N_TILES, ROUTER_LANES), F32).at[:, EXPERT_LANE0:EXPERT_LANE0 + N_EXPERTS].set(
        loff.astype(F32))
    offrow = jnp.broadcast_to(offrow[:, None, :], (N_TILES, SUBLANES, ROUTER_LANES))
    experts = jnp.arange(N_EXPERTS, dtype=I32)
    present = padded > 0
    ordinal = jnp.cumsum(present.astype(I32)) - 1
    at_or_after = lax.cummin(jnp.where(present, experts, N_EXPERTS), reverse=True)
    after = jnp.concatenate([at_or_after[1:], jnp.full((1,), N_EXPERTS, I32)])
    after = jnp.where(after == N_EXPERTS, -1, after)
    pick = (blk_expert[:, None] == experts[None, :]).astype(I32)
    blk_slot = jnp.sum(pick * (ordinal % 2)[None, :], axis=1).astype(I32)
    blk_next = jnp.sum(pick * after[None, :], axis=1).astype(I32)
    return {
        "blk_slot": blk_slot, "blk_next": blk_next,
        "loff": loff.reshape(-1).astype(I32), "c8": c8.reshape(-1).astype(I32),
        "dst": dst.reshape(-1).astype(I32), "nslot": nslot.astype(I32),
        "zdst": (base + tot).astype(I32), "zcnt": (padded - tot).astype(I32),
        "offrow": offrow, "blk_expert": blk_expert,
        "n_active": (end[-1:] // MOE_BLOCK).astype(I32),
    }


def _split_router_weights(w_router_group, w_router_expert):
    w_e = w_router_expert.transpose(1, 0, 2).reshape(D_MODEL, N_EXPERTS)
    w = jnp.concatenate([w_router_group, w_e], axis=1)
    w = jnp.pad(w, ((0, 0), (0, ROUTER_LANES - w.shape[1])))
    hi = w.astype(BF16)
    lo = (w - hi.astype(F32)).astype(BF16)
    return jnp.concatenate([hi, lo], axis=1)


def _input_weights(w):
    cols = []
    for g in range(N_GROUPS):
        for part in range(3):
            c0 = part * ATTN_WIDTH + g * GROUP_WIDTH
            blk = w[:, c0:c0 + GROUP_WIDTH]
            cols.append(blk * HEAD_DIM ** -0.5 if part == 0 else blk)
    cols.append(w[:, QKV_WIDTH:])
    return jnp.concatenate(cols, axis=1).astype(BF16)


def kernel(x, rel_bias, norm_mix_g, w_in, pool_w, pool_scale, w_proj_attn, w_proj_pool, w_out,
           norm_ffn_g, w_router_group, w_router_expert, w_gate_e, w_up_e, w_down_e, norm_final_g):
    h = x.reshape(TOKENS, D_MODEL)
    for l in range(DEPTH):
        qkv, u, gates = _project(h, norm_mix_g[l][None], _input_weights(w_in[l]))
        attn_outs = [_attention_group(qkv[g], rel_bias, g) for g in range(N_GROUPS)]
        h = _mix_out(h, attn_outs, u, gates, pool_w[l].astype(BF16), pool_scale[l][None],
                     w_proj_attn[l].astype(BF16), w_proj_pool[l].astype(BF16), w_out[l].astype(BF16))
        xn, route_i, route_w, tile_counts = _route(
            h, norm_ffn_g[l][None], _split_router_weights(w_router_group[l], w_router_expert[l]))
        tables = _routing_tables(tile_counts)
        x_pad = _dispatch(tables, xn, route_i, route_w)
        y_pad = _experts(l, tables, x_pad, w_gate_e, w_up_e, w_down_e)
        h = _combine(tables, h, route_i, y_pad, norm_final_g[None], l == DEPTH - 1)
    return h.reshape(BATCH, SEQ, D_MODEL)
```

```python
import functools
import math

import jax
import jax.numpy as jnp
from jax import lax
from jax.experimental import pallas as pl
from jax.experimental.pallas import tpu as pltpu

F32 = jnp.float32
BF16 = jnp.bfloat16
I32 = jnp.int32
U32 = jnp.uint32

D_MODEL = 1024
BATCH = 8
SEQ = 2048
TOKENS = BATCH * SEQ
DEPTH = 2

HEAD_DIM = 64
HEADS_PER_GROUP = 4
GROUP_WIDTH = HEADS_PER_GROUP * HEAD_DIM
DILATION_PATTERNS = ((128, 1), (512, 4), (2048, 16))
N_GROUPS = len(DILATION_PATTERNS)
N_ATTN_HEADS = N_GROUPS * HEADS_PER_GROUP
ATTN_WIDTH = N_ATTN_HEADS * HEAD_DIM
QKV_WIDTH = 3 * ATTN_WIDTH
GROUP_QKV = 3 * GROUP_WIDTH
N_SIDE = 64
assert all(w // (2 * d) == N_SIDE for w, d in DILATION_PATTERNS)
POOL_WINDOWS = (2, 4, 8, 16)
POOL_GROUP_WIDTH = 128
POOL_WIDTH = len(POOL_WINDOWS) * POOL_GROUP_WIDTH
POOL_HALO = max(POOL_WINDOWS) // 2
N_BRANCHES = 2
GATE_WIDTH = N_BRANCHES * D_MODEL
IN_WIDTH = QKV_WIDTH + POOL_WIDTH + GATE_WIDTH
N_BUCKETS = 32
MAX_DISTANCE = 1024
N_EXPERT_GROUPS = 8
EXPERTS_PER_GROUP = 8
N_EXPERTS = N_EXPERT_GROUPS * EXPERTS_PER_GROUP
TOP_K = 2
D_EXPERT = 512
N_ASSIGN = TOKENS * TOP_K
EPS = 1e-6
NEG_INF = -1e30

LANES = 128
SUBLANES = 8
ROW_TILE = 512
N_TILES = TOKENS // ROW_TILE
TILES_PER_SEQ = SEQ // ROW_TILE
Q_BLOCK = 128
K_BLOCK = Q_BLOCK + 2 * N_SIDE
ROUTER_LANES = 128
EXPERT_LANE0 = N_EXPERT_GROUPS
VMEM_LIMIT = 56 * 1024 * 1024

RUN_ALIGN = SUBLANES
MOE_BLOCK = 256
HALF = D_MODEL // 2
X_WORDS = HALF + LANES
SLOT_CHUNK = 512
RUN_UNROLL = 4
MAX_SLOTS = -(-(TOP_K * ROW_TILE + N_EXPERTS * (RUN_ALIGN - 1)) // SLOT_CHUNK) * SLOT_CHUNK
PAD_ROWS = (N_ASSIGN + N_TILES * N_EXPERTS * (RUN_ALIGN - 1)
            + N_EXPERTS * (MOE_BLOCK - RUN_ALIGN))
assert PAD_ROWS % MOE_BLOCK == 0
N_MOE_BLOCKS = PAD_ROWS // MOE_BLOCK
HIGH_HALF = 0xFFFF0000


def _cparams(*sem):
    return pltpu.CompilerParams(dimension_semantics=sem, vmem_limit_bytes=VMEM_LIMIT)


def _rms(h, g):
    r = lax.rsqrt(jnp.mean(h * h, axis=-1, keepdims=True) + EPS)
    return (h * r) * g


def _dot(a, b):
    return jnp.dot(a, b, preferred_element_type=F32)


def _row_spec(width):
    return pl.BlockSpec((ROW_TILE, width), lambda i, *_: (i, 0))


def _full_spec(a):
    return pl.BlockSpec(a.shape, lambda i, *_: (0,) * a.ndim)


def _bits(x):
    return lax.bitcast_convert_type(x, U32)


def _pack_halves(a, b):
    return (_bits(a) & jnp.uint32(HIGH_HALF)) | (_bits(b) >> 16)


def _unpack_halves(words):
    hi = lax.bitcast_convert_type(words & jnp.uint32(HIGH_HALF), F32)
    lo = lax.bitcast_convert_type(words << 16, F32)
    return hi.astype(BF16), lo.astype(BF16)


def _proj_kernel(h_ref, g_ref, w_ref, q0_ref, q1_ref, q2_ref, u_ref, gate_ref, slabs):
    xn = _rms(h_ref[...], g_ref[...]).astype(BF16)
    n_slabs = GROUP_QKV // LANES
    for g, out_ref in enumerate((q0_ref, q1_ref, q2_ref)):
        dilation = DILATION_PATTERNS[g][1]
        res = _dot(xn, w_ref[:, g * GROUP_QKV:(g + 1) * GROUP_QKV])
        if dilation == 1:
            out_ref[0, 0] = res.astype(BF16)
            continue
        for s in range(n_slabs):
            slabs[s] = res[:, s * LANES:(s + 1) * LANES]
        n = ROW_TILE // dilation
        for r in range(dilation):
            rows = [slabs[s, pl.ds(r, n, stride=dilation), :] for s in range(n_slabs)]
            out_ref[0, r] = jnp.concatenate(rows, axis=1).astype(BF16)
    u_ref[...] = _dot(xn, w_ref[:, QKV_WIDTH:QKV_WIDTH + POOL_WIDTH])
    gates = _dot(xn, w_ref[:, QKV_WIDTH + POOL_WIDTH:IN_WIDTH])
    gate_ref[...] = jax.nn.sigmoid(gates).astype(BF16)


def _project(h, g, w_bf16):
    qkv_shapes, qkv_specs = [], []
    for _, d in DILATION_PATTERNS:
        qkv_shapes.append(jax.ShapeDtypeStruct((BATCH, d, SEQ // d, GROUP_QKV), BF16))
        qkv_specs.append(pl.BlockSpec((1, d, ROW_TILE // d, GROUP_QKV),
                                      lambda i: (i // TILES_PER_SEQ, 0, i % TILES_PER_SEQ, 0)))
    res = pl.pallas_call(
        _proj_kernel,
        grid=(N_TILES,),
        in_specs=[_row_spec(D_MODEL), _full_spec(g), _full_spec(w_bf16)],
        out_specs=qkv_specs + [_row_spec(POOL_WIDTH), _row_spec(GATE_WIDTH)],
        out_shape=qkv_shapes + [jax.ShapeDtypeStruct((TOKENS, POOL_WIDTH), F32),
                                jax.ShapeDtypeStruct((TOKENS, GATE_WIDTH), BF16)],
        scratch_shapes=[pltpu.VMEM((GROUP_QKV // LANES, ROW_TILE, LANES), F32)],
        compiler_params=_cparams("parallel"),
        name="proj",
    )(h, g, w_bf16)
    return res[:N_GROUPS], res[N_GROUPS], res[N_GROUPS + 1]


def _attn_kernel(q0_ref, q1_ref, q2_ref, b0_ref, b1_ref, b2_ref, o_ref, kpad, vpad, num, den, top):
    zpad = jnp.zeros((N_SIDE, GROUP_WIDTH), BF16)
    kpad[0:N_SIDE, :] = zpad
    vpad[0:N_SIDE, :] = zpad
    for g, (qkv_ref, bias_ref) in enumerate(((q0_ref, b0_ref), (q1_ref, b1_ref), (q2_ref, b2_ref))):
        dilation = DILATION_PATTERNS[g][1]
        sub_len = SEQ // dilation
        kpad[N_SIDE + sub_len:2 * N_SIDE + sub_len, :] = zpad
        vpad[N_SIDE + sub_len:2 * N_SIDE + sub_len, :] = zpad
        _attn_group(qkv_ref, bias_ref, kpad, vpad, num, den, top, dilation=dilation, first=g == 0)

    def finish(c, carry):
        rows = pl.ds(pl.multiple_of(c * ROW_TILE, ROW_TILE), ROW_TILE)
        merged = [num[half, rows, :] / den[half, rows, :] for half in range(GROUP_WIDTH // LANES)]
        o_ref[0, rows, :] = jnp.concatenate(merged, axis=1).astype(BF16)
        return carry

    lax.fori_loop(0, SEQ // ROW_TILE, finish, 0)


def _attn_group(qkv_ref, bias_ref, kpad, vpad, num, den, top, *, dilation, first):
    sub_len = SEQ // dilation
    head_of_lane = lax.broadcasted_iota(I32, (1, GROUP_WIDTH), 1) // HEAD_DIM
    qi = lax.broadcasted_iota(I32, (Q_BLOCK, K_BLOCK), 0)
    ki = lax.broadcasted_iota(I32, (Q_BLOCK, K_BLOCK), 1)
    band = jnp.abs(ki - N_SIDE - qi) <= N_SIDE

    def by_head(cols):
        out = cols[HEADS_PER_GROUP - 1]
        for h in range(HEADS_PER_GROUP - 2, -1, -1):
            out = jnp.where(head_of_lane == h, cols[h], out)
        return out

    def residue(r, carry):
        kpad[N_SIDE:N_SIDE + sub_len, :] = qkv_ref[0, r, :, GROUP_WIDTH:2 * GROUP_WIDTH]
        vpad[N_SIDE:N_SIDE + sub_len, :] = qkv_ref[0, r, :, 2 * GROUP_WIDTH:3 * GROUP_WIDTH]

        def block(i, carry2):
            r0 = pl.multiple_of(i * Q_BLOCK, Q_BLOCK)
            qb = qkv_ref[0, r, pl.ds(r0, Q_BLOCK), 0:GROUP_WIDTH]
            kw = kpad[pl.ds(r0, K_BLOCK), :]
            vw = vpad[pl.ds(r0, K_BLOCK), :]
            zero = jnp.zeros_like(qb)
            q_heads = jnp.concatenate(
                [jnp.where(head_of_lane == h, qb, zero) for h in range(HEADS_PER_GROUP)], axis=0)
            s = lax.dot_general(q_heads, kw, (((1,), (1,)), ((), ())), preferred_element_type=F32)
            s = s.reshape(HEADS_PER_GROUP, Q_BLOCK, K_BLOCK) + bias_ref[...]
            kpos = r0 - N_SIDE + ki
            valid = band & (kpos >= 0) & (kpos < sub_len)
            s = jnp.where(valid[None], s, NEG_INF)
            m = jnp.max(s, axis=-1, keepdims=True)
            p = jnp.exp(s - m)
            psum = jnp.sum(p, axis=-1, keepdims=True)
            pb = p.astype(BF16)
            p_cat = jnp.concatenate([pb[h] for h in range(HEADS_PER_GROUP)], axis=1)
            zv = jnp.zeros_like(vw)
            v_heads = jnp.concatenate(
                [jnp.where(head_of_lane == h, vw, zv) for h in range(HEADS_PER_GROUP)], axis=0)
            o = _dot(p_cat, v_heads)
            lse = m + jnp.log(psum)
            o = o / by_head([psum[h] for h in range(HEADS_PER_GROUP)])
            lse_lanes = by_head([lse[h] for h in range(HEADS_PER_GROUP)])
            if dilation == 1:
                rows = pl.ds(r0, Q_BLOCK)
            else:
                rows = pl.ds(r + dilation * r0, Q_BLOCK, stride=dilation)
            for half in range(GROUP_WIDTH // LANES):
                lanes = slice(half * LANES, (half + 1) * LANES)
                if first:
                    num[half, rows, :] = o[:, lanes]
                    den[half, rows, :] = jnp.ones((Q_BLOCK, LANES), F32)
                    top[half, rows, :] = lse_lanes[:, lanes]
                else:
                    old = top[half, rows, :]
                    new = jnp.maximum(old, lse_lanes[:, lanes])
                    keep = jnp.exp(old - new)
                    add = jnp.exp(lse_lanes[:, lanes] - new)
                    num[half, rows, :] = keep * num[half, rows, :] + add * o[:, lanes]
                    den[half, rows, :] = keep * den[half, rows, :] + add
                    top[half, rows, :] = new
            return carry2

        lax.fori_loop(0, sub_len // Q_BLOCK, block, 0)
        return carry

    lax.fori_loop(0, dilation, residue, 0)


def _t5_bucket(rel):
    nb = N_BUCKETS // 2
    ret = jnp.where(rel > 0, nb, 0)
    n = jnp.abs(rel)
    max_exact = nb // 2
    nf = jnp.maximum(n, max_exact).astype(F32)
    large = max_exact + (jnp.log(nf / max_exact) / math.log(MAX_DISTANCE / max_exact)
                         * (nb - max_exact)).astype(I32)
    large = jnp.minimum(large, nb - 1)
    return ret + jnp.where(n < max_exact, n, large)


def _band_bias(rel_bias, group, dilation):
    qi = jnp.arange(Q_BLOCK)[:, None]
    ki = jnp.arange(K_BLOCK)[None, :]
    bucket = _t5_bucket((ki - N_SIDE - qi) * dilation)
    tab = rel_bias[:, group * HEADS_PER_GROUP:(group + 1) * HEADS_PER_GROUP]
    onehot = (bucket[:, :, None] == jnp.arange(N_BUCKETS)[None, None, :]).astype(F32)
    bias = jnp.einsum('qkb,bh->hqk', onehot, tab, precision=lax.Precision.HIGHEST)
    return bias.astype(F32)


def _attention(qkv, rel_bias):
    biases = [_band_bias(rel_bias, g, d) for g, (_, d) in enumerate(DILATION_PATTERNS)]
    slab = pltpu.VMEM((GROUP_WIDTH // LANES, SEQ, LANES), F32)
    pad = pltpu.VMEM((SEQ + 2 * N_SIDE, GROUP_WIDTH), BF16)
    out = pl.pallas_call(
        _attn_kernel,
        grid=(BATCH,),
        in_specs=[pl.BlockSpec((1,) + a.shape[1:], lambda b: (b, 0, 0, 0)) for a in qkv]
                 + [_full_spec(b) for b in biases],
        out_specs=pl.BlockSpec((1, SEQ, GROUP_WIDTH), lambda b: (b, 0, 0)),
        out_shape=jax.ShapeDtypeStruct((BATCH, SEQ, GROUP_WIDTH), BF16),
        scratch_shapes=[pad, pad, slab, slab, slab],
        compiler_params=_cparams("parallel"),
        name="attn",
    )(*qkv, *biases)
    return out.reshape(TOKENS, GROUP_WIDTH)


def _mixout_kernel(h_ref, attn_ref, u_ref, uprev_ref, unext_ref, gate_ref,
                   pw_ref, ps_ref, wpa_ref, wpp_ref, wo_ref, out_ref):
    j = pl.program_id(0) % TILES_PER_SEQ
    y_attn = _dot(attn_ref[...], wpa_ref[...])

    u = u_ref[...]
    prev = jnp.where(j == 0, 0.0, uprev_ref[0])
    nxt = jnp.where(j == TILES_PER_SEQ - 1, 0.0, unext_ref[0])
    ext = jnp.concatenate([prev, u, nxt], axis=0)
    pos = j * ROW_TILE + lax.broadcasted_iota(I32, (ROW_TILE, 1), 0)
    mixed = []
    for gi, w in enumerate(POOL_WINDOWS):
        half = w // 2
        sl = slice(gi * POOL_GROUP_WIDTH, (gi + 1) * POOL_GROUP_WIDTH)
        run = ext[:, sl]
        span = 1
        while 2 * span < w:
            run = run[:-span] + run[span:]
            span *= 2
        lo = POOL_HALO - half
        acc = run[lo:lo + ROW_TILE] + run[lo + half:lo + half + ROW_TILE]
        cnt = (jnp.minimum(pos + half, SEQ) - jnp.maximum(pos - half, 0)).astype(F32)
        pooled = acc / cnt - u[:, sl]
        mixed.append(_dot(pooled.astype(BF16), pw_ref[gi]) * ps_ref[:, sl])
    y_pool = _dot(jnp.concatenate(mixed, axis=1).astype(BF16), wpp_ref[...])

    y = (gate_ref[:, 0:D_MODEL] * y_attn.astype(BF16)
         + gate_ref[:, D_MODEL:GATE_WIDTH] * y_pool.astype(BF16))
    out_ref[...] = h_ref[...] + _dot(y, wo_ref[...])


def _mix_out(h, attn, u, gates, pool_w, pool_scale, w_proj_attn, w_proj_pool, w_out):
    halo_blocks = ROW_TILE // POOL_HALO
    u3 = u.reshape(TOKENS // POOL_HALO, POOL_HALO, POOL_WIDTH)
    last = TOKENS // POOL_HALO - 1
    prev_spec = pl.BlockSpec((1, POOL_HALO, POOL_WIDTH),
                             lambda i: (jnp.maximum(i * halo_blocks - 1, 0), 0, 0))
    next_spec = pl.BlockSpec((1, POOL_HALO, POOL_WIDTH),
                             lambda i: (jnp.minimum((i + 1) * halo_blocks, last), 0, 0))
    weights = (pool_w, pool_scale, w_proj_attn, w_proj_pool, w_out)
    return pl.pallas_call(
        _mixout_kernel,
        grid=(N_TILES,),
        in_specs=[_row_spec(D_MODEL), _row_spec(GROUP_WIDTH),
                  _row_spec(POOL_WIDTH), prev_spec, next_spec, _row_spec(GATE_WIDTH)]
                 + [_full_spec(w) for w in weights],
        out_specs=_row_spec(D_MODEL),
        out_shape=jax.ShapeDtypeStruct((TOKENS, D_MODEL), F32),
        compiler_params=_cparams("parallel"),
        name="mixout",
    )(h, attn, u, u3, u3, gates, *weights)


def _router_kernel(h_ref, g_ref, wr_ref, xn_ref, ri_ref, rw_ref, cnt_ref):
    xn = _rms(h_ref[...], g_ref[...])
    hi = xn.astype(BF16)
    xn_ref[...] = hi
    lo = (xn - hi.astype(F32)).astype(BF16)
    both = _dot(hi, wr_ref[...])
    lg = both[:, 0:ROUTER_LANES] + (both[:, ROUTER_LANES:] + _dot(lo, wr_ref[:, 0:ROUTER_LANES]))

    lane = lax.broadcasted_iota(I32, (ROW_TILE, ROUTER_LANES), 1)
    lanef = lane.astype(F32)
    low = jnp.float32(-3.0e38)
    far = jnp.float32(ROUTER_LANES)
    first = lambda hit: jnp.min(jnp.where(hit, lanef, far), axis=-1, keepdims=True)

    is_group = lane < N_EXPERT_GROUPS
    gl = jnp.where(is_group, lg, low)
    gmax = jnp.max(gl, axis=-1, keepdims=True)
    gidx = first(gl == gmax).astype(I32)
    gden = jnp.sum(jnp.where(is_group, jnp.exp(gl - gmax), 0.0), axis=-1, keepdims=True)
    g_p = 1.0 / gden

    in_group = ((lane >= EXPERT_LANE0) & (lane < EXPERT_LANE0 + N_EXPERTS)
                & ((lane - EXPERT_LANE0) // EXPERTS_PER_GROUP == gidx))
    el = jnp.where(in_group, lg, low)
    t1 = jnp.max(el, axis=-1, keepdims=True)
    l1 = first(in_group & (el == t1))
    rest = in_group & (lanef != l1)
    el2 = jnp.where(rest, lg, low)
    t2 = jnp.max(el2, axis=-1, keepdims=True)
    l2 = first(rest & (el2 == t2))
    e2 = jnp.exp(t2 - t1)
    w1 = g_p * (1.0 / (1.0 + e2))
    w2 = g_p * (e2 / (1.0 + e2))

    hit1 = lanef == l1
    hit2 = lanef == l2
    onehot = (hit1 | hit2).astype(BF16)
    ri = lax.broadcasted_iota(I32, (ROW_TILE, ROW_TILE), 0)
    ci = lax.broadcasted_iota(I32, (ROW_TILE, ROW_TILE), 1)
    before = (ci < ri).astype(BF16)
    seen = _dot(before, onehot)
    r1 = jnp.sum(jnp.where(hit1, seen, 0.0), axis=-1, keepdims=True)
    r2 = jnp.sum(jnp.where(hit2, seen, 0.0), axis=-1, keepdims=True)

    packed = jnp.zeros((ROW_TILE, ROUTER_LANES), F32)
    for k, v in enumerate((l1, l2, r1, r2)):
        packed = jnp.where(lane == k, v, packed)
    ri_ref[...] = packed.astype(I32)
    rw_ref[...] = jnp.where(lane == 0, w1, jnp.where(lane == 1, w2, 0.0))
    counts = jnp.sum(onehot.astype(F32), axis=0, keepdims=True)
    cnt_ref[0] = jnp.broadcast_to(counts, (SUBLANES, ROUTER_LANES))


def _route(h, g, wr_split):
    return pl.pallas_call(
        _router_kernel,
        grid=(N_TILES,),
        in_specs=[_row_spec(D_MODEL), _full_spec(g), _full_spec(wr_split)],
        out_specs=[_row_spec(D_MODEL), _row_spec(ROUTER_LANES), _row_spec(ROUTER_LANES),
                   pl.BlockSpec((1, SUBLANES, ROUTER_LANES), lambda i: (i, 0, 0))],
        out_shape=[jax.ShapeDtypeStruct((TOKENS, D_MODEL), BF16),
                   jax.ShapeDtypeStruct((TOKENS, ROUTER_LANES), I32),
                   jax.ShapeDtypeStruct((TOKENS, ROUTER_LANES), F32),
                   jax.ShapeDtypeStruct((N_TILES, SUBLANES, ROUTER_LANES), F32)],
        compiler_params=_cparams("parallel"),
        name="router",
    )(h, g, wr_split)


def _slots(ri_ref, off_ref):
    ri = ri_ref[...]
    lane = lax.broadcasted_iota(I32, (ROW_TILE, ROUTER_LANES), 1)
    off = off_ref[0, 0:1, :]
    pick = lambda k: jnp.sum(jnp.where(lane == ri[:, k:k + 1], off, 0.0), axis=-1, keepdims=True)
    return (pick(0) + ri[:, 2:3].astype(F32), pick(1) + ri[:, 3:4].astype(F32))


def _as_rows(cols):
    eye = (lax.broadcasted_iota(I32, (ROW_TILE, ROW_TILE), 0)
           == lax.broadcasted_iota(I32, (ROW_TILE, ROW_TILE), 1))
    return [jnp.sum(jnp.where(eye, c, 0.0), axis=0, keepdims=True) for c in cols]


def _run_copy(tile, e, loff_s, c8_s, dst_s, buf_ref, slot, hbm_ref, sem, to_hbm):
    k = tile * N_EXPERTS + e
    n = pl.multiple_of(c8_s[k], RUN_ALIGN)
    vm = buf_ref.at[slot, pl.ds(pl.multiple_of(loff_s[k], RUN_ALIGN), n)]
    hb = hbm_ref.at[pl.ds(pl.multiple_of(dst_s[k], RUN_ALIGN), n)]
    src, dst = (vm, hb) if to_hbm else (hb, vm)
    return n, pltpu.make_async_copy(src, dst, sem.at[slot])


def _loop(lo, hi, fn, unroll=1):
    def body(e, carry):
        fn(e)
        return carry
    lax.fori_loop(lo, hi, body, 0, unroll=unroll)


def _wait_rows(n, buf_ref, slot, hbm_ref, sem, to_hbm):
    vm = buf_ref.at[slot, pl.ds(0, n)]
    hb = hbm_ref.at[pl.ds(0, n)]
    src, dst = (vm, hb) if to_hbm else (hb, vm)
    pltpu.make_async_copy(src, dst, sem.at[slot]).wait()


def _start(n, cp):
    @pl.when(n > 0)
    def _():
        cp.start()


def _wait(n, cp):
    @pl.when(n > 0)
    def _():
        cp.wait()


def _dispatch_kernel(loff_s, c8_s, dst_s, nslot_s, zdst_s, zcnt_s, nact_s,
                     xn_ref, ri_ref, rw_ref, off_ref, xpad_hbm, sorted_buf, zero_buf, sem, zsem):
    i = pl.program_id(0)
    slot = i % 2
    last = pl.num_programs(0) - 1

    def zero_copy(e):
        n = pl.multiple_of(zcnt_s[e], RUN_ALIGN)
        dst = xpad_hbm.at[pl.ds(pl.multiple_of(zdst_s[e], RUN_ALIGN), n)]
        return n, pltpu.make_async_copy(zero_buf.at[pl.ds(0, n)], dst, zsem)

    def tail_copy(b):
        dst = xpad_hbm.at[pl.ds(pl.multiple_of(b * MOE_BLOCK, MOE_BLOCK), MOE_BLOCK)]
        return pltpu.make_async_copy(zero_buf, dst, zsem)

    @pl.when(i == 0)
    def _():
        zero_buf[...] = jnp.zeros_like(zero_buf)
        _loop(0, N_EXPERTS, lambda e: _start(*zero_copy(e)))
        _loop(nact_s[0], N_MOE_BLOCKS, lambda b: tail_copy(b).start())

    s1, s2 = _slots(ri_ref, off_ref)
    rw = rw_ref[...]
    s1_row, s2_row, w1_row, w2_row = _as_rows([s1, s2, rw[:, 0:1], rw[:, 1:2]])
    xn = xn_ref[...]
    lane = lax.broadcasted_iota(I32, (SLOT_CHUNK, LANES), 1)
    for c in range(MAX_SLOTS // SLOT_CHUNK):
        @pl.when(c * SLOT_CHUNK < nslot_s[i])
        def _():
            rows = slice(c * SLOT_CHUNK, (c + 1) * SLOT_CHUNK)
            srow = (c * SLOT_CHUNK
                    + lax.broadcasted_iota(I32, (SLOT_CHUNK, ROW_TILE), 0)).astype(F32)
            hit1 = srow == s1_row
            hit2 = srow == s2_row
            xs = _dot((hit1 | hit2).astype(BF16), xn)
            sorted_buf[slot, rows, 0:HALF] = _pack_halves(xs[:, 0:HALF], xs[:, HALF:D_MODEL])
            ws = jnp.sum(jnp.where(hit1, w1_row, 0.0) + jnp.where(hit2, w2_row, 0.0),
                         axis=-1, keepdims=True)
            sorted_buf[slot, rows, HALF:X_WORDS] = jnp.where(lane == 0, _bits(ws), jnp.uint32(0))

    copy = lambda tile, sl: (lambda e: _run_copy(tile, e, loff_s, c8_s, dst_s, sorted_buf, sl,
                                                 xpad_hbm, sem, True))
    mine = copy(i, slot)
    _loop(0, N_EXPERTS, lambda e: _start(*mine(e)), unroll=RUN_UNROLL)
    tile_rows = lambda t: pl.multiple_of(nslot_s[t], RUN_ALIGN)

    @pl.when(i > 0)
    def _():
        _wait_rows(tile_rows(i - 1), sorted_buf, 1 - slot, xpad_hbm, sem, True)

    @pl.when(i == last)
    def _():
        _wait_rows(tile_rows(i), sorted_buf, slot, xpad_hbm, sem, True)
        _loop(0, N_EXPERTS, lambda e: _wait(*zero_copy(e)))
        _loop(nact_s[0], N_MOE_BLOCKS, lambda b: tail_copy(b).wait())


def _dispatch(tables, xn, route_i, route_w):
    tile_row = pl.BlockSpec((1, SUBLANES, ROUTER_LANES), lambda i, *_: (i, 0, 0))
    grid_spec = pltpu.PrefetchScalarGridSpec(
        num_scalar_prefetch=7,
        grid=(N_TILES,),
        in_specs=[_row_spec(D_MODEL), _row_spec(ROUTER_LANES), _row_spec(ROUTER_LANES), tile_row],
        out_specs=pl.BlockSpec(memory_space=pl.ANY),
        scratch_shapes=[pltpu.VMEM((2, MAX_SLOTS, X_WORDS), U32),
                        pltpu.VMEM((MOE_BLOCK, X_WORDS), U32),
                        pltpu.SemaphoreType.DMA((2,)),
                        pltpu.SemaphoreType.DMA(())],
    )
    return pl.pallas_call(
        _dispatch_kernel,
        grid_spec=grid_spec,
        out_shape=jax.ShapeDtypeStruct((PAD_ROWS, X_WORDS), U32),
        compiler_params=_cparams("arbitrary"),
        name="dispatch",
    )(tables["loff"], tables["c8"], tables["dst"], tables["nslot"], tables["zdst"], tables["zcnt"],
      tables["n_active"], xn, route_i, route_w, tables["offrow"])


def _combine_kernel(loff_s, c8_s, dst_s, nslot_s,
                    h_ref, ri_ref, off_ref, g_ref, ypad_hbm, out_ref,
                    ybuf, sem, *, final_norm):
    i = pl.program_id(0)
    slot = i % 2
    last = pl.num_programs(0) - 1
    fetch = lambda tile, sl: (lambda e: _run_copy(tile, e, loff_s, c8_s, dst_s, ybuf, sl,
                                                  ypad_hbm, sem, False))

    @pl.when(i == 0)
    def _():
        ybuf[...] = jnp.zeros_like(ybuf)
        first = fetch(i, slot)
        _loop(0, N_EXPERTS, lambda e: _start(*first(e)), unroll=RUN_UNROLL)

    @pl.when(i < last)
    def _():
        nxt = fetch(i + 1, 1 - slot)
        _loop(0, N_EXPERTS, lambda e: _start(*nxt(e)), unroll=RUN_UNROLL)

    _wait_rows(pl.multiple_of(nslot_s[i], RUN_ALIGN), ybuf, slot, ypad_hbm, sem, False)

    s1, s2 = _slots(ri_ref, off_ref)
    scol = lax.broadcasted_iota(I32, (ROW_TILE, MAX_SLOTS), 1).astype(F32)
    pick = ((scol == s1) | (scol == s2)).astype(BF16)
    y_hi, y_lo = _unpack_halves(ybuf[slot])
    h = h_ref[...] + jnp.concatenate([_dot(pick, y_hi), _dot(pick, y_lo)], axis=1)
    out_ref[...] = _rms(h, g_ref[...]) if final_norm else h


def _combine(tables, h, route_i, y_pad, g, final_norm):
    tile_row = pl.BlockSpec((1, SUBLANES, ROUTER_LANES), lambda i, *_: (i, 0, 0))
    grid_spec = pltpu.PrefetchScalarGridSpec(
        num_scalar_prefetch=4,
        grid=(N_TILES,),
        in_specs=[_row_spec(D_MODEL), _row_spec(ROUTER_LANES), tile_row, _full_spec(g),
                  pl.BlockSpec(memory_space=pl.ANY)],
        out_specs=_row_spec(D_MODEL),
        scratch_shapes=[pltpu.VMEM((2, MAX_SLOTS, HALF), U32),
                        pltpu.SemaphoreType.DMA((2,))],
    )
    return pl.pallas_call(
        functools.partial(_combine_kernel, final_norm=final_norm),
        grid_spec=grid_spec,
        out_shape=jax.ShapeDtypeStruct((TOKENS, D_MODEL), F32),
        compiler_params=_cparams("arbitrary"),
        name="combine",
    )(tables["loff"], tables["c8"], tables["dst"], tables["nslot"],
      h, route_i, tables["offrow"], g, y_pad)


def _expert_kernel(be_ref, slot_ref, next_ref, nact_ref, x_ref, wg_hbm, wu_hbm, wd_hbm, y_ref,
                   wg_f, wu_f, wd_f, wg_b, wu_b, wd_b, sem, *, layer):
    i = pl.program_id(0)
    staged = ((wg_hbm, wg_f, wg_b), (wu_hbm, wu_f, wu_b), (wd_hbm, wd_f, wd_b))

    def fetch(e, slot):
        return [pltpu.make_async_copy(hbm.at[layer, e], f32.at[slot], sem.at[slot, k])
                for k, (hbm, f32, _) in enumerate(staged)]

    @pl.when(i >= nact_ref[0])
    def _():
        y_ref[...] = jnp.zeros_like(y_ref)

    @pl.when(i < nact_ref[0])
    def _():
        e = be_ref[i]
        slot = slot_ref[i]

        @pl.when((i == 0) | (e != be_ref[jnp.maximum(i - 1, 0)]))
        def _():
            @pl.when(i == 0)
            def _():
                for cp in fetch(e, slot):
                    cp.start()
            for cp in fetch(e, slot):
                cp.wait()
            nxt = next_ref[i]

            @pl.when(nxt >= 0)
            def _():
                for cp in fetch(nxt, 1 - slot):
                    cp.start()
            for _, f32, b16 in staged:
                b16[...] = f32[slot].astype(BF16)

        x_hi, x_lo = _unpack_halves(x_ref[:, 0:HALF])
        row_w = lax.bitcast_convert_type(x_ref[:, HALF:HALF + 1], F32)
        gate = _dot(x_hi, wg_b[0:HALF, :]) + _dot(x_lo, wg_b[HALF:D_MODEL, :])
        up = _dot(x_hi, wu_b[0:HALF, :]) + _dot(x_lo, wu_b[HALF:D_MODEL, :])
        hmid = (jax.nn.silu(gate) * up).astype(BF16)
        y = (_dot(hmid, wd_b[...]) * row_w).astype(BF16).astype(F32)
        y_ref[...] = _pack_halves(y[:, 0:HALF], y[:, HALF:D_MODEL])


def _experts(layer, tables, x_pad, w_gate, w_up, w_down):
    hbm = pl.BlockSpec(memory_space=pl.ANY)
    up_shape, down_shape = (D_MODEL, D_EXPERT), (D_EXPERT, D_MODEL)
    grid_spec = pltpu.PrefetchScalarGridSpec(
        num_scalar_prefetch=4,
        grid=(N_MOE_BLOCKS,),
        in_specs=[pl.BlockSpec((MOE_BLOCK, X_WORDS),
                               lambda i, be, sl, nx, na: (jnp.minimum(i, na[0] - 1), 0)),
                  hbm, hbm, hbm],
        out_specs=pl.BlockSpec((MOE_BLOCK, HALF), lambda i, *_: (i, 0)),
        scratch_shapes=[pltpu.VMEM((2,) + up_shape, F32), pltpu.VMEM((2,) + up_shape, F32),
                        pltpu.VMEM((2,) + down_shape, F32),
                        pltpu.VMEM(up_shape, BF16), pltpu.VMEM(up_shape, BF16),
                        pltpu.VMEM(down_shape, BF16),
                        pltpu.SemaphoreType.DMA((2, 3))],
    )
    return pl.pallas_call(
        functools.partial(_expert_kernel, layer=layer),
        grid_spec=grid_spec,
        out_shape=jax.ShapeDtypeStruct((PAD_ROWS, HALF), U32),
        compiler_params=_cparams("arbitrary"),
        name="experts",
    )(tables["blk_expert"], tables["blk_slot"], tables["blk_next"], tables["n_active"],
      x_pad, w_gate, w_up, w_down)


def _routing_tables(tile_counts):
    cnt = tile_counts[:, 0, EXPERT_LANE0:EXPERT_LANE0 + N_EXPERTS].astype(I32)
    c8 = (cnt + RUN_ALIGN - 1) // RUN_ALIGN * RUN_ALIGN
    loff = jnp.cumsum(c8, axis=1) - c8
    nslot = jnp.sum(c8, axis=1)
    tot = jnp.sum(c8, axis=0)
    padded = (tot + MOE_BLOCK - 1) // MOE_BLOCK * MOE_BLOCK
    end = jnp.cumsum(padded)
    base = end - padded
    dst = base[None, :] + jnp.cumsum(c8, axis=0) - c8
    blk_start = jnp.arange(N_MOE_BLOCKS, dtype=I32) * MOE_BLOCK
    blk_expert = jnp.minimum(jnp.sum((end[None, :] <= blk_start[:, None]).astype(I32), axis=1),
                             N_EXPERTS - 1).astype(I32)
    offrow = jnp.zeros((N_TILES, ROUTER_LANES), F32).at[:, EXPERT_LANE0:EXPERT_LANE0 + N_EXPERTS].set(
        loff.astype(F32))
    offrow = jnp.broadcast_to(offrow[:, None, :], (N_TILES, SUBLANES, ROUTER_LANES))
    experts = jnp.arange(N_EXPERTS, dtype=I32)
    present = padded > 0
    ordinal = jnp.cumsum(present.astype(I32)) - 1
    at_or_after = lax.cummin(jnp.where(present, experts, N_EXPERTS), reverse=True)
    after = jnp.concatenate([at_or_after[1:], jnp.full((1,), N_EXPERTS, I32)])
    after = jnp.where(after == N_EXPERTS, -1, after)
    pick = (blk_expert[:, None] == experts[None, :]).astype(I32)
    blk_slot = jnp.sum(pick * (ordinal % 2)[None, :], axis=1).astype(I32)
    blk_next = jnp.sum(pick * after[None, :], axis=1).astype(I32)
    return {
        "blk_slot": blk_slot, "blk_next": blk_next,
        "loff": loff.reshape(-1).astype(I32), "c8": c8.reshape(-1).astype(I32),
        "dst": dst.reshape(-1).astype(I32), "nslot": nslot.astype(I32),
        "zdst": (base + tot).astype(I32), "zcnt": (padded - tot).astype(I32),
        "offrow": offrow, "blk_expert": blk_expert,
        "n_active": (end[-1:] // MOE_BLOCK).astype(I32),
    }


def _split_router_weights(w_router_group, w_router_expert):
    w_e = w_router_expert.transpose(1, 0, 2).reshape(D_MODEL, N_EXPERTS)
    w = jnp.concatenate([w_router_group, w_e], axis=1)
    w = jnp.pad(w, ((0, 0), (0, ROUTER_LANES - w.shape[1])))
    hi = w.astype(BF16)
    lo = (w - hi.astype(F32)).astype(BF16)
    return jnp.concatenate([hi, lo], axis=1)


def _input_weights(w):
    cols = []
    for g in range(N_GROUPS):
        for part in range(3):
            c0 = part * ATTN_WIDTH + g * GROUP_WIDTH
            blk = w[:, c0:c0 + GROUP_WIDTH]
            cols.append(blk * HEAD_DIM ** -0.5 if part == 0 else blk)
    cols.append(w[:, QKV_WIDTH:])
    return jnp.concatenate(cols, axis=1).astype(BF16)


def kernel(x, rel_bias, norm_mix_g, w_in, pool_w, pool_scale, w_proj_attn, w_proj_pool, w_out,
           norm_ffn_g, w_router_group, w_router_expert, w_gate_e, w_up_e, w_down_e, norm_final_g):
    h = x.reshape(TOKENS, D_MODEL)
    for l in range(DEPTH):
        qkv, u, gates = _project(h, norm_mix_g[l][None], _input_weights(w_in[l]))
        h = _mix_out(h, _attention(qkv, rel_bias), u, gates, pool_w[l].astype(BF16), pool_scale[l][None],
                     w_proj_attn[l].astype(BF16), w_proj_pool[l].astype(BF16), w_out[l].astype(BF16))
        xn, route_i, route_w, tile_counts = _route(
            h, norm_ffn_g[l][None], _split_router_weights(w_router_group[l], w_router_expert[l]))
        tables = _routing_tables(tile_counts)
        x_pad = _dispatch(tables, xn, route_i, route_w)
        y_pad = _experts(l, tables, x_pad, w_gate_e, w_up_e, w_down_e)
        h = _combine(tables, h, route_i, y_pad, norm_final_g[None], l == DEPTH - 1)
    return h.reshape(BATCH, SEQ, D_MODEL)
```

```python
import functools
import math

import jax
import jax.numpy as jnp
from jax import lax
from jax.experimental import pallas as pl
from jax.experimental.pallas import tpu as pltpu

F32 = jnp.float32
BF16 = jnp.bfloat16
I32 = jnp.int32
U32 = jnp.uint32

D_MODEL = 1024
BATCH = 8
SEQ = 2048
TOKENS = BATCH * SEQ
DEPTH = 2

HEAD_DIM = 64
HEADS_PER_GROUP = 4
GROUP_WIDTH = HEADS_PER_GROUP * HEAD_DIM
DILATION_PATTERNS = ((128, 1), (512, 4), (2048, 16))
N_GROUPS = len(DILATION_PATTERNS)
N_ATTN_HEADS = N_GROUPS * HEADS_PER_GROUP
ATTN_WIDTH = N_ATTN_HEADS * HEAD_DIM
QKV_WIDTH = 3 * ATTN_WIDTH
GROUP_QKV = 3 * GROUP_WIDTH
N_SIDE = 64
assert all(w // (2 * d) == N_SIDE for w, d in DILATION_PATTERNS)
POOL_WINDOWS = (2, 4, 8, 16)
POOL_GROUP_WIDTH = 128
POOL_WIDTH = len(POOL_WINDOWS) * POOL_GROUP_WIDTH
POOL_HALO = max(POOL_WINDOWS) // 2
N_BRANCHES = 2
GATE_WIDTH = N_BRANCHES * D_MODEL
IN_WIDTH = QKV_WIDTH + POOL_WIDTH + GATE_WIDTH
N_BUCKETS = 32
MAX_DISTANCE = 1024
N_EXPERT_GROUPS = 8
EXPERTS_PER_GROUP = 8
N_EXPERTS = N_EXPERT_GROUPS * EXPERTS_PER_GROUP
TOP_K = 2
D_EXPERT = 512
N_ASSIGN = TOKENS * TOP_K
EPS = 1e-6
NEG_INF = -1e30

LANES = 128
SUBLANES = 8
ROW_TILE = 512
N_TILES = TOKENS // ROW_TILE
TILES_PER_SEQ = SEQ // ROW_TILE
Q_BLOCK = 128
K_BLOCK = Q_BLOCK + 2 * N_SIDE
PAD_WAYS = 2
ROUTER_LANES = 128
EXPERT_LANE0 = N_EXPERT_GROUPS
VMEM_LIMIT = 56 * 1024 * 1024

RUN_ALIGN = SUBLANES
MOE_BLOCK = 256
HALF = D_MODEL // 2
X_WORDS = HALF + LANES
SLOT_CHUNK = 512
RUN_UNROLL = 4
W_STAGES = 3
MAX_SLOTS = -(-(TOP_K * ROW_TILE + N_EXPERTS * (RUN_ALIGN - 1)) // SLOT_CHUNK) * SLOT_CHUNK
PAD_ROWS = (N_ASSIGN + N_TILES * N_EXPERTS * (RUN_ALIGN - 1)
            + N_EXPERTS * (MOE_BLOCK - RUN_ALIGN))
assert PAD_ROWS % MOE_BLOCK == 0
N_MOE_BLOCKS = PAD_ROWS // MOE_BLOCK
HIGH_HALF = 0xFFFF0000


def _cparams(*sem):
    return pltpu.CompilerParams(dimension_semantics=sem, vmem_limit_bytes=VMEM_LIMIT)


def _rms(h, g):
    r = lax.rsqrt(jnp.mean(h * h, axis=-1, keepdims=True) + EPS)
    return (h * r) * g


def _dot(a, b):
    return jnp.dot(a, b, preferred_element_type=F32)


def _row_spec(width):
    return pl.BlockSpec((ROW_TILE, width), lambda i, *_: (i, 0))


def _full_spec(a):
    return pl.BlockSpec(a.shape, lambda i, *_: (0,) * a.ndim)


def _bits(x):
    return lax.bitcast_convert_type(x, U32)


def _pack_halves(a, b):
    return (_bits(a) & jnp.uint32(HIGH_HALF)) | (_bits(b) >> 16)


def _unpack_halves(words):
    hi = lax.bitcast_convert_type(words & jnp.uint32(HIGH_HALF), F32)
    lo = lax.bitcast_convert_type(words << 16, F32)
    return hi.astype(BF16), lo.astype(BF16)


def _proj_kernel(h_ref, g_ref, w_ref, q0_ref, q1_ref, q2_ref, u_ref, gate_ref, slabs):
    xn = _rms(h_ref[...], g_ref[...]).astype(BF16)
    n_slabs = GROUP_QKV // LANES
    for g, out_ref in enumerate((q0_ref, q1_ref, q2_ref)):
        dilation = DILATION_PATTERNS[g][1]
        res = _dot(xn, w_ref[:, g * GROUP_QKV:(g + 1) * GROUP_QKV])
        if dilation == 1:
            out_ref[0, 0] = res.astype(BF16)
            continue
        for s in range(n_slabs):
            slabs[s] = res[:, s * LANES:(s + 1) * LANES]
        n = ROW_TILE // dilation
        for r in range(dilation):
            rows = [slabs[s, pl.ds(r, n, stride=dilation), :] for s in range(n_slabs)]
            out_ref[0, r] = jnp.concatenate(rows, axis=1).astype(BF16)
    u_ref[...] = _dot(xn, w_ref[:, QKV_WIDTH:QKV_WIDTH + POOL_WIDTH])
    gates = _dot(xn, w_ref[:, QKV_WIDTH + POOL_WIDTH:IN_WIDTH])
    gate_ref[...] = jax.nn.sigmoid(gates).astype(BF16)


def _project(h, g, w_bf16):
    qkv_shapes, qkv_specs = [], []
    for _, d in DILATION_PATTERNS:
        qkv_shapes.append(jax.ShapeDtypeStruct((BATCH, d, SEQ // d, GROUP_QKV), BF16))
        qkv_specs.append(pl.BlockSpec((1, d, ROW_TILE // d, GROUP_QKV),
                                      lambda i: (i // TILES_PER_SEQ, 0, i % TILES_PER_SEQ, 0)))
    res = pl.pallas_call(
        _proj_kernel,
        grid=(N_TILES,),
        in_specs=[_row_spec(D_MODEL), _full_spec(g), _full_spec(w_bf16)],
        out_specs=qkv_specs + [_row_spec(POOL_WIDTH), _row_spec(GATE_WIDTH)],
        out_shape=qkv_shapes + [jax.ShapeDtypeStruct((TOKENS, POOL_WIDTH), F32),
                                jax.ShapeDtypeStruct((TOKENS, GATE_WIDTH), BF16)],
        scratch_shapes=[pltpu.VMEM((GROUP_QKV // LANES, ROW_TILE, LANES), F32)],
        compiler_params=_cparams("parallel"),
        name="proj",
    )(h, g, w_bf16)
    return res[:N_GROUPS], res[N_GROUPS], res[N_GROUPS + 1]


def _attn_kernel(q0_ref, q1_ref, q2_ref, b0_ref, b1_ref, b2_ref, o_ref, pads, num, den, top):
    zpad = jnp.zeros((N_SIDE, GROUP_WIDTH), BF16)
    for g, (qkv_ref, bias_ref) in enumerate(((q0_ref, b0_ref), (q1_ref, b1_ref), (q2_ref, b2_ref))):
        dilation = DILATION_PATTERNS[g][1]
        sub_len = SEQ // dilation
        for way in range(PAD_WAYS):
            for kv in range(2):
                pads[way, kv, 0:N_SIDE, :] = zpad
                pads[way, kv, N_SIDE + sub_len:2 * N_SIDE + sub_len, :] = zpad
        _attn_group(qkv_ref, bias_ref, pads, num, den, top, dilation=dilation, first=g == 0)

    def finish(c, carry):
        rows = pl.ds(pl.multiple_of(c * ROW_TILE, ROW_TILE), ROW_TILE)
        merged = [num[half, rows, :] / den[half, rows, :] for half in range(GROUP_WIDTH // LANES)]
        o_ref[0, rows, :] = jnp.concatenate(merged, axis=1).astype(BF16)
        return carry

    lax.fori_loop(0, SEQ // ROW_TILE, finish, 0)


def _attn_group(qkv_ref, bias_ref, pads, num, den, top, *, dilation, first):
    sub_len = SEQ // dilation
    n_blocks = sub_len // Q_BLOCK
    head_of_lane = lax.broadcasted_iota(I32, (1, GROUP_WIDTH), 1) // HEAD_DIM

    def by_head(cols):
        out = cols[HEADS_PER_GROUP - 1]
        for h in range(HEADS_PER_GROUP - 2, -1, -1):
            out = jnp.where(head_of_lane == h, cols[h], out)
        return out

    def load_keys(r, way):
        pads[way, 0, N_SIDE:N_SIDE + sub_len, :] = qkv_ref[0, r, :, GROUP_WIDTH:2 * GROUP_WIDTH]
        pads[way, 1, N_SIDE:N_SIDE + sub_len, :] = qkv_ref[0, r, :, 2 * GROUP_WIDTH:3 * GROUP_WIDTH]

    def block(r, way, i):
        if True:
            r0 = i * Q_BLOCK if isinstance(i, int) else pl.multiple_of(i * Q_BLOCK, Q_BLOCK)
            qb = qkv_ref[0, r, pl.ds(r0, Q_BLOCK), 0:GROUP_WIDTH]
            kw = pads[way, 0, pl.ds(r0, K_BLOCK), :]
            vw = pads[way, 1, pl.ds(r0, K_BLOCK), :]
            zero = jnp.zeros_like(qb)
            q_heads = jnp.concatenate(
                [jnp.where(head_of_lane == h, qb, zero) for h in range(HEADS_PER_GROUP)], axis=0)
            s = lax.dot_general(q_heads, kw, (((1,), (1,)), ((), ())), preferred_element_type=F32)
            if isinstance(i, int):
                edge = int(i == 0) + 2 * int(i == n_blocks - 1)
            else:
                edge = (i == 0).astype(I32) + 2 * (i == n_blocks - 1).astype(I32)
            s = s.reshape(HEADS_PER_GROUP, Q_BLOCK, K_BLOCK) + bias_ref[edge]
            m = jnp.max(s, axis=-1, keepdims=True)
            p = jnp.exp(s - m)
            psum = jnp.sum(p, axis=-1, keepdims=True)
            pb = p.astype(BF16)
            p_cat = jnp.concatenate([pb[h] for h in range(HEADS_PER_GROUP)], axis=1)
            zv = jnp.zeros_like(vw)
            v_heads = jnp.concatenate(
                [jnp.where(head_of_lane == h, vw, zv) for h in range(HEADS_PER_GROUP)], axis=0)
            o = _dot(p_cat, v_heads)
            lse = m + jnp.log(psum)
            o = o / by_head([psum[h] for h in range(HEADS_PER_GROUP)])
            lse_lanes = by_head([lse[h] for h in range(HEADS_PER_GROUP)])
            if dilation == 1:
                rows = pl.ds(r0, Q_BLOCK)
            else:
                rows = pl.ds(r + dilation * r0, Q_BLOCK, stride=dilation)
            for half in range(GROUP_WIDTH // LANES):
                lanes = slice(half * LANES, (half + 1) * LANES)
                if first:
                    num[half, rows, :] = o[:, lanes]
                    den[half, rows, :] = jnp.ones((Q_BLOCK, LANES), F32)
                    top[half, rows, :] = lse_lanes[:, lanes]
                else:
                    old = top[half, rows, :]
                    new = jnp.maximum(old, lse_lanes[:, lanes])
                    keep = jnp.exp(old - new)
                    add = jnp.exp(lse_lanes[:, lanes] - new)
                    num[half, rows, :] = keep * num[half, rows, :] + add * o[:, lanes]
                    den[half, rows, :] = keep * den[half, rows, :] + add
                    top[half, rows, :] = new

    def blocks_of(r, way):
        if n_blocks == 1:
            block(r, way, 0)
        else:
            def two(j, carry):
                block(r, way, 2 * j)
                block(r, way, 2 * j + 1)
                return carry
            lax.fori_loop(0, n_blocks // 2, two, 0)

    if dilation == 1:
        load_keys(0, 0)
        blocks_of(0, 0)
    else:
        def residues(j, carry):
            for way in range(PAD_WAYS):
                load_keys(PAD_WAYS * j + way, way)
            for way in range(PAD_WAYS):
                blocks_of(PAD_WAYS * j + way, way)
            return carry
        lax.fori_loop(0, dilation // PAD_WAYS, residues, 0)


def _t5_bucket(rel):
    nb = N_BUCKETS // 2
    ret = jnp.where(rel > 0, nb, 0)
    n = jnp.abs(rel)
    max_exact = nb // 2
    nf = jnp.maximum(n, max_exact).astype(F32)
    large = max_exact + (jnp.log(nf / max_exact) / math.log(MAX_DISTANCE / max_exact)
                         * (nb - max_exact)).astype(I32)
    large = jnp.minimum(large, nb - 1)
    return ret + jnp.where(n < max_exact, n, large)


def _band_bias(rel_bias, group, dilation):
    qi = jnp.arange(Q_BLOCK)[:, None]
    ki = jnp.arange(K_BLOCK)[None, :]
    bucket = _t5_bucket((ki - N_SIDE - qi) * dilation)
    tab = rel_bias[:, group * HEADS_PER_GROUP:(group + 1) * HEADS_PER_GROUP]
    onehot = (bucket[:, :, None] == jnp.arange(N_BUCKETS)[None, None, :]).astype(F32)
    bias = jnp.einsum('qkb,bh->hqk', onehot, tab, precision=lax.Precision.HIGHEST).astype(F32)
    band = jnp.abs(ki - N_SIDE - qi) <= N_SIDE
    variants = []
    for edge in range(4):
        ok = band
        if edge & 1:
            ok = ok & (ki >= N_SIDE)
        if edge & 2:
            ok = ok & (ki < Q_BLOCK + N_SIDE)
        variants.append(jnp.where(ok[None], bias, NEG_INF))
    return jnp.stack(variants)


def _attention(qkv, rel_bias):
    biases = [_band_bias(rel_bias, g, d) for g, (_, d) in enumerate(DILATION_PATTERNS)]
    slab = pltpu.VMEM((GROUP_WIDTH // LANES, SEQ, LANES), F32)
    pads = pltpu.VMEM((PAD_WAYS, 2, SEQ + 2 * N_SIDE, GROUP_WIDTH), BF16)
    out = pl.pallas_call(
        _attn_kernel,
        grid=(BATCH,),
        in_specs=[pl.BlockSpec((1,) + a.shape[1:], lambda b: (b, 0, 0, 0)) for a in qkv]
                 + [_full_spec(b) for b in biases],
        out_specs=pl.BlockSpec((1, SEQ, GROUP_WIDTH), lambda b: (b, 0, 0)),
        out_shape=jax.ShapeDtypeStruct((BATCH, SEQ, GROUP_WIDTH), BF16),
        scratch_shapes=[pads, slab, slab, slab],
        compiler_params=_cparams("parallel"),
        name="attn",
    )(*qkv, *biases)
    return out.reshape(TOKENS, GROUP_WIDTH)


def _mixout_kernel(h_ref, attn_ref, u_ref, uprev_ref, unext_ref, gate_ref,
                   pw_ref, ps_ref, wpa_ref, wpp_ref, wo_ref, out_ref):
    j = pl.program_id(0) % TILES_PER_SEQ
    y_attn = _dot(attn_ref[...], wpa_ref[...])

    u = u_ref[...]
    prev = jnp.where(j == 0, 0.0, uprev_ref[0])
    nxt = jnp.where(j == TILES_PER_SEQ - 1, 0.0, unext_ref[0])
    ext = jnp.concatenate([prev, u, nxt], axis=0)
    pos = j * ROW_TILE + lax.broadcasted_iota(I32, (ROW_TILE, 1), 0)
    mixed = []
    for gi, w in enumerate(POOL_WINDOWS):
        half = w // 2
        sl = slice(gi * POOL_GROUP_WIDTH, (gi + 1) * POOL_GROUP_WIDTH)
        run = ext[:, sl]
        span = 1
        while 2 * span < w:
            run = run[:-span] + run[span:]
            span *= 2
        lo = POOL_HALO - half
        acc = run[lo:lo + ROW_TILE] + run[lo + half:lo + half + ROW_TILE]
        cnt = (jnp.minimum(pos + half, SEQ) - jnp.maximum(pos - half, 0)).astype(F32)
        pooled = acc / cnt - u[:, sl]
        mixed.append(_dot(pooled.astype(BF16), pw_ref[gi]) * ps_ref[:, sl])
    y_pool = _dot(jnp.concatenate(mixed, axis=1).astype(BF16), wpp_ref[...])

    y = (gate_ref[:, 0:D_MODEL] * y_attn.astype(BF16)
         + gate_ref[:, D_MODEL:GATE_WIDTH] * y_pool.astype(BF16))
    out_ref[...] = h_ref[...] + _dot(y, wo_ref[...])


def _mix_out(h, attn, u, gates, pool_w, pool_scale, w_proj_attn, w_proj_pool, w_out):
    halo_blocks = ROW_TILE // POOL_HALO
    u3 = u.reshape(TOKENS // POOL_HALO, POOL_HALO, POOL_WIDTH)
    last = TOKENS // POOL_HALO - 1
    prev_spec = pl.BlockSpec((1, POOL_HALO, POOL_WIDTH),
                             lambda i: (jnp.maximum(i * halo_blocks - 1, 0), 0, 0))
    next_spec = pl.BlockSpec((1, POOL_HALO, POOL_WIDTH),
                             lambda i: (jnp.minimum((i + 1) * halo_blocks, last), 0, 0))
    weights = (pool_w, pool_scale, w_proj_attn, w_proj_pool, w_out)
    return pl.pallas_call(
        _mixout_kernel,
        grid=(N_TILES,),
        in_specs=[_row_spec(D_MODEL), _row_spec(GROUP_WIDTH),
                  _row_spec(POOL_WIDTH), prev_spec, next_spec, _row_spec(GATE_WIDTH)]
                 + [_full_spec(w) for w in weights],
        out_specs=_row_spec(D_MODEL),
        out_shape=jax.ShapeDtypeStruct((TOKENS, D_MODEL), F32),
        compiler_params=_cparams("parallel"),
        name="mixout",
    )(h, attn, u, u3, u3, gates, *weights)


def _router_kernel(h_ref, g_ref, wr_ref, xn_ref, ri_ref, rw_ref, cnt_ref):
    xn = _rms(h_ref[...], g_ref[...])
    hi = xn.astype(BF16)
    xn_ref[...] = hi
    lo = (xn - hi.astype(F32)).astype(BF16)
    both = _dot(hi, wr_ref[...])
    lg = both[:, 0:ROUTER_LANES] + (both[:, ROUTER_LANES:] + _dot(lo, wr_ref[:, 0:ROUTER_LANES]))

    lane = lax.broadcasted_iota(I32, (ROW_TILE, ROUTER_LANES), 1)
    lanef = lane.astype(F32)
    low = jnp.float32(-3.0e38)
    far = jnp.float32(ROUTER_LANES)
    first = lambda hit: jnp.min(jnp.where(hit, lanef, far), axis=-1, keepdims=True)

    is_group = lane < N_EXPERT_GROUPS
    gl = jnp.where(is_group, lg, low)
    gmax = jnp.max(gl, axis=-1, keepdims=True)
    gidx = first(gl == gmax).astype(I32)
    gden = jnp.sum(jnp.where(is_group, jnp.exp(gl - gmax), 0.0), axis=-1, keepdims=True)
    g_p = 1.0 / gden

    in_group = ((lane >= EXPERT_LANE0) & (lane < EXPERT_LANE0 + N_EXPERTS)
                & ((lane - EXPERT_LANE0) // EXPERTS_PER_GROUP == gidx))
    el = jnp.where(in_group, lg, low)
    t1 = jnp.max(el, axis=-1, keepdims=True)
    l1 = first(in_group & (el == t1))
    rest = in_group & (lanef != l1)
    el2 = jnp.where(rest, lg, low)
    t2 = jnp.max(el2, axis=-1, keepdims=True)
    l2 = first(rest & (el2 == t2))
    e2 = jnp.exp(t2 - t1)
    w1 = g_p * (1.0 / (1.0 + e2))
    w2 = g_p * (e2 / (1.0 + e2))

    hit1 = lanef == l1
    hit2 = lanef == l2
    onehot = (hit1 | hit2).astype(BF16)
    ri = lax.broadcasted_iota(I32, (ROW_TILE, ROW_TILE), 0)
    ci = lax.broadcasted_iota(I32, (ROW_TILE, ROW_TILE), 1)
    before = (ci < ri).astype(BF16)
    seen = _dot(before, onehot)
    r1 = jnp.sum(jnp.where(hit1, seen, 0.0), axis=-1, keepdims=True)
    r2 = jnp.sum(jnp.where(hit2, seen, 0.0), axis=-1, keepdims=True)

    packed = jnp.zeros((ROW_TILE, ROUTER_LANES), F32)
    for k, v in enumerate((l1, l2, r1, r2)):
        packed = jnp.where(lane == k, v, packed)
    ri_ref[...] = packed.astype(I32)
    rw_ref[...] = jnp.where(lane == 0, w1, jnp.where(lane == 1, w2, 0.0))
    counts = jnp.sum(onehot.astype(F32), axis=0, keepdims=True)
    cnt_ref[0] = jnp.broadcast_to(counts, (SUBLANES, ROUTER_LANES))


def _route(h, g, wr_split):
    return pl.pallas_call(
        _router_kernel,
        grid=(N_TILES,),
        in_specs=[_row_spec(D_MODEL), _full_spec(g), _full_spec(wr_split)],
        out_specs=[_row_spec(D_MODEL), _row_spec(ROUTER_LANES), _row_spec(ROUTER_LANES),
                   pl.BlockSpec((1, SUBLANES, ROUTER_LANES), lambda i: (i, 0, 0))],
        out_shape=[jax.ShapeDtypeStruct((TOKENS, D_MODEL), BF16),
                   jax.ShapeDtypeStruct((TOKENS, ROUTER_LANES), I32),
                   jax.ShapeDtypeStruct((TOKENS, ROUTER_LANES), F32),
                   jax.ShapeDtypeStruct((N_TILES, SUBLANES, ROUTER_LANES), F32)],
        compiler_params=_cparams("parallel"),
        name="router",
    )(h, g, wr_split)


def _slots(ri_ref, off_ref):
    ri = ri_ref[...]
    lane = lax.broadcasted_iota(I32, (ROW_TILE, ROUTER_LANES), 1)
    off = off_ref[0, 0:1, :]
    pick = lambda k: jnp.sum(jnp.where(lane == ri[:, k:k + 1], off, 0.0), axis=-1, keepdims=True)
    return (pick(0) + ri[:, 2:3].astype(F32), pick(1) + ri[:, 3:4].astype(F32))


def _as_rows(cols):
    eye = (lax.broadcasted_iota(I32, (ROW_TILE, ROW_TILE), 0)
           == lax.broadcasted_iota(I32, (ROW_TILE, ROW_TILE), 1))
    return [jnp.sum(jnp.where(eye, c, 0.0), axis=0, keepdims=True) for c in cols]


def _run_copy(tile, e, loff_s, c8_s, dst_s, buf_ref, slot, hbm_ref, sem, to_hbm):
    k = tile * N_EXPERTS + e
    n = pl.multiple_of(c8_s[k], RUN_ALIGN)
    vm = buf_ref.at[slot, pl.ds(pl.multiple_of(loff_s[k], RUN_ALIGN), n)]
    hb = hbm_ref.at[pl.ds(pl.multiple_of(dst_s[k], RUN_ALIGN), n)]
    src, dst = (vm, hb) if to_hbm else (hb, vm)
    return n, pltpu.make_async_copy(src, dst, sem.at[slot])


def _loop(lo, hi, fn, unroll=1):
    def body(e, carry):
        fn(e)
        return carry
    lax.fori_loop(lo, hi, body, 0, unroll=unroll)


def _wait_rows(n, buf_ref, slot, hbm_ref, sem, to_hbm):
    vm = buf_ref.at[slot, pl.ds(0, n)]
    hb = hbm_ref.at[pl.ds(0, n)]
    src, dst = (vm, hb) if to_hbm else (hb, vm)
    pltpu.make_async_copy(src, dst, sem.at[slot]).wait()


def _start(n, cp):
    @pl.when(n > 0)
    def _():
        cp.start()


def _wait(n, cp):
    @pl.when(n > 0)
    def _():
        cp.wait()


def _dispatch_kernel(loff_s, c8_s, dst_s, nslot_s, zdst_s, zcnt_s, nact_s,
                     xn_ref, ri_ref, rw_ref, off_ref, xpad_hbm, sorted_buf, zero_buf, sem, zsem):
    i = pl.program_id(0)
    slot = i % 2
    last = pl.num_programs(0) - 1

    def zero_copy(e):
        n = pl.multiple_of(zcnt_s[e], RUN_ALIGN)
        dst = xpad_hbm.at[pl.ds(pl.multiple_of(zdst_s[e], RUN_ALIGN), n)]
        return n, pltpu.make_async_copy(zero_buf.at[pl.ds(0, n)], dst, zsem)

    def tail_copy(b):
        dst = xpad_hbm.at[pl.ds(pl.multiple_of(b * MOE_BLOCK, MOE_BLOCK), MOE_BLOCK)]
        return pltpu.make_async_copy(zero_buf, dst, zsem)

    @pl.when(i == 0)
    def _():
        zero_buf[...] = jnp.zeros_like(zero_buf)
        _loop(0, N_EXPERTS, lambda e: _start(*zero_copy(e)))
        _loop(nact_s[0], N_MOE_BLOCKS, lambda b: tail_copy(b).start())

    s1, s2 = _slots(ri_ref, off_ref)
    rw = rw_ref[...]
    s1_row, s2_row, w1_row, w2_row = _as_rows([s1, s2, rw[:, 0:1], rw[:, 1:2]])
    xn = xn_ref[...]
    lane = lax.broadcasted_iota(I32, (SLOT_CHUNK, LANES), 1)
    for c in range(MAX_SLOTS // SLOT_CHUNK):
        @pl.when(c * SLOT_CHUNK < nslot_s[i])
        def _():
            rows = slice(c * SLOT_CHUNK, (c + 1) * SLOT_CHUNK)
            srow = (c * SLOT_CHUNK
                    + lax.broadcasted_iota(I32, (SLOT_CHUNK, ROW_TILE), 0)).astype(F32)
            hit1 = srow == s1_row
            hit2 = srow == s2_row
            xs = _dot((hit1 | hit2).astype(BF16), xn)
            sorted_buf[slot, rows, 0:HALF] = _pack_halves(xs[:, 0:HALF], xs[:, HALF:D_MODEL])
            ws = jnp.sum(jnp.where(hit1, w1_row, 0.0) + jnp.where(hit2, w2_row, 0.0),
                         axis=-1, keepdims=True)
            sorted_buf[slot, rows, HALF:X_WORDS] = jnp.where(lane == 0, _bits(ws), jnp.uint32(0))

    copy = lambda tile, sl: (lambda e: _run_copy(tile, e, loff_s, c8_s, dst_s, sorted_buf, sl,
                                                 xpad_hbm, sem, True))
    mine = copy(i, slot)
    _loop(0, N_EXPERTS, lambda e: _start(*mine(e)), unroll=RUN_UNROLL)
    tile_rows = lambda t: pl.multiple_of(nslot_s[t], RUN_ALIGN)

    @pl.when(i > 0)
    def _():
        _wait_rows(tile_rows(i - 1), sorted_buf, 1 - slot, xpad_hbm, sem, True)

    @pl.when(i == last)
    def _():
        _wait_rows(tile_rows(i), sorted_buf, slot, xpad_hbm, sem, True)
        _loop(0, N_EXPERTS, lambda e: _wait(*zero_copy(e)))
        _loop(nact_s[0], N_MOE_BLOCKS, lambda b: tail_copy(b).wait())


def _dispatch(tables, xn, route_i, route_w):
    tile_row = pl.BlockSpec((1, SUBLANES, ROUTER_LANES), lambda i, *_: (i, 0, 0))
    grid_spec = pltpu.PrefetchScalarGridSpec(
        num_scalar_prefetch=7,
        grid=(N_TILES,),
        in_specs=[_row_spec(D_MODEL), _row_spec(ROUTER_LANES), _row_spec(ROUTER_LANES), tile_row],
        out_specs=pl.BlockSpec(memory_space=pl.ANY),
        scratch_shapes=[pltpu.VMEM((2, MAX_SLOTS, X_WORDS), U32),
                        pltpu.VMEM((MOE_BLOCK, X_WORDS), U32),
                        pltpu.SemaphoreType.DMA((2,)),
                        pltpu.SemaphoreType.DMA(())],
    )
    return pl.pallas_call(
        _dispatch_kernel,
        grid_spec=grid_spec,
        out_shape=jax.ShapeDtypeStruct((PAD_ROWS, X_WORDS), U32),
        compiler_params=_cparams("arbitrary"),
        name="dispatch",
    )(tables["loff"], tables["c8"], tables["dst"], tables["nslot"], tables["zdst"], tables["zcnt"],
      tables["n_active"], xn, route_i, route_w, tables["offrow"])


def _combine_kernel(loff_s, c8_s, dst_s, nslot_s,
                    h_ref, ri_ref, off_ref, g_ref, ypad_hbm, out_ref,
                    ybuf, sem, *, final_norm):
    i = pl.program_id(0)
    slot = i % 2
    last = pl.num_programs(0) - 1
    fetch = lambda tile, sl: (lambda e: _run_copy(tile, e, loff_s, c8_s, dst_s, ybuf, sl,
                                                  ypad_hbm, sem, False))

    @pl.when(i == 0)
    def _():
        ybuf[...] = jnp.zeros_like(ybuf)
        first = fetch(i, slot)
        _loop(0, N_EXPERTS, lambda e: _start(*first(e)), unroll=RUN_UNROLL)

    @pl.when(i < last)
    def _():
        nxt = fetch(i + 1, 1 - slot)
        _loop(0, N_EXPERTS, lambda e: _start(*nxt(e)), unroll=RUN_UNROLL)

    _wait_rows(pl.multiple_of(nslot_s[i], RUN_ALIGN), ybuf, slot, ypad_hbm, sem, False)

    s1, s2 = _slots(ri_ref, off_ref)
    scol = lax.broadcasted_iota(I32, (ROW_TILE, MAX_SLOTS), 1).astype(F32)
    pick = ((scol == s1) | (scol == s2)).astype(BF16)
    y_hi, y_lo = _unpack_halves(ybuf[slot])
    h = h_ref[...] + jnp.concatenate([_dot(pick, y_hi), _dot(pick, y_lo)], axis=1)
    out_ref[...] = _rms(h, g_ref[...]) if final_norm else h


def _combine(tables, h, route_i, y_pad, g, final_norm):
    tile_row = pl.BlockSpec((1, SUBLANES, ROUTER_LANES), lambda i, *_: (i, 0, 0))
    grid_spec = pltpu.PrefetchScalarGridSpec(
        num_scalar_prefetch=4,
        grid=(N_TILES,),
        in_specs=[_row_spec(D_MODEL), _row_spec(ROUTER_LANES), tile_row, _full_spec(g),
                  pl.BlockSpec(memory_space=pl.ANY)],
        out_specs=_row_spec(D_MODEL),
        scratch_shapes=[pltpu.VMEM((2, MAX_SLOTS, HALF), U32),
                        pltpu.SemaphoreType.DMA((2,))],
    )
    return pl.pallas_call(
        functools.partial(_combine_kernel, final_norm=final_norm),
        grid_spec=grid_spec,
        out_shape=jax.ShapeDtypeStruct((TOKENS, D_MODEL), F32),
        compiler_params=_cparams("arbitrary"),
        name="combine",
    )(tables["loff"], tables["c8"], tables["dst"], tables["nslot"],
      h, route_i, tables["offrow"], g, y_pad)


def _expert_kernel(be_ref, slot_ref, next_ref, nact_ref, x_ref, wg_hbm, wu_hbm, wd_hbm, y_ref,
                   wg_f, wu_f, wd_f, wg_b, wu_b, wd_b, sem, *, layer):
    i = pl.program_id(0)
    staged = ((wg_hbm, wg_f, wg_b), (wu_hbm, wu_f, wu_b), (wd_hbm, wd_f, wd_b))

    def fetch(e, slot):
        return [pltpu.make_async_copy(hbm.at[layer, e], f32.at[slot], sem.at[slot, k])
                for k, (hbm, f32, _) in enumerate(staged)]

    @pl.when(i >= nact_ref[0])
    def _():
        y_ref[...] = jnp.zeros_like(y_ref)

    @pl.when(i < nact_ref[0])
    def _():
        e = be_ref[i]
        slot = slot_ref[i]

        def request(ahead, block):
            nxt = next_ref[(ahead - 1) * N_MOE_BLOCKS + block]

            @pl.when(nxt >= 0)
            def _():
                for cp in fetch(nxt, (slot + ahead) % W_STAGES):
                    cp.start()

        @pl.when((i == 0) | (e != be_ref[jnp.maximum(i - 1, 0)]))
        def _():
            @pl.when(i == 0)
            def _():
                for cp in fetch(e, slot):
                    cp.start()
                for ahead in range(1, W_STAGES - 1):
                    request(ahead, i)
            for cp in fetch(e, slot):
                cp.wait()
            request(W_STAGES - 1, i)
            for _, f32, b16 in staged:
                b16[...] = f32[slot].astype(BF16)

        x_hi, x_lo = _unpack_halves(x_ref[:, 0:HALF])
        row_w = lax.bitcast_convert_type(x_ref[:, HALF:HALF + 1], F32)
        gate = _dot(x_hi, wg_b[0:HALF, :]) + _dot(x_lo, wg_b[HALF:D_MODEL, :])
        up = _dot(x_hi, wu_b[0:HALF, :]) + _dot(x_lo, wu_b[HALF:D_MODEL, :])
        hmid = (jax.nn.silu(gate) * up).astype(BF16)
        y = (_dot(hmid, wd_b[...]) * row_w).astype(BF16).astype(F32)
        y_ref[...] = _pack_halves(y[:, 0:HALF], y[:, HALF:D_MODEL])


def _experts(layer, tables, x_pad, w_gate, w_up, w_down):
    hbm = pl.BlockSpec(memory_space=pl.ANY)
    up_shape, down_shape = (D_MODEL, D_EXPERT), (D_EXPERT, D_MODEL)
    grid_spec = pltpu.PrefetchScalarGridSpec(
        num_scalar_prefetch=4,
        grid=(N_MOE_BLOCKS,),
        in_specs=[pl.BlockSpec((MOE_BLOCK, X_WORDS),
                               lambda i, be, sl, nx, na: (jnp.minimum(i, na[0] - 1), 0)),
                  hbm, hbm, hbm],
        out_specs=pl.BlockSpec((MOE_BLOCK, HALF), lambda i, *_: (i, 0)),
        scratch_shapes=[pltpu.VMEM((W_STAGES,) + up_shape, F32),
                        pltpu.VMEM((W_STAGES,) + up_shape, F32),
                        pltpu.VMEM((W_STAGES,) + down_shape, F32),
                        pltpu.VMEM(up_shape, BF16), pltpu.VMEM(up_shape, BF16),
                        pltpu.VMEM(down_shape, BF16),
                        pltpu.SemaphoreType.DMA((W_STAGES, 3))],
    )
    return pl.pallas_call(
        functools.partial(_expert_kernel, layer=layer),
        grid_spec=grid_spec,
        out_shape=jax.ShapeDtypeStruct((PAD_ROWS, HALF), U32),
        compiler_params=_cparams("arbitrary"),
        name="experts",
    )(tables["blk_expert"], tables["blk_slot"], tables["blk_next"], tables["n_active"],
      x_pad, w_gate, w_up, w_down)


def _routing_tables(tile_counts):
    cnt = tile_counts[:, 0, EXPERT_LANE0:EXPERT_LANE0 + N_EXPERTS].astype(I32)
    c8 = (cnt + RUN_ALIGN - 1) // RUN_ALIGN * RUN_ALIGN
    loff = jnp.cumsum(c8, axis=1) - c8
    nslot = jnp.sum(c8, axis=1)
    tot = jnp.sum(c8, axis=0)
    padded = (tot + MOE_BLOCK - 1) // MOE_BLOCK * MOE_BLOCK
    end = jnp.cumsum(padded)
    base = end - padded
    dst = base[None, :] + jnp.cumsum(c8, axis=0) - c8
    blk_start = jnp.arange(N_MOE_BLOCKS, dtype=I32) * MOE_BLOCK
    blk_expert = jnp.minimum(jnp.sum((end[None, :] <= blk_start[:, None]).astype(I32), axis=1),
                             N_EXPERTS - 1).astype(I32)
    offrow = jnp.zeros((N_TILES, ROUTER_LANES), F32).at[:, EXPERT_LANE0:EXPERT_LANE0 + N_EXPERTS].set(
        loff.astype(F32))
    offrow = jnp.broadcast_to(offrow[:, None, :], (N_TILES, SUBLANES, ROUTER_LANES))
    experts = jnp.arange(N_EXPERTS, dtype=I32)
    present = padded > 0
    ordinal = jnp.cumsum(present.astype(I32)) - 1
    at_or_after = lax.cummin(jnp.where(present, experts, N_EXPERTS), reverse=True)
    after = jnp.concatenate([at_or_after[1:], jnp.full((1,), N_EXPERTS, I32)])
    hop = jnp.concatenate([after, jnp.full((1,), N_EXPERTS, I32)])
    pick = (blk_expert[:, None] == experts[None, :]).astype(I32)
    blk_slot = jnp.sum(pick * (ordinal % W_STAGES)[None, :], axis=1).astype(I32)
    ahead, blk_next = experts, []
    for _ in range(W_STAGES - 1):
        ahead = jnp.sum((ahead[:, None] == jnp.arange(N_EXPERTS + 1)[None, :]) * hop[None, :], axis=1)
        per_block = jnp.sum(pick * ahead[None, :], axis=1)
        blk_next.append(jnp.where(per_block >= N_EXPERTS, -1, per_block))
    blk_next = jnp.concatenate(blk_next).astype(I32)
    return {
        "blk_slot": blk_slot, "blk_next": blk_next,
        "loff": loff.reshape(-1).astype(I32), "c8": c8.reshape(-1).astype(I32),
        "dst": dst.reshape(-1).astype(I32), "nslot": nslot.astype(I32),
        "zdst": (base + tot).astype(I32), "zcnt": (padded - tot).astype(I32),
        "offrow": offrow, "blk_expert": blk_expert,
        "n_active": (end[-1:] // MOE_BLOCK).astype(I32),
    }


def _split_router_weights(w_router_group, w_router_expert):
    w_e = w_router_expert.transpose(1, 0, 2).reshape(D_MODEL, N_EXPERTS)
    w = jnp.concatenate([w_router_group, w_e], axis=1)
    w = jnp.pad(w, ((0, 0), (0, ROUTER_LANES - w.shape[1])))
    hi = w.astype(BF16)
    lo = (w - hi.astype(F32)).astype(BF16)
    return jnp.concatenate([hi, lo], axis=1)


def _input_weights(w):
    cols = []
    for g in range(N_GROUPS):
        for part in range(3):
            c0 = part * ATTN_WIDTH + g * GROUP_WIDTH
            blk = w[:, c0:c0 + GROUP_WIDTH]
            cols.append(blk * HEAD_DIM ** -0.5 if part == 0 else blk)
    cols.append(w[:, QKV_WIDTH:])
    return jnp.concatenate(cols, axis=1).astype(BF16)


def kernel(x, rel_bias, norm_mix_g, w_in, pool_w, pool_scale, w_proj_attn, w_proj_pool, w_out,
           norm_ffn_g, w_router_group, w_router_expert, w_gate_e, w_up_e, w_down_e, norm_final_g):
    h = x.reshape(TOKENS, D_MODEL)
    for l in range(DEPTH):
        qkv, u, gates = _project(h, norm_mix_g[l][None], _input_weights(w_in[l]))
        h = _mix_out(h, _attention(qkv, rel_bias), u, gates, pool_w[l].astype(BF16), pool_scale[l][None],
                     w_proj_attn[l].astype(BF16), w_proj_pool[l].astype(BF16), w_out[l].astype(BF16))
        xn, route_i, route_w, tile_counts = _route(
            h, norm_ffn_g[l][None], _split_router_weights(w_router_group[l], w_router_expert[l]))
        tables = _routing_tables(tile_counts)
        x_pad = _dispatch(tables, xn, route_i, route_w)
        y_pad = _experts(l, tables, x_pad, w_gate_e, w_up_e, w_down_e)
        h = _combine(tables, h, route_i, y_pad, norm_final_g[None], l == DEPTH - 1)
    return h.reshape(BATCH, SEQ, D_MODEL)
```

```python
import functools
import math

import jax
import jax.numpy as jnp
from jax import lax
from jax.experimental import pallas as pl
from jax.experimental.pallas import tpu as pltpu

F32 = jnp.float32
BF16 = jnp.bfloat16
I32 = jnp.int32
U32 = jnp.uint32

D_MODEL = 1024
BATCH = 8
SEQ = 2048
TOKENS = BATCH * SEQ
DEPTH = 2

HEAD_DIM = 64
HEADS_PER_GROUP = 4
GROUP_WIDTH = HEADS_PER_GROUP * HEAD_DIM
DILATION_PATTERNS = ((128, 1), (512, 4), (2048, 16))
N_GROUPS = len(DILATION_PATTERNS)
N_ATTN_HEADS = N_GROUPS * HEADS_PER_GROUP
ATTN_WIDTH = N_ATTN_HEADS * HEAD_DIM
QKV_WIDTH = 3 * ATTN_WIDTH
GROUP_QKV = 3 * GROUP_WIDTH
N_SIDE = 64
assert all(w // (2 * d) == N_SIDE for w, d in DILATION_PATTERNS)
POOL_WINDOWS = (2, 4, 8, 16)
POOL_GROUP_WIDTH = 128
POOL_WIDTH = len(POOL_WINDOWS) * POOL_GROUP_WIDTH
POOL_HALO = max(POOL_WINDOWS) // 2
N_BRANCHES = 2
GATE_WIDTH = N_BRANCHES * D_MODEL
IN_WIDTH = QKV_WIDTH + POOL_WIDTH + GATE_WIDTH
N_BUCKETS = 32
MAX_DISTANCE = 1024
N_EXPERT_GROUPS = 8
EXPERTS_PER_GROUP = 8
N_EXPERTS = N_EXPERT_GROUPS * EXPERTS_PER_GROUP
TOP_K = 2
D_EXPERT = 512
N_ASSIGN = TOKENS * TOP_K
EPS = 1e-6
NEG_INF = -1e30

LANES = 128
SUBLANES = 8
ROW_TILE = 512
N_TILES = TOKENS // ROW_TILE
TILES_PER_SEQ = SEQ // ROW_TILE
Q_BLOCK = 128
K_BLOCK = Q_BLOCK + 2 * N_SIDE
PAD_WAYS = 2
ROUTER_LANES = 128
EXPERT_LANE0 = N_EXPERT_GROUPS
VMEM_LIMIT = 56 * 1024 * 1024

RUN_ALIGN = SUBLANES
MOE_BLOCK = 336
HALF = D_MODEL // 2
X_WORDS = HALF + LANES
SLOT_CHUNK = 512
RUN_UNROLL = 4
W_STAGES = 3
MAX_SLOTS = -(-(TOP_K * ROW_TILE + N_EXPERTS * (RUN_ALIGN - 1)) // SLOT_CHUNK) * SLOT_CHUNK
N_MOE_BLOCKS = -(-(N_ASSIGN + N_TILES * N_EXPERTS * (RUN_ALIGN - 1)
                   + N_EXPERTS * (MOE_BLOCK - RUN_ALIGN)) // MOE_BLOCK)
PAD_ROWS = N_MOE_BLOCKS * MOE_BLOCK
HIGH_HALF = 0xFFFF0000


def _cparams(*sem):
    return pltpu.CompilerParams(dimension_semantics=sem, vmem_limit_bytes=VMEM_LIMIT)


def _rms(h, g):
    r = lax.rsqrt(jnp.mean(h * h, axis=-1, keepdims=True) + EPS)
    return (h * r) * g


def _dot(a, b):
    return jnp.dot(a, b, preferred_element_type=F32)


def _row_spec(width):
    return pl.BlockSpec((ROW_TILE, width), lambda i, *_: (i, 0))


def _full_spec(a):
    return pl.BlockSpec(a.shape, lambda i, *_: (0,) * a.ndim)


def _bits(x):
    return lax.bitcast_convert_type(x, U32)


def _pack_halves(a, b):
    return (_bits(a) & jnp.uint32(HIGH_HALF)) | (_bits(b) >> 16)


def _unpack_halves(words):
    hi = lax.bitcast_convert_type(words & jnp.uint32(HIGH_HALF), F32)
    lo = lax.bitcast_convert_type(words << 16, F32)
    return hi.astype(BF16), lo.astype(BF16)


def _proj_kernel(h_ref, g_ref, w_ref, q0_ref, q1_ref, q2_ref, u_ref, gate_ref, slabs):
    xn = _rms(h_ref[...], g_ref[...]).astype(BF16)
    n_slabs = GROUP_QKV // LANES
    for g, out_ref in enumerate((q0_ref, q1_ref, q2_ref)):
        dilation = DILATION_PATTERNS[g][1]
        res = _dot(xn, w_ref[:, g * GROUP_QKV:(g + 1) * GROUP_QKV])
        if dilation == 1:
            out_ref[0, 0] = res.astype(BF16)
            continue
        for s in range(n_slabs):
            slabs[s] = res[:, s * LANES:(s + 1) * LANES]
        n = ROW_TILE // dilation
        for r in range(dilation):
            rows = [slabs[s, pl.ds(r, n, stride=dilation), :] for s in range(n_slabs)]
            out_ref[0, r] = jnp.concatenate(rows, axis=1).astype(BF16)
    u_ref[...] = _dot(xn, w_ref[:, QKV_WIDTH:QKV_WIDTH + POOL_WIDTH])
    gates = _dot(xn, w_ref[:, QKV_WIDTH + POOL_WIDTH:IN_WIDTH])
    gate_ref[...] = jax.nn.sigmoid(gates).astype(BF16)


def _project(h, g, w_bf16):
    qkv_shapes, qkv_specs = [], []
    for _, d in DILATION_PATTERNS:
        qkv_shapes.append(jax.ShapeDtypeStruct((BATCH, d, SEQ // d, GROUP_QKV), BF16))
        qkv_specs.append(pl.BlockSpec((1, d, ROW_TILE // d, GROUP_QKV),
                                      lambda i: (i // TILES_PER_SEQ, 0, i % TILES_PER_SEQ, 0)))
    res = pl.pallas_call(
        _proj_kernel,
        grid=(N_TILES,),
        in_specs=[_row_spec(D_MODEL), _full_spec(g), _full_spec(w_bf16)],
        out_specs=qkv_specs + [_row_spec(POOL_WIDTH), _row_spec(GATE_WIDTH)],
        out_shape=qkv_shapes + [jax.ShapeDtypeStruct((TOKENS, POOL_WIDTH), F32),
                                jax.ShapeDtypeStruct((TOKENS, GATE_WIDTH), BF16)],
        scratch_shapes=[pltpu.VMEM((GROUP_QKV // LANES, ROW_TILE, LANES), F32)],
        compiler_params=_cparams("parallel"),
        name="proj",
    )(h, g, w_bf16)
    return res[:N_GROUPS], res[N_GROUPS], res[N_GROUPS + 1]


def _attn_kernel(q0_ref, q1_ref, q2_ref, b0_ref, b1_ref, b2_ref, o_ref, pads, num, den, top):
    zpad = jnp.zeros((N_SIDE, GROUP_WIDTH), BF16)
    groups = ((q2_ref, b2_ref, 2), (q1_ref, b1_ref, 1), (q0_ref, b0_ref, 0))
    for order, (qkv_ref, bias_ref, g) in enumerate(groups):
        dilation = DILATION_PATTERNS[g][1]
        sub_len = SEQ // dilation
        for way in range(PAD_WAYS):
            for kv in range(2):
                pads[way, kv, 0:N_SIDE, :] = zpad
                pads[way, kv, N_SIDE + sub_len:2 * N_SIDE + sub_len, :] = zpad
        _attn_group(qkv_ref, bias_ref, pads, num, den, top, dilation=dilation, first=order == 0)

    def finish(c, carry):
        rows = pl.ds(pl.multiple_of(c * ROW_TILE, ROW_TILE), ROW_TILE)
        merged = [num[half, rows, :] / den[half, rows, :] for half in range(GROUP_WIDTH // LANES)]
        o_ref[0, rows, :] = jnp.concatenate(merged, axis=1).astype(BF16)
        return carry

    lax.fori_loop(0, SEQ // ROW_TILE, finish, 0)


def _attn_group(qkv_ref, bias_ref, pads, num, den, top, *, dilation, first):
    sub_len = SEQ // dilation
    n_blocks = sub_len // Q_BLOCK
    head_of_lane = lax.broadcasted_iota(I32, (1, GROUP_WIDTH), 1) // HEAD_DIM

    def by_head(cols):
        out = cols[HEADS_PER_GROUP - 1]
        for h in range(HEADS_PER_GROUP - 2, -1, -1):
            out = jnp.where(head_of_lane == h, cols[h], out)
        return out

    def load_keys(r, way):
        pads[way, 0, N_SIDE:N_SIDE + sub_len, :] = qkv_ref[0, r, :, GROUP_WIDTH:2 * GROUP_WIDTH]
        pads[way, 1, N_SIDE:N_SIDE + sub_len, :] = qkv_ref[0, r, :, 2 * GROUP_WIDTH:3 * GROUP_WIDTH]

    def block(r, way, i):
        if True:
            r0 = i * Q_BLOCK if isinstance(i, int) else pl.multiple_of(i * Q_BLOCK, Q_BLOCK)
            qb = qkv_ref[0, r, pl.ds(r0, Q_BLOCK), 0:GROUP_WIDTH]
            kw = pads[way, 0, pl.ds(r0, K_BLOCK), :]
            vw = pads[way, 1, pl.ds(r0, K_BLOCK), :]
            zero = jnp.zeros_like(qb)
            q_heads = jnp.concatenate(
                [jnp.where(head_of_lane == h, qb, zero) for h in range(HEADS_PER_GROUP)], axis=0)
            s = lax.dot_general(q_heads, kw, (((1,), (1,)), ((), ())), preferred_element_type=F32)
            if isinstance(i, int):
                edge = int(i == 0) + 2 * int(i == n_blocks - 1)
            else:
                edge = (i == 0).astype(I32) + 2 * (i == n_blocks - 1).astype(I32)
            s = s.reshape(HEADS_PER_GROUP, Q_BLOCK, K_BLOCK) + bias_ref[edge]
            m = jnp.max(s, axis=-1, keepdims=True)
            p = jnp.exp(s - m)
            psum = jnp.sum(p, axis=-1, keepdims=True)
            pb = p.astype(BF16)
            p_cat = jnp.concatenate([pb[h] for h in range(HEADS_PER_GROUP)], axis=1)
            zv = jnp.zeros_like(vw)
            v_heads = jnp.concatenate(
                [jnp.where(head_of_lane == h, vw, zv) for h in range(HEADS_PER_GROUP)], axis=0)
            o = _dot(p_cat, v_heads)
            lse = m + jnp.log(psum)
            o = o / by_head([psum[h] for h in range(HEADS_PER_GROUP)])
            lse_lanes = by_head([lse[h] for h in range(HEADS_PER_GROUP)])
            if dilation == 1:
                rows = pl.ds(r0, Q_BLOCK)
            else:
                rows = pl.ds(r + dilation * r0, Q_BLOCK, stride=dilation)
            for half in range(GROUP_WIDTH // LANES):
                lanes = slice(half * LANES, (half + 1) * LANES)
                if first:
                    num[half, rows, :] = o[:, lanes]
                    den[half, rows, :] = jnp.ones((Q_BLOCK, LANES), F32)
                    top[half, rows, :] = lse_lanes[:, lanes]
                else:
                    old = top[half, rows, :]
                    new = jnp.maximum(old, lse_lanes[:, lanes])
                    keep = jnp.exp(old - new)
                    add = jnp.exp(lse_lanes[:, lanes] - new)
                    num[half, rows, :] = keep * num[half, rows, :] + add * o[:, lanes]
                    den[half, rows, :] = keep * den[half, rows, :] + add
                    top[half, rows, :] = new

    def blocks_of(r, way):
        if n_blocks == 1:
            block(r, way, 0)
        else:
            def two(j, carry):
                block(r, way, 2 * j)
                block(r, way, 2 * j + 1)
                return carry
            lax.fori_loop(0, n_blocks // 2, two, 0)

    if dilation == 1:
        load_keys(0, 0)
        blocks_of(0, 0)
    else:
        def residues(j, carry):
            for way in range(PAD_WAYS):
                load_keys(PAD_WAYS * j + way, way)
            for way in range(PAD_WAYS):
                blocks_of(PAD_WAYS * j + way, way)
            return carry
        lax.fori_loop(0, dilation // PAD_WAYS, residues, 0)


def _t5_bucket(rel):
    nb = N_BUCKETS // 2
    ret = jnp.where(rel > 0, nb, 0)
    n = jnp.abs(rel)
    max_exact = nb // 2
    nf = jnp.maximum(n, max_exact).astype(F32)
    large = max_exact + (jnp.log(nf / max_exact) / math.log(MAX_DISTANCE / max_exact)
                         * (nb - max_exact)).astype(I32)
    large = jnp.minimum(large, nb - 1)
    return ret + jnp.where(n < max_exact, n, large)


def _band_bias(rel_bias, group, dilation):
    qi = jnp.arange(Q_BLOCK)[:, None]
    ki = jnp.arange(K_BLOCK)[None, :]
    bucket = _t5_bucket((ki - N_SIDE - qi) * dilation)
    tab = rel_bias[:, group * HEADS_PER_GROUP:(group + 1) * HEADS_PER_GROUP]
    onehot = (bucket[:, :, None] == jnp.arange(N_BUCKETS)[None, None, :]).astype(F32)
    bias = jnp.einsum('qkb,bh->hqk', onehot, tab, precision=lax.Precision.HIGHEST).astype(F32)
    band = jnp.abs(ki - N_SIDE - qi) <= N_SIDE
    variants = []
    for edge in range(4):
        ok = band
        if edge & 1:
            ok = ok & (ki >= N_SIDE)
        if edge & 2:
            ok = ok & (ki < Q_BLOCK + N_SIDE)
        variants.append(jnp.where(ok[None], bias, NEG_INF))
    return jnp.stack(variants)


def _attention(qkv, rel_bias):
    biases = [_band_bias(rel_bias, g, d) for g, (_, d) in enumerate(DILATION_PATTERNS)]
    slab = pltpu.VMEM((GROUP_WIDTH // LANES, SEQ, LANES), F32)
    pads = pltpu.VMEM((PAD_WAYS, 2, SEQ + 2 * N_SIDE, GROUP_WIDTH), BF16)
    out = pl.pallas_call(
        _attn_kernel,
        grid=(BATCH,),
        in_specs=[pl.BlockSpec((1,) + a.shape[1:], lambda b: (b, 0, 0, 0)) for a in qkv]
                 + [_full_spec(b) for b in biases],
        out_specs=pl.BlockSpec((1, SEQ, GROUP_WIDTH), lambda b: (b, 0, 0)),
        out_shape=jax.ShapeDtypeStruct((BATCH, SEQ, GROUP_WIDTH), BF16),
        scratch_shapes=[pads, slab, slab, slab],
        compiler_params=_cparams("parallel"),
        name="attn",
    )(*qkv, *biases)
    return out.reshape(TOKENS, GROUP_WIDTH)


def _mixout_kernel(h_ref, attn_ref, u_ref, uprev_ref, unext_ref, gate_ref,
                   pw_ref, ps_ref, wpa_ref, wpp_ref, wo_ref, out_ref):
    j = pl.program_id(0) % TILES_PER_SEQ
    y_attn = _dot(attn_ref[...], wpa_ref[...])

    u = u_ref[...]
    prev = jnp.where(j == 0, 0.0, uprev_ref[0])
    nxt = jnp.where(j == TILES_PER_SEQ - 1, 0.0, unext_ref[0])
    ext = jnp.concatenate([prev, u, nxt], axis=0)
    pos = j * ROW_TILE + lax.broadcasted_iota(I32, (ROW_TILE, 1), 0)
    mixed = []
    for gi, w in enumerate(POOL_WINDOWS):
        half = w // 2
        sl = slice(gi * POOL_GROUP_WIDTH, (gi + 1) * POOL_GROUP_WIDTH)
        run = ext[:, sl]
        span = 1
        while 2 * span < w:
            run = run[:-span] + run[span:]
            span *= 2
        lo = POOL_HALO - half
        acc = run[lo:lo + ROW_TILE] + run[lo + half:lo + half + ROW_TILE]
        cnt = (jnp.minimum(pos + half, SEQ) - jnp.maximum(pos - half, 0)).astype(F32)
        pooled = acc / cnt - u[:, sl]
        mixed.append(_dot(pooled.astype(BF16), pw_ref[gi]) * ps_ref[:, sl])
    y_pool = _dot(jnp.concatenate(mixed, axis=1).astype(BF16), wpp_ref[...])

    y = (gate_ref[:, 0:D_MODEL] * y_attn.astype(BF16)
         + gate_ref[:, D_MODEL:GATE_WIDTH] * y_pool.astype(BF16))
    out_ref[...] = h_ref[...] + _dot(y, wo_ref[...])


def _mix_out(h, attn, u, gates, pool_w, pool_scale, w_proj_attn, w_proj_pool, w_out):
    halo_blocks = ROW_TILE // POOL_HALO
    u3 = u.reshape(TOKENS // POOL_HALO, POOL_HALO, POOL_WIDTH)
    last = TOKENS // POOL_HALO - 1
    prev_spec = pl.BlockSpec((1, POOL_HALO, POOL_WIDTH),
                             lambda i: (jnp.maximum(i * halo_blocks - 1, 0), 0, 0))
    next_spec = pl.BlockSpec((1, POOL_HALO, POOL_WIDTH),
                             lambda i: (jnp.minimum((i + 1) * halo_blocks, last), 0, 0))
    weights = (pool_w, pool_scale, w_proj_attn, w_proj_pool, w_out)
    return pl.pallas_call(
        _mixout_kernel,
        grid=(N_TILES,),
        in_specs=[_row_spec(D_MODEL), _row_spec(GROUP_WIDTH),
                  _row_spec(POOL_WIDTH), prev_spec, next_spec, _row_spec(GATE_WIDTH)]
                 + [_full_spec(w) for w in weights],
        out_specs=_row_spec(D_MODEL),
        out_shape=jax.ShapeDtypeStruct((TOKENS, D_MODEL), F32),
        compiler_params=_cparams("parallel"),
        name="mixout",
    )(h, attn, u, u3, u3, gates, *weights)


def _router_kernel(h_ref, g_ref, wr_ref, xn_ref, ri_ref, rw_ref, cnt_ref):
    xn = _rms(h_ref[...], g_ref[...])
    hi = xn.astype(BF16)
    xn_ref[...] = hi
    lo = (xn - hi.astype(F32)).astype(BF16)
    both = _dot(hi, wr_ref[...])
    lg = both[:, 0:ROUTER_LANES] + (both[:, ROUTER_LANES:] + _dot(lo, wr_ref[:, 0:ROUTER_LANES]))

    lane = lax.broadcasted_iota(I32, (ROW_TILE, ROUTER_LANES), 1)
    lanef = lane.astype(F32)
    low = jnp.float32(-3.0e38)
    far = jnp.float32(ROUTER_LANES)
    first = lambda hit: jnp.min(jnp.where(hit, lanef, far), axis=-1, keepdims=True)

    is_group = lane < N_EXPERT_GROUPS
    gl = jnp.where(is_group, lg, low)
    gmax = jnp.max(gl, axis=-1, keepdims=True)
    gidx = first(gl == gmax).astype(I32)
    gden = jnp.sum(jnp.where(is_group, jnp.exp(gl - gmax), 0.0), axis=-1, keepdims=True)
    g_p = 1.0 / gden

    in_group = ((lane >= EXPERT_LANE0) & (lane < EXPERT_LANE0 + N_EXPERTS)
                & ((lane - EXPERT_LANE0) // EXPERTS_PER_GROUP == gidx))
    el = jnp.where(in_group, lg, low)
    t1 = jnp.max(el, axis=-1, keepdims=True)
    l1 = first(in_group & (el == t1))
    rest = in_group & (lanef != l1)
    el2 = jnp.where(rest, lg, low)
    t2 = jnp.max(el2, axis=-1, keepdims=True)
    l2 = first(rest & (el2 == t2))
    e2 = jnp.exp(t2 - t1)
    w1 = g_p * (1.0 / (1.0 + e2))
    w2 = g_p * (e2 / (1.0 + e2))

    hit1 = lanef == l1
    hit2 = lanef == l2
    onehot = (hit1 | hit2).astype(BF16)
    ri = lax.broadcasted_iota(I32, (ROW_TILE, ROW_TILE), 0)
    ci = lax.broadcasted_iota(I32, (ROW_TILE, ROW_TILE), 1)
    before = (ci < ri).astype(BF16)
    seen = _dot(before, onehot)
    r1 = jnp.sum(jnp.where(hit1, seen, 0.0), axis=-1, keepdims=True)
    r2 = jnp.sum(jnp.where(hit2, seen, 0.0), axis=-1, keepdims=True)

    packed = jnp.zeros((ROW_TILE, ROUTER_LANES), F32)
    for k, v in enumerate((l1, l2, r1, r2)):
        packed = jnp.where(lane == k, v, packed)
    ri_ref[...] = packed.astype(I32)
    rw_ref[...] = jnp.where(lane == 0, w1, jnp.where(lane == 1, w2, 0.0))
    counts = jnp.sum(onehot.astype(F32), axis=0, keepdims=True)
    cnt_ref[0] = jnp.broadcast_to(counts, (SUBLANES, ROUTER_LANES))


def _route(h, g, wr_split):
    return pl.pallas_call(
        _router_kernel,
        grid=(N_TILES,),
        in_specs=[_row_spec(D_MODEL), _full_spec(g), _full_spec(wr_split)],
        out_specs=[_row_spec(D_MODEL), _row_spec(ROUTER_LANES), _row_spec(ROUTER_LANES),
                   pl.BlockSpec((1, SUBLANES, ROUTER_LANES), lambda i: (i, 0, 0))],
        out_shape=[jax.ShapeDtypeStruct((TOKENS, D_MODEL), BF16),
                   jax.ShapeDtypeStruct((TOKENS, ROUTER_LANES), I32),
                   jax.ShapeDtypeStruct((TOKENS, ROUTER_LANES), F32),
                   jax.ShapeDtypeStruct((N_TILES, SUBLANES, ROUTER_LANES), F32)],
        compiler_params=_cparams("parallel"),
        name="router",
    )(h, g, wr_split)


def _slots(ri_ref, off_ref):
    ri = ri_ref[...]
    lane = lax.broadcasted_iota(I32, (ROW_TILE, ROUTER_LANES), 1)
    off = off_ref[0, 0:1, :]
    pick = lambda k: jnp.sum(jnp.where(lane == ri[:, k:k + 1], off, 0.0), axis=-1, keepdims=True)
    return (pick(0) + ri[:, 2:3].astype(F32), pick(1) + ri[:, 3:4].astype(F32))


def _as_rows(cols):
    eye = (lax.broadcasted_iota(I32, (ROW_TILE, ROW_TILE), 0)
           == lax.broadcasted_iota(I32, (ROW_TILE, ROW_TILE), 1))
    return [jnp.sum(jnp.where(eye, c, 0.0), axis=0, keepdims=True) for c in cols]


def _run_copy(tile, e, loff_s, c8_s, dst_s, buf_ref, slot, hbm_ref, sem, to_hbm):
    k = tile * N_EXPERTS + e
    n = pl.multiple_of(c8_s[k], RUN_ALIGN)
    vm = buf_ref.at[slot, pl.ds(pl.multiple_of(loff_s[k], RUN_ALIGN), n)]
    hb = hbm_ref.at[pl.ds(pl.multiple_of(dst_s[k], RUN_ALIGN), n)]
    src, dst = (vm, hb) if to_hbm else (hb, vm)
    return n, pltpu.make_async_copy(src, dst, sem.at[slot])


def _loop(lo, hi, fn, unroll=1):
    def body(e, carry):
        fn(e)
        return carry
    lax.fori_loop(lo, hi, body, 0, unroll=unroll)


def _wait_rows(n, buf_ref, slot, hbm_ref, sem, to_hbm):
    vm = buf_ref.at[slot, pl.ds(0, n)]
    hb = hbm_ref.at[pl.ds(0, n)]
    src, dst = (vm, hb) if to_hbm else (hb, vm)
    pltpu.make_async_copy(src, dst, sem.at[slot]).wait()


def _start(n, cp):
    @pl.when(n > 0)
    def _():
        cp.start()


def _wait(n, cp):
    @pl.when(n > 0)
    def _():
        cp.wait()


def _dispatch_kernel(loff_s, c8_s, dst_s, nslot_s, zdst_s, zcnt_s, nact_s,
                     xn_ref, ri_ref, rw_ref, off_ref, xpad_hbm, sorted_buf, zero_buf, sem, zsem):
    i = pl.program_id(0)
    slot = i % 2
    last = pl.num_programs(0) - 1

    def zero_copy(e):
        n = pl.multiple_of(zcnt_s[e], RUN_ALIGN)
        dst = xpad_hbm.at[pl.ds(pl.multiple_of(zdst_s[e], RUN_ALIGN), n)]
        return n, pltpu.make_async_copy(zero_buf.at[pl.ds(0, n)], dst, zsem)

    def tail_copy(b):
        dst = xpad_hbm.at[pl.ds(pl.multiple_of(b * MOE_BLOCK, MOE_BLOCK), MOE_BLOCK)]
        return pltpu.make_async_copy(zero_buf, dst, zsem)

    @pl.when(i == 0)
    def _():
        zero_buf[...] = jnp.zeros_like(zero_buf)
        _loop(0, N_EXPERTS, lambda e: _start(*zero_copy(e)))
        _loop(nact_s[0], N_MOE_BLOCKS, lambda b: tail_copy(b).start())

    s1, s2 = _slots(ri_ref, off_ref)
    rw = rw_ref[...]
    s1_row, s2_row, w1_row, w2_row = _as_rows([s1, s2, rw[:, 0:1], rw[:, 1:2]])
    xn = xn_ref[...]
    lane = lax.broadcasted_iota(I32, (SLOT_CHUNK, LANES), 1)
    for c in range(MAX_SLOTS // SLOT_CHUNK):
        @pl.when(c * SLOT_CHUNK < nslot_s[i])
        def _():
            rows = slice(c * SLOT_CHUNK, (c + 1) * SLOT_CHUNK)
            srow = (c * SLOT_CHUNK
                    + lax.broadcasted_iota(I32, (SLOT_CHUNK, ROW_TILE), 0)).astype(F32)
            hit1 = srow == s1_row
            hit2 = srow == s2_row
            xs = _dot((hit1 | hit2).astype(BF16), xn)
            sorted_buf[slot, rows, 0:HALF] = _pack_halves(xs[:, 0:HALF], xs[:, HALF:D_MODEL])
            ws = jnp.sum(jnp.where(hit1, w1_row, 0.0) + jnp.where(hit2, w2_row, 0.0),
                         axis=-1, keepdims=True)
            sorted_buf[slot, rows, HALF:X_WORDS] = jnp.where(lane == 0, _bits(ws), jnp.uint32(0))

    copy = lambda tile, sl: (lambda e: _run_copy(tile, e, loff_s, c8_s, dst_s, sorted_buf, sl,
                                                 xpad_hbm, sem, True))
    mine = copy(i, slot)
    _loop(0, N_EXPERTS, lambda e: _start(*mine(e)), unroll=RUN_UNROLL)
    tile_rows = lambda t: pl.multiple_of(nslot_s[t], RUN_ALIGN)

    @pl.when(i > 0)
    def _():
        _wait_rows(tile_rows(i - 1), sorted_buf, 1 - slot, xpad_hbm, sem, True)

    @pl.when(i == last)
    def _():
        _wait_rows(tile_rows(i), sorted_buf, slot, xpad_hbm, sem, True)
        _loop(0, N_EXPERTS, lambda e: _wait(*zero_copy(e)))
        _loop(nact_s[0], N_MOE_BLOCKS, lambda b: tail_copy(b).wait())


def _dispatch(tables, xn, route_i, route_w):
    tile_row = pl.BlockSpec((1, SUBLANES, ROUTER_LANES), lambda i, *_: (i, 0, 0))
    grid_spec = pltpu.PrefetchScalarGridSpec(
        num_scalar_prefetch=7,
        grid=(N_TILES,),
        in_specs=[_row_spec(D_MODEL), _row_spec(ROUTER_LANES), _row_spec(ROUTER_LANES), tile_row],
        out_specs=pl.BlockSpec(memory_space=pl.ANY),
        scratch_shapes=[pltpu.VMEM((2, MAX_SLOTS, X_WORDS), U32),
                        pltpu.VMEM((MOE_BLOCK, X_WORDS), U32),
                        pltpu.SemaphoreType.DMA((2,)),
                        pltpu.SemaphoreType.DMA(())],
    )
    return pl.pallas_call(
        _dispatch_kernel,
        grid_spec=grid_spec,
        out_shape=jax.ShapeDtypeStruct((PAD_ROWS, X_WORDS), U32),
        compiler_params=_cparams("arbitrary"),
        name="dispatch",
    )(tables["loff"], tables["c8"], tables["dst"], tables["nslot"], tables["zdst"], tables["zcnt"],
      tables["n_active"], xn, route_i, route_w, tables["offrow"])


def _combine_kernel(loff_s, c8_s, dst_s, nslot_s,
                    h_ref, ri_ref, off_ref, g_ref, ypad_hbm, out_ref,
                    ybuf, sem, *, final_norm):
    i = pl.program_id(0)
    slot = i % 2
    last = pl.num_programs(0) - 1
    fetch = lambda tile, sl: (lambda e: _run_copy(tile, e, loff_s, c8_s, dst_s, ybuf, sl,
                                                  ypad_hbm, sem, False))

    @pl.when(i == 0)
    def _():
        ybuf[...] = jnp.zeros_like(ybuf)
        first = fetch(i, slot)
        _loop(0, N_EXPERTS, lambda e: _start(*first(e)), unroll=RUN_UNROLL)

    @pl.when(i < last)
    def _():
        nxt = fetch(i + 1, 1 - slot)
        _loop(0, N_EXPERTS, lambda e: _start(*nxt(e)), unroll=RUN_UNROLL)

    _wait_rows(pl.multiple_of(nslot_s[i], RUN_ALIGN), ybuf, slot, ypad_hbm, sem, False)

    s1, s2 = _slots(ri_ref, off_ref)
    scol = lax.broadcasted_iota(I32, (ROW_TILE, MAX_SLOTS), 1).astype(F32)
    pick = ((scol == s1) | (scol == s2)).astype(BF16)
    y_hi, y_lo = _unpack_halves(ybuf[slot])
    h = h_ref[...] + jnp.concatenate([_dot(pick, y_hi), _dot(pick, y_lo)], axis=1)
    out_ref[...] = _rms(h, g_ref[...]) if final_norm else h


def _combine(tables, h, route_i, y_pad, g, final_norm):
    tile_row = pl.BlockSpec((1, SUBLANES, ROUTER_LANES), lambda i, *_: (i, 0, 0))
    grid_spec = pltpu.PrefetchScalarGridSpec(
        num_scalar_prefetch=4,
        grid=(N_TILES,),
        in_specs=[_row_spec(D_MODEL), _row_spec(ROUTER_LANES), tile_row, _full_spec(g),
                  pl.BlockSpec(memory_space=pl.ANY)],
        out_specs=_row_spec(D_MODEL),
        scratch_shapes=[pltpu.VMEM((2, MAX_SLOTS, HALF), U32),
                        pltpu.SemaphoreType.DMA((2,))],
    )
    return pl.pallas_call(
        functools.partial(_combine_kernel, final_norm=final_norm),
        grid_spec=grid_spec,
        out_shape=jax.ShapeDtypeStruct((TOKENS, D_MODEL), F32),
        compiler_params=_cparams("arbitrary"),
        name="combine",
    )(tables["loff"], tables["c8"], tables["dst"], tables["nslot"],
      h, route_i, tables["offrow"], g, y_pad)


def _expert_kernel(be_ref, slot_ref, next_ref, nact_ref, x_ref, wg_hbm, wu_hbm, wd_hbm, y_hbm,
                   wg_f, wu_f, wd_f, wg_b, wu_b, wd_b, ybuf, zero_buf, sem, ysem, zsem, *, layer):
    i = pl.program_id(0)
    n_active = nact_ref[0]
    staged = ((wg_hbm, wg_f, wg_b), (wu_hbm, wu_f, wu_b), (wd_hbm, wd_f, wd_b))

    def fetch(e, slot):
        return [pltpu.make_async_copy(hbm.at[layer, e], f32.at[slot], sem.at[slot, k])
                for k, (hbm, f32, _) in enumerate(staged)]

    def block_rows(b):
        return y_hbm.at[pl.ds(pl.multiple_of(b * MOE_BLOCK, MOE_BLOCK), MOE_BLOCK)]

    def put(b):
        return pltpu.make_async_copy(ybuf.at[b % 2], block_rows(b), ysem.at[b % 2])

    def tail_copy(b):
        return pltpu.make_async_copy(zero_buf, block_rows(b), zsem)

    @pl.when(i == 0)
    def _():
        zero_buf[...] = jnp.zeros_like(zero_buf)
        _loop(n_active, N_MOE_BLOCKS, lambda b: tail_copy(b).start())

    @pl.when(i == pl.num_programs(0) - 1)
    def _():
        _loop(n_active, N_MOE_BLOCKS, lambda b: tail_copy(b).wait())

    @pl.when(i < n_active)
    def _():
        e = be_ref[i]
        slot = slot_ref[i]

        def request(ahead, block):
            nxt = next_ref[(ahead - 1) * N_MOE_BLOCKS + block]

            @pl.when(nxt >= 0)
            def _():
                for cp in fetch(nxt, (slot + ahead) % W_STAGES):
                    cp.start()

        @pl.when((i == 0) | (e != be_ref[jnp.maximum(i - 1, 0)]))
        def _():
            @pl.when(i == 0)
            def _():
                for cp in fetch(e, slot):
                    cp.start()
                for ahead in range(1, W_STAGES - 1):
                    request(ahead, i)
            for cp in fetch(e, slot):
                cp.wait()
            request(W_STAGES - 1, i)
            for _, f32, b16 in staged:
                b16[...] = f32[slot].astype(BF16)

        x_hi, x_lo = _unpack_halves(x_ref[:, 0:HALF])
        row_w = lax.bitcast_convert_type(x_ref[:, HALF:HALF + 1], F32)
        gate = _dot(x_hi, wg_b[0:HALF, :]) + _dot(x_lo, wg_b[HALF:D_MODEL, :])
        up = _dot(x_hi, wu_b[0:HALF, :]) + _dot(x_lo, wu_b[HALF:D_MODEL, :])
        hmid = (jax.nn.silu(gate) * up).astype(BF16)
        y = (_dot(hmid, wd_b[...]) * row_w).astype(BF16).astype(F32)
        ybuf[i % 2] = _pack_halves(y[:, 0:HALF], y[:, HALF:D_MODEL])
        put(i).start()

        @pl.when(i > 0)
        def _():
            put(i - 1).wait()

        @pl.when(i == n_active - 1)
        def _():
            put(i).wait()


def _experts(layer, tables, x_pad, w_gate, w_up, w_down):
    hbm = pl.BlockSpec(memory_space=pl.ANY)
    up_shape, down_shape = (D_MODEL, D_EXPERT), (D_EXPERT, D_MODEL)
    grid_spec = pltpu.PrefetchScalarGridSpec(
        num_scalar_prefetch=4,
        grid=(N_MOE_BLOCKS,),
        in_specs=[pl.BlockSpec((MOE_BLOCK, X_WORDS),
                               lambda i, be, sl, nx, na: (jnp.minimum(i, na[0] - 1), 0)),
                  hbm, hbm, hbm],
        out_specs=hbm,
        scratch_shapes=[pltpu.VMEM((W_STAGES,) + up_shape, F32),
                        pltpu.VMEM((W_STAGES,) + up_shape, F32),
                        pltpu.VMEM((W_STAGES,) + down_shape, F32),
                        pltpu.VMEM(up_shape, BF16), pltpu.VMEM(up_shape, BF16),
                        pltpu.VMEM(down_shape, BF16),
                        pltpu.VMEM((2, MOE_BLOCK, HALF), U32),
                        pltpu.VMEM((MOE_BLOCK, HALF), U32),
                        pltpu.SemaphoreType.DMA((W_STAGES, 3)),
                        pltpu.SemaphoreType.DMA((2,)),
                        pltpu.SemaphoreType.DMA(())],
    )
    return pl.pallas_call(
        functools.partial(_expert_kernel, layer=layer),
        grid_spec=grid_spec,
        out_shape=jax.ShapeDtypeStruct((PAD_ROWS, HALF), U32),
        compiler_params=_cparams("arbitrary"),
        name="experts",
    )(tables["blk_expert"], tables["blk_slot"], tables["blk_next"], tables["n_active"],
      x_pad, w_gate, w_up, w_down)


def _routing_tables(tile_counts):
    cnt = tile_counts[:, 0, EXPERT_LANE0:EXPERT_LANE0 + N_EXPERTS].astype(I32)
    c8 = (cnt + RUN_ALIGN - 1) // RUN_ALIGN * RUN_ALIGN
    loff = jnp.cumsum(c8, axis=1) - c8
    nslot = jnp.sum(c8, axis=1)
    tot = jnp.sum(c8, axis=0)
    padded = (tot + MOE_BLOCK - 1) // MOE_BLOCK * MOE_BLOCK
    end = jnp.cumsum(padded)
    base = end - padded
    dst = base[None, :] + jnp.cumsum(c8, axis=0) - c8
    blk_start = jnp.arange(N_MOE_BLOCKS, dtype=I32) * MOE_BLOCK
    blk_expert = jnp.minimum(jnp.sum((end[None, :] <= blk_start[:, None]).astype(I32), axis=1),
                             N_EXPERTS - 1).astype(I32)
    offrow = jnp.zeros((N_TILES, ROUTER_LANES), F32).at[:, EXPERT_LANE0:EXPERT_LANE0 + N_EXPERTS].set(
        loff.astype(F32))
    offrow = jnp.broadcast_to(offrow[:, None, :], (N_TILES, SUBLANES, ROUTER_LANES))
    experts = jnp.arange(N_EXPERTS, dtype=I32)
    present = padded > 0
    ordinal = jnp.cumsum(present.astype(I32)) - 1
    at_or_after = lax.cummin(jnp.where(present, experts, N_EXPERTS), reverse=True)
    after = jnp.concatenate([at_or_after[1:], jnp.full((1,), N_EXPERTS, I32)])
    hop = jnp.concatenate([after, jnp.full((1,), N_EXPERTS, I32)])
    pick = (blk_expert[:, None] == experts[None, :]).astype(I32)
    blk_slot = jnp.sum(pick * (ordinal % W_STAGES)[None, :], axis=1).astype(I32)
    ahead, blk_next = experts, []
    for _ in range(W_STAGES - 1):
        ahead = jnp.sum((ahead[:, None] == jnp.arange(N_EXPERTS + 1)[None, :]) * hop[None, :], axis=1)
        per_block = jnp.sum(pick * ahead[None, :], axis=1)
        blk_next.append(jnp.where(per_block >= N_EXPERTS, -1, per_block))
    blk_next = jnp.concatenate(blk_next).astype(I32)
    return {
        "blk_slot": blk_slot, "blk_next": blk_next,
        "loff": loff.reshape(-1).astype(I32), "c8": c8.reshape(-1).astype(I32),
        "dst": dst.reshape(-1).astype(I32), "nslot": nslot.astype(I32),
        "zdst": (base + tot).astype(I32), "zcnt": (padded - tot).astype(I32),
        "offrow": offrow, "blk_expert": blk_expert,
        "n_active": (end[-1:] // MOE_BLOCK).astype(I32),
    }


def _split_router_weights(w_router_group, w_router_expert):
    w_e = w_router_expert.transpose(1, 0, 2).reshape(D_MODEL, N_EXPERTS)
    w = jnp.concatenate([w_router_group, w_e], axis=1)
    w = jnp.pad(w, ((0, 0), (0, ROUTER_LANES - w.shape[1])))
    hi = w.astype(BF16)
    lo = (w - hi.astype(F32)).astype(BF16)
    return jnp.concatenate([hi, lo], axis=1)


def _input_weights(w):
    cols = []
    for g in range(N_GROUPS):
        for part in range(3):
            c0 = part * ATTN_WIDTH + g * GROUP_WIDTH
            blk = w[:, c0:c0 + GROUP_WIDTH]
            cols.append(blk * HEAD_DIM ** -0.5 if part == 0 else blk)
    cols.append(w[:, QKV_WIDTH:])
    return jnp.concatenate(cols, axis=1).astype(BF16)


def kernel(x, rel_bias, norm_mix_g, w_in, pool_w, pool_scale, w_proj_attn, w_proj_pool, w_out,
           norm_ffn_g, w_router_group, w_router_expert, w_gate_e, w_up_e, w_down_e, norm_final_g):
    h = x.reshape(TOKENS, D_MODEL)
    for l in range(DEPTH):
        qkv, u, gates = _project(h, norm_mix_g[l][None], _input_weights(w_in[l]))
        h = _mix_out(h, _attention(qkv, rel_bias), u, gates, pool_w[l].astype(BF16), pool_scale[l][None],
                     w_proj_attn[l].astype(BF16), w_proj_pool[l].astype(BF16), w_out[l].astype(BF16))
        xn, route_i, route_w, tile_counts = _route(
            h, norm_ffn_g[l][None], _split_router_weights(w_router_group[l], w_router_expert[l]))
        tables = _routing_tables(tile_counts)
        x_pad = _dispatch(tables, xn, route_i, route_w)
        y_pad = _experts(l, tables, x_pad, w_gate_e, w_up_e, w_down_e)
        h = _combine(tables, h, route_i, y_pad, norm_final_g[None], l == DEPTH - 1)
    return h.reshape(BATCH, SEQ, D_MODEL)
```

```python
import functools
import math

import jax
import jax.numpy as jnp
from jax import lax
from jax.experimental import pallas as pl
from jax.experimental.pallas import tpu as pltpu

F32 = jnp.float32
BF16 = jnp.bfloat16
I32 = jnp.int32
U32 = jnp.uint32

D_MODEL = 1024
BATCH = 8
SEQ = 2048
TOKENS = BATCH * SEQ
DEPTH = 2

HEAD_DIM = 64
HEADS_PER_GROUP = 4
GROUP_WIDTH = HEADS_PER_GROUP * HEAD_DIM
DILATION_PATTERNS = ((128, 1), (512, 4), (2048, 16))
N_GROUPS = len(DILATION_PATTERNS)
N_ATTN_HEADS = N_GROUPS * HEADS_PER_GROUP
ATTN_WIDTH = N_ATTN_HEADS * HEAD_DIM
QKV_WIDTH = 3 * ATTN_WIDTH
GROUP_QKV = 3 * GROUP_WIDTH
N_SIDE = 64
assert all(w // (2 * d) == N_SIDE for w, d in DILATION_PATTERNS)
POOL_WINDOWS = (2, 4, 8, 16)
POOL_GROUP_WIDTH = 128
POOL_WIDTH = len(POOL_WINDOWS) * POOL_GROUP_WIDTH
POOL_HALO = max(POOL_WINDOWS) // 2
N_BRANCHES = 2
GATE_WIDTH = N_BRANCHES * D_MODEL
IN_WIDTH = QKV_WIDTH + POOL_WIDTH + GATE_WIDTH
N_BUCKETS = 32
MAX_DISTANCE = 1024
N_EXPERT_GROUPS = 8
EXPERTS_PER_GROUP = 8
N_EXPERTS = N_EXPERT_GROUPS * EXPERTS_PER_GROUP
TOP_K = 2
D_EXPERT = 512
N_ASSIGN = TOKENS * TOP_K
EPS = 1e-6
NEG_INF = -1e30
LOG2_E = math.log2(math.e)
Q_SCALE = HEAD_DIM ** -0.5 * LOG2_E

LANES = 128
SUBLANES = 8
ROW_TILE = 512
N_TILES = TOKENS // ROW_TILE
TILES_PER_SEQ = SEQ // ROW_TILE
Q_BLOCK = 128
K_BLOCK = Q_BLOCK + 2 * N_SIDE
PAD_WAYS = 2
ROUTER_LANES = 128
EXPERT_LANE0 = N_EXPERT_GROUPS
VMEM_LIMIT = 56 * 1024 * 1024

RUN_ALIGN = SUBLANES
MOE_BLOCK = 336
HALF = D_MODEL // 2
X_WORDS = HALF + LANES
SLOT_CHUNK = 256
RUN_UNROLL = 4
W_STAGES = 3
MAX_SLOTS = -(-(TOP_K * ROW_TILE + N_EXPERTS * (RUN_ALIGN - 1)) // SLOT_CHUNK) * SLOT_CHUNK
SLOT_CHUNKS = (tuple((s, ROW_TILE) for s in range(0, TOP_K * ROW_TILE, ROW_TILE))
               + tuple((s, SLOT_CHUNK) for s in range(TOP_K * ROW_TILE, MAX_SLOTS, SLOT_CHUNK)))
COMBINE_BASE = MAX_SLOTS - SLOT_CHUNK
N_MOE_BLOCKS = -(-(N_ASSIGN + N_TILES * N_EXPERTS * (RUN_ALIGN - 1)
                   + N_EXPERTS * (MOE_BLOCK - RUN_ALIGN)) // MOE_BLOCK)
PAD_ROWS = N_MOE_BLOCKS * MOE_BLOCK
HIGH_HALF = 0xFFFF0000


def _cparams(*sem):
    return pltpu.CompilerParams(dimension_semantics=sem, vmem_limit_bytes=VMEM_LIMIT)


def _rms(h, g):
    r = lax.rsqrt(jnp.mean(h * h, axis=-1, keepdims=True) + EPS)
    return (h * r) * g


def _dot(a, b):
    return jnp.dot(a, b, preferred_element_type=F32)


def _row_spec(width):
    return pl.BlockSpec((ROW_TILE, width), lambda i, *_: (i, 0))


def _full_spec(a):
    return pl.BlockSpec(a.shape, lambda i, *_: (0,) * a.ndim)


def _bits(x):
    return lax.bitcast_convert_type(x, U32)


def _pack_halves(a, b):
    return (_bits(a) & jnp.uint32(HIGH_HALF)) | (_bits(b) >> 16)


def _unpack_halves(words):
    hi = lax.bitcast_convert_type(words & jnp.uint32(HIGH_HALF), F32)
    lo = lax.bitcast_convert_type(words << 16, F32)
    return hi.astype(BF16), lo.astype(BF16)


def _proj_kernel(h_ref, g_ref, w_ref, q0_ref, q1_ref, q2_ref, u_ref, gate_ref, slabs):
    xn = _rms(h_ref[...], g_ref[...]).astype(BF16)
    n_slabs = GROUP_QKV // LANES
    for g, out_ref in enumerate((q0_ref, q1_ref, q2_ref)):
        dilation = DILATION_PATTERNS[g][1]
        res = _dot(xn, w_ref[:, g * GROUP_QKV:(g + 1) * GROUP_QKV])
        res = jnp.concatenate([res[:, 0:GROUP_WIDTH] * Q_SCALE, res[:, GROUP_WIDTH:]], axis=1)
        if dilation == 1:
            out_ref[0, 0] = res.astype(BF16)
            continue
        for s in range(n_slabs):
            slabs[s] = res[:, s * LANES:(s + 1) * LANES]
        n = ROW_TILE // dilation
        for r in range(dilation):
            rows = [slabs[s, pl.ds(r, n, stride=dilation), :] for s in range(n_slabs)]
            out_ref[0, r] = jnp.concatenate(rows, axis=1).astype(BF16)
    u_ref[...] = _dot(xn, w_ref[:, QKV_WIDTH:QKV_WIDTH + POOL_WIDTH])
    gates = _dot(xn, w_ref[:, QKV_WIDTH + POOL_WIDTH:IN_WIDTH])
    gate_ref[...] = jax.nn.sigmoid(gates).astype(BF16)


def _project(h, g, w_bf16):
    qkv_shapes, qkv_specs = [], []
    for _, d in DILATION_PATTERNS:
        qkv_shapes.append(jax.ShapeDtypeStruct((BATCH, d, SEQ // d, GROUP_QKV), BF16))
        qkv_specs.append(pl.BlockSpec((1, d, ROW_TILE // d, GROUP_QKV),
                                      lambda i: (i // TILES_PER_SEQ, 0, i % TILES_PER_SEQ, 0)))
    res = pl.pallas_call(
        _proj_kernel,
        grid=(N_TILES,),
        in_specs=[_row_spec(D_MODEL), _full_spec(g), _full_spec(w_bf16)],
        out_specs=qkv_specs + [_row_spec(POOL_WIDTH), _row_spec(GATE_WIDTH)],
        out_shape=qkv_shapes + [jax.ShapeDtypeStruct((TOKENS, POOL_WIDTH), F32),
                                jax.ShapeDtypeStruct((TOKENS, GATE_WIDTH), BF16)],
        scratch_shapes=[pltpu.VMEM((GROUP_QKV // LANES, ROW_TILE, LANES), F32)],
        compiler_params=_cparams("parallel"),
        name="proj",
    )(h, g, w_bf16)
    return res[:N_GROUPS], res[N_GROUPS], res[N_GROUPS + 1]


def _attn_kernel(q0_ref, q1_ref, q2_ref, b0_ref, b1_ref, b2_ref, o_ref, pads, num, den, top):
    zpad = jnp.zeros((N_SIDE, GROUP_WIDTH), BF16)
    groups = ((q2_ref, b2_ref, 2), (q1_ref, b1_ref, 1), (q0_ref, b0_ref, 0))
    for order, (qkv_ref, bias_ref, g) in enumerate(groups):
        dilation = DILATION_PATTERNS[g][1]
        sub_len = SEQ // dilation
        for way in range(PAD_WAYS):
            for part in range(1 + HEADS_PER_GROUP):
                pads[way, part, 0:N_SIDE, :] = zpad
                pads[way, part, N_SIDE + sub_len:2 * N_SIDE + sub_len, :] = zpad
        _attn_group(qkv_ref, bias_ref, pads, num, den, top, dilation=dilation, first=order == 0)

    def finish(c, carry):
        rows = pl.ds(pl.multiple_of(c * ROW_TILE, ROW_TILE), ROW_TILE)
        merged = [num[half, rows, :] / den[half, rows, :] for half in range(GROUP_WIDTH // LANES)]
        o_ref[0, rows, :] = jnp.concatenate(merged, axis=1).astype(BF16)
        return carry

    lax.fori_loop(0, SEQ // ROW_TILE, finish, 0)


def _attn_group(qkv_ref, bias_ref, pads, num, den, top, *, dilation, first):
    sub_len = SEQ // dilation
    n_blocks = sub_len // Q_BLOCK
    head_of_lane = lax.broadcasted_iota(I32, (1, GROUP_WIDTH), 1) // HEAD_DIM

    def by_head(cols):
        out = cols[HEADS_PER_GROUP - 1]
        for h in range(HEADS_PER_GROUP - 2, -1, -1):
            out = jnp.where(head_of_lane == h, cols[h], out)
        return out

    def load_keys(r, way):
        pads[way, 0, N_SIDE:N_SIDE + sub_len, :] = qkv_ref[0, r, :, GROUP_WIDTH:2 * GROUP_WIDTH]
        v = qkv_ref[0, r, :, 2 * GROUP_WIDTH:3 * GROUP_WIDTH]
        for h in range(HEADS_PER_GROUP):
            pads[way, 1 + h, N_SIDE:N_SIDE + sub_len, :] = jnp.where(head_of_lane == h, v, jnp.zeros_like(v))

    def block(r, way, i):
        if True:
            r0 = i * Q_BLOCK if isinstance(i, int) else pl.multiple_of(i * Q_BLOCK, Q_BLOCK)
            qb = qkv_ref[0, r, pl.ds(r0, Q_BLOCK), 0:GROUP_WIDTH]
            kw = pads[way, 0, pl.ds(r0, K_BLOCK), :]
            zero = jnp.zeros_like(qb)
            q_heads = jnp.concatenate(
                [jnp.where(head_of_lane == h, qb, zero) for h in range(HEADS_PER_GROUP)], axis=0)
            s = lax.dot_general(q_heads, kw, (((1,), (1,)), ((), ())), preferred_element_type=F32)
            if isinstance(i, int):
                edge = int(i == 0) + 2 * int(i == n_blocks - 1)
            else:
                edge = (i == 0).astype(I32) + 2 * (i == n_blocks - 1).astype(I32)
            s = s.reshape(HEADS_PER_GROUP, Q_BLOCK, K_BLOCK) + bias_ref[edge]
            m = jnp.max(s, axis=-1, keepdims=True)
            p = jnp.exp2(s - m)
            psum = jnp.sum(p, axis=-1, keepdims=True)
            pb = p.astype(BF16)
            p_cat = jnp.concatenate([pb[h] for h in range(HEADS_PER_GROUP)], axis=1)
            v_heads = jnp.concatenate(
                [pads[way, 1 + h, pl.ds(r0, K_BLOCK), :] for h in range(HEADS_PER_GROUP)], axis=0)
            o = _dot(p_cat, v_heads)
            lse = m + jnp.log2(psum)
            o = o / by_head([psum[h] for h in range(HEADS_PER_GROUP)])
            lse_lanes = by_head([lse[h] for h in range(HEADS_PER_GROUP)])
            if dilation == 1:
                rows = pl.ds(r0, Q_BLOCK)
            else:
                rows = pl.ds(r + dilation * r0, Q_BLOCK, stride=dilation)
            for half in range(GROUP_WIDTH // LANES):
                lanes = slice(half * LANES, (half + 1) * LANES)
                if first:
                    num[half, rows, :] = o[:, lanes]
                    den[half, rows, :] = jnp.ones((Q_BLOCK, LANES), F32)
                    top[half, rows, :] = lse_lanes[:, lanes]
                else:
                    old = top[half, rows, :]
                    new = jnp.maximum(old, lse_lanes[:, lanes])
                    keep = jnp.exp2(old - new)
                    add = jnp.exp2(lse_lanes[:, lanes] - new)
                    num[half, rows, :] = keep * num[half, rows, :] + add * o[:, lanes]
                    den[half, rows, :] = keep * den[half, rows, :] + add
                    top[half, rows, :] = new

    def blocks_of(r, way):
        if n_blocks == 1:
            block(r, way, 0)
        else:
            def two(j, carry):
                block(r, way, 2 * j)
                block(r, way, 2 * j + 1)
                return carry
            lax.fori_loop(0, n_blocks // 2, two, 0)

    if dilation == 1:
        load_keys(0, 0)
        blocks_of(0, 0)
    else:
        def residues(j, carry):
            for way in range(PAD_WAYS):
                load_keys(PAD_WAYS * j + way, way)
            for way in range(PAD_WAYS):
                blocks_of(PAD_WAYS * j + way, way)
            return carry
        lax.fori_loop(0, dilation // PAD_WAYS, residues, 0)


def _t5_bucket(rel):
    nb = N_BUCKETS // 2
    ret = jnp.where(rel > 0, nb, 0)
    n = jnp.abs(rel)
    max_exact = nb // 2
    nf = jnp.maximum(n, max_exact).astype(F32)
    large = max_exact + (jnp.log(nf / max_exact) / math.log(MAX_DISTANCE / max_exact)
                         * (nb - max_exact)).astype(I32)
    large = jnp.minimum(large, nb - 1)
    return ret + jnp.where(n < max_exact, n, large)


def _band_bias(rel_bias, group, dilation):
    qi = jnp.arange(Q_BLOCK)[:, None]
    ki = jnp.arange(K_BLOCK)[None, :]
    bucket = _t5_bucket((ki - N_SIDE - qi) * dilation)
    tab = rel_bias[:, group * HEADS_PER_GROUP:(group + 1) * HEADS_PER_GROUP]
    onehot = (bucket[:, :, None] == jnp.arange(N_BUCKETS)[None, None, :]).astype(F32)
    bias = jnp.einsum('qkb,bh->hqk', onehot, tab, precision=lax.Precision.HIGHEST).astype(F32)
    bias = bias * LOG2_E
    band = jnp.abs(ki - N_SIDE - qi) <= N_SIDE
    variants = []
    for edge in range(4):
        ok = band
        if edge & 1:
            ok = ok & (ki >= N_SIDE)
        if edge & 2:
            ok = ok & (ki < Q_BLOCK + N_SIDE)
        variants.append(jnp.where(ok[None], bias, NEG_INF))
    return jnp.stack(variants)


def _attention(qkv, rel_bias):
    biases = [_band_bias(rel_bias, g, d) for g, (_, d) in enumerate(DILATION_PATTERNS)]
    slab = pltpu.VMEM((GROUP_WIDTH // LANES, SEQ, LANES), F32)
    pads = pltpu.VMEM((PAD_WAYS, 1 + HEADS_PER_GROUP, SEQ + 2 * N_SIDE, GROUP_WIDTH), BF16)
    out = pl.pallas_call(
        _attn_kernel,
        grid=(BATCH,),
        in_specs=[pl.BlockSpec((1,) + a.shape[1:], lambda b: (b, 0, 0, 0)) for a in qkv]
                 + [_full_spec(b) for b in biases],
        out_specs=pl.BlockSpec((1, SEQ, GROUP_WIDTH), lambda b: (b, 0, 0)),
        out_shape=jax.ShapeDtypeStruct((BATCH, SEQ, GROUP_WIDTH), BF16),
        scratch_shapes=[pads, slab, slab, slab],
        compiler_params=_cparams("parallel"),
        name="attn",
    )(*qkv, *biases)
    return out.reshape(TOKENS, GROUP_WIDTH)


def _mixout_kernel(h_ref, attn_ref, u_ref, uprev_ref, unext_ref, gate_ref,
                   pw_ref, ps_ref, wpa_ref, wpp_ref, wo_ref, out_ref):
    j = pl.program_id(0) % TILES_PER_SEQ
    y_attn = _dot(attn_ref[...], wpa_ref[...])

    u = u_ref[...]
    prev = jnp.where(j == 0, 0.0, uprev_ref[0])
    nxt = jnp.where(j == TILES_PER_SEQ - 1, 0.0, unext_ref[0])
    ext = jnp.concatenate([prev, u, nxt], axis=0)
    pos = j * ROW_TILE + lax.broadcasted_iota(I32, (ROW_TILE, 1), 0)
    mixed = []
    for gi, w in enumerate(POOL_WINDOWS):
        half = w // 2
        sl = slice(gi * POOL_GROUP_WIDTH, (gi + 1) * POOL_GROUP_WIDTH)
        run = ext[:, sl]
        span = 1
        while 2 * span < w:
            run = run[:-span] + run[span:]
            span *= 2
        lo = POOL_HALO - half
        acc = run[lo:lo + ROW_TILE] + run[lo + half:lo + half + ROW_TILE]
        cnt = (jnp.minimum(pos + half, SEQ) - jnp.maximum(pos - half, 0)).astype(F32)
        pooled = acc / cnt - u[:, sl]
        mixed.append(_dot(pooled.astype(BF16), pw_ref[gi]) * ps_ref[:, sl])
    y_pool = _dot(jnp.concatenate(mixed, axis=1).astype(BF16), wpp_ref[...])

    y = (gate_ref[:, 0:D_MODEL] * y_attn.astype(BF16)
         + gate_ref[:, D_MODEL:GATE_WIDTH] * y_pool.astype(BF16))
    out_ref[...] = h_ref[...] + _dot(y, wo_ref[...])


def _mix_out(h, attn, u, gates, pool_w, pool_scale, w_proj_attn, w_proj_pool, w_out):
    halo_blocks = ROW_TILE // POOL_HALO
    u3 = u.reshape(TOKENS // POOL_HALO, POOL_HALO, POOL_WIDTH)
    last = TOKENS // POOL_HALO - 1
    prev_spec = pl.BlockSpec((1, POOL_HALO, POOL_WIDTH),
                             lambda i: (jnp.maximum(i * halo_blocks - 1, 0), 0, 0))
    next_spec = pl.BlockSpec((1, POOL_HALO, POOL_WIDTH),
                             lambda i: (jnp.minimum((i + 1) * halo_blocks, last), 0, 0))
    weights = (pool_w, pool_scale, w_proj_attn, w_proj_pool, w_out)
    return pl.pallas_call(
        _mixout_kernel,
        grid=(N_TILES,),
        in_specs=[_row_spec(D_MODEL), _row_spec(GROUP_WIDTH),
                  _row_spec(POOL_WIDTH), prev_spec, next_spec, _row_spec(GATE_WIDTH)]
                 + [_full_spec(w) for w in weights],
        out_specs=_row_spec(D_MODEL),
        out_shape=jax.ShapeDtypeStruct((TOKENS, D_MODEL), F32),
        compiler_params=_cparams("parallel"),
        name="mixout",
    )(h, attn, u, u3, u3, gates, *weights)


def _router_kernel(h_ref, g_ref, wr_ref, xn_ref, ri_ref, rw_ref, cnt_ref):
    xn = _rms(h_ref[...], g_ref[...])
    hi = xn.astype(BF16)
    xn_ref[...] = hi
    lo = (xn - hi.astype(F32)).astype(BF16)
    both = _dot(hi, wr_ref[...])
    lg = both[:, 0:ROUTER_LANES] + (both[:, ROUTER_LANES:] + _dot(lo, wr_ref[:, 0:ROUTER_LANES]))

    lane = lax.broadcasted_iota(I32, (ROW_TILE, ROUTER_LANES), 1)
    lanef = lane.astype(F32)
    low = jnp.float32(-3.0e38)
    far = jnp.float32(ROUTER_LANES)
    first = lambda hit: jnp.min(jnp.where(hit, lanef, far), axis=-1, keepdims=True)

    is_group = lane < N_EXPERT_GROUPS
    gl = jnp.where(is_group, lg, low)
    gmax = jnp.max(gl, axis=-1, keepdims=True)
    gidx = first(gl == gmax).astype(I32)
    gden = jnp.sum(jnp.where(is_group, jnp.exp(gl - gmax), 0.0), axis=-1, keepdims=True)
    g_p = 1.0 / gden

    in_group = ((lane >= EXPERT_LANE0) & (lane < EXPERT_LANE0 + N_EXPERTS)
                & ((lane - EXPERT_LANE0) // EXPERTS_PER_GROUP == gidx))
    el = jnp.where(in_group, lg, low)
    t1 = jnp.max(el, axis=-1, keepdims=True)
    l1 = first(in_group & (el == t1))
    rest = in_group & (lanef != l1)
    el2 = jnp.where(rest, lg, low)
    t2 = jnp.max(el2, axis=-1, keepdims=True)
    l2 = first(rest & (el2 == t2))
    e2 = jnp.exp(t2 - t1)
    w1 = g_p * (1.0 / (1.0 + e2))
    w2 = g_p * (e2 / (1.0 + e2))

    hit1 = lanef == l1
    hit2 = lanef == l2
    onehot = (hit1 | hit2).astype(BF16)
    ri = lax.broadcasted_iota(I32, (ROW_TILE, ROW_TILE), 0)
    ci = lax.broadcasted_iota(I32, (ROW_TILE, ROW_TILE), 1)
    before = (ci < ri).astype(BF16)
    seen = _dot(before, onehot)
    r1 = jnp.sum(jnp.where(hit1, seen, 0.0), axis=-1, keepdims=True)
    r2 = jnp.sum(jnp.where(hit2, seen, 0.0), axis=-1, keepdims=True)

    packed = jnp.zeros((ROW_TILE, ROUTER_LANES), F32)
    for k, v in enumerate((l1, l2, r1, r2)):
        packed = jnp.where(lane == k, v, packed)
    ri_ref[...] = packed.astype(I32)
    rw_ref[...] = jnp.where(lane == 0, w1, jnp.where(lane == 1, w2, 0.0))
    counts = jnp.sum(onehot.astype(F32), axis=0, keepdims=True)
    cnt_ref[0] = jnp.broadcast_to(counts, (SUBLANES, ROUTER_LANES))


def _route(h, g, wr_split):
    return pl.pallas_call(
        _router_kernel,
        grid=(N_TILES,),
        in_specs=[_row_spec(D_MODEL), _full_spec(g), _full_spec(wr_split)],
        out_specs=[_row_spec(D_MODEL), _row_spec(ROUTER_LANES), _row_spec(ROUTER_LANES),
                   pl.BlockSpec((1, SUBLANES, ROUTER_LANES), lambda i: (i, 0, 0))],
        out_shape=[jax.ShapeDtypeStruct((TOKENS, D_MODEL), BF16),
                   jax.ShapeDtypeStruct((TOKENS, ROUTER_LANES), I32),
                   jax.ShapeDtypeStruct((TOKENS, ROUTER_LANES), F32),
                   jax.ShapeDtypeStruct((N_TILES, SUBLANES, ROUTER_LANES), F32)],
        compiler_params=_cparams("parallel"),
        name="router",
    )(h, g, wr_split)


def _slots(ri_ref, off_ref):
    ri = ri_ref[...]
    lane = lax.broadcasted_iota(I32, (ROW_TILE, ROUTER_LANES), 1)
    off = off_ref[0, 0:1, :]
    pick = lambda k: jnp.sum(jnp.where(lane == ri[:, k:k + 1], off, 0.0), axis=-1, keepdims=True)
    return (pick(0) + ri[:, 2:3].astype(F32), pick(1) + ri[:, 3:4].astype(F32))


def _as_rows(cols):
    eye = (lax.broadcasted_iota(I32, (ROW_TILE, ROW_TILE), 0)
           == lax.broadcasted_iota(I32, (ROW_TILE, ROW_TILE), 1))
    return [jnp.sum(jnp.where(eye, c, 0.0), axis=0, keepdims=True) for c in cols]


def _run_copy(tile, e, loff_s, c8_s, dst_s, buf_ref, slot, hbm_ref, sem, to_hbm):
    k = tile * N_EXPERTS + e
    n = pl.multiple_of(c8_s[k], RUN_ALIGN)
    vm = buf_ref.at[slot, pl.ds(pl.multiple_of(loff_s[k], RUN_ALIGN), n)]
    hb = hbm_ref.at[pl.ds(pl.multiple_of(dst_s[k], RUN_ALIGN), n)]
    src, dst = (vm, hb) if to_hbm else (hb, vm)
    return n, pltpu.make_async_copy(src, dst, sem.at[slot])


def _loop(lo, hi, fn, unroll=1):
    def body(e, carry):
        fn(e)
        return carry
    lax.fori_loop(lo, hi, body, 0, unroll=unroll)


def _wait_rows(n, buf_ref, slot, hbm_ref, sem, to_hbm):
    vm = buf_ref.at[slot, pl.ds(0, n)]
    hb = hbm_ref.at[pl.ds(0, n)]
    src, dst = (vm, hb) if to_hbm else (hb, vm)
    pltpu.make_async_copy(src, dst, sem.at[slot]).wait()


def _start(n, cp):
    @pl.when(n > 0)
    def _():
        cp.start()


def _wait(n, cp):
    @pl.when(n > 0)
    def _():
        cp.wait()


def _dispatch_kernel(loff_s, c8_s, dst_s, nslot_s, zdst_s, zcnt_s, nact_s,
                     xn_ref, ri_ref, rw_ref, off_ref, xpad_hbm, sorted_buf, zero_buf, sem, zsem):
    i = pl.program_id(0)
    slot = i % 2
    last = pl.num_programs(0) - 1

    def zero_copy(e):
        n = pl.multiple_of(zcnt_s[e], RUN_ALIGN)
        dst = xpad_hbm.at[pl.ds(pl.multiple_of(zdst_s[e], RUN_ALIGN), n)]
        return n, pltpu.make_async_copy(zero_buf.at[pl.ds(0, n)], dst, zsem)

    def tail_copy(b):
        dst = xpad_hbm.at[pl.ds(pl.multiple_of(b * MOE_BLOCK, MOE_BLOCK), MOE_BLOCK)]
        return pltpu.make_async_copy(zero_buf, dst, zsem)

    @pl.when(i == 0)
    def _():
        zero_buf[...] = jnp.zeros_like(zero_buf)
        _loop(0, N_EXPERTS, lambda e: _start(*zero_copy(e)))
        _loop(nact_s[0], N_MOE_BLOCKS, lambda b: tail_copy(b).start())

    s1, s2 = _slots(ri_ref, off_ref)
    rw = rw_ref[...]
    s1_row, s2_row, w1_row, w2_row = _as_rows([s1, s2, rw[:, 0:1], rw[:, 1:2]])
    xn = xn_ref[...]
    for first, size in SLOT_CHUNKS:
        @pl.when(first < nslot_s[i])
        def _():
            rows = slice(first, first + size)
            lane = lax.broadcasted_iota(I32, (size, LANES), 1)
            srow = (first + lax.broadcasted_iota(I32, (size, ROW_TILE), 0)).astype(F32)
            hit1 = srow == s1_row
            hit2 = srow == s2_row
            xs = _dot((hit1 | hit2).astype(BF16), xn)
            sorted_buf[slot, rows, 0:HALF] = _pack_halves(xs[:, 0:HALF], xs[:, HALF:D_MODEL])
            ws = jnp.sum(jnp.where(hit1, w1_row, 0.0) + jnp.where(hit2, w2_row, 0.0),
                         axis=-1, keepdims=True)
            sorted_buf[slot, rows, HALF:X_WORDS] = jnp.where(lane == 0, _bits(ws), jnp.uint32(0))

    copy = lambda tile, sl: (lambda e: _run_copy(tile, e, loff_s, c8_s, dst_s, sorted_buf, sl,
                                                 xpad_hbm, sem, True))
    mine = copy(i, slot)
    _loop(0, N_EXPERTS, lambda e: _start(*mine(e)), unroll=RUN_UNROLL)
    tile_rows = lambda t: pl.multiple_of(nslot_s[t], RUN_ALIGN)

    @pl.when(i > 0)
    def _():
        _wait_rows(tile_rows(i - 1), sorted_buf, 1 - slot, xpad_hbm, sem, True)

    @pl.when(i == last)
    def _():
        _wait_rows(tile_rows(i), sorted_buf, slot, xpad_hbm, sem, True)
        _loop(0, N_EXPERTS, lambda e: _wait(*zero_copy(e)))
        _loop(nact_s[0], N_MOE_BLOCKS, lambda b: tail_copy(b).wait())


def _dispatch(tables, xn, route_i, route_w):
    tile_row = pl.BlockSpec((1, SUBLANES, ROUTER_LANES), lambda i, *_: (i, 0, 0))
    grid_spec = pltpu.PrefetchScalarGridSpec(
        num_scalar_prefetch=7,
        grid=(N_TILES,),
        in_specs=[_row_spec(D_MODEL), _row_spec(ROUTER_LANES), _row_spec(ROUTER_LANES), tile_row],
        out_specs=pl.BlockSpec(memory_space=pl.ANY),
        scratch_shapes=[pltpu.VMEM((2, MAX_SLOTS, X_WORDS), U32),
                        pltpu.VMEM((MOE_BLOCK, X_WORDS), U32),
                        pltpu.SemaphoreType.DMA((2,)),
                        pltpu.SemaphoreType.DMA(())],
    )
    return pl.pallas_call(
        _dispatch_kernel,
        grid_spec=grid_spec,
        out_shape=jax.ShapeDtypeStruct((PAD_ROWS, X_WORDS), U32),
        compiler_params=_cparams("arbitrary"),
        name="dispatch",
    )(tables["loff"], tables["c8"], tables["dst"], tables["nslot"], tables["zdst"], tables["zcnt"],
      tables["n_active"], xn, route_i, route_w, tables["offrow"])


def _combine_kernel(loff_s, c8_s, dst_s, nslot_s,
                    h_ref, ri_ref, off_ref, g_ref, ypad_hbm, out_ref,
                    ybuf, sem, *, final_norm):
    i = pl.program_id(0)
    slot = i % 2
    last = pl.num_programs(0) - 1
    fetch = lambda tile, sl: (lambda e: _run_copy(tile, e, loff_s, c8_s, dst_s, ybuf, sl,
                                                  ypad_hbm, sem, False))

    @pl.when(i == 0)
    def _():
        ybuf[...] = jnp.zeros_like(ybuf)
        first = fetch(i, slot)
        _loop(0, N_EXPERTS, lambda e: _start(*first(e)), unroll=RUN_UNROLL)

    @pl.when(i < last)
    def _():
        nxt = fetch(i + 1, 1 - slot)
        _loop(0, N_EXPERTS, lambda e: _start(*nxt(e)), unroll=RUN_UNROLL)

    _wait_rows(pl.multiple_of(nslot_s[i], RUN_ALIGN), ybuf, slot, ypad_hbm, sem, False)

    s1, s2 = _slots(ri_ref, off_ref)

    def gathered(first, size):
        scol = (first + lax.broadcasted_iota(I32, (ROW_TILE, size), 1)).astype(F32)
        pick = ((scol == s1) | (scol == s2)).astype(BF16)
        y_hi, y_lo = _unpack_halves(ybuf[slot, first:first + size, :])
        return jnp.concatenate([_dot(pick, y_hi), _dot(pick, y_lo)], axis=1)

    finish = (lambda v: _rms(v, g_ref[...])) if final_norm else (lambda v: v)
    h = h_ref[...] + gathered(0, COMBINE_BASE)
    spill = nslot_s[i] > COMBINE_BASE

    @pl.when(jnp.logical_not(spill))
    def _():
        out_ref[...] = finish(h)

    @pl.when(spill)
    def _():
        out_ref[...] = finish(h + gathered(COMBINE_BASE, MAX_SLOTS - COMBINE_BASE))


def _combine(tables, h, route_i, y_pad, g, final_norm):
    tile_row = pl.BlockSpec((1, SUBLANES, ROUTER_LANES), lambda i, *_: (i, 0, 0))
    grid_spec = pltpu.PrefetchScalarGridSpec(
        num_scalar_prefetch=4,
        grid=(N_TILES,),
        in_specs=[_row_spec(D_MODEL), _row_spec(ROUTER_LANES), tile_row, _full_spec(g),
                  pl.BlockSpec(memory_space=pl.ANY)],
        out_specs=_row_spec(D_MODEL),
        scratch_shapes=[pltpu.VMEM((2, MAX_SLOTS, HALF), U32),
                        pltpu.SemaphoreType.DMA((2,))],
    )
    return pl.pallas_call(
        functools.partial(_combine_kernel, final_norm=final_norm),
        grid_spec=grid_spec,
        out_shape=jax.ShapeDtypeStruct((TOKENS, D_MODEL), F32),
        compiler_params=_cparams("arbitrary"),
        name="combine",
    )(tables["loff"], tables["c8"], tables["dst"], tables["nslot"],
      h, route_i, tables["offrow"], g, y_pad)


def _expert_kernel(be_ref, slot_ref, next_ref, nact_ref, x_ref, wg_hbm, wu_hbm, wd_hbm, y_hbm,
                   wg_f, wu_f, wd_f, wg_b, wu_b, wd_b, ybuf, zero_buf, sem, ysem, zsem, *, layer):
    i = pl.program_id(0)
    n_active = nact_ref[0]
    staged = ((wg_hbm, wg_f, wg_b), (wu_hbm, wu_f, wu_b), (wd_hbm, wd_f, wd_b))

    def fetch(e, slot):
        return [pltpu.make_async_copy(hbm.at[layer, e], f32.at[slot], sem.at[slot, k])
                for k, (hbm, f32, _) in enumerate(staged)]

    def block_rows(b):
        return y_hbm.at[pl.ds(pl.multiple_of(b * MOE_BLOCK, MOE_BLOCK), MOE_BLOCK)]

    def put(b):
        return pltpu.make_async_copy(ybuf.at[b % 2], block_rows(b), ysem.at[b % 2])

    def tail_copy(b):
        return pltpu.make_async_copy(zero_buf, block_rows(b), zsem)

    @pl.when(i == 0)
    def _():
        zero_buf[...] = jnp.zeros_like(zero_buf)
        _loop(n_active, N_MOE_BLOCKS, lambda b: tail_copy(b).start())

    @pl.when(i == pl.num_programs(0) - 1)
    def _():
        _loop(n_active, N_MOE_BLOCKS, lambda b: tail_copy(b).wait())

    @pl.when(i < n_active)
    def _():
        e = be_ref[i]
        slot = slot_ref[i]

        def request(ahead, block):
            nxt = next_ref[(ahead - 1) * N_MOE_BLOCKS + block]

            @pl.when(nxt >= 0)
            def _():
                for cp in fetch(nxt, (slot + ahead) % W_STAGES):
                    cp.start()

        @pl.when((i == 0) | (e != be_ref[jnp.maximum(i - 1, 0)]))
        def _():
            @pl.when(i == 0)
            def _():
                for cp in fetch(e, slot):
                    cp.start()
                for ahead in range(1, W_STAGES - 1):
                    request(ahead, i)
            for cp in fetch(e, slot):
                cp.wait()
            request(W_STAGES - 1, i)
            for _, f32, b16 in staged:
                b16[...] = f32[slot].astype(BF16)

        x_hi, x_lo = _unpack_halves(x_ref[:, 0:HALF])
        row_w = lax.bitcast_convert_type(x_ref[:, HALF:HALF + 1], F32)
        gate = _dot(x_hi, wg_b[0:HALF, :]) + _dot(x_lo, wg_b[HALF:D_MODEL, :])
        up = _dot(x_hi, wu_b[0:HALF, :]) + _dot(x_lo, wu_b[HALF:D_MODEL, :])
        hmid = (jax.nn.silu(gate) * up).astype(BF16)
        y = (_dot(hmid, wd_b[...]) * row_w).astype(BF16).astype(F32)
        ybuf[i % 2] = _pack_halves(y[:, 0:HALF], y[:, HALF:D_MODEL])
        put(i).start()

        @pl.when(i > 0)
        def _():
            put(i - 1).wait()

        @pl.when(i == n_active - 1)
        def _():
            put(i).wait()


def _experts(layer, tables, x_pad, w_gate, w_up, w_down):
    hbm = pl.BlockSpec(memory_space=pl.ANY)
    up_shape, down_shape = (D_MODEL, D_EXPERT), (D_EXPERT, D_MODEL)
    grid_spec = pltpu.PrefetchScalarGridSpec(
        num_scalar_prefetch=4,
        grid=(N_MOE_BLOCKS,),
        in_specs=[pl.BlockSpec((MOE_BLOCK, X_WORDS),
                               lambda i, be, sl, nx, na: (jnp.minimum(i, na[0] - 1), 0)),
                  hbm, hbm, hbm],
        out_specs=hbm,
        scratch_shapes=[pltpu.VMEM((W_STAGES,) + up_shape, F32),
                        pltpu.VMEM((W_STAGES,) + up_shape, F32),
                        pltpu.VMEM((W_STAGES,) + down_shape, F32),
                        pltpu.VMEM(up_shape, BF16), pltpu.VMEM(up_shape, BF16),
                        pltpu.VMEM(down_shape, BF16),
                        pltpu.VMEM((2, MOE_BLOCK, HALF), U32),
                        pltpu.VMEM((MOE_BLOCK, HALF), U32),
                        pltpu.SemaphoreType.DMA((W_STAGES, 3)),
                        pltpu.SemaphoreType.DMA((2,)),
                        pltpu.SemaphoreType.DMA(())],
    )
    return pl.pallas_call(
        functools.partial(_expert_kernel, layer=layer),
        grid_spec=grid_spec,
        out_shape=jax.ShapeDtypeStruct((PAD_ROWS, HALF), U32),
        compiler_params=_cparams("arbitrary"),
        name="experts",
    )(tables["blk_expert"], tables["blk_slot"], tables["blk_next"], tables["n_active"],
      x_pad, w_gate, w_up, w_down)


def _routing_tables(tile_counts):
    cnt = tile_counts[:, 0, EXPERT_LANE0:EXPERT_LANE0 + N_EXPERTS].astype(I32)
    c8 = (cnt + RUN_ALIGN - 1) // RUN_ALIGN * RUN_ALIGN
    loff = jnp.cumsum(c8, axis=1) - c8
    nslot = jnp.sum(c8, axis=1)
    tot = jnp.sum(c8, axis=0)
    padded = (tot + MOE_BLOCK - 1) // MOE_BLOCK * MOE_BLOCK
    end = jnp.cumsum(padded)
    base = end - padded
    dst = base[None, :] + jnp.cumsum(c8, axis=0) - c8
    blk_start = jnp.arange(N_MOE_BLOCKS, dtype=I32) * MOE_BLOCK
    blk_expert = jnp.minimum(jnp.sum((end[None, :] <= blk_start[:, None]).astype(I32), axis=1),
                             N_EXPERTS - 1).astype(I32)
    offrow = jnp.zeros((N_TILES, ROUTER_LANES), F32).at[:, EXPERT_LANE0:EXPERT_LANE0 + N_EXPERTS].set(
        loff.astype(F32))
    offrow = jnp.broadcast_to(offrow[:, None, :], (N_TILES, SUBLANES, ROUTER_LANES))
    experts = jnp.arange(N_EXPERTS, dtype=I32)
    present = padded > 0
    ordinal = jnp.cumsum(present.astype(I32)) - 1
    at_or_after = lax.cummin(jnp.where(present, experts, N_EXPERTS), reverse=True)
    after = jnp.concatenate([at_or_after[1:], jnp.full((1,), N_EXPERTS, I32)])
    hop = jnp.concatenate([after, jnp.full((1,), N_EXPERTS, I32)])
    pick = (blk_expert[:, None] == experts[None, :]).astype(I32)
    blk_slot = jnp.sum(pick * (ordinal % W_STAGES)[None, :], axis=1).astype(I32)
    ahead, blk_next = experts, []
    for _ in range(W_STAGES - 1):
        ahead = jnp.sum((ahead[:, None] == jnp.arange(N_EXPERTS + 1)[None, :]) * hop[None, :], axis=1)
        per_block = jnp.sum(pick * ahead[None, :], axis=1)
        blk_next.append(jnp.where(per_block >= N_EXPERTS, -1, per_block))
    blk_next = jnp.concatenate(blk_next).astype(I32)
    return {
        "blk_slot": blk_slot, "blk_next": blk_next,
        "loff": loff.reshape(-1).astype(I32), "c8": c8.reshape(-1).astype(I32),
        "dst": dst.reshape(-1).astype(I32), "nslot": nslot.astype(I32),
        "zdst": (base + tot).astype(I32), "zcnt": (padded - tot).astype(I32),
        "offrow": offrow, "blk_expert": blk_expert,
        "n_active": (end[-1:] // MOE_BLOCK).astype(I32),
    }


def _split_router_weights(w_router_group, w_router_expert):
    w_e = w_router_expert.transpose(1, 0, 2).reshape(D_MODEL, N_EXPERTS)
    w = jnp.concatenate([w_router_group, w_e], axis=1)
    w = jnp.pad(w, ((0, 0), (0, ROUTER_LANES - w.shape[1])))
    hi = w.astype(BF16)
    lo = (w - hi.astype(F32)).astype(BF16)
    return jnp.concatenate([hi, lo], axis=1)


def _input_weights(w):
    cols = []
    for g in range(N_GROUPS):
        for part in range(3):
            c0 = part * ATTN_WIDTH + g * GROUP_WIDTH
            cols.append(w[:, c0:c0 + GROUP_WIDTH])
    cols.append(w[:, QKV_WIDTH:])
    return jnp.concatenate(cols, axis=1).astype(BF16)


def kernel(x, rel_bias, norm_mix_g, w_in, pool_w, pool_scale, w_proj_attn, w_proj_pool, w_out,
           norm_ffn_g, w_router_group, w_router_expert, w_gate_e, w_up_e, w_down_e, norm_final_g):
    h = x.reshape(TOKENS, D_MODEL)
    for l in range(DEPTH):
        qkv, u, gates = _project(h, norm_mix_g[l][None], _input_weights(w_in[l]))
        h = _mix_out(h, _attention(qkv, rel_bias), u, gates, pool_w[l].astype(BF16), pool_scale[l][None],
                     w_proj_attn[l].astype(BF16), w_proj_pool[l].astype(BF16), w_out[l].astype(BF16))
        xn, route_i, route_w, tile_counts = _route(
            h, norm_ffn_g[l][None], _split_router_weights(w_router_group[l], w_router_expert[l]))
        tables = _routing_tables(tile_counts)
        x_pad = _dispatch(tables, xn, route_i, route_w)
        y_pad = _experts(l, tables, x_pad, w_gate_e, w_up_e, w_down_e)
        h = _combine(tables, h, route_i, y_pad, norm_final_g[None], l == DEPTH - 1)
    return h.reshape(BATCH, SEQ, D_MODEL)
```

```python
import functools
import math

import jax
import jax.numpy as jnp
from jax import lax
from jax.experimental import pallas as pl
from jax.experimental.pallas import tpu as pltpu

F32 = jnp.float32
BF16 = jnp.bfloat16
I32 = jnp.int32
U32 = jnp.uint32

D_MODEL = 1024
BATCH = 8
SEQ = 2048
TOKENS = BATCH * SEQ
DEPTH = 2

HEAD_DIM = 64
HEADS_PER_GROUP = 4
GROUP_WIDTH = HEADS_PER_GROUP * HEAD_DIM
DILATION_PATTERNS = ((128, 1), (512, 4), (2048, 16))
N_GROUPS = len(DILATION_PATTERNS)
N_ATTN_HEADS = N_GROUPS * HEADS_PER_GROUP
ATTN_WIDTH = N_ATTN_HEADS * HEAD_DIM
QKV_WIDTH = 3 * ATTN_WIDTH
GROUP_QKV = 3 * GROUP_WIDTH
N_SIDE = 64
assert all(w // (2 * d) == N_SIDE for w, d in DILATION_PATTERNS)
POOL_WINDOWS = (2, 4, 8, 16)
POOL_GROUP_WIDTH = 128
POOL_WIDTH = len(POOL_WINDOWS) * POOL_GROUP_WIDTH
POOL_HALO = max(POOL_WINDOWS) // 2
N_BRANCHES = 2
GATE_WIDTH = N_BRANCHES * D_MODEL
IN_WIDTH = QKV_WIDTH + POOL_WIDTH + GATE_WIDTH
N_BUCKETS = 32
MAX_DISTANCE = 1024
N_EXPERT_GROUPS = 8
EXPERTS_PER_GROUP = 8
N_EXPERTS = N_EXPERT_GROUPS * EXPERTS_PER_GROUP
TOP_K = 2
D_EXPERT = 512
N_ASSIGN = TOKENS * TOP_K
EPS = 1e-6
NEG_INF = -1e30
LOG2_E = math.log2(math.e)
Q_SCALE = HEAD_DIM ** -0.5 * LOG2_E

LANES = 128
SUBLANES = 8
ROW_TILE = 512
N_TILES = TOKENS // ROW_TILE
TILES_PER_SEQ = SEQ // ROW_TILE
Q_BLOCK = 128
K_BLOCK = Q_BLOCK + 2 * N_SIDE
PAD_WAYS = 4
BLOCKS_IN_FLIGHT = 4
ROUTER_LANES = 128
EXPERT_LANE0 = N_EXPERT_GROUPS
VMEM_LIMIT = 56 * 1024 * 1024

RUN_ALIGN = SUBLANES
MOE_BLOCK = 336
HALF = D_MODEL // 2
X_WORDS = HALF + LANES
SLOT_CHUNK = 256
RUN_UNROLL = 4
W_STAGES = 2
MAX_SLOTS = -(-(TOP_K * ROW_TILE + N_EXPERTS * (RUN_ALIGN - 1)) // SLOT_CHUNK) * SLOT_CHUNK
SLOT_CHUNKS = (tuple((s, ROW_TILE) for s in range(0, TOP_K * ROW_TILE, ROW_TILE))
               + tuple((s, SLOT_CHUNK) for s in range(TOP_K * ROW_TILE, MAX_SLOTS, SLOT_CHUNK)))
COMBINE_BASE = MAX_SLOTS - SLOT_CHUNK
N_MOE_BLOCKS = -(-(N_ASSIGN + N_TILES * N_EXPERTS * (RUN_ALIGN - 1)
                   + N_EXPERTS * (MOE_BLOCK - RUN_ALIGN)) // MOE_BLOCK)
PAD_ROWS = N_MOE_BLOCKS * MOE_BLOCK
HIGH_HALF = 0xFFFF0000


def _cparams(*sem):
    return pltpu.CompilerParams(dimension_semantics=sem, vmem_limit_bytes=VMEM_LIMIT)


def _rms(h, g):
    r = lax.rsqrt(jnp.mean(h * h, axis=-1, keepdims=True) + EPS)
    return (h * r) * g


def _dot(a, b):
    return jnp.dot(a, b, preferred_element_type=F32)


def _row_spec(width):
    return pl.BlockSpec((ROW_TILE, width), lambda i, *_: (i, 0))


def _full_spec(a):
    return pl.BlockSpec(a.shape, lambda i, *_: (0,) * a.ndim)


def _bits(x):
    return lax.bitcast_convert_type(x, U32)


def _pack_halves(a, b):
    return (_bits(a) & jnp.uint32(HIGH_HALF)) | (_bits(b) >> 16)


def _unpack_halves(words):
    hi = lax.bitcast_convert_type(words & jnp.uint32(HIGH_HALF), F32)
    lo = lax.bitcast_convert_type(words << 16, F32)
    return hi.astype(BF16), lo.astype(BF16)


def _proj_kernel(h_ref, g_ref, w_ref, q0_ref, q1_ref, q2_ref, u_ref, gate_ref, slabs):
    xn = _rms(h_ref[...], g_ref[...]).astype(BF16)
    n_slabs = GROUP_QKV // LANES
    for g, out_ref in enumerate((q0_ref, q1_ref, q2_ref)):
        dilation = DILATION_PATTERNS[g][1]
        res = _dot(xn, w_ref[:, g * GROUP_QKV:(g + 1) * GROUP_QKV])
        res = jnp.concatenate([res[:, 0:GROUP_WIDTH] * Q_SCALE, res[:, GROUP_WIDTH:]], axis=1)
        if dilation == 1:
            out_ref[0, 0] = res.astype(BF16)
            continue
        for s in range(n_slabs):
            slabs[s] = res[:, s * LANES:(s + 1) * LANES]
        n = ROW_TILE // dilation
        for r in range(dilation):
            rows = [slabs[s, pl.ds(r, n, stride=dilation), :] for s in range(n_slabs)]
            out_ref[0, r] = jnp.concatenate(rows, axis=1).astype(BF16)
    u_ref[...] = _dot(xn, w_ref[:, QKV_WIDTH:QKV_WIDTH + POOL_WIDTH])
    gates = _dot(xn, w_ref[:, QKV_WIDTH + POOL_WIDTH:IN_WIDTH])
    gate_ref[...] = jax.nn.sigmoid(gates).astype(BF16)


def _project(h, g, w_bf16):
    qkv_shapes, qkv_specs = [], []
    for _, d in DILATION_PATTERNS:
        qkv_shapes.append(jax.ShapeDtypeStruct((BATCH, d, SEQ // d, GROUP_QKV), BF16))
        qkv_specs.append(pl.BlockSpec((1, d, ROW_TILE // d, GROUP_QKV),
                                      lambda i: (i // TILES_PER_SEQ, 0, i % TILES_PER_SEQ, 0)))
    res = pl.pallas_call(
        _proj_kernel,
        grid=(N_TILES,),
        in_specs=[_row_spec(D_MODEL), _full_spec(g), _full_spec(w_bf16)],
        out_specs=qkv_specs + [_row_spec(POOL_WIDTH), _row_spec(GATE_WIDTH)],
        out_shape=qkv_shapes + [jax.ShapeDtypeStruct((TOKENS, POOL_WIDTH), F32),
                                jax.ShapeDtypeStruct((TOKENS, GATE_WIDTH), BF16)],
        scratch_shapes=[pltpu.VMEM((GROUP_QKV // LANES, ROW_TILE, LANES), F32)],
        compiler_params=_cparams("parallel"),
        name="proj",
    )(h, g, w_bf16)
    return res[:N_GROUPS], res[N_GROUPS], res[N_GROUPS + 1]


def _attn_kernel(q0_ref, q1_ref, q2_ref, b0_ref, b1_ref, b2_ref, o_ref, pads, num, den, top):
    zpad = jnp.zeros((N_SIDE, GROUP_WIDTH), BF16)
    groups = ((q2_ref, b2_ref, 2), (q1_ref, b1_ref, 1), (q0_ref, b0_ref, 0))
    for order, (qkv_ref, bias_ref, g) in enumerate(groups):
        dilation = DILATION_PATTERNS[g][1]
        sub_len = SEQ // dilation
        for way in range(PAD_WAYS):
            for kv in range(2):
                pads[way, kv, 0:N_SIDE, :] = zpad
                pads[way, kv, N_SIDE + sub_len:2 * N_SIDE + sub_len, :] = zpad
        _attn_group(qkv_ref, bias_ref, pads, num, den, top, dilation=dilation, first=order == 0)

    def finish(c, carry):
        rows = pl.ds(pl.multiple_of(c * ROW_TILE, ROW_TILE), ROW_TILE)
        merged = [num[half, rows, :] / den[half, rows, :] for half in range(GROUP_WIDTH // LANES)]
        o_ref[0, rows, :] = jnp.concatenate(merged, axis=1).astype(BF16)
        return carry

    lax.fori_loop(0, SEQ // ROW_TILE, finish, 0)


def _attn_group(qkv_ref, bias_ref, pads, num, den, top, *, dilation, first):
    sub_len = SEQ // dilation
    n_blocks = sub_len // Q_BLOCK
    head_of_lane = lax.broadcasted_iota(I32, (1, GROUP_WIDTH), 1) // HEAD_DIM

    def by_head(cols):
        out = cols[HEADS_PER_GROUP - 1]
        for h in range(HEADS_PER_GROUP - 2, -1, -1):
            out = jnp.where(head_of_lane == h, cols[h], out)
        return out

    def load_keys(r, way):
        pads[way, 0, N_SIDE:N_SIDE + sub_len, :] = qkv_ref[0, r, :, GROUP_WIDTH:2 * GROUP_WIDTH]
        pads[way, 1, N_SIDE:N_SIDE + sub_len, :] = qkv_ref[0, r, :, 2 * GROUP_WIDTH:3 * GROUP_WIDTH]

    def block(r, way, i):
        if True:
            r0 = i * Q_BLOCK if isinstance(i, int) else pl.multiple_of(i * Q_BLOCK, Q_BLOCK)
            qb = qkv_ref[0, r, pl.ds(r0, Q_BLOCK), 0:GROUP_WIDTH]
            kw = pads[way, 0, pl.ds(r0, K_BLOCK), :]
            vw = pads[way, 1, pl.ds(r0, K_BLOCK), :]
            zero = jnp.zeros_like(qb)
            if isinstance(i, int):
                edge = int(i == 0) + 2 * int(i == n_blocks - 1)
            else:
                edge = (i == 0).astype(I32) + 2 * (i == n_blocks - 1).astype(I32)
            q_heads = jnp.concatenate(
                [jnp.where(head_of_lane == h, qb, zero) for h in range(HEADS_PER_GROUP)], axis=0)
            s = lax.dot_general(q_heads, kw, (((1,), (1,)), ((), ())), preferred_element_type=F32)
            s = s.reshape(HEADS_PER_GROUP, Q_BLOCK, K_BLOCK) + bias_ref[edge]
            m = jnp.max(s, axis=-1, keepdims=True)
            p = jnp.exp2(s - m)
            psum = jnp.sum(p, axis=-1, keepdims=True)
            pb = p.astype(BF16)
            p_cat = jnp.concatenate([pb[h] for h in range(HEADS_PER_GROUP)], axis=1)
            zv = jnp.zeros_like(vw)
            v_heads = jnp.concatenate(
                [jnp.where(head_of_lane == h, vw, zv) for h in range(HEADS_PER_GROUP)], axis=0)
            o = _dot(p_cat, v_heads)
            lse = m + jnp.log2(psum)
            o = o / by_head([psum[h] for h in range(HEADS_PER_GROUP)])
            lse_lanes = by_head([lse[h] for h in range(HEADS_PER_GROUP)])
            if dilation == 1:
                rows = pl.ds(r0, Q_BLOCK)
            else:
                rows = pl.ds(r + dilation * r0, Q_BLOCK, stride=dilation)
            for half in range(GROUP_WIDTH // LANES):
                lanes = slice(half * LANES, (half + 1) * LANES)
                if first:
                    num[half, rows, :] = o[:, lanes]
                    den[half, rows, :] = jnp.ones((Q_BLOCK, LANES), F32)
                    top[half, rows, :] = lse_lanes[:, lanes]
                else:
                    old = top[half, rows, :]
                    new = jnp.maximum(old, lse_lanes[:, lanes])
                    keep = jnp.exp2(old - new)
                    add = jnp.exp2(lse_lanes[:, lanes] - new)
                    num[half, rows, :] = keep * num[half, rows, :] + add * o[:, lanes]
                    den[half, rows, :] = keep * den[half, rows, :] + add
                    top[half, rows, :] = new

    def blocks_of(r, way):
        if n_blocks == 1:
            block(r, way, 0)
        else:
            def several(j, carry):
                for k in range(BLOCKS_IN_FLIGHT):
                    block(r, way, BLOCKS_IN_FLIGHT * j + k)
                return carry
            lax.fori_loop(0, n_blocks // BLOCKS_IN_FLIGHT, several, 0)

    if dilation == 1:
        load_keys(0, 0)
        blocks_of(0, 0)
    else:
        def residues(j, carry):
            for way in range(ways):
                load_keys(ways * j + way, way)
            for way in range(ways):
                blocks_of(ways * j + way, way)
            return carry
        ways = min(PAD_WAYS, dilation, max(1, 2 * BLOCKS_IN_FLIGHT // n_blocks))
        lax.fori_loop(0, dilation // ways, residues, 0)


def _t5_bucket(rel):
    nb = N_BUCKETS // 2
    ret = jnp.where(rel > 0, nb, 0)
    n = jnp.abs(rel)
    max_exact = nb // 2
    nf = jnp.maximum(n, max_exact).astype(F32)
    large = max_exact + (jnp.log(nf / max_exact) / math.log(MAX_DISTANCE / max_exact)
                         * (nb - max_exact)).astype(I32)
    large = jnp.minimum(large, nb - 1)
    return ret + jnp.where(n < max_exact, n, large)


def _band_bias(rel_bias, group, dilation):
    qi = jnp.arange(Q_BLOCK)[:, None]
    ki = jnp.arange(K_BLOCK)[None, :]
    bucket = _t5_bucket((ki - N_SIDE - qi) * dilation)
    tab = rel_bias[:, group * HEADS_PER_GROUP:(group + 1) * HEADS_PER_GROUP]
    onehot = (bucket[:, :, None] == jnp.arange(N_BUCKETS)[None, None, :]).astype(F32)
    bias = jnp.einsum('qkb,bh->hqk', onehot, tab, precision=lax.Precision.HIGHEST).astype(F32)
    bias = bias * LOG2_E
    band = jnp.abs(ki - N_SIDE - qi) <= N_SIDE
    variants = []
    for edge in range(4):
        ok = band
        if edge & 1:
            ok = ok & (ki >= N_SIDE)
        if edge & 2:
            ok = ok & (ki < Q_BLOCK + N_SIDE)
        variants.append(jnp.where(ok[None], bias, NEG_INF))
    return jnp.stack(variants)


def _attention(qkv, rel_bias):
    biases = [_band_bias(rel_bias, g, d) for g, (_, d) in enumerate(DILATION_PATTERNS)]
    slab = pltpu.VMEM((GROUP_WIDTH // LANES, SEQ, LANES), F32)
    pads = pltpu.VMEM((PAD_WAYS, 2, SEQ + 2 * N_SIDE, GROUP_WIDTH), BF16)
    out = pl.pallas_call(
        _attn_kernel,
        grid=(BATCH,),
        in_specs=[pl.BlockSpec((1,) + a.shape[1:], lambda b: (b, 0, 0, 0)) for a in qkv]
                 + [_full_spec(b) for b in biases],
        out_specs=pl.BlockSpec((1, SEQ, GROUP_WIDTH), lambda b: (b, 0, 0)),
        out_shape=jax.ShapeDtypeStruct((BATCH, SEQ, GROUP_WIDTH), BF16),
        scratch_shapes=[pads, slab, slab, slab],
        compiler_params=_cparams("parallel"),
        name="attn",
    )(*qkv, *biases)
    return out.reshape(TOKENS, GROUP_WIDTH)


def _mixout_kernel(h_ref, attn_ref, u_ref, uprev_ref, unext_ref, gate_ref,
                   pw_ref, ps_ref, wpa_ref, wpp_ref, wo_ref, out_ref):
    j = pl.program_id(0) % TILES_PER_SEQ
    y_attn = _dot(attn_ref[...], wpa_ref[...])

    u = u_ref[...]
    prev = jnp.where(j == 0, 0.0, uprev_ref[0])
    nxt = jnp.where(j == TILES_PER_SEQ - 1, 0.0, unext_ref[0])
    ext = jnp.concatenate([prev, u, nxt], axis=0)
    pos = j * ROW_TILE + lax.broadcasted_iota(I32, (ROW_TILE, 1), 0)
    mixed = []
    for gi, w in enumerate(POOL_WINDOWS):
        half = w // 2
        sl = slice(gi * POOL_GROUP_WIDTH, (gi + 1) * POOL_GROUP_WIDTH)
        run = ext[:, sl]
        span = 1
        while 2 * span < w:
            run = run[:-span] + run[span:]
            span *= 2
        lo = POOL_HALO - half
        acc = run[lo:lo + ROW_TILE] + run[lo + half:lo + half + ROW_TILE]
        cnt = (jnp.minimum(pos + half, SEQ) - jnp.maximum(pos - half, 0)).astype(F32)
        pooled = acc / cnt - u[:, sl]
        mixed.append(_dot(pooled.astype(BF16), pw_ref[gi]) * ps_ref[:, sl])
    y_pool = _dot(jnp.concatenate(mixed, axis=1).astype(BF16), wpp_ref[...])

    y = (gate_ref[:, 0:D_MODEL] * y_attn.astype(BF16)
         + gate_ref[:, D_MODEL:GATE_WIDTH] * y_pool.astype(BF16))
    out_ref[...] = h_ref[...] + _dot(y, wo_ref[...])


def _mix_out(h, attn, u, gates, pool_w, pool_scale, w_proj_attn, w_proj_pool, w_out):
    halo_blocks = ROW_TILE // POOL_HALO
    u3 = u.reshape(TOKENS // POOL_HALO, POOL_HALO, POOL_WIDTH)
    last = TOKENS // POOL_HALO - 1
    prev_spec = pl.BlockSpec((1, POOL_HALO, POOL_WIDTH),
                             lambda i: (jnp.maximum(i * halo_blocks - 1, 0), 0, 0))
    next_spec = pl.BlockSpec((1, POOL_HALO, POOL_WIDTH),
                             lambda i: (jnp.minimum((i + 1) * halo_blocks, last), 0, 0))
    weights = (pool_w, pool_scale, w_proj_attn, w_proj_pool, w_out)
    return pl.pallas_call(
        _mixout_kernel,
        grid=(N_TILES,),
        in_specs=[_row_spec(D_MODEL), _row_spec(GROUP_WIDTH),
                  _row_spec(POOL_WIDTH), prev_spec, next_spec, _row_spec(GATE_WIDTH)]
                 + [_full_spec(w) for w in weights],
        out_specs=_row_spec(D_MODEL),
        out_shape=jax.ShapeDtypeStruct((TOKENS, D_MODEL), F32),
        compiler_params=_cparams("parallel"),
        name="mixout",
    )(h, attn, u, u3, u3, gates, *weights)


def _router_kernel(h_ref, g_ref, wr_ref, xn_ref, ri_ref, rw_ref, cnt_ref):
    xn = _rms(h_ref[...], g_ref[...])
    hi = xn.astype(BF16)
    xn_ref[...] = hi
    lo = (xn - hi.astype(F32)).astype(BF16)
    both = _dot(hi, wr_ref[...])
    lg = both[:, 0:ROUTER_LANES] + (both[:, ROUTER_LANES:] + _dot(lo, wr_ref[:, 0:ROUTER_LANES]))

    lane = lax.broadcasted_iota(I32, (ROW_TILE, ROUTER_LANES), 1)
    lanef = lane.astype(F32)
    low = jnp.float32(-3.0e38)
    far = jnp.float32(ROUTER_LANES)
    first = lambda hit: jnp.min(jnp.where(hit, lanef, far), axis=-1, keepdims=True)

    is_group = lane < N_EXPERT_GROUPS
    gl = jnp.where(is_group, lg, low)
    gmax = jnp.max(gl, axis=-1, keepdims=True)
    gidx = first(gl == gmax).astype(I32)
    gden = jnp.sum(jnp.where(is_group, jnp.exp(gl - gmax), 0.0), axis=-1, keepdims=True)
    g_p = 1.0 / gden

    in_group = ((lane >= EXPERT_LANE0) & (lane < EXPERT_LANE0 + N_EXPERTS)
                & ((lane - EXPERT_LANE0) // EXPERTS_PER_GROUP == gidx))
    el = jnp.where(in_group, lg, low)
    t1 = jnp.max(el, axis=-1, keepdims=True)
    l1 = first(in_group & (el == t1))
    rest = in_group & (lanef != l1)
    el2 = jnp.where(rest, lg, low)
    t2 = jnp.max(el2, axis=-1, keepdims=True)
    l2 = first(rest & (el2 == t2))
    e2 = jnp.exp(t2 - t1)
    w1 = g_p * (1.0 / (1.0 + e2))
    w2 = g_p * (e2 / (1.0 + e2))

    hit1 = lanef == l1
    hit2 = lanef == l2
    onehot = (hit1 | hit2).astype(BF16)
    ri = lax.broadcasted_iota(I32, (ROW_TILE, ROW_TILE), 0)
    ci = lax.broadcasted_iota(I32, (ROW_TILE, ROW_TILE), 1)
    before = (ci < ri).astype(BF16)
    seen = _dot(before, onehot)
    r1 = jnp.sum(jnp.where(hit1, seen, 0.0), axis=-1, keepdims=True)
    r2 = jnp.sum(jnp.where(hit2, seen, 0.0), axis=-1, keepdims=True)

    packed = jnp.zeros((ROW_TILE, ROUTER_LANES), F32)
    for k, v in enumerate((l1, l2, r1, r2)):
        packed = jnp.where(lane == k, v, packed)
    ri_ref[...] = packed.astype(I32)
    rw_ref[...] = jnp.where(lane == 0, w1, jnp.where(lane == 1, w2, 0.0))
    counts = jnp.sum(onehot.astype(F32), axis=0, keepdims=True)
    cnt_ref[0] = jnp.broadcast_to(counts, (SUBLANES, ROUTER_LANES))


def _route(h, g, wr_split):
    return pl.pallas_call(
        _router_kernel,
        grid=(N_TILES,),
        in_specs=[_row_spec(D_MODEL), _full_spec(g), _full_spec(wr_split)],
        out_specs=[_row_spec(D_MODEL), _row_spec(ROUTER_LANES), _row_spec(ROUTER_LANES),
                   pl.BlockSpec((1, SUBLANES, ROUTER_LANES), lambda i: (i, 0, 0))],
        out_shape=[jax.ShapeDtypeStruct((TOKENS, D_MODEL), BF16),
                   jax.ShapeDtypeStruct((TOKENS, ROUTER_LANES), I32),
                   jax.ShapeDtypeStruct((TOKENS, ROUTER_LANES), F32),
                   jax.ShapeDtypeStruct((N_TILES, SUBLANES, ROUTER_LANES), F32)],
        compiler_params=_cparams("parallel"),
        name="router",
    )(h, g, wr_split)


def _slots(ri_ref, off_ref):
    ri = ri_ref[...]
    lane = lax.broadcasted_iota(I32, (ROW_TILE, ROUTER_LANES), 1)
    off = off_ref[0, 0:1, :]
    pick = lambda k: jnp.sum(jnp.where(lane == ri[:, k:k + 1], off, 0.0), axis=-1, keepdims=True)
    return (pick(0) + ri[:, 2:3].astype(F32), pick(1) + ri[:, 3:4].astype(F32))


def _as_rows(cols):
    eye = (lax.broadcasted_iota(I32, (ROW_TILE, ROW_TILE), 0)
           == lax.broadcasted_iota(I32, (ROW_TILE, ROW_TILE), 1))
    return [jnp.sum(jnp.where(eye, c, 0.0), axis=0, keepdims=True) for c in cols]


def _run_copy(tile, e, loff_s, c8_s, dst_s, buf_ref, slot, hbm_ref, sem, to_hbm):
    k = tile * N_EXPERTS + e
    n = pl.multiple_of(c8_s[k], RUN_ALIGN)
    vm = buf_ref.at[slot, pl.ds(pl.multiple_of(loff_s[k], RUN_ALIGN), n)]
    hb = hbm_ref.at[pl.ds(pl.multiple_of(dst_s[k], RUN_ALIGN), n)]
    src, dst = (vm, hb) if to_hbm else (hb, vm)
    return n, pltpu.make_async_copy(src, dst, sem.at[slot])


def _loop(lo, hi, fn, unroll=1):
    def body(e, carry):
        fn(e)
        return carry
    lax.fori_loop(lo, hi, body, 0, unroll=unroll)


def _wait_rows(n, buf_ref, slot, hbm_ref, sem, to_hbm):
    vm = buf_ref.at[slot, pl.ds(0, n)]
    hb = hbm_ref.at[pl.ds(0, n)]
    src, dst = (vm, hb) if to_hbm else (hb, vm)
    pltpu.make_async_copy(src, dst, sem.at[slot]).wait()


def _start(n, cp):
    @pl.when(n > 0)
    def _():
        cp.start()


def _wait(n, cp):
    @pl.when(n > 0)
    def _():
        cp.wait()


def _dispatch_kernel(loff_s, c8_s, dst_s, nslot_s, zdst_s, zcnt_s, nact_s,
                     xn_ref, ri_ref, rw_ref, off_ref, xpad_hbm, sorted_buf, zero_buf, sem, zsem):
    i = pl.program_id(0)
    slot = i % 2
    last = pl.num_programs(0) - 1

    def zero_copy(e):
        n = pl.multiple_of(zcnt_s[e], RUN_ALIGN)
        dst = xpad_hbm.at[pl.ds(pl.multiple_of(zdst_s[e], RUN_ALIGN), n)]
        return n, pltpu.make_async_copy(zero_buf.at[pl.ds(0, n)], dst, zsem)

    def tail_copy(b):
        dst = xpad_hbm.at[pl.ds(pl.multiple_of(b * MOE_BLOCK, MOE_BLOCK), MOE_BLOCK)]
        return pltpu.make_async_copy(zero_buf, dst, zsem)

    @pl.when(i == 0)
    def _():
        zero_buf[...] = jnp.zeros_like(zero_buf)
        _loop(0, N_EXPERTS, lambda e: _start(*zero_copy(e)))
        _loop(nact_s[0], N_MOE_BLOCKS, lambda b: tail_copy(b).start())

    s1, s2 = _slots(ri_ref, off_ref)
    rw = rw_ref[...]
    s1_row, s2_row, w1_row, w2_row = _as_rows([s1, s2, rw[:, 0:1], rw[:, 1:2]])
    xn = xn_ref[...]
    for first, size in SLOT_CHUNKS:
        @pl.when(first < nslot_s[i])
        def _():
            rows = slice(first, first + size)
            lane = lax.broadcasted_iota(I32, (size, LANES), 1)
            srow = (first + lax.broadcasted_iota(I32, (size, ROW_TILE), 0)).astype(F32)
            hit1 = srow == s1_row
            hit2 = srow == s2_row
            xs = _dot((hit1 | hit2).astype(BF16), xn)
            sorted_buf[slot, rows, 0:HALF] = _pack_halves(xs[:, 0:HALF], xs[:, HALF:D_MODEL])
            ws = jnp.sum(jnp.where(hit1, w1_row, 0.0) + jnp.where(hit2, w2_row, 0.0),
                         axis=-1, keepdims=True)
            sorted_buf[slot, rows, HALF:X_WORDS] = jnp.where(lane == 0, _bits(ws), jnp.uint32(0))

    copy = lambda tile, sl: (lambda e: _run_copy(tile, e, loff_s, c8_s, dst_s, sorted_buf, sl,
                                                 xpad_hbm, sem, True))
    mine = copy(i, slot)
    _loop(0, N_EXPERTS, lambda e: _start(*mine(e)), unroll=RUN_UNROLL)
    tile_rows = lambda t: pl.multiple_of(nslot_s[t], RUN_ALIGN)

    @pl.when(i > 0)
    def _():
        _wait_rows(tile_rows(i - 1), sorted_buf, 1 - slot, xpad_hbm, sem, True)

    @pl.when(i == last)
    def _():
        _wait_rows(tile_rows(i), sorted_buf, slot, xpad_hbm, sem, True)
        _loop(0, N_EXPERTS, lambda e: _wait(*zero_copy(e)))
        _loop(nact_s[0], N_MOE_BLOCKS, lambda b: tail_copy(b).wait())


def _dispatch(tables, xn, route_i, route_w):
    tile_row = pl.BlockSpec((1, SUBLANES, ROUTER_LANES), lambda i, *_: (i, 0, 0))
    grid_spec = pltpu.PrefetchScalarGridSpec(
        num_scalar_prefetch=7,
        grid=(N_TILES,),
        in_specs=[_row_spec(D_MODEL), _row_spec(ROUTER_LANES), _row_spec(ROUTER_LANES), tile_row],
        out_specs=pl.BlockSpec(memory_space=pl.ANY),
        scratch_shapes=[pltpu.VMEM((2, MAX_SLOTS, X_WORDS), U32),
                        pltpu.VMEM((MOE_BLOCK, X_WORDS), U32),
                        pltpu.SemaphoreType.DMA((2,)),
                        pltpu.SemaphoreType.DMA(())],
    )
    return pl.pallas_call(
        _dispatch_kernel,
        grid_spec=grid_spec,
        out_shape=jax.ShapeDtypeStruct((PAD_ROWS, X_WORDS), U32),
        compiler_params=_cparams("arbitrary"),
        name="dispatch",
    )(tables["loff"], tables["c8"], tables["dst"], tables["nslot"], tables["zdst"], tables["zcnt"],
      tables["n_active"], xn, route_i, route_w, tables["offrow"])


def _combine_kernel(loff_s, c8_s, dst_s, nslot_s,
                    h_ref, ri_ref, off_ref, g_ref, ypad_hbm, out_ref,
                    ybuf, sem, *, final_norm):
    i = pl.program_id(0)
    slot = i % 2
    last = pl.num_programs(0) - 1
    fetch = lambda tile, sl: (lambda e: _run_copy(tile, e, loff_s, c8_s, dst_s, ybuf, sl,
                                                  ypad_hbm, sem, False))

    @pl.when(i == 0)
    def _():
        ybuf[...] = jnp.zeros_like(ybuf)
        first = fetch(i, slot)
        _loop(0, N_EXPERTS, lambda e: _start(*first(e)), unroll=RUN_UNROLL)

    @pl.when(i < last)
    def _():
        nxt = fetch(i + 1, 1 - slot)
        _loop(0, N_EXPERTS, lambda e: _start(*nxt(e)), unroll=RUN_UNROLL)

    _wait_rows(pl.multiple_of(nslot_s[i], RUN_ALIGN), ybuf, slot, ypad_hbm, sem, False)

    s1, s2 = _slots(ri_ref, off_ref)

    def gathered(first, size):
        scol = (first + lax.broadcasted_iota(I32, (ROW_TILE, size), 1)).astype(F32)
        pick = ((scol == s1) | (scol == s2)).astype(BF16)
        y_hi, y_lo = _unpack_halves(ybuf[slot, first:first + size, :])
        return jnp.concatenate([_dot(pick, y_hi), _dot(pick, y_lo)], axis=1)

    finish = (lambda v: _rms(v, g_ref[...])) if final_norm else (lambda v: v)
    h = h_ref[...] + gathered(0, COMBINE_BASE)
    spill = nslot_s[i] > COMBINE_BASE

    @pl.when(jnp.logical_not(spill))
    def _():
        out_ref[...] = finish(h)

    @pl.when(spill)
    def _():
        out_ref[...] = finish(h + gathered(COMBINE_BASE, MAX_SLOTS - COMBINE_BASE))


def _combine(tables, h, route_i, y_pad, g, final_norm):
    tile_row = pl.BlockSpec((1, SUBLANES, ROUTER_LANES), lambda i, *_: (i, 0, 0))
    grid_spec = pltpu.PrefetchScalarGridSpec(
        num_scalar_prefetch=4,
        grid=(N_TILES,),
        in_specs=[_row_spec(D_MODEL), _row_spec(ROUTER_LANES), tile_row, _full_spec(g),
                  pl.BlockSpec(memory_space=pl.ANY)],
        out_specs=_row_spec(D_MODEL),
        scratch_shapes=[pltpu.VMEM((2, MAX_SLOTS, HALF), U32),
                        pltpu.SemaphoreType.DMA((2,))],
    )
    return pl.pallas_call(
        functools.partial(_combine_kernel, final_norm=final_norm),
        grid_spec=grid_spec,
        out_shape=jax.ShapeDtypeStruct((TOKENS, D_MODEL), F32),
        compiler_params=_cparams("arbitrary"),
        name="combine",
    )(tables["loff"], tables["c8"], tables["dst"], tables["nslot"],
      h, route_i, tables["offrow"], g, y_pad)


def _expert_kernel(be_ref, slot_ref, next_ref, nact_ref, x_ref, wg_hbm, wu_hbm, wd_hbm, y_hbm,
                   wg_f, wu_f, wd_f, wg_b, wu_b, wd_b, ybuf, zero_buf, sem, ysem, zsem, *, layer):
    i = pl.program_id(0)
    n_active = nact_ref[0]
    staged = ((wg_hbm, wg_f, wg_b), (wu_hbm, wu_f, wu_b), (wd_hbm, wd_f, wd_b))

    def fetch(e, slot):
        return [pltpu.make_async_copy(hbm.at[layer, e], f32.at[slot], sem.at[slot, k])
                for k, (hbm, f32, _) in enumerate(staged)]

    def block_rows(b):
        return y_hbm.at[pl.ds(pl.multiple_of(b * MOE_BLOCK, MOE_BLOCK), MOE_BLOCK)]

    def put(b):
        return pltpu.make_async_copy(ybuf.at[b % 2], block_rows(b), ysem.at[b % 2])

    def tail_copy(b):
        return pltpu.make_async_copy(zero_buf, block_rows(b), zsem)

    @pl.when(i == 0)
    def _():
        zero_buf[...] = jnp.zeros_like(zero_buf)
        _loop(n_active, N_MOE_BLOCKS, lambda b: tail_copy(b).start())

    @pl.when(i == pl.num_programs(0) - 1)
    def _():
        _loop(n_active, N_MOE_BLOCKS, lambda b: tail_copy(b).wait())

    @pl.when(i < n_active)
    def _():
        e = be_ref[i]
        slot = slot_ref[i]

        def request(ahead, block):
            nxt = next_ref[(ahead - 1) * N_MOE_BLOCKS + block]

            @pl.when(nxt >= 0)
            def _():
                for cp in fetch(nxt, (slot + ahead) % W_STAGES):
                    cp.start()

        @pl.when((i == 0) | (e != be_ref[jnp.maximum(i - 1, 0)]))
        def _():
            @pl.when(i == 0)
            def _():
                for cp in fetch(e, slot):
                    cp.start()
                for ahead in range(1, W_STAGES - 1):
                    request(ahead, i)
            for cp in fetch(e, slot):
                cp.wait()
            request(W_STAGES - 1, i)
            for _, f32, b16 in staged:
                b16[...] = f32[slot].astype(BF16)

        x_hi, x_lo = _unpack_halves(x_ref[:, 0:HALF])
        row_w = lax.bitcast_convert_type(x_ref[:, HALF:HALF + 1], F32)
        gate = _dot(x_hi, wg_b[0:HALF, :]) + _dot(x_lo, wg_b[HALF:D_MODEL, :])
        up = _dot(x_hi, wu_b[0:HALF, :]) + _dot(x_lo, wu_b[HALF:D_MODEL, :])
        hmid = (jax.nn.silu(gate) * up).astype(BF16)
        y = (_dot(hmid, wd_b[...]) * row_w).astype(BF16).astype(F32)
        ybuf[i % 2] = _pack_halves(y[:, 0:HALF], y[:, HALF:D_MODEL])
        put(i).start()

        @pl.when(i > 0)
        def _():
            put(i - 1).wait()

        @pl.when(i == n_active - 1)
        def _():
            put(i).wait()


def _experts(layer, tables, x_pad, w_gate, w_up, w_down):
    hbm = pl.BlockSpec(memory_space=pl.ANY)
    up_shape, down_shape = (D_MODEL, D_EXPERT), (D_EXPERT, D_MODEL)
    grid_spec = pltpu.PrefetchScalarGridSpec(
        num_scalar_prefetch=4,
        grid=(N_MOE_BLOCKS,),
        in_specs=[pl.BlockSpec((MOE_BLOCK, X_WORDS),
                               lambda i, be, sl, nx, na: (jnp.minimum(i, na[0] - 1), 0)),
                  hbm, hbm, hbm],
        out_specs=hbm,
        scratch_shapes=[pltpu.VMEM((W_STAGES,) + up_shape, F32),
                        pltpu.VMEM((W_STAGES,) + up_shape, F32),
                        pltpu.VMEM((W_STAGES,) + down_shape, F32),
                        pltpu.VMEM(up_shape, BF16), pltpu.VMEM(up_shape, BF16),
                        pltpu.VMEM(down_shape, BF16),
                        pltpu.VMEM((2, MOE_BLOCK, HALF), U32),
                        pltpu.VMEM((MOE_BLOCK, HALF), U32),
                        pltpu.SemaphoreType.DMA((W_STAGES, 3)),
                        pltpu.SemaphoreType.DMA((2,)),
                        pltpu.SemaphoreType.DMA(())],
    )
    return pl.pallas_call(
        functools.partial(_expert_kernel, layer=layer),
        grid_spec=grid_spec,
        out_shape=jax.ShapeDtypeStruct((PAD_ROWS, HALF), U32),
        compiler_params=_cparams("arbitrary"),
        name="experts",
    )(tables["blk_expert"], tables["blk_slot"], tables["blk_next"], tables["n_active"],
      x_pad, w_gate, w_up, w_down)


def _routing_tables(tile_counts):
    cnt = tile_counts[:, 0, EXPERT_LANE0:EXPERT_LANE0 + N_EXPERTS].astype(I32)
    c8 = (cnt + RUN_ALIGN - 1) // RUN_ALIGN * RUN_ALIGN
    loff = jnp.cumsum(c8, axis=1) - c8
    nslot = jnp.sum(c8, axis=1)
    tot = jnp.sum(c8, axis=0)
    padded = (tot + MOE_BLOCK - 1) // MOE_BLOCK * MOE_BLOCK
    end = jnp.cumsum(padded)
    base = end - padded
    dst = base[None, :] + jnp.cumsum(c8, axis=0) - c8
    blk_start = jnp.arange(N_MOE_BLOCKS, dtype=I32) * MOE_BLOCK
    blk_expert = jnp.minimum(jnp.sum((end[None, :] <= blk_start[:, None]).astype(I32), axis=1),
                             N_EXPERTS - 1).astype(I32)
    offrow = jnp.zeros((N_TILES, ROUTER_LANES), F32).at[:, EXPERT_LANE0:EXPERT_LANE0 + N_EXPERTS].set(
        loff.astype(F32))
    offrow = jnp.broadcast_to(offrow[:, None, :], (N_TILES, SUBLANES, ROUTER_LANES))
    experts = jnp.arange(N_EXPERTS, dtype=I32)
    present = padded > 0
    ordinal = jnp.cumsum(present.astype(I32)) - 1
    at_or_after = lax.cummin(jnp.where(present, experts, N_EXPERTS), reverse=True)
    after = jnp.concatenate([at_or_after[1:], jnp.full((1,), N_EXPERTS, I32)])
    hop = jnp.concatenate([after, jnp.full((1,), N_EXPERTS, I32)])
    pick = (blk_expert[:, None] == experts[None, :]).astype(I32)
    blk_slot = jnp.sum(pick * (ordinal % W_STAGES)[None, :], axis=1).astype(I32)
    ahead, blk_next = experts, []
    for _ in range(W_STAGES - 1):
        ahead = jnp.sum((ahead[:, None] == jnp.arange(N_EXPERTS + 1)[None, :]) * hop[None, :], axis=1)
        per_block = jnp.sum(pick * ahead[None, :], axis=1)
        blk_next.append(jnp.where(per_block >= N_EXPERTS, -1, per_block))
    blk_next = jnp.concatenate(blk_next).astype(I32)
    return {
        "blk_slot": blk_slot, "blk_next": blk_next,
        "loff": loff.reshape(-1).astype(I32), "c8": c8.reshape(-1).astype(I32),
        "dst": dst.reshape(-1).astype(I32), "nslot": nslot.astype(I32),
        "zdst": (base + tot).astype(I32), "zcnt": (padded - tot).astype(I32),
        "offrow": offrow, "blk_expert": blk_expert,
        "n_active": (end[-1:] // MOE_BLOCK).astype(I32),
    }


def _split_router_weights(w_router_group, w_router_expert):
    w_e = w_router_expert.transpose(1, 0, 2).reshape(D_MODEL, N_EXPERTS)
    w = jnp.concatenate([w_router_group, w_e], axis=1)
    w = jnp.pad(w, ((0, 0), (0, ROUTER_LANES - w.shape[1])))
    hi = w.astype(BF16)
    lo = (w - hi.astype(F32)).astype(BF16)
    return jnp.concatenate([hi, lo], axis=1)


def _input_weights(w):
    cols = []
    for g in range(N_GROUPS):
        for part in range(3):
            c0 = part * ATTN_WIDTH + g * GROUP_WIDTH
            cols.append(w[:, c0:c0 + GROUP_WIDTH])
    cols.append(w[:, QKV_WIDTH:])
    return jnp.concatenate(cols, axis=1).astype(BF16)


def kernel(x, rel_bias, norm_mix_g, w_in, pool_w, pool_scale, w_proj_attn, w_proj_pool, w_out,
           norm_ffn_g, w_router_group, w_router_expert, w_gate_e, w_up_e, w_down_e, norm_final_g):
    h = x.reshape(TOKENS, D_MODEL)
    for l in range(DEPTH):
        qkv, u, gates = _project(h, norm_mix_g[l][None], _input_weights(w_in[l]))
        h = _mix_out(h, _attention(qkv, rel_bias), u, gates, pool_w[l].astype(BF16), pool_scale[l][None],
                     w_proj_attn[l].astype(BF16), w_proj_pool[l].astype(BF16), w_out[l].astype(BF16))
        xn, route_i, route_w, tile_counts = _route(
            h, norm_ffn_g[l][None], _split_router_weights(w_router_group[l], w_router_expert[l]))
        tables = _routing_tables(tile_counts)
        x_pad = _dispatch(tables, xn, route_i, route_w)
        y_pad = _experts(l, tables, x_pad, w_gate_e, w_up_e, w_down_e)
        h = _combine(tables, h, route_i, y_pad, norm_final_g[None], l == DEPTH - 1)
    return h.reshape(BATCH, SEQ, D_MODEL)
```

```python
import functools
import math

import jax
import jax.numpy as jnp
from jax import lax
from jax.experimental import pallas as pl
from jax.experimental.pallas import tpu as pltpu

F32 = jnp.float32
BF16 = jnp.bfloat16
I32 = jnp.int32
U32 = jnp.uint32

D_MODEL = 1024
BATCH = 8
SEQ = 2048
TOKENS = BATCH * SEQ
DEPTH = 2

HEAD_DIM = 64
HEADS_PER_GROUP = 4
GROUP_WIDTH = HEADS_PER_GROUP * HEAD_DIM
DILATION_PATTERNS = ((128, 1), (512, 4), (2048, 16))
N_GROUPS = len(DILATION_PATTERNS)
N_ATTN_HEADS = N_GROUPS * HEADS_PER_GROUP
ATTN_WIDTH = N_ATTN_HEADS * HEAD_DIM
QKV_WIDTH = 3 * ATTN_WIDTH
GROUP_QKV = 3 * GROUP_WIDTH
N_SIDE = 64
assert all(w // (2 * d) == N_SIDE for w, d in DILATION_PATTERNS)
POOL_WINDOWS = (2, 4, 8, 16)
POOL_GROUP_WIDTH = 128
POOL_WIDTH = len(POOL_WINDOWS) * POOL_GROUP_WIDTH
POOL_HALO = max(POOL_WINDOWS) // 2
N_BRANCHES = 2
GATE_WIDTH = N_BRANCHES * D_MODEL
IN_WIDTH = QKV_WIDTH + POOL_WIDTH + GATE_WIDTH
N_BUCKETS = 32
MAX_DISTANCE = 1024
N_EXPERT_GROUPS = 8
EXPERTS_PER_GROUP = 8
N_EXPERTS = N_EXPERT_GROUPS * EXPERTS_PER_GROUP
TOP_K = 2
D_EXPERT = 512
N_ASSIGN = TOKENS * TOP_K
EPS = 1e-6
NEG_INF = -1e30
LOG2_E = math.log2(math.e)
Q_SCALE = HEAD_DIM ** -0.5 * LOG2_E

LANES = 128
SUBLANES = 8
ROW_TILE = 512
N_TILES = TOKENS // ROW_TILE
TILES_PER_SEQ = SEQ // ROW_TILE
Q_BLOCK = 128
K_BLOCK = Q_BLOCK + 2 * N_SIDE
PAD_WAYS = 4
BLOCKS_IN_FLIGHT = 4
ROUTER_LANES = 128
EXPERT_LANE0 = N_EXPERT_GROUPS
VMEM_LIMIT = 56 * 1024 * 1024

RUN_ALIGN = SUBLANES
MOE_BLOCK = 336
HALF = D_MODEL // 2
X_WORDS = HALF + LANES
SLOT_CHUNK = 256
RUN_UNROLL = 4
W_STAGES = 2
MAX_SLOTS = -(-(TOP_K * ROW_TILE + N_EXPERTS * (RUN_ALIGN - 1)) // SLOT_CHUNK) * SLOT_CHUNK
SLOT_CHUNKS = (tuple((s, ROW_TILE) for s in range(0, TOP_K * ROW_TILE, ROW_TILE))
               + tuple((s, SLOT_CHUNK) for s in range(TOP_K * ROW_TILE, MAX_SLOTS, SLOT_CHUNK)))
COMBINE_BASE = MAX_SLOTS - SLOT_CHUNK
N_MOE_BLOCKS = -(-(N_ASSIGN + N_TILES * N_EXPERTS * (RUN_ALIGN - 1)
                   + N_EXPERTS * (MOE_BLOCK - RUN_ALIGN)) // MOE_BLOCK)
PAD_ROWS = N_MOE_BLOCKS * MOE_BLOCK
HIGH_HALF = 0xFFFF0000


def _cparams(*sem):
    return pltpu.CompilerParams(dimension_semantics=sem, vmem_limit_bytes=VMEM_LIMIT)


def _rms(h, g):
    r = lax.rsqrt(jnp.mean(h * h, axis=-1, keepdims=True) + EPS)
    return (h * r) * g


def _dot(a, b):
    return jnp.dot(a, b, preferred_element_type=F32)


def _row_spec(width):
    return pl.BlockSpec((ROW_TILE, width), lambda i, *_: (i, 0))


def _full_spec(a):
    return pl.BlockSpec(a.shape, lambda i, *_: (0,) * a.ndim)


def _bits(x):
    return lax.bitcast_convert_type(x, U32)


def _pack_halves(a, b):
    return (_bits(a) & jnp.uint32(HIGH_HALF)) | (_bits(b) >> 16)


def _unpack_halves(words):
    hi = lax.bitcast_convert_type(words & jnp.uint32(HIGH_HALF), F32)
    lo = lax.bitcast_convert_type(words << 16, F32)
    return hi.astype(BF16), lo.astype(BF16)


def _proj_kernel(h_ref, g_ref, w_ref, q0_ref, q1_ref, q2_ref, u_ref, gate_ref, slabs):
    xn = _rms(h_ref[...], g_ref[...]).astype(BF16)
    n_slabs = GROUP_QKV // LANES
    for g, out_ref in enumerate((q0_ref, q1_ref, q2_ref)):
        dilation = DILATION_PATTERNS[g][1]
        res = _dot(xn, w_ref[:, g * GROUP_QKV:(g + 1) * GROUP_QKV])
        res = jnp.concatenate([res[:, 0:GROUP_WIDTH] * Q_SCALE, res[:, GROUP_WIDTH:]], axis=1)
        if dilation == 1:
            out_ref[0, 0] = res.astype(BF16)
            continue
        for s in range(n_slabs):
            slabs[s] = res[:, s * LANES:(s + 1) * LANES]
        n = ROW_TILE // dilation
        for r in range(dilation):
            rows = [slabs[s, pl.ds(r, n, stride=dilation), :] for s in range(n_slabs)]
            out_ref[0, r] = jnp.concatenate(rows, axis=1).astype(BF16)
    u_ref[...] = _dot(xn, w_ref[:, QKV_WIDTH:QKV_WIDTH + POOL_WIDTH])
    gates = _dot(xn, w_ref[:, QKV_WIDTH + POOL_WIDTH:IN_WIDTH])
    gate_ref[...] = jax.nn.sigmoid(gates).astype(BF16)


def _project(h, g, w_bf16):
    qkv_shapes, qkv_specs = [], []
    for _, d in DILATION_PATTERNS:
        qkv_shapes.append(jax.ShapeDtypeStruct((BATCH, d, SEQ // d, GROUP_QKV), BF16))
        qkv_specs.append(pl.BlockSpec((1, d, ROW_TILE // d, GROUP_QKV),
                                      lambda i: (i // TILES_PER_SEQ, 0, i % TILES_PER_SEQ, 0)))
    res = pl.pallas_call(
        _proj_kernel,
        grid=(N_TILES,),
        in_specs=[_row_spec(D_MODEL), _full_spec(g), _full_spec(w_bf16)],
        out_specs=qkv_specs + [_row_spec(POOL_WIDTH), _row_spec(GATE_WIDTH)],
        out_shape=qkv_shapes + [jax.ShapeDtypeStruct((TOKENS, POOL_WIDTH), F32),
                                jax.ShapeDtypeStruct((TOKENS, GATE_WIDTH), BF16)],
        scratch_shapes=[pltpu.VMEM((GROUP_QKV // LANES, ROW_TILE, LANES), F32)],
        compiler_params=_cparams("parallel"),
        name="proj",
    )(h, g, w_bf16)
    return res[:N_GROUPS], res[N_GROUPS], res[N_GROUPS + 1]


def _attn_kernel(q0_ref, q1_ref, q2_ref, b0_ref, b1_ref, b2_ref, o_ref, pads, num, den, top):
    zpad = jnp.zeros((N_SIDE, GROUP_WIDTH), BF16)
    groups = ((q2_ref, b2_ref, 2), (q1_ref, b1_ref, 1), (q0_ref, b0_ref, 0))
    for order, (qkv_ref, bias_ref, g) in enumerate(groups):
        dilation = DILATION_PATTERNS[g][1]
        sub_len = SEQ // dilation
        for way in range(PAD_WAYS):
            for kv in range(2):
                pads[way, kv, 0:N_SIDE, :] = zpad
                pads[way, kv, N_SIDE + sub_len:2 * N_SIDE + sub_len, :] = zpad
        _attn_group(qkv_ref, bias_ref, pads, num, den, top, dilation=dilation, first=order == 0)

    def finish(c, carry):
        rows = pl.ds(pl.multiple_of(c * ROW_TILE, ROW_TILE), ROW_TILE)
        merged = [num[half, rows, :] / den[half, rows, :] for half in range(GROUP_WIDTH // LANES)]
        o_ref[0, rows, :] = jnp.concatenate(merged, axis=1).astype(BF16)
        return carry

    lax.fori_loop(0, SEQ // ROW_TILE, finish, 0)


def _attn_group(qkv_ref, bias_ref, pads, num, den, top, *, dilation, first):
    sub_len = SEQ // dilation
    n_blocks = sub_len // Q_BLOCK
    head_of_lane = lax.broadcasted_iota(I32, (1, GROUP_WIDTH), 1) // HEAD_DIM

    def by_head(cols):
        out = cols[HEADS_PER_GROUP - 1]
        for h in range(HEADS_PER_GROUP - 2, -1, -1):
            out = jnp.where(head_of_lane == h, cols[h], out)
        return out

    def load_keys(r, way):
        pads[way, 0, N_SIDE:N_SIDE + sub_len, :] = qkv_ref[0, r, :, GROUP_WIDTH:2 * GROUP_WIDTH]
        pads[way, 1, N_SIDE:N_SIDE + sub_len, :] = qkv_ref[0, r, :, 2 * GROUP_WIDTH:3 * GROUP_WIDTH]

    def block(r, way, i):
        if True:
            r0 = i * Q_BLOCK if isinstance(i, int) else pl.multiple_of(i * Q_BLOCK, Q_BLOCK)
            qb = qkv_ref[0, r, pl.ds(r0, Q_BLOCK), 0:GROUP_WIDTH]
            kw = pads[way, 0, pl.ds(r0, K_BLOCK), :]
            vw = pads[way, 1, pl.ds(r0, K_BLOCK), :]
            zero = jnp.zeros_like(qb)
            if isinstance(i, int):
                edge = int(i == 0) + 2 * int(i == n_blocks - 1)
            else:
                edge = (i == 0).astype(I32) + 2 * (i == n_blocks - 1).astype(I32)
            q_heads = jnp.concatenate(
                [jnp.where(head_of_lane == h, qb, zero) for h in range(HEADS_PER_GROUP)], axis=0)
            s = lax.dot_general(q_heads, kw, (((1,), (1,)), ((), ())), preferred_element_type=F32)
            s = s.reshape(HEADS_PER_GROUP, Q_BLOCK, K_BLOCK) + bias_ref[edge]
            m = jnp.max(s, axis=-1, keepdims=True)
            p = jnp.exp2(s - m)
            psum = jnp.sum(p, axis=-1, keepdims=True)
            pb = p.astype(BF16)
            p_cat = jnp.concatenate([pb[h] for h in range(HEADS_PER_GROUP)], axis=1)
            zv = jnp.zeros_like(vw)
            v_heads = jnp.concatenate(
                [jnp.where(head_of_lane == h, vw, zv) for h in range(HEADS_PER_GROUP)], axis=0)
            o = _dot(p_cat, v_heads)
            lse = m + jnp.log2(psum)
            o = o / by_head([psum[h] for h in range(HEADS_PER_GROUP)])
            lse_lanes = by_head([lse[h] for h in range(HEADS_PER_GROUP)])
            if dilation == 1:
                rows = pl.ds(r0, Q_BLOCK)
            else:
                rows = pl.ds(r + dilation * r0, Q_BLOCK, stride=dilation)
            for half in range(GROUP_WIDTH // LANES):
                lanes = slice(half * LANES, (half + 1) * LANES)
                if first:
                    num[half, rows, :] = o[:, lanes]
                    den[half, rows, :] = jnp.ones((Q_BLOCK, LANES), F32)
                    top[half, rows, :] = lse_lanes[:, lanes]
                else:
                    old = top[half, rows, :]
                    new = jnp.maximum(old, lse_lanes[:, lanes])
                    keep = jnp.exp2(old - new)
                    add = jnp.exp2(lse_lanes[:, lanes] - new)
                    num[half, rows, :] = keep * num[half, rows, :] + add * o[:, lanes]
                    den[half, rows, :] = keep * den[half, rows, :] + add
                    top[half, rows, :] = new

    def blocks_of(r, way):
        if n_blocks == 1:
            block(r, way, 0)
        else:
            def several(j, carry):
                for k in range(BLOCKS_IN_FLIGHT):
                    block(r, way, BLOCKS_IN_FLIGHT * j + k)
                return carry
            lax.fori_loop(0, n_blocks // BLOCKS_IN_FLIGHT, several, 0)

    if dilation == 1:
        load_keys(0, 0)
        blocks_of(0, 0)
    else:
        def residues(j, carry):
            for way in range(ways):
                load_keys(ways * j + way, way)
            for way in range(ways):
                blocks_of(ways * j + way, way)
            return carry
        ways = min(PAD_WAYS, dilation, max(1, 2 * BLOCKS_IN_FLIGHT // n_blocks))
        lax.fori_loop(0, dilation // ways, residues, 0)


def _t5_bucket(rel):
    nb = N_BUCKETS // 2
    ret = jnp.where(rel > 0, nb, 0)
    n = jnp.abs(rel)
    max_exact = nb // 2
    nf = jnp.maximum(n, max_exact).astype(F32)
    large = max_exact + (jnp.log(nf / max_exact) / math.log(MAX_DISTANCE / max_exact)
                         * (nb - max_exact)).astype(I32)
    large = jnp.minimum(large, nb - 1)
    return ret + jnp.where(n < max_exact, n, large)


def _band_bias(rel_bias, group, dilation):
    qi = jnp.arange(Q_BLOCK)[:, None]
    ki = jnp.arange(K_BLOCK)[None, :]
    bucket = _t5_bucket((ki - N_SIDE - qi) * dilation)
    tab = rel_bias[:, group * HEADS_PER_GROUP:(group + 1) * HEADS_PER_GROUP]
    onehot = (bucket[:, :, None] == jnp.arange(N_BUCKETS)[None, None, :]).astype(F32)
    bias = jnp.einsum('qkb,bh->hqk', onehot, tab, precision=lax.Precision.HIGHEST).astype(F32)
    bias = bias * LOG2_E
    band = jnp.abs(ki - N_SIDE - qi) <= N_SIDE
    variants = []
    for edge in range(4):
        ok = band
        if edge & 1:
            ok = ok & (ki >= N_SIDE)
        if edge & 2:
            ok = ok & (ki < Q_BLOCK + N_SIDE)
        variants.append(jnp.where(ok[None], bias, NEG_INF))
    return jnp.stack(variants)


def _attention(qkv, rel_bias):
    biases = [_band_bias(rel_bias, g, d) for g, (_, d) in enumerate(DILATION_PATTERNS)]
    slab = pltpu.VMEM((GROUP_WIDTH // LANES, SEQ, LANES), F32)
    pads = pltpu.VMEM((PAD_WAYS, 2, SEQ + 2 * N_SIDE, GROUP_WIDTH), BF16)
    out = pl.pallas_call(
        _attn_kernel,
        grid=(BATCH,),
        in_specs=[pl.BlockSpec((1,) + a.shape[1:], lambda b: (b, 0, 0, 0)) for a in qkv]
                 + [_full_spec(b) for b in biases],
        out_specs=pl.BlockSpec((1, SEQ, GROUP_WIDTH), lambda b: (b, 0, 0)),
        out_shape=jax.ShapeDtypeStruct((BATCH, SEQ, GROUP_WIDTH), BF16),
        scratch_shapes=[pads, slab, slab, slab],
        compiler_params=_cparams("parallel"),
        name="attn",
    )(*qkv, *biases)
    return out.reshape(TOKENS, GROUP_WIDTH)


def _mixout_kernel(h_ref, attn_ref, u_ref, uprev_ref, unext_ref, inv_ref, gate_ref,
                   pw_ref, ps_ref, wpa_ref, wpp_ref, wo_ref, out_ref, runs):
    j = pl.program_id(0) % TILES_PER_SEQ
    y_attn = _dot(attn_ref[...], wpa_ref[...])

    u = u_ref[...]
    prev = jnp.where(j == 0, 0.0, uprev_ref[0])
    nxt = jnp.where(j == TILES_PER_SEQ - 1, 0.0, unext_ref[0])
    ext_rows = ROW_TILE + 2 * POOL_HALO
    runs[:, ext_rows:ext_rows + POOL_HALO, :] = jnp.zeros((2, POOL_HALO, POOL_GROUP_WIDTH), F32)
    mixed = []
    for gi, w in enumerate(POOL_WINDOWS):
        half = w // 2
        sl = slice(gi * POOL_GROUP_WIDTH, (gi + 1) * POOL_GROUP_WIDTH)
        runs[0, 0:POOL_HALO, :] = prev[:, sl]
        runs[0, POOL_HALO:POOL_HALO + ROW_TILE, :] = u[:, sl]
        runs[0, POOL_HALO + ROW_TILE:ext_rows, :] = nxt[:, sl]
        src, span = 0, 1
        while 2 * span < w:
            runs[1 - src, 0:ext_rows, :] = runs[src, 0:ext_rows, :] + runs[src, span:span + ext_rows, :]
            src, span = 1 - src, 2 * span
        lo = POOL_HALO - half
        acc = runs[src, lo:lo + ROW_TILE, :] + runs[src, lo + half:lo + half + ROW_TILE, :]
        pooled = acc * inv_ref[:, sl] - u[:, sl]
        mixed.append(_dot(pooled.astype(BF16), pw_ref[gi]) * ps_ref[:, sl])
    y_pool = _dot(jnp.concatenate(mixed, axis=1).astype(BF16), wpp_ref[...])

    y = (gate_ref[:, 0:D_MODEL] * y_attn.astype(BF16)
         + gate_ref[:, D_MODEL:GATE_WIDTH] * y_pool.astype(BF16))
    out_ref[...] = h_ref[...] + _dot(y, wo_ref[...])


def _mix_out(h, attn, u, gates, pool_w, pool_scale, w_proj_attn, w_proj_pool, w_out):
    halo_blocks = ROW_TILE // POOL_HALO
    u3 = u.reshape(TOKENS // POOL_HALO, POOL_HALO, POOL_WIDTH)
    last = TOKENS // POOL_HALO - 1
    prev_spec = pl.BlockSpec((1, POOL_HALO, POOL_WIDTH),
                             lambda i: (jnp.maximum(i * halo_blocks - 1, 0), 0, 0))
    next_spec = pl.BlockSpec((1, POOL_HALO, POOL_WIDTH),
                             lambda i: (jnp.minimum((i + 1) * halo_blocks, last), 0, 0))
    inv_spec = pl.BlockSpec((ROW_TILE, POOL_WIDTH), lambda i: (i % TILES_PER_SEQ, 0))
    weights = (pool_w, pool_scale, w_proj_attn, w_proj_pool, w_out)
    return pl.pallas_call(
        _mixout_kernel,
        grid=(N_TILES,),
        in_specs=[_row_spec(D_MODEL), _row_spec(GROUP_WIDTH),
                  _row_spec(POOL_WIDTH), prev_spec, next_spec, inv_spec, _row_spec(GATE_WIDTH)]
                 + [_full_spec(w) for w in weights],
        out_specs=_row_spec(D_MODEL),
        out_shape=jax.ShapeDtypeStruct((TOKENS, D_MODEL), F32),
        scratch_shapes=[pltpu.VMEM((2, ROW_TILE + 3 * POOL_HALO, POOL_GROUP_WIDTH), F32)],
        compiler_params=_cparams("parallel"),
        name="mixout",
    )(h, attn, u, u3, u3, _pool_inverse_counts(), gates, *weights)


def _pool_inverse_counts():
    pos = jnp.arange(SEQ, dtype=I32)[:, None]
    half = jnp.repeat(jnp.asarray(POOL_WINDOWS, I32) // 2, POOL_GROUP_WIDTH)[None, :]
    count = jnp.minimum(pos + half, SEQ) - jnp.maximum(pos - half, 0)
    return 1.0 / count.astype(F32)


def _router_kernel(h_ref, g_ref, wr_ref, xn_ref, ri_ref, rw_ref, cnt_ref):
    xn = _rms(h_ref[...], g_ref[...])
    hi = xn.astype(BF16)
    xn_ref[...] = hi
    lo = (xn - hi.astype(F32)).astype(BF16)
    both = _dot(hi, wr_ref[...])
    lg = both[:, 0:ROUTER_LANES] + (both[:, ROUTER_LANES:] + _dot(lo, wr_ref[:, 0:ROUTER_LANES]))

    lane = lax.broadcasted_iota(I32, (ROW_TILE, ROUTER_LANES), 1)
    lanef = lane.astype(F32)
    low = jnp.float32(-3.0e38)
    far = jnp.float32(ROUTER_LANES)
    first = lambda hit: jnp.min(jnp.where(hit, lanef, far), axis=-1, keepdims=True)

    is_group = lane < N_EXPERT_GROUPS
    gl = jnp.where(is_group, lg, low)
    gmax = jnp.max(gl, axis=-1, keepdims=True)
    gidx = first(gl == gmax).astype(I32)
    gden = jnp.sum(jnp.where(is_group, jnp.exp(gl - gmax), 0.0), axis=-1, keepdims=True)
    g_p = 1.0 / gden

    in_group = ((lane >= EXPERT_LANE0) & (lane < EXPERT_LANE0 + N_EXPERTS)
                & ((lane - EXPERT_LANE0) // EXPERTS_PER_GROUP == gidx))
    el = jnp.where(in_group, lg, low)
    t1 = jnp.max(el, axis=-1, keepdims=True)
    l1 = first(in_group & (el == t1))
    rest = in_group & (lanef != l1)
    el2 = jnp.where(rest, lg, low)
    t2 = jnp.max(el2, axis=-1, keepdims=True)
    l2 = first(rest & (el2 == t2))
    e2 = jnp.exp(t2 - t1)
    w1 = g_p * (1.0 / (1.0 + e2))
    w2 = g_p * (e2 / (1.0 + e2))

    hit1 = lanef == l1
    hit2 = lanef == l2
    onehot = (hit1 | hit2).astype(BF16)
    ri = lax.broadcasted_iota(I32, (ROW_TILE, ROW_TILE), 0)
    ci = lax.broadcasted_iota(I32, (ROW_TILE, ROW_TILE), 1)
    before = (ci < ri).astype(BF16)
    seen = _dot(before, onehot)
    r1 = jnp.sum(jnp.where(hit1, seen, 0.0), axis=-1, keepdims=True)
    r2 = jnp.sum(jnp.where(hit2, seen, 0.0), axis=-1, keepdims=True)

    packed = jnp.zeros((ROW_TILE, ROUTER_LANES), F32)
    for k, v in enumerate((l1, l2, r1, r2)):
        packed = jnp.where(lane == k, v, packed)
    ri_ref[...] = packed.astype(I32)
    rw_ref[...] = jnp.where(lane == 0, w1, jnp.where(lane == 1, w2, 0.0))
    counts = jnp.sum(onehot.astype(F32), axis=0, keepdims=True)
    cnt_ref[0] = jnp.broadcast_to(counts, (SUBLANES, ROUTER_LANES))


def _route(h, g, wr_split):
    return pl.pallas_call(
        _router_kernel,
        grid=(N_TILES,),
        in_specs=[_row_spec(D_MODEL), _full_spec(g), _full_spec(wr_split)],
        out_specs=[_row_spec(D_MODEL), _row_spec(ROUTER_LANES), _row_spec(ROUTER_LANES),
                   pl.BlockSpec((1, SUBLANES, ROUTER_LANES), lambda i: (i, 0, 0))],
        out_shape=[jax.ShapeDtypeStruct((TOKENS, D_MODEL), BF16),
                   jax.ShapeDtypeStruct((TOKENS, ROUTER_LANES), I32),
                   jax.ShapeDtypeStruct((TOKENS, ROUTER_LANES), F32),
                   jax.ShapeDtypeStruct((N_TILES, SUBLANES, ROUTER_LANES), F32)],
        compiler_params=_cparams("parallel"),
        name="router",
    )(h, g, wr_split)


def _slots(ri_ref, off_ref):
    ri = ri_ref[...]
    lane = lax.broadcasted_iota(I32, (ROW_TILE, ROUTER_LANES), 1)
    off = off_ref[0, 0:1, :]
    pick = lambda k: jnp.sum(jnp.where(lane == ri[:, k:k + 1], off, 0.0), axis=-1, keepdims=True)
    return (pick(0) + ri[:, 2:3].astype(F32), pick(1) + ri[:, 3:4].astype(F32))


def _as_rows(cols):
    eye = (lax.broadcasted_iota(I32, (ROW_TILE, ROW_TILE), 0)
           == lax.broadcasted_iota(I32, (ROW_TILE, ROW_TILE), 1))
    return [jnp.sum(jnp.where(eye, c, 0.0), axis=0, keepdims=True) for c in cols]


def _run_copy(tile, e, loff_s, c8_s, dst_s, buf_ref, slot, hbm_ref, sem, to_hbm):
    k = tile * N_EXPERTS + e
    n = pl.multiple_of(c8_s[k], RUN_ALIGN)
    vm = buf_ref.at[slot, pl.ds(pl.multiple_of(loff_s[k], RUN_ALIGN), n)]
    hb = hbm_ref.at[pl.ds(pl.multiple_of(dst_s[k], RUN_ALIGN), n)]
    src, dst = (vm, hb) if to_hbm else (hb, vm)
    return n, pltpu.make_async_copy(src, dst, sem.at[slot])


def _loop(lo, hi, fn, unroll=1):
    def body(e, carry):
        fn(e)
        return carry
    lax.fori_loop(lo, hi, body, 0, unroll=unroll)


def _wait_rows(n, buf_ref, slot, hbm_ref, sem, to_hbm):
    vm = buf_ref.at[slot, pl.ds(0, n)]
    hb = hbm_ref.at[pl.ds(0, n)]
    src, dst = (vm, hb) if to_hbm else (hb, vm)
    pltpu.make_async_copy(src, dst, sem.at[slot]).wait()


def _start(n, cp):
    @pl.when(n > 0)
    def _():
        cp.start()


def _wait(n, cp):
    @pl.when(n > 0)
    def _():
        cp.wait()


def _dispatch_kernel(loff_s, c8_s, dst_s, nslot_s, zdst_s, zcnt_s, nact_s,
                     xn_ref, ri_ref, rw_ref, off_ref, xpad_hbm, sorted_buf, zero_buf, sem, zsem):
    i = pl.program_id(0)
    slot = i % 2
    last = pl.num_programs(0) - 1

    def zero_copy(e):
        n = pl.multiple_of(zcnt_s[e], RUN_ALIGN)
        dst = xpad_hbm.at[pl.ds(pl.multiple_of(zdst_s[e], RUN_ALIGN), n)]
        return n, pltpu.make_async_copy(zero_buf.at[pl.ds(0, n)], dst, zsem)

    def tail_copy(b):
        dst = xpad_hbm.at[pl.ds(pl.multiple_of(b * MOE_BLOCK, MOE_BLOCK), MOE_BLOCK)]
        return pltpu.make_async_copy(zero_buf, dst, zsem)

    @pl.when(i == 0)
    def _():
        zero_buf[...] = jnp.zeros_like(zero_buf)
        _loop(0, N_EXPERTS, lambda e: _start(*zero_copy(e)))
        _loop(nact_s[0], N_MOE_BLOCKS, lambda b: tail_copy(b).start())

    s1, s2 = _slots(ri_ref, off_ref)
    rw = rw_ref[...]
    s1_row, s2_row, w1_row, w2_row = _as_rows([s1, s2, rw[:, 0:1], rw[:, 1:2]])
    xn = xn_ref[...]
    for first, size in SLOT_CHUNKS:
        @pl.when(first < nslot_s[i])
        def _():
            rows = slice(first, first + size)
            lane = lax.broadcasted_iota(I32, (size, LANES), 1)
            srow = (first + lax.broadcasted_iota(I32, (size, ROW_TILE), 0)).astype(F32)
            hit1 = srow == s1_row
            hit2 = srow == s2_row
            xs = _dot((hit1 | hit2).astype(BF16), xn)
            sorted_buf[slot, rows, 0:HALF] = _pack_halves(xs[:, 0:HALF], xs[:, HALF:D_MODEL])
            ws = jnp.sum(jnp.where(hit1, w1_row, 0.0) + jnp.where(hit2, w2_row, 0.0),
                         axis=-1, keepdims=True)
            sorted_buf[slot, rows, HALF:X_WORDS] = jnp.where(lane == 0, _bits(ws), jnp.uint32(0))

    copy = lambda tile, sl: (lambda e: _run_copy(tile, e, loff_s, c8_s, dst_s, sorted_buf, sl,
                                                 xpad_hbm, sem, True))
    mine = copy(i, slot)
    _loop(0, N_EXPERTS, lambda e: _start(*mine(e)), unroll=RUN_UNROLL)
    tile_rows = lambda t: pl.multiple_of(nslot_s[t], RUN_ALIGN)

    @pl.when(i > 0)
    def _():
        _wait_rows(tile_rows(i - 1), sorted_buf, 1 - slot, xpad_hbm, sem, True)

    @pl.when(i == last)
    def _():
        _wait_rows(tile_rows(i), sorted_buf, slot, xpad_hbm, sem, True)
        _loop(0, N_EXPERTS, lambda e: _wait(*zero_copy(e)))
        _loop(nact_s[0], N_MOE_BLOCKS, lambda b: tail_copy(b).wait())


def _dispatch(tables, xn, route_i, route_w):
    tile_row = pl.BlockSpec((1, SUBLANES, ROUTER_LANES), lambda i, *_: (i, 0, 0))
    grid_spec = pltpu.PrefetchScalarGridSpec(
        num_scalar_prefetch=7,
        grid=(N_TILES,),
        in_specs=[_row_spec(D_MODEL), _row_spec(ROUTER_LANES), _row_spec(ROUTER_LANES), tile_row],
        out_specs=pl.BlockSpec(memory_space=pl.ANY),
        scratch_shapes=[pltpu.VMEM((2, MAX_SLOTS, X_WORDS), U32),
                        pltpu.VMEM((MOE_BLOCK, X_WORDS), U32),
                        pltpu.SemaphoreType.DMA((2,)),
                        pltpu.SemaphoreType.DMA(())],
    )
    return pl.pallas_call(
        _dispatch_kernel,
        grid_spec=grid_spec,
        out_shape=jax.ShapeDtypeStruct((PAD_ROWS, X_WORDS), U32),
        compiler_params=_cparams("arbitrary"),
        name="dispatch",
    )(tables["loff"], tables["c8"], tables["dst"], tables["nslot"], tables["zdst"], tables["zcnt"],
      tables["n_active"], xn, route_i, route_w, tables["offrow"])


def _combine_kernel(loff_s, c8_s, dst_s, nslot_s,
                    h_ref, ri_ref, off_ref, g_ref, ypad_hbm, out_ref,
                    ybuf, sem, *, final_norm):
    i = pl.program_id(0)
    slot = i % 2
    last = pl.num_programs(0) - 1
    fetch = lambda tile, sl: (lambda e: _run_copy(tile, e, loff_s, c8_s, dst_s, ybuf, sl,
                                                  ypad_hbm, sem, False))

    @pl.when(i == 0)
    def _():
        ybuf[...] = jnp.zeros_like(ybuf)
        first = fetch(i, slot)
        _loop(0, N_EXPERTS, lambda e: _start(*first(e)), unroll=RUN_UNROLL)

    @pl.when(i < last)
    def _():
        nxt = fetch(i + 1, 1 - slot)
        _loop(0, N_EXPERTS, lambda e: _start(*nxt(e)), unroll=RUN_UNROLL)

    _wait_rows(pl.multiple_of(nslot_s[i], RUN_ALIGN), ybuf, slot, ypad_hbm, sem, False)

    s1, s2 = _slots(ri_ref, off_ref)

    def gathered(first, size):
        scol = (first + lax.broadcasted_iota(I32, (ROW_TILE, size), 1)).astype(F32)
        pick = ((scol == s1) | (scol == s2)).astype(BF16)
        y_hi, y_lo = _unpack_halves(ybuf[slot, first:first + size, :])
        return jnp.concatenate([_dot(pick, y_hi), _dot(pick, y_lo)], axis=1)

    finish = (lambda v: _rms(v, g_ref[...])) if final_norm else (lambda v: v)
    h = h_ref[...] + gathered(0, COMBINE_BASE)
    spill = nslot_s[i] > COMBINE_BASE

    @pl.when(jnp.logical_not(spill))
    def _():
        out_ref[...] = finish(h)

    @pl.when(spill)
    def _():
        out_ref[...] = finish(h + gathered(COMBINE_BASE, MAX_SLOTS - COMBINE_BASE))


def _combine(tables, h, route_i, y_pad, g, final_norm):
    tile_row = pl.BlockSpec((1, SUBLANES, ROUTER_LANES), lambda i, *_: (i, 0, 0))
    grid_spec = pltpu.PrefetchScalarGridSpec(
        num_scalar_prefetch=4,
        grid=(N_TILES,),
        in_specs=[_row_spec(D_MODEL), _row_spec(ROUTER_LANES), tile_row, _full_spec(g),
                  pl.BlockSpec(memory_space=pl.ANY)],
        out_specs=_row_spec(D_MODEL),
        scratch_shapes=[pltpu.VMEM((2, MAX_SLOTS, HALF), U32),
                        pltpu.SemaphoreType.DMA((2,))],
    )
    return pl.pallas_call(
        functools.partial(_combine_kernel, final_norm=final_norm),
        grid_spec=grid_spec,
        out_shape=jax.ShapeDtypeStruct((TOKENS, D_MODEL), F32),
        compiler_params=_cparams("arbitrary"),
        name="combine",
    )(tables["loff"], tables["c8"], tables["dst"], tables["nslot"],
      h, route_i, tables["offrow"], g, y_pad)


def _expert_kernel(be_ref, slot_ref, next_ref, nact_ref, x_ref, wg_hbm, wu_hbm, wd_hbm, y_hbm,
                   wg_f, wu_f, wd_f, wg_b, wu_b, wd_b, ybuf, zero_buf, sem, ysem, zsem, *, layer):
    i = pl.program_id(0)
    n_active = nact_ref[0]
    staged = ((wg_hbm, wg_f, wg_b), (wu_hbm, wu_f, wu_b), (wd_hbm, wd_f, wd_b))

    def fetch(e, slot):
        return [pltpu.make_async_copy(hbm.at[layer, e], f32.at[slot], sem.at[slot, k])
                for k, (hbm, f32, _) in enumerate(staged)]

    def block_rows(b):
        return y_hbm.at[pl.ds(pl.multiple_of(b * MOE_BLOCK, MOE_BLOCK), MOE_BLOCK)]

    def put(b):
        return pltpu.make_async_copy(ybuf.at[b % 2], block_rows(b), ysem.at[b % 2])

    def tail_copy(b):
        return pltpu.make_async_copy(zero_buf, block_rows(b), zsem)

    @pl.when(i == 0)
    def _():
        zero_buf[...] = jnp.zeros_like(zero_buf)
        _loop(n_active, N_MOE_BLOCKS, lambda b: tail_copy(b).start())

    @pl.when(i == pl.num_programs(0) - 1)
    def _():
        _loop(n_active, N_MOE_BLOCKS, lambda b: tail_copy(b).wait())

    @pl.when(i < n_active)
    def _():
        e = be_ref[i]
        slot = slot_ref[i]

        def request(ahead, block):
            nxt = next_ref[(ahead - 1) * N_MOE_BLOCKS + block]

            @pl.when(nxt >= 0)
            def _():
                for cp in fetch(nxt, (slot + ahead) % W_STAGES):
                    cp.start()

        @pl.when((i == 0) | (e != be_ref[jnp.maximum(i - 1, 0)]))
        def _():
            @pl.when(i == 0)
            def _():
                for cp in fetch(e, slot):
                    cp.start()
                for ahead in range(1, W_STAGES - 1):
                    request(ahead, i)
            for cp in fetch(e, slot):
                cp.wait()
            request(W_STAGES - 1, i)
            for _, f32, b16 in staged:
                b16[...] = f32[slot].astype(BF16)

        x_hi, x_lo = _unpack_halves(x_ref[:, 0:HALF])
        row_w = lax.bitcast_convert_type(x_ref[:, HALF:HALF + 1], F32)
        gate = _dot(x_hi, wg_b[0:HALF, :]) + _dot(x_lo, wg_b[HALF:D_MODEL, :])
        up = _dot(x_hi, wu_b[0:HALF, :]) + _dot(x_lo, wu_b[HALF:D_MODEL, :])
        hmid = (jax.nn.silu(gate) * up).astype(BF16)
        y = (_dot(hmid, wd_b[...]) * row_w).astype(BF16).astype(F32)
        ybuf[i % 2] = _pack_halves(y[:, 0:HALF], y[:, HALF:D_MODEL])
        put(i).start()

        @pl.when(i > 0)
        def _():
            put(i - 1).wait()

        @pl.when(i == n_active - 1)
        def _():
            put(i).wait()


def _experts(layer, tables, x_pad, w_gate, w_up, w_down):
    hbm = pl.BlockSpec(memory_space=pl.ANY)
    up_shape, down_shape = (D_MODEL, D_EXPERT), (D_EXPERT, D_MODEL)
    grid_spec = pltpu.PrefetchScalarGridSpec(
        num_scalar_prefetch=4,
        grid=(N_MOE_BLOCKS,),
        in_specs=[pl.BlockSpec((MOE_BLOCK, X_WORDS),
                               lambda i, be, sl, nx, na: (jnp.minimum(i, na[0] - 1), 0)),
                  hbm, hbm, hbm],
        out_specs=hbm,
        scratch_shapes=[pltpu.VMEM((W_STAGES,) + up_shape, F32),
                        pltpu.VMEM((W_STAGES,) + up_shape, F32),
                        pltpu.VMEM((W_STAGES,) + down_shape, F32),
                        pltpu.VMEM(up_shape, BF16), pltpu.VMEM(up_shape, BF16),
                        pltpu.VMEM(down_shape, BF16),
                        pltpu.VMEM((2, MOE_BLOCK, HALF), U32),
                        pltpu.VMEM((MOE_BLOCK, HALF), U32),
                        pltpu.SemaphoreType.DMA((W_STAGES, 3)),
                        pltpu.SemaphoreType.DMA((2,)),
                        pltpu.SemaphoreType.DMA(())],
    )
    return pl.pallas_call(
        functools.partial(_expert_kernel, layer=layer),
        grid_spec=grid_spec,
        out_shape=jax.ShapeDtypeStruct((PAD_ROWS, HALF), U32),
        compiler_params=_cparams("arbitrary"),
        name="experts",
    )(tables["blk_expert"], tables["blk_slot"], tables["blk_next"], tables["n_active"],
      x_pad, w_gate, w_up, w_down)


def _routing_tables(tile_counts):
    cnt = tile_counts[:, 0, EXPERT_LANE0:EXPERT_LANE0 + N_EXPERTS].astype(I32)
    c8 = (cnt + RUN_ALIGN - 1) // RUN_ALIGN * RUN_ALIGN
    loff = jnp.cumsum(c8, axis=1) - c8
    nslot = jnp.sum(c8, axis=1)
    tot = jnp.sum(c8, axis=0)
    padded = (tot + MOE_BLOCK - 1) // MOE_BLOCK * MOE_BLOCK
    end = jnp.cumsum(padded)
    base = end - padded
    dst = base[None, :] + jnp.cumsum(c8, axis=0) - c8
    blk_start = jnp.arange(N_MOE_BLOCKS, dtype=I32) * MOE_BLOCK
    blk_expert = jnp.minimum(jnp.sum((end[None, :] <= blk_start[:, None]).astype(I32), axis=1),
                             N_EXPERTS - 1).astype(I32)
    offrow = jnp.zeros((N_TILES, ROUTER_LANES), F32).at[:, EXPERT_LANE0:EXPERT_LANE0 + N_EXPERTS].set(
        loff.astype(F32))
    offrow = jnp.broadcast_to(offrow[:, None, :], (N_TILES, SUBLANES, ROUTER_LANES))
    experts = jnp.arange(N_EXPERTS, dtype=I32)
    present = padded > 0
    ordinal = jnp.cumsum(present.astype(I32)) - 1
    at_or_after = lax.cummin(jnp.where(present, experts, N_EXPERTS), reverse=True)
    after = jnp.concatenate([at_or_after[1:], jnp.full((1,), N_EXPERTS, I32)])
    hop = jnp.concatenate([after, jnp.full((1,), N_EXPERTS, I32)])
    pick = (blk_expert[:, None] == experts[None, :]).astype(I32)
    blk_slot = jnp.sum(pick * (ordinal % W_STAGES)[None, :], axis=1).astype(I32)
    ahead, blk_next = experts, []
    for _ in range(W_STAGES - 1):
        ahead = jnp.sum((ahead[:, None] == jnp.arange(N_EXPERTS + 1)[None, :]) * hop[None, :], axis=1)
        per_block = jnp.sum(pick * ahead[None, :], axis=1)
        blk_next.append(jnp.where(per_block >= N_EXPERTS, -1, per_block))
    blk_next = jnp.concatenate(blk_next).astype(I32)
    return {
        "blk_slot": blk_slot, "blk_next": blk_next,
        "loff": loff.reshape(-1).astype(I32), "c8": c8.reshape(-1).astype(I32),
        "dst": dst.reshape(-1).astype(I32), "nslot": nslot.astype(I32),
        "zdst": (base + tot).astype(I32), "zcnt": (padded - tot).astype(I32),
        "offrow": offrow, "blk_expert": blk_expert,
        "n_active": (end[-1:] // MOE_BLOCK).astype(I32),
    }


def _split_router_weights(w_router_group, w_router_expert):
    w_e = w_router_expert.transpose(1, 0, 2).reshape(D_MODEL, N_EXPERTS)
    w = jnp.concatenate([w_router_group, w_e], axis=1)
    w = jnp.pad(w, ((0, 0), (0, ROUTER_LANES - w.shape[1])))
    hi = w.astype(BF16)
    lo = (w - hi.astype(F32)).astype(BF16)
    return jnp.concatenate([hi, lo], axis=1)


def _input_weights(w):
    cols = []
    for g in range(N_GROUPS):
        for part in range(3):
            c0 = part * ATTN_WIDTH + g * GROUP_WIDTH
            cols.append(w[:, c0:c0 + GROUP_WIDTH])
    cols.append(w[:, QKV_WIDTH:])
    return jnp.concatenate(cols, axis=1).astype(BF16)


def kernel(x, rel_bias, norm_mix_g, w_in, pool_w, pool_scale, w_proj_attn, w_proj_pool, w_out,
           norm_ffn_g, w_router_group, w_router_expert, w_gate_e, w_up_e, w_down_e, norm_final_g):
    h = x.reshape(TOKENS, D_MODEL)
    for l in range(DEPTH):
        qkv, u, gates = _project(h, norm_mix_g[l][None], _input_weights(w_in[l]))
        h = _mix_out(h, _attention(qkv, rel_bias), u, gates, pool_w[l].astype(BF16), pool_scale[l][None],
                     w_proj_attn[l].astype(BF16), w_proj_pool[l].astype(BF16), w_out[l].astype(BF16))
        xn, route_i, route_w, tile_counts = _route(
            h, norm_ffn_g[l][None], _split_router_weights(w_router_group[l], w_router_expert[l]))
        tables = _routing_tables(tile_counts)
        x_pad = _dispatch(tables, xn, route_i, route_w)
        y_pad = _experts(l, tables, x_pad, w_gate_e, w_up_e, w_down_e)
        h = _combine(tables, h, route_i, y_pad, norm_final_g[None], l == DEPTH - 1)
    return h.reshape(BATCH, SEQ, D_MODEL)
```

```python
import functools
import math

import jax
import jax.numpy as jnp
from jax import lax
from jax.experimental import pallas as pl
from jax.experimental.pallas import tpu as pltpu

F32 = jnp.float32
BF16 = jnp.bfloat16
I32 = jnp.int32
U32 = jnp.uint32

D_MODEL = 1024
BATCH = 8
SEQ = 2048
TOKENS = BATCH * SEQ
DEPTH = 2

HEAD_DIM = 64
HEADS_PER_GROUP = 4
GROUP_WIDTH = HEADS_PER_GROUP * HEAD_DIM
DILATION_PATTERNS = ((128, 1), (512, 4), (2048, 16))
N_GROUPS = len(DILATION_PATTERNS)
N_ATTN_HEADS = N_GROUPS * HEADS_PER_GROUP
ATTN_WIDTH = N_ATTN_HEADS * HEAD_DIM
QKV_WIDTH = 3 * ATTN_WIDTH
GROUP_QKV = 3 * GROUP_WIDTH
N_SIDE = 64
assert all(w // (2 * d) == N_SIDE for w, d in DILATION_PATTERNS)
POOL_WINDOWS = (2, 4, 8, 16)
POOL_GROUP_WIDTH = 128
POOL_WIDTH = len(POOL_WINDOWS) * POOL_GROUP_WIDTH
POOL_HALO = max(POOL_WINDOWS) // 2
N_BRANCHES = 2
GATE_WIDTH = N_BRANCHES * D_MODEL
IN_WIDTH = QKV_WIDTH + POOL_WIDTH + GATE_WIDTH
N_BUCKETS = 32
MAX_DISTANCE = 1024
N_EXPERT_GROUPS = 8
EXPERTS_PER_GROUP = 8
N_EXPERTS = N_EXPERT_GROUPS * EXPERTS_PER_GROUP
TOP_K = 2
D_EXPERT = 512
N_ASSIGN = TOKENS * TOP_K
EPS = 1e-6
NEG_INF = -1e30
LOG2_E = math.log2(math.e)
Q_SCALE = HEAD_DIM ** -0.5 * LOG2_E

LANES = 128
SUBLANES = 8
ROW_TILE = 512
N_TILES = TOKENS // ROW_TILE
TILES_PER_SEQ = SEQ // ROW_TILE
Q_BLOCK = 128
K_BLOCK = Q_BLOCK + 2 * N_SIDE
PAD_WAYS = 4
BLOCKS_IN_FLIGHT = 4
ROUTER_LANES = 128
EXPERT_LANE0 = N_EXPERT_GROUPS
VMEM_LIMIT = 56 * 1024 * 1024

RUN_ALIGN = SUBLANES
MOE_BLOCK = 336
HALF = D_MODEL // 2
X_WORDS = HALF + LANES
SLOT_CHUNK = 256
RUN_UNROLL = 4
W_STAGES = 2
MAX_SLOTS = -(-(TOP_K * ROW_TILE + N_EXPERTS * (RUN_ALIGN - 1)) // SLOT_CHUNK) * SLOT_CHUNK
SLOT_CHUNKS = (tuple((s, ROW_TILE) for s in range(0, TOP_K * ROW_TILE, ROW_TILE))
               + tuple((s, SLOT_CHUNK) for s in range(TOP_K * ROW_TILE, MAX_SLOTS, SLOT_CHUNK)))
COMBINE_BASE = MAX_SLOTS - SLOT_CHUNK
N_MOE_BLOCKS = -(-(N_ASSIGN + N_TILES * N_EXPERTS * (RUN_ALIGN - 1)
                   + N_EXPERTS * (MOE_BLOCK - RUN_ALIGN)) // MOE_BLOCK)
PAD_ROWS = N_MOE_BLOCKS * MOE_BLOCK
HIGH_HALF = 0xFFFF0000


def _cparams(*sem):
    return pltpu.CompilerParams(dimension_semantics=sem, vmem_limit_bytes=VMEM_LIMIT)


def _rms(h, g):
    r = lax.rsqrt(jnp.mean(h * h, axis=-1, keepdims=True) + EPS)
    return (h * r) * g


def _dot(a, b):
    return jnp.dot(a, b, preferred_element_type=F32)


def _row_spec(width):
    return pl.BlockSpec((ROW_TILE, width), lambda i, *_: (i, 0))


def _full_spec(a):
    return pl.BlockSpec(a.shape, lambda i, *_: (0,) * a.ndim)


def _bits(x):
    return lax.bitcast_convert_type(x, U32)


def _pack_halves(a, b):
    return (_bits(a) & jnp.uint32(HIGH_HALF)) | (_bits(b) >> 16)


def _unpack_halves(words):
    hi = lax.bitcast_convert_type(words & jnp.uint32(HIGH_HALF), F32)
    lo = lax.bitcast_convert_type(words << 16, F32)
    return hi.astype(BF16), lo.astype(BF16)


def _proj_kernel(h_ref, g_ref, w_ref, wd_ref, q0_ref, q1_ref, q2_ref, u_ref, gate_ref, wdb_ref, slabs):
    wdb_ref[...] = wd_ref[...].astype(BF16)
    xn = _rms(h_ref[...], g_ref[...]).astype(BF16)
    n_slabs = GROUP_QKV // LANES
    for g, out_ref in enumerate((q0_ref, q1_ref, q2_ref)):
        dilation = DILATION_PATTERNS[g][1]
        res = _dot(xn, w_ref[:, g * GROUP_QKV:(g + 1) * GROUP_QKV])
        res = jnp.concatenate([res[:, 0:GROUP_WIDTH] * Q_SCALE, res[:, GROUP_WIDTH:]], axis=1)
        if dilation == 1:
            out_ref[0, 0] = res.astype(BF16)
            continue
        for s in range(n_slabs):
            slabs[s] = res[:, s * LANES:(s + 1) * LANES]
        n = ROW_TILE // dilation
        for r in range(dilation):
            rows = [slabs[s, pl.ds(r, n, stride=dilation), :] for s in range(n_slabs)]
            out_ref[0, r] = jnp.concatenate(rows, axis=1).astype(BF16)
    u_ref[...] = _dot(xn, w_ref[:, QKV_WIDTH:QKV_WIDTH + POOL_WIDTH])
    gates = _dot(xn, w_ref[:, QKV_WIDTH + POOL_WIDTH:IN_WIDTH])
    gate_ref[...] = jax.nn.sigmoid(gates).astype(BF16)


def _project(layer, h, g, w_bf16, w_down):
    qkv_shapes, qkv_specs = [], []
    for _, d in DILATION_PATTERNS:
        qkv_shapes.append(jax.ShapeDtypeStruct((BATCH, d, SEQ // d, GROUP_QKV), BF16))
        qkv_specs.append(pl.BlockSpec((1, d, ROW_TILE // d, GROUP_QKV),
                                      lambda i: (i // TILES_PER_SEQ, 0, i % TILES_PER_SEQ, 0)))
    layer_rows = N_EXPERTS * D_EXPERT
    step_rows = layer_rows // N_TILES
    res = pl.pallas_call(
        _proj_kernel,
        grid=(N_TILES,),
        in_specs=[_row_spec(D_MODEL), _full_spec(g), _full_spec(w_bf16),
                  pl.BlockSpec((step_rows, D_MODEL), lambda i: (layer * N_TILES + i, 0))],
        out_specs=qkv_specs + [_row_spec(POOL_WIDTH), _row_spec(GATE_WIDTH),
                               pl.BlockSpec((step_rows, D_MODEL), lambda i: (i, 0))],
        out_shape=qkv_shapes + [jax.ShapeDtypeStruct((TOKENS, POOL_WIDTH), F32),
                                jax.ShapeDtypeStruct((TOKENS, GATE_WIDTH), BF16),
                                jax.ShapeDtypeStruct((layer_rows, D_MODEL), BF16)],
        scratch_shapes=[pltpu.VMEM((GROUP_QKV // LANES, ROW_TILE, LANES), F32)],
        compiler_params=_cparams("parallel"),
        name="proj",
    )(h, g, w_bf16, w_down.reshape(DEPTH * layer_rows, D_MODEL))
    w_down_b = res[N_GROUPS + 2].reshape(N_EXPERTS, D_EXPERT, D_MODEL)
    return res[:N_GROUPS], res[N_GROUPS], res[N_GROUPS + 1], w_down_b


def _attn_kernel(q0_ref, q1_ref, q2_ref, b0_ref, b1_ref, b2_ref, o_ref, pads, num, den, top):
    zpad = jnp.zeros((N_SIDE, GROUP_WIDTH), BF16)
    groups = ((q2_ref, b2_ref, 2), (q1_ref, b1_ref, 1), (q0_ref, b0_ref, 0))
    for order, (qkv_ref, bias_ref, g) in enumerate(groups):
        dilation = DILATION_PATTERNS[g][1]
        sub_len = SEQ // dilation
        for way in range(PAD_WAYS):
            for kv in range(2):
                pads[way, kv, 0:N_SIDE, :] = zpad
                pads[way, kv, N_SIDE + sub_len:2 * N_SIDE + sub_len, :] = zpad
        _attn_group(qkv_ref, bias_ref, pads, num, den, top, dilation=dilation, first=order == 0)

    def finish(c, carry):
        rows = pl.ds(pl.multiple_of(c * ROW_TILE, ROW_TILE), ROW_TILE)
        merged = [num[half, rows, :] / den[half, rows, :] for half in range(GROUP_WIDTH // LANES)]
        o_ref[0, rows, :] = jnp.concatenate(merged, axis=1).astype(BF16)
        return carry

    lax.fori_loop(0, SEQ // ROW_TILE, finish, 0)


def _attn_group(qkv_ref, bias_ref, pads, num, den, top, *, dilation, first):
    sub_len = SEQ // dilation
    n_blocks = sub_len // Q_BLOCK
    head_of_lane = lax.broadcasted_iota(I32, (1, GROUP_WIDTH), 1) // HEAD_DIM

    def by_head(cols):
        out = cols[HEADS_PER_GROUP - 1]
        for h in range(HEADS_PER_GROUP - 2, -1, -1):
            out = jnp.where(head_of_lane == h, cols[h], out)
        return out

    def load_keys(r, way):
        pads[way, 0, N_SIDE:N_SIDE + sub_len, :] = qkv_ref[0, r, :, GROUP_WIDTH:2 * GROUP_WIDTH]
        pads[way, 1, N_SIDE:N_SIDE + sub_len, :] = qkv_ref[0, r, :, 2 * GROUP_WIDTH:3 * GROUP_WIDTH]

    def block(r, way, i):
        if True:
            r0 = i * Q_BLOCK if isinstance(i, int) else pl.multiple_of(i * Q_BLOCK, Q_BLOCK)
            qb = qkv_ref[0, r, pl.ds(r0, Q_BLOCK), 0:GROUP_WIDTH]
            kw = pads[way, 0, pl.ds(r0, K_BLOCK), :]
            vw = pads[way, 1, pl.ds(r0, K_BLOCK), :]
            zero = jnp.zeros_like(qb)
            if isinstance(i, int):
                edge = int(i == 0) + 2 * int(i == n_blocks - 1)
            else:
                edge = (i == 0).astype(I32) + 2 * (i == n_blocks - 1).astype(I32)
            q_heads = jnp.concatenate(
                [jnp.where(head_of_lane == h, qb, zero) for h in range(HEADS_PER_GROUP)], axis=0)
            s = lax.dot_general(q_heads, kw, (((1,), (1,)), ((), ())), preferred_element_type=F32)
            s = s.reshape(HEADS_PER_GROUP, Q_BLOCK, K_BLOCK) + bias_ref[edge]
            m = jnp.max(s, axis=-1, keepdims=True)
            p = jnp.exp2(s - m)
            psum = jnp.sum(p, axis=-1, keepdims=True)
            pb = p.astype(BF16)
            p_cat = jnp.concatenate([pb[h] for h in range(HEADS_PER_GROUP)], axis=1)
            zv = jnp.zeros_like(vw)
            v_heads = jnp.concatenate(
                [jnp.where(head_of_lane == h, vw, zv) for h in range(HEADS_PER_GROUP)], axis=0)
            o = _dot(p_cat, v_heads)
            lse = m + jnp.log2(psum)
            o = o / by_head([psum[h] for h in range(HEADS_PER_GROUP)])
            lse_lanes = by_head([lse[h] for h in range(HEADS_PER_GROUP)])
            if dilation == 1:
                rows = pl.ds(r0, Q_BLOCK)
            else:
                rows = pl.ds(r + dilation * r0, Q_BLOCK, stride=dilation)
            for half in range(GROUP_WIDTH // LANES):
                lanes = slice(half * LANES, (half + 1) * LANES)
                if first:
                    num[half, rows, :] = o[:, lanes]
                    den[half, rows, :] = jnp.ones((Q_BLOCK, LANES), F32)
                    top[half, rows, :] = lse_lanes[:, lanes]
                else:
                    old = top[half, rows, :]
                    new = jnp.maximum(old, lse_lanes[:, lanes])
                    keep = jnp.exp2(old - new)
                    add = jnp.exp2(lse_lanes[:, lanes] - new)
                    num[half, rows, :] = keep * num[half, rows, :] + add * o[:, lanes]
                    den[half, rows, :] = keep * den[half, rows, :] + add
                    top[half, rows, :] = new

    def blocks_of(r, way):
        if n_blocks == 1:
            block(r, way, 0)
        else:
            def several(j, carry):
                for k in range(BLOCKS_IN_FLIGHT):
                    block(r, way, BLOCKS_IN_FLIGHT * j + k)
                return carry
            lax.fori_loop(0, n_blocks // BLOCKS_IN_FLIGHT, several, 0)

    if dilation == 1:
        load_keys(0, 0)
        blocks_of(0, 0)
    else:
        def residues(j, carry):
            for way in range(ways):
                load_keys(ways * j + way, way)
            for way in range(ways):
                blocks_of(ways * j + way, way)
            return carry
        ways = min(PAD_WAYS, dilation, max(1, 2 * BLOCKS_IN_FLIGHT // n_blocks))
        lax.fori_loop(0, dilation // ways, residues, 0)


def _t5_bucket(rel):
    nb = N_BUCKETS // 2
    ret = jnp.where(rel > 0, nb, 0)
    n = jnp.abs(rel)
    max_exact = nb // 2
    nf = jnp.maximum(n, max_exact).astype(F32)
    large = max_exact + (jnp.log(nf / max_exact) / math.log(MAX_DISTANCE / max_exact)
                         * (nb - max_exact)).astype(I32)
    large = jnp.minimum(large, nb - 1)
    return ret + jnp.where(n < max_exact, n, large)


def _band_bias(rel_bias, group, dilation):
    qi = jnp.arange(Q_BLOCK)[:, None]
    ki = jnp.arange(K_BLOCK)[None, :]
    bucket = _t5_bucket((ki - N_SIDE - qi) * dilation)
    tab = rel_bias[:, group * HEADS_PER_GROUP:(group + 1) * HEADS_PER_GROUP]
    onehot = (bucket[:, :, None] == jnp.arange(N_BUCKETS)[None, None, :]).astype(F32)
    bias = jnp.einsum('qkb,bh->hqk', onehot, tab, precision=lax.Precision.HIGHEST).astype(F32)
    bias = bias * LOG2_E
    band = jnp.abs(ki - N_SIDE - qi) <= N_SIDE
    variants = []
    for edge in range(4):
        ok = band
        if edge & 1:
            ok = ok & (ki >= N_SIDE)
        if edge & 2:
            ok = ok & (ki < Q_BLOCK + N_SIDE)
        variants.append(jnp.where(ok[None], bias, NEG_INF))
    return jnp.stack(variants)


def _attention(qkv, rel_bias):
    biases = [_band_bias(rel_bias, g, d) for g, (_, d) in enumerate(DILATION_PATTERNS)]
    slab = pltpu.VMEM((GROUP_WIDTH // LANES, SEQ, LANES), F32)
    pads = pltpu.VMEM((PAD_WAYS, 2, SEQ + 2 * N_SIDE, GROUP_WIDTH), BF16)
    out = pl.pallas_call(
        _attn_kernel,
        grid=(BATCH,),
        in_specs=[pl.BlockSpec((1,) + a.shape[1:], lambda b: (b, 0, 0, 0)) for a in qkv]
                 + [_full_spec(b) for b in biases],
        out_specs=pl.BlockSpec((1, SEQ, GROUP_WIDTH), lambda b: (b, 0, 0)),
        out_shape=jax.ShapeDtypeStruct((BATCH, SEQ, GROUP_WIDTH), BF16),
        scratch_shapes=[pads, slab, slab, slab],
        compiler_params=_cparams("parallel"),
        name="attn",
    )(*qkv, *biases)
    return out.reshape(TOKENS, GROUP_WIDTH)


def _mixout_kernel(h_ref, attn_ref, u_ref, uprev_ref, unext_ref, gate_ref,
                   pw_ref, ps_ref, wpa_ref, wpp_ref, wo_ref, out_ref, runs):
    j = pl.program_id(0) % TILES_PER_SEQ
    y_attn = _dot(attn_ref[...], wpa_ref[...])

    u = u_ref[...]
    prev = jnp.where(j == 0, 0.0, uprev_ref[0])
    nxt = jnp.where(j == TILES_PER_SEQ - 1, 0.0, unext_ref[0])
    ext_rows = ROW_TILE + 2 * POOL_HALO
    runs[:, ext_rows:ext_rows + POOL_HALO, :] = jnp.zeros((2, POOL_HALO, POOL_GROUP_WIDTH), F32)
    pos = j * ROW_TILE + lax.broadcasted_iota(I32, (ROW_TILE, 1), 0)
    mixed = []
    for gi, w in enumerate(POOL_WINDOWS):
        half = w // 2
        sl = slice(gi * POOL_GROUP_WIDTH, (gi + 1) * POOL_GROUP_WIDTH)
        runs[0, 0:POOL_HALO, :] = prev[:, sl]
        runs[0, POOL_HALO:POOL_HALO + ROW_TILE, :] = u[:, sl]
        runs[0, POOL_HALO + ROW_TILE:ext_rows, :] = nxt[:, sl]
        src, span = 0, 1
        while 2 * span < w:
            runs[1 - src, 0:ext_rows, :] = runs[src, 0:ext_rows, :] + runs[src, span:span + ext_rows, :]
            src, span = 1 - src, 2 * span
        lo = POOL_HALO - half
        acc = runs[src, lo:lo + ROW_TILE, :] + runs[src, lo + half:lo + half + ROW_TILE, :]
        cnt = (jnp.minimum(pos + half, SEQ) - jnp.maximum(pos - half, 0)).astype(F32)
        pooled = acc / cnt - u[:, sl]
        mixed.append(_dot(pooled.astype(BF16), pw_ref[gi]) * ps_ref[:, sl])
    y_pool = _dot(jnp.concatenate(mixed, axis=1).astype(BF16), wpp_ref[...])

    y = (gate_ref[:, 0:D_MODEL] * y_attn.astype(BF16)
         + gate_ref[:, D_MODEL:GATE_WIDTH] * y_pool.astype(BF16))
    out_ref[...] = h_ref[...] + _dot(y, wo_ref[...])


def _mix_out(h, attn, u, gates, pool_w, pool_scale, w_proj_attn, w_proj_pool, w_out):
    halo_blocks = ROW_TILE // POOL_HALO
    u3 = u.reshape(TOKENS // POOL_HALO, POOL_HALO, POOL_WIDTH)
    last = TOKENS // POOL_HALO - 1
    prev_spec = pl.BlockSpec((1, POOL_HALO, POOL_WIDTH),
                             lambda i: (jnp.maximum(i * halo_blocks - 1, 0), 0, 0))
    next_spec = pl.BlockSpec((1, POOL_HALO, POOL_WIDTH),
                             lambda i: (jnp.minimum((i + 1) * halo_blocks, last), 0, 0))
    weights = (pool_w, pool_scale, w_proj_attn, w_proj_pool, w_out)
    return pl.pallas_call(
        _mixout_kernel,
        grid=(N_TILES,),
        in_specs=[_row_spec(D_MODEL), _row_spec(GROUP_WIDTH),
                  _row_spec(POOL_WIDTH), prev_spec, next_spec, _row_spec(GATE_WIDTH)]
                 + [_full_spec(w) for w in weights],
        out_specs=_row_spec(D_MODEL),
        out_shape=jax.ShapeDtypeStruct((TOKENS, D_MODEL), F32),
        scratch_shapes=[pltpu.VMEM((2, ROW_TILE + 3 * POOL_HALO, POOL_GROUP_WIDTH), F32)],
        compiler_params=_cparams("parallel"),
        name="mixout",
    )(h, attn, u, u3, u3, gates, *weights)


def _router_kernel(h_ref, g_ref, wr_ref, xn_ref, ri_ref, rw_ref, cnt_ref):
    xn = _rms(h_ref[...], g_ref[...])
    hi = xn.astype(BF16)
    xn_ref[...] = hi
    lo = (xn - hi.astype(F32)).astype(BF16)
    both = _dot(hi, wr_ref[...])
    lg = both[:, 0:ROUTER_LANES] + (both[:, ROUTER_LANES:] + _dot(lo, wr_ref[:, 0:ROUTER_LANES]))

    lane = lax.broadcasted_iota(I32, (ROW_TILE, ROUTER_LANES), 1)
    lanef = lane.astype(F32)
    low = jnp.float32(-3.0e38)
    far = jnp.float32(ROUTER_LANES)
    first = lambda hit: jnp.min(jnp.where(hit, lanef, far), axis=-1, keepdims=True)

    is_group = lane < N_EXPERT_GROUPS
    gl = jnp.where(is_group, lg, low)
    gmax = jnp.max(gl, axis=-1, keepdims=True)
    gidx = first(gl == gmax).astype(I32)
    gden = jnp.sum(jnp.where(is_group, jnp.exp(gl - gmax), 0.0), axis=-1, keepdims=True)
    g_p = 1.0 / gden

    in_group = ((lane >= EXPERT_LANE0) & (lane < EXPERT_LANE0 + N_EXPERTS)
                & ((lane - EXPERT_LANE0) // EXPERTS_PER_GROUP == gidx))
    el = jnp.where(in_group, lg, low)
    t1 = jnp.max(el, axis=-1, keepdims=True)
    l1 = first(in_group & (el == t1))
    rest = in_group & (lanef != l1)
    el2 = jnp.where(rest, lg, low)
    t2 = jnp.max(el2, axis=-1, keepdims=True)
    l2 = first(rest & (el2 == t2))
    e2 = jnp.exp(t2 - t1)
    w1 = g_p * (1.0 / (1.0 + e2))
    w2 = g_p * (e2 / (1.0 + e2))

    hit1 = lanef == l1
    hit2 = lanef == l2
    onehot = (hit1 | hit2).astype(BF16)
    ri = lax.broadcasted_iota(I32, (ROW_TILE, ROW_TILE), 0)
    ci = lax.broadcasted_iota(I32, (ROW_TILE, ROW_TILE), 1)
    before = (ci < ri).astype(BF16)
    seen = _dot(before, onehot)
    r1 = jnp.sum(jnp.where(hit1, seen, 0.0), axis=-1, keepdims=True)
    r2 = jnp.sum(jnp.where(hit2, seen, 0.0), axis=-1, keepdims=True)

    packed = jnp.zeros((ROW_TILE, ROUTER_LANES), F32)
    for k, v in enumerate((l1, l2, r1, r2)):
        packed = jnp.where(lane == k, v, packed)
    ri_ref[...] = packed.astype(I32)
    rw_ref[...] = jnp.where(lane == 0, w1, jnp.where(lane == 1, w2, 0.0))
    counts = jnp.sum(onehot.astype(F32), axis=0, keepdims=True)
    cnt_ref[0] = jnp.broadcast_to(counts, (SUBLANES, ROUTER_LANES))


def _route(h, g, wr_split):
    return pl.pallas_call(
        _router_kernel,
        grid=(N_TILES,),
        in_specs=[_row_spec(D_MODEL), _full_spec(g), _full_spec(wr_split)],
        out_specs=[_row_spec(D_MODEL), _row_spec(ROUTER_LANES), _row_spec(ROUTER_LANES),
                   pl.BlockSpec((1, SUBLANES, ROUTER_LANES), lambda i: (i, 0, 0))],
        out_shape=[jax.ShapeDtypeStruct((TOKENS, D_MODEL), BF16),
                   jax.ShapeDtypeStruct((TOKENS, ROUTER_LANES), I32),
                   jax.ShapeDtypeStruct((TOKENS, ROUTER_LANES), F32),
                   jax.ShapeDtypeStruct((N_TILES, SUBLANES, ROUTER_LANES), F32)],
        compiler_params=_cparams("parallel"),
        name="router",
    )(h, g, wr_split)


def _slots(ri_ref, off_ref):
    ri = ri_ref[...]
    lane = lax.broadcasted_iota(I32, (ROW_TILE, ROUTER_LANES), 1)
    off = off_ref[0, 0:1, :]
    pick = lambda k: jnp.sum(jnp.where(lane == ri[:, k:k + 1], off, 0.0), axis=-1, keepdims=True)
    return (pick(0) + ri[:, 2:3].astype(F32), pick(1) + ri[:, 3:4].astype(F32))


def _as_rows(cols):
    eye = (lax.broadcasted_iota(I32, (ROW_TILE, ROW_TILE), 0)
           == lax.broadcasted_iota(I32, (ROW_TILE, ROW_TILE), 1))
    return [jnp.sum(jnp.where(eye, c, 0.0), axis=0, keepdims=True) for c in cols]


def _run_copy(tile, e, loff_s, c8_s, dst_s, buf_ref, slot, hbm_ref, sem, to_hbm):
    k = tile * N_EXPERTS + e
    n = pl.multiple_of(c8_s[k], RUN_ALIGN)
    vm = buf_ref.at[slot, pl.ds(pl.multiple_of(loff_s[k], RUN_ALIGN), n)]
    hb = hbm_ref.at[pl.ds(pl.multiple_of(dst_s[k], RUN_ALIGN), n)]
    src, dst = (vm, hb) if to_hbm else (hb, vm)
    return n, pltpu.make_async_copy(src, dst, sem.at[slot])


def _loop(lo, hi, fn, unroll=1):
    def body(e, carry):
        fn(e)
        return carry
    lax.fori_loop(lo, hi, body, 0, unroll=unroll)


def _wait_rows(n, buf_ref, slot, hbm_ref, sem, to_hbm):
    vm = buf_ref.at[slot, pl.ds(0, n)]
    hb = hbm_ref.at[pl.ds(0, n)]
    src, dst = (vm, hb) if to_hbm else (hb, vm)
    pltpu.make_async_copy(src, dst, sem.at[slot]).wait()


def _start(n, cp):
    @pl.when(n > 0)
    def _():
        cp.start()


def _wait(n, cp):
    @pl.when(n > 0)
    def _():
        cp.wait()


def _dispatch_kernel(loff_s, c8_s, dst_s, nslot_s, zdst_s, zcnt_s, nact_s,
                     xn_ref, ri_ref, rw_ref, off_ref, xpad_hbm, sorted_buf, zero_buf, sem, zsem):
    i = pl.program_id(0)
    slot = i % 2
    last = pl.num_programs(0) - 1

    def zero_copy(e):
        n = pl.multiple_of(zcnt_s[e], RUN_ALIGN)
        dst = xpad_hbm.at[pl.ds(pl.multiple_of(zdst_s[e], RUN_ALIGN), n)]
        return n, pltpu.make_async_copy(zero_buf.at[pl.ds(0, n)], dst, zsem)

    def tail_copy(b):
        dst = xpad_hbm.at[pl.ds(pl.multiple_of(b * MOE_BLOCK, MOE_BLOCK), MOE_BLOCK)]
        return pltpu.make_async_copy(zero_buf, dst, zsem)

    @pl.when(i == 0)
    def _():
        zero_buf[...] = jnp.zeros_like(zero_buf)
        _loop(0, N_EXPERTS, lambda e: _start(*zero_copy(e)))
        _loop(nact_s[0], N_MOE_BLOCKS, lambda b: tail_copy(b).start())

    s1, s2 = _slots(ri_ref, off_ref)
    rw = rw_ref[...]
    s1_row, s2_row, w1_row, w2_row = _as_rows([s1, s2, rw[:, 0:1], rw[:, 1:2]])
    xn = xn_ref[...]
    for first, size in SLOT_CHUNKS:
        @pl.when(first < nslot_s[i])
        def _():
            rows = slice(first, first + size)
            lane = lax.broadcasted_iota(I32, (size, LANES), 1)
            srow = (first + lax.broadcasted_iota(I32, (size, ROW_TILE), 0)).astype(F32)
            hit1 = srow == s1_row
            hit2 = srow == s2_row
            xs = _dot((hit1 | hit2).astype(BF16), xn)
            sorted_buf[slot, rows, 0:HALF] = _pack_halves(xs[:, 0:HALF], xs[:, HALF:D_MODEL])
            ws = jnp.sum(jnp.where(hit1, w1_row, 0.0) + jnp.where(hit2, w2_row, 0.0),
                         axis=-1, keepdims=True)
            sorted_buf[slot, rows, HALF:X_WORDS] = jnp.where(lane == 0, _bits(ws), jnp.uint32(0))

    copy = lambda tile, sl: (lambda e: _run_copy(tile, e, loff_s, c8_s, dst_s, sorted_buf, sl,
                                                 xpad_hbm, sem, True))
    mine = copy(i, slot)
    _loop(0, N_EXPERTS, lambda e: _start(*mine(e)), unroll=RUN_UNROLL)
    tile_rows = lambda t: pl.multiple_of(nslot_s[t], RUN_ALIGN)

    @pl.when(i > 0)
    def _():
        _wait_rows(tile_rows(i - 1), sorted_buf, 1 - slot, xpad_hbm, sem, True)

    @pl.when(i == last)
    def _():
        _wait_rows(tile_rows(i), sorted_buf, slot, xpad_hbm, sem, True)
        _loop(0, N_EXPERTS, lambda e: _wait(*zero_copy(e)))
        _loop(nact_s[0], N_MOE_BLOCKS, lambda b: tail_copy(b).wait())


def _dispatch(tables, xn, route_i, route_w):
    tile_row = pl.BlockSpec((1, SUBLANES, ROUTER_LANES), lambda i, *_: (i, 0, 0))
    grid_spec = pltpu.PrefetchScalarGridSpec(
        num_scalar_prefetch=7,
        grid=(N_TILES,),
        in_specs=[_row_spec(D_MODEL), _row_spec(ROUTER_LANES), _row_spec(ROUTER_LANES), tile_row],
        out_specs=pl.BlockSpec(memory_space=pl.ANY),
        scratch_shapes=[pltpu.VMEM((2, MAX_SLOTS, X_WORDS), U32),
                        pltpu.VMEM((MOE_BLOCK, X_WORDS), U32),
                        pltpu.SemaphoreType.DMA((2,)),
                        pltpu.SemaphoreType.DMA(())],
    )
    return pl.pallas_call(
        _dispatch_kernel,
        grid_spec=grid_spec,
        out_shape=jax.ShapeDtypeStruct((PAD_ROWS, X_WORDS), U32),
        compiler_params=_cparams("arbitrary"),
        name="dispatch",
    )(tables["loff"], tables["c8"], tables["dst"], tables["nslot"], tables["zdst"], tables["zcnt"],
      tables["n_active"], xn, route_i, route_w, tables["offrow"])


def _combine_kernel(loff_s, c8_s, dst_s, nslot_s,
                    h_ref, ri_ref, off_ref, g_ref, ypad_hbm, out_ref,
                    ybuf, sem, *, final_norm):
    i = pl.program_id(0)
    slot = i % 2
    last = pl.num_programs(0) - 1
    fetch = lambda tile, sl: (lambda e: _run_copy(tile, e, loff_s, c8_s, dst_s, ybuf, sl,
                                                  ypad_hbm, sem, False))

    @pl.when(i == 0)
    def _():
        ybuf[...] = jnp.zeros_like(ybuf)
        first = fetch(i, slot)
        _loop(0, N_EXPERTS, lambda e: _start(*first(e)), unroll=RUN_UNROLL)

    @pl.when(i < last)
    def _():
        nxt = fetch(i + 1, 1 - slot)
        _loop(0, N_EXPERTS, lambda e: _start(*nxt(e)), unroll=RUN_UNROLL)

    _wait_rows(pl.multiple_of(nslot_s[i], RUN_ALIGN), ybuf, slot, ypad_hbm, sem, False)

    s1, s2 = _slots(ri_ref, off_ref)

    def gathered(first, size):
        scol = (first + lax.broadcasted_iota(I32, (ROW_TILE, size), 1)).astype(F32)
        pick = ((scol == s1) | (scol == s2)).astype(BF16)
        y_hi, y_lo = _unpack_halves(ybuf[slot, first:first + size, :])
        return jnp.concatenate([_dot(pick, y_hi), _dot(pick, y_lo)], axis=1)

    finish = (lambda v: _rms(v, g_ref[...])) if final_norm else (lambda v: v)
    h = h_ref[...] + gathered(0, COMBINE_BASE)
    spill = nslot_s[i] > COMBINE_BASE

    @pl.when(jnp.logical_not(spill))
    def _():
        out_ref[...] = finish(h)

    @pl.when(spill)
    def _():
        out_ref[...] = finish(h + gathered(COMBINE_BASE, MAX_SLOTS - COMBINE_BASE))


def _combine(tables, h, route_i, y_pad, g, final_norm):
    tile_row = pl.BlockSpec((1, SUBLANES, ROUTER_LANES), lambda i, *_: (i, 0, 0))
    grid_spec = pltpu.PrefetchScalarGridSpec(
        num_scalar_prefetch=4,
        grid=(N_TILES,),
        in_specs=[_row_spec(D_MODEL), _row_spec(ROUTER_LANES), tile_row, _full_spec(g),
                  pl.BlockSpec(memory_space=pl.ANY)],
        out_specs=_row_spec(D_MODEL),
        scratch_shapes=[pltpu.VMEM((2, MAX_SLOTS, HALF), U32),
                        pltpu.SemaphoreType.DMA((2,))],
    )
    return pl.pallas_call(
        functools.partial(_combine_kernel, final_norm=final_norm),
        grid_spec=grid_spec,
        out_shape=jax.ShapeDtypeStruct((TOKENS, D_MODEL), F32),
        compiler_params=_cparams("arbitrary"),
        name="combine",
    )(tables["loff"], tables["c8"], tables["dst"], tables["nslot"],
      h, route_i, tables["offrow"], g, y_pad)


def _expert_kernel(be_ref, slot_ref, next_ref, nact_ref, x_ref, wg_hbm, wu_hbm, wd_hbm, y_hbm,
                   wg_f, wu_f, wg_b, wu_b, wd_s, ybuf, zero_buf, sem, ysem, zsem, *, layer):
    i = pl.program_id(0)
    n_active = nact_ref[0]
    staged = ((wg_hbm, wg_f, wg_b), (wu_hbm, wu_f, wu_b))

    def fetch(e, slot):
        copies = [pltpu.make_async_copy(hbm.at[layer, e], f32.at[slot], sem.at[slot, k])
                  for k, (hbm, f32, _) in enumerate(staged)]
        return copies + [pltpu.make_async_copy(wd_hbm.at[e], wd_s.at[slot], sem.at[slot, len(staged)])]

    def block_rows(b):
        return y_hbm.at[pl.ds(pl.multiple_of(b * MOE_BLOCK, MOE_BLOCK), MOE_BLOCK)]

    def put(b):
        return pltpu.make_async_copy(ybuf.at[b % 2], block_rows(b), ysem.at[b % 2])

    def tail_copy(b):
        return pltpu.make_async_copy(zero_buf, block_rows(b), zsem)

    @pl.when(i == 0)
    def _():
        zero_buf[...] = jnp.zeros_like(zero_buf)
        _loop(n_active, N_MOE_BLOCKS, lambda b: tail_copy(b).start())

    @pl.when(i == pl.num_programs(0) - 1)
    def _():
        _loop(n_active, N_MOE_BLOCKS, lambda b: tail_copy(b).wait())

    @pl.when(i < n_active)
    def _():
        e = be_ref[i]
        slot = slot_ref[i]

        def request(ahead, block):
            nxt = next_ref[(ahead - 1) * N_MOE_BLOCKS + block]

            @pl.when(nxt >= 0)
            def _():
                for cp in fetch(nxt, (slot + ahead) % W_STAGES):
                    cp.start()

        @pl.when((i == 0) | (e != be_ref[jnp.maximum(i - 1, 0)]))
        def _():
            @pl.when(i == 0)
            def _():
                for cp in fetch(e, slot):
                    cp.start()
                for ahead in range(1, W_STAGES - 1):
                    request(ahead, i)
            for cp in fetch(e, slot):
                cp.wait()
            request(W_STAGES - 1, i)
            for _, f32, b16 in staged:
                b16[...] = f32[slot].astype(BF16)

        x_hi, x_lo = _unpack_halves(x_ref[:, 0:HALF])
        row_w = lax.bitcast_convert_type(x_ref[:, HALF:HALF + 1], F32)
        gate = _dot(x_hi, wg_b[0:HALF, :]) + _dot(x_lo, wg_b[HALF:D_MODEL, :])
        up = _dot(x_hi, wu_b[0:HALF, :]) + _dot(x_lo, wu_b[HALF:D_MODEL, :])
        hmid = (jax.nn.silu(gate) * up).astype(BF16)
        y = (_dot(hmid, wd_s[slot]) * row_w).astype(BF16).astype(F32)
        ybuf[i % 2] = _pack_halves(y[:, 0:HALF], y[:, HALF:D_MODEL])
        put(i).start()

        @pl.when(i > 0)
        def _():
            put(i - 1).wait()

        @pl.when(i == n_active - 1)
        def _():
            put(i).wait()


def _experts(layer, tables, x_pad, w_gate, w_up, w_down):
    hbm = pl.BlockSpec(memory_space=pl.ANY)
    up_shape, down_shape = (D_MODEL, D_EXPERT), (D_EXPERT, D_MODEL)
    grid_spec = pltpu.PrefetchScalarGridSpec(
        num_scalar_prefetch=4,
        grid=(N_MOE_BLOCKS,),
        in_specs=[pl.BlockSpec((MOE_BLOCK, X_WORDS),
                               lambda i, be, sl, nx, na: (jnp.minimum(i, na[0] - 1), 0)),
                  hbm, hbm, hbm],
        out_specs=hbm,
        scratch_shapes=[pltpu.VMEM((W_STAGES,) + up_shape, F32),
                        pltpu.VMEM((W_STAGES,) + up_shape, F32),
                        pltpu.VMEM(up_shape, BF16), pltpu.VMEM(up_shape, BF16),
                        pltpu.VMEM((W_STAGES,) + down_shape, BF16),
                        pltpu.VMEM((2, MOE_BLOCK, HALF), U32),
                        pltpu.VMEM((MOE_BLOCK, HALF), U32),
                        pltpu.SemaphoreType.DMA((W_STAGES, 3)),
                        pltpu.SemaphoreType.DMA((2,)),
                        pltpu.SemaphoreType.DMA(())],
    )
    return pl.pallas_call(
        functools.partial(_expert_kernel, layer=layer),
        grid_spec=grid_spec,
        out_shape=jax.ShapeDtypeStruct((PAD_ROWS, HALF), U32),
        compiler_params=_cparams("arbitrary"),
        name="experts",
    )(tables["blk_expert"], tables["blk_slot"], tables["blk_next"], tables["n_active"],
      x_pad, w_gate, w_up, w_down)


def _routing_tables(tile_counts):
    cnt = tile_counts[:, 0, EXPERT_LANE0:EXPERT_LANE0 + N_EXPERTS].astype(I32)
    c8 = (cnt + RUN_ALIGN - 1) // RUN_ALIGN * RUN_ALIGN
    loff = jnp.cumsum(c8, axis=1) - c8
    nslot = jnp.sum(c8, axis=1)
    tot = jnp.sum(c8, axis=0)
    padded = (tot + MOE_BLOCK - 1) // MOE_BLOCK * MOE_BLOCK
    end = jnp.cumsum(padded)
    base = end - padded
    dst = base[None, :] + jnp.cumsum(c8, axis=0) - c8
    blk_start = jnp.arange(N_MOE_BLOCKS, dtype=I32) * MOE_BLOCK
    blk_expert = jnp.minimum(jnp.sum((end[None, :] <= blk_start[:, None]).astype(I32), axis=1),
                             N_EXPERTS - 1).astype(I32)
    offrow = jnp.zeros((N_TILES, ROUTER_LANES), F32).at[:, EXPERT_LANE0:EXPERT_LANE0 + N_EXPERTS].set(
        loff.astype(F32))
    offrow = jnp.broadcast_to(offrow[:, None, :], (N_TILES, SUBLANES, ROUTER_LANES))
    experts = jnp.arange(N_EXPERTS, dtype=I32)
    present = padded > 0
    ordinal = jnp.cumsum(present.astype(I32)) - 1
    at_or_after = lax.cummin(jnp.where(present, experts, N_EXPERTS), reverse=True)
    after = jnp.concatenate([at_or_after[1:], jnp.full((1,), N_EXPERTS, I32)])
    hop = jnp.concatenate([after, jnp.full((1,), N_EXPERTS, I32)])
    pick = (blk_expert[:, None] == experts[None, :]).astype(I32)
    blk_slot = jnp.sum(pick * (ordinal % W_STAGES)[None, :], axis=1).astype(I32)
    ahead, blk_next = experts, []
    for _ in range(W_STAGES - 1):
        ahead = jnp.sum((ahead[:, None] == jnp.arange(N_EXPERTS + 1)[None, :]) * hop[None, :], axis=1)
        per_block = jnp.sum(pick * ahead[None, :], axis=1)
        blk_next.append(jnp.where(per_block >= N_EXPERTS, -1, per_block))
    blk_next = jnp.concatenate(blk_next).astype(I32)
    return {
        "blk_slot": blk_slot, "blk_next": blk_next,
        "loff": loff.reshape(-1).astype(I32), "c8": c8.reshape(-1).astype(I32),
        "dst": dst.reshape(-1).astype(I32), "nslot": nslot.astype(I32),
        "zdst": (base + tot).astype(I32), "zcnt": (padded - tot).astype(I32),
        "offrow": offrow, "blk_expert": blk_expert,
        "n_active": (end[-1:] // MOE_BLOCK).astype(I32),
    }


def _split_router_weights(w_router_group, w_router_expert):
    w_e = w_router_expert.transpose(1, 0, 2).reshape(D_MODEL, N_EXPERTS)
    w = jnp.concatenate([w_router_group, w_e], axis=1)
    w = jnp.pad(w, ((0, 0), (0, ROUTER_LANES - w.shape[1])))
    hi = w.astype(BF16)
    lo = (w - hi.astype(F32)).astype(BF16)
    return jnp.concatenate([hi, lo], axis=1)


def _input_weights(w):
    cols = []
    for g in range(N_GROUPS):
        for part in range(3):
            c0 = part * ATTN_WIDTH + g * GROUP_WIDTH
            cols.append(w[:, c0:c0 + GROUP_WIDTH])
    cols.append(w[:, QKV_WIDTH:])
    return jnp.concatenate(cols, axis=1).astype(BF16)


def kernel(x, rel_bias, norm_mix_g, w_in, pool_w, pool_scale, w_proj_attn, w_proj_pool, w_out,
           norm_ffn_g, w_router_group, w_router_expert, w_gate_e, w_up_e, w_down_e, norm_final_g):
    h = x.reshape(TOKENS, D_MODEL)
    for l in range(DEPTH):
        qkv, u, gates, w_down_b = _project(l, h, norm_mix_g[l][None], _input_weights(w_in[l]), w_down_e)
        h = _mix_out(h, _attention(qkv, rel_bias), u, gates, pool_w[l].astype(BF16), pool_scale[l][None],
                     w_proj_attn[l].astype(BF16), w_proj_pool[l].astype(BF16), w_out[l].astype(BF16))
        xn, route_i, route_w, tile_counts = _route(
            h, norm_ffn_g[l][None], _split_router_weights(w_router_group[l], w_router_expert[l]))
        tables = _routing_tables(tile_counts)
        x_pad = _dispatch(tables, xn, route_i, route_w)
        y_pad = _experts(l, tables, x_pad, w_gate_e, w_up_e, w_down_b)
        h = _combine(tables, h, route_i, y_pad, norm_final_g[None], l == DEPTH - 1)
    return h.reshape(BATCH, SEQ, D_MODEL)
```

```python
import functools
import math

import jax
import jax.numpy as jnp
from jax import lax
from jax.experimental import pallas as pl
from jax.experimental.pallas import tpu as pltpu

F32 = jnp.float32
BF16 = jnp.bfloat16
I32 = jnp.int32
U32 = jnp.uint32

D_MODEL = 1024
BATCH = 8
SEQ = 2048
TOKENS = BATCH * SEQ
DEPTH = 2

HEAD_DIM = 64
HEADS_PER_GROUP = 4
GROUP_WIDTH = HEADS_PER_GROUP * HEAD_DIM
DILATION_PATTERNS = ((128, 1), (512, 4), (2048, 16))
N_GROUPS = len(DILATION_PATTERNS)
N_ATTN_HEADS = N_GROUPS * HEADS_PER_GROUP
ATTN_WIDTH = N_ATTN_HEADS * HEAD_DIM
QKV_WIDTH = 3 * ATTN_WIDTH
GROUP_QKV = 3 * GROUP_WIDTH
N_SIDE = 64
assert all(w // (2 * d) == N_SIDE for w, d in DILATION_PATTERNS)
POOL_WINDOWS = (2, 4, 8, 16)
POOL_GROUP_WIDTH = 128
POOL_WIDTH = len(POOL_WINDOWS) * POOL_GROUP_WIDTH
POOL_HALO = max(POOL_WINDOWS) // 2
N_BRANCHES = 2
GATE_WIDTH = N_BRANCHES * D_MODEL
IN_WIDTH = QKV_WIDTH + POOL_WIDTH + GATE_WIDTH
N_BUCKETS = 32
MAX_DISTANCE = 1024
N_EXPERT_GROUPS = 8
EXPERTS_PER_GROUP = 8
N_EXPERTS = N_EXPERT_GROUPS * EXPERTS_PER_GROUP
TOP_K = 2
D_EXPERT = 512
N_ASSIGN = TOKENS * TOP_K
EPS = 1e-6
NEG_INF = -1e30
LOG2_E = math.log2(math.e)
Q_SCALE = HEAD_DIM ** -0.5 * LOG2_E

LANES = 128
SUBLANES = 8
ROW_TILE = 512
N_TILES = TOKENS // ROW_TILE
TILES_PER_SEQ = SEQ // ROW_TILE
Q_BLOCK = 128
K_BLOCK = Q_BLOCK + 2 * N_SIDE
PAD_WAYS = 4
BLOCKS_IN_FLIGHT = 4
ROUTER_LANES = 128
EXPERT_LANE0 = N_EXPERT_GROUPS
VMEM_LIMIT = 56 * 1024 * 1024

RUN_ALIGN = SUBLANES
MOE_BLOCK = 336
HALF = D_MODEL // 2
X_WORDS = HALF + LANES
SLOT_CHUNK = 256
RUN_UNROLL = 4
W_STAGES = 2
MAX_SLOTS = -(-(TOP_K * ROW_TILE + N_EXPERTS * (RUN_ALIGN - 1)) // SLOT_CHUNK) * SLOT_CHUNK
SLOT_CHUNKS = (tuple((s, ROW_TILE) for s in range(0, TOP_K * ROW_TILE, ROW_TILE))
               + tuple((s, SLOT_CHUNK) for s in range(TOP_K * ROW_TILE, MAX_SLOTS, SLOT_CHUNK)))
COMBINE_BASE = MAX_SLOTS - SLOT_CHUNK
N_MOE_BLOCKS = -(-(N_ASSIGN + N_TILES * N_EXPERTS * (RUN_ALIGN - 1)
                   + N_EXPERTS * (MOE_BLOCK - RUN_ALIGN)) // MOE_BLOCK)
PAD_ROWS = N_MOE_BLOCKS * MOE_BLOCK
HIGH_HALF = 0xFFFF0000


def _cparams(*sem):
    return pltpu.CompilerParams(dimension_semantics=sem, vmem_limit_bytes=VMEM_LIMIT)


def _rms(h, g):
    r = lax.rsqrt(jnp.mean(h * h, axis=-1, keepdims=True) + EPS)
    return (h * r) * g


def _dot(a, b):
    return jnp.dot(a, b, preferred_element_type=F32)


def _row_spec(width):
    return pl.BlockSpec((ROW_TILE, width), lambda i, *_: (i, 0))


def _full_spec(a):
    return pl.BlockSpec(a.shape, lambda i, *_: (0,) * a.ndim)


def _bits(x):
    return lax.bitcast_convert_type(x, U32)


def _pack_halves(a, b):
    return (_bits(a) & jnp.uint32(HIGH_HALF)) | (_bits(b) >> 16)


def _unpack_halves(words):
    hi = lax.bitcast_convert_type(words & jnp.uint32(HIGH_HALF), F32)
    lo = lax.bitcast_convert_type(words << 16, F32)
    return hi.astype(BF16), lo.astype(BF16)


def _proj_kernel(h_ref, g_ref, w_ref, wd_ref, q0_ref, q1_ref, q2_ref, u_ref, gate_ref, wdb_ref, slabs):
    wdb_ref[...] = wd_ref[...].astype(BF16)
    xn = _rms(h_ref[...], g_ref[...]).astype(BF16)
    n_slabs = GROUP_QKV // LANES
    for g, out_ref in enumerate((q0_ref, q1_ref, q2_ref)):
        dilation = DILATION_PATTERNS[g][1]
        res = _dot(xn, w_ref[:, g * GROUP_QKV:(g + 1) * GROUP_QKV])
        res = jnp.concatenate([res[:, 0:GROUP_WIDTH] * Q_SCALE, res[:, GROUP_WIDTH:]], axis=1)
        if dilation == 1:
            out_ref[0, 0] = res.astype(BF16)
            continue
        for s in range(n_slabs):
            slabs[s] = res[:, s * LANES:(s + 1) * LANES]
        n = ROW_TILE // dilation
        for r in range(dilation):
            rows = [slabs[s, pl.ds(r, n, stride=dilation), :] for s in range(n_slabs)]
            out_ref[0, r] = jnp.concatenate(rows, axis=1).astype(BF16)
    u_ref[...] = _dot(xn, w_ref[:, QKV_WIDTH:QKV_WIDTH + POOL_WIDTH])
    gates = _dot(xn, w_ref[:, QKV_WIDTH + POOL_WIDTH:IN_WIDTH])
    gate_ref[...] = jax.nn.sigmoid(gates).astype(BF16)


def _project(layer, h, g, w_bf16, w_down):
    qkv_shapes, qkv_specs = [], []
    for _, d in DILATION_PATTERNS:
        qkv_shapes.append(jax.ShapeDtypeStruct((BATCH, d, SEQ // d, GROUP_QKV), BF16))
        qkv_specs.append(pl.BlockSpec((1, d, ROW_TILE // d, GROUP_QKV),
                                      lambda i: (i // TILES_PER_SEQ, 0, i % TILES_PER_SEQ, 0)))
    layer_rows = N_EXPERTS * D_EXPERT
    step_rows = layer_rows // N_TILES
    res = pl.pallas_call(
        _proj_kernel,
        grid=(N_TILES,),
        in_specs=[_row_spec(D_MODEL), _full_spec(g), _full_spec(w_bf16),
                  pl.BlockSpec((step_rows, D_MODEL), lambda i: (layer * N_TILES + i, 0))],
        out_specs=qkv_specs + [_row_spec(POOL_WIDTH), _row_spec(GATE_WIDTH),
                               pl.BlockSpec((step_rows, D_MODEL), lambda i: (i, 0))],
        out_shape=qkv_shapes + [jax.ShapeDtypeStruct((TOKENS, POOL_WIDTH), F32),
                                jax.ShapeDtypeStruct((TOKENS, GATE_WIDTH), BF16),
                                jax.ShapeDtypeStruct((layer_rows, D_MODEL), BF16)],
        scratch_shapes=[pltpu.VMEM((GROUP_QKV // LANES, ROW_TILE, LANES), F32)],
        compiler_params=_cparams("parallel"),
        name="proj",
    )(h, g, w_bf16, w_down.reshape(DEPTH * layer_rows, D_MODEL))
    w_down_b = res[N_GROUPS + 2].reshape(N_EXPERTS, D_EXPERT, D_MODEL)
    return res[:N_GROUPS], res[N_GROUPS], res[N_GROUPS + 1], w_down_b


def _attn_kernel(q0_ref, q1_ref, q2_ref, b0_ref, b1_ref, b2_ref, o_ref, pads, num, den, top):
    zpad = jnp.zeros((N_SIDE, GROUP_WIDTH), BF16)
    groups = ((q2_ref, b2_ref, 2), (q1_ref, b1_ref, 1), (q0_ref, b0_ref, 0))
    for order, (qkv_ref, bias_ref, g) in enumerate(groups):
        dilation = DILATION_PATTERNS[g][1]
        sub_len = SEQ // dilation
        for way in range(PAD_WAYS):
            for kv in range(2):
                pads[way, kv, 0:N_SIDE, :] = zpad
                pads[way, kv, N_SIDE + sub_len:2 * N_SIDE + sub_len, :] = zpad
        _attn_group(qkv_ref, bias_ref, pads, num, den, top, dilation=dilation, first=order == 0)

    def finish(c, carry):
        rows = pl.ds(pl.multiple_of(c * ROW_TILE, ROW_TILE), ROW_TILE)
        merged = [num[half, rows, :] / den[half, rows, :] for half in range(GROUP_WIDTH // LANES)]
        o_ref[0, rows, :] = jnp.concatenate(merged, axis=1).astype(BF16)
        return carry

    lax.fori_loop(0, SEQ // ROW_TILE, finish, 0)


def _attn_group(qkv_ref, bias_ref, pads, num, den, top, *, dilation, first):
    sub_len = SEQ // dilation
    n_blocks = sub_len // Q_BLOCK
    head_of_lane = lax.broadcasted_iota(I32, (1, GROUP_WIDTH), 1) // HEAD_DIM

    def by_head(cols):
        out = cols[HEADS_PER_GROUP - 1]
        for h in range(HEADS_PER_GROUP - 2, -1, -1):
            out = jnp.where(head_of_lane == h, cols[h], out)
        return out

    def load_keys(r, way):
        pads[way, 0, N_SIDE:N_SIDE + sub_len, :] = qkv_ref[0, r, :, GROUP_WIDTH:2 * GROUP_WIDTH]
        pads[way, 1, N_SIDE:N_SIDE + sub_len, :] = qkv_ref[0, r, :, 2 * GROUP_WIDTH:3 * GROUP_WIDTH]

    def block(r, way, i):
        if True:
            r0 = i * Q_BLOCK if isinstance(i, int) else pl.multiple_of(i * Q_BLOCK, Q_BLOCK)
            qb = qkv_ref[0, r, pl.ds(r0, Q_BLOCK), 0:GROUP_WIDTH]
            kw = pads[way, 0, pl.ds(r0, K_BLOCK), :]
            vw = pads[way, 1, pl.ds(r0, K_BLOCK), :]
            zero = jnp.zeros_like(qb)
            if isinstance(i, int):
                edge = int(i == 0) + 2 * int(i == n_blocks - 1)
            else:
                edge = (i == 0).astype(I32) + 2 * (i == n_blocks - 1).astype(I32)
            q_heads = jnp.concatenate(
                [jnp.where(head_of_lane == h, qb, zero) for h in range(HEADS_PER_GROUP)], axis=0)
            s = lax.dot_general(q_heads, kw, (((1,), (1,)), ((), ())), preferred_element_type=F32)
            s = s.reshape(HEADS_PER_GROUP, Q_BLOCK, K_BLOCK) + bias_ref[edge]
            m = jnp.max(s, axis=-1, keepdims=True)
            p = jnp.exp2(s - m)
            psum = jnp.sum(p, axis=-1, keepdims=True)
            pb = p.astype(BF16)
            p_cat = jnp.concatenate([pb[h] for h in range(HEADS_PER_GROUP)], axis=1)
            zv = jnp.zeros_like(vw)
            v_heads = jnp.concatenate(
                [jnp.where(head_of_lane == h, vw, zv) for h in range(HEADS_PER_GROUP)], axis=0)
            o = _dot(p_cat, v_heads)
            lse = m + jnp.log2(psum)
            o = o / by_head([psum[h] for h in range(HEADS_PER_GROUP)])
            lse_lanes = by_head([lse[h] for h in range(HEADS_PER_GROUP)])
            if dilation == 1:
                rows = pl.ds(r0, Q_BLOCK)
            else:
                rows = pl.ds(r + dilation * r0, Q_BLOCK, stride=dilation)
            for half in range(GROUP_WIDTH // LANES):
                lanes = slice(half * LANES, (half + 1) * LANES)
                if first:
                    num[half, rows, :] = o[:, lanes]
                    den[half, rows, :] = jnp.ones((Q_BLOCK, LANES), F32)
                    top[half, rows, :] = lse_lanes[:, lanes]
                else:
                    old = top[half, rows, :]
                    new = jnp.maximum(old, lse_lanes[:, lanes])
                    keep = jnp.exp2(old - new)
                    add = jnp.exp2(lse_lanes[:, lanes] - new)
                    num[half, rows, :] = keep * num[half, rows, :] + add * o[:, lanes]
                    den[half, rows, :] = keep * den[half, rows, :] + add
                    top[half, rows, :] = new

    def blocks_of(r, way):
        if n_blocks == 1:
            block(r, way, 0)
        else:
            def several(j, carry):
                for k in range(BLOCKS_IN_FLIGHT):
                    block(r, way, BLOCKS_IN_FLIGHT * j + k)
                return carry
            lax.fori_loop(0, n_blocks // BLOCKS_IN_FLIGHT, several, 0)

    if dilation == 1:
        load_keys(0, 0)
        blocks_of(0, 0)
    else:
        def residues(j, carry):
            for way in range(ways):
                load_keys(ways * j + way, way)
            for way in range(ways):
                blocks_of(ways * j + way, way)
            return carry
        ways = min(PAD_WAYS, dilation, max(1, 2 * BLOCKS_IN_FLIGHT // n_blocks))
        lax.fori_loop(0, dilation // ways, residues, 0)


def _t5_bucket(rel):
    nb = N_BUCKETS // 2
    ret = jnp.where(rel > 0, nb, 0)
    n = jnp.abs(rel)
    max_exact = nb // 2
    nf = jnp.maximum(n, max_exact).astype(F32)
    large = max_exact + (jnp.log(nf / max_exact) / math.log(MAX_DISTANCE / max_exact)
                         * (nb - max_exact)).astype(I32)
    large = jnp.minimum(large, nb - 1)
    return ret + jnp.where(n < max_exact, n, large)


def _band_bias(rel_bias, group, dilation):
    qi = jnp.arange(Q_BLOCK)[:, None]
    ki = jnp.arange(K_BLOCK)[None, :]
    bucket = _t5_bucket((ki - N_SIDE - qi) * dilation)
    tab = rel_bias[:, group * HEADS_PER_GROUP:(group + 1) * HEADS_PER_GROUP]
    onehot = (bucket[:, :, None] == jnp.arange(N_BUCKETS)[None, None, :]).astype(F32)
    bias = jnp.einsum('qkb,bh->hqk', onehot, tab, precision=lax.Precision.HIGHEST).astype(F32)
    bias = bias * LOG2_E
    band = jnp.abs(ki - N_SIDE - qi) <= N_SIDE
    variants = []
    for edge in range(4):
        ok = band
        if edge & 1:
            ok = ok & (ki >= N_SIDE)
        if edge & 2:
            ok = ok & (ki < Q_BLOCK + N_SIDE)
        variants.append(jnp.where(ok[None], bias, NEG_INF))
    return jnp.stack(variants)


def _attention(qkv, rel_bias):
    biases = [_band_bias(rel_bias, g, d) for g, (_, d) in enumerate(DILATION_PATTERNS)]
    slab = pltpu.VMEM((GROUP_WIDTH // LANES, SEQ, LANES), F32)
    pads = pltpu.VMEM((PAD_WAYS, 2, SEQ + 2 * N_SIDE, GROUP_WIDTH), BF16)
    out = pl.pallas_call(
        _attn_kernel,
        grid=(BATCH,),
        in_specs=[pl.BlockSpec((1,) + a.shape[1:], lambda b: (b, 0, 0, 0)) for a in qkv]
                 + [_full_spec(b) for b in biases],
        out_specs=pl.BlockSpec((1, SEQ, GROUP_WIDTH), lambda b: (b, 0, 0)),
        out_shape=jax.ShapeDtypeStruct((BATCH, SEQ, GROUP_WIDTH), BF16),
        scratch_shapes=[pads, slab, slab, slab],
        compiler_params=_cparams("parallel"),
        name="attn",
    )(*qkv, *biases)
    return out.reshape(TOKENS, GROUP_WIDTH)


def _mixout_kernel(h_ref, attn_ref, u_ref, uprev_ref, unext_ref, gate_ref,
                   pw_ref, ps_ref, wpa_ref, wpp_ref, wo_ref, gffn_ref, wr_ref,
                   out_ref, xn_ref, ri_ref, rw_ref, cnt_ref, runs):
    j = pl.program_id(0) % TILES_PER_SEQ
    y_attn = _dot(attn_ref[...], wpa_ref[...])

    u = u_ref[...]
    prev = jnp.where(j == 0, 0.0, uprev_ref[0])
    nxt = jnp.where(j == TILES_PER_SEQ - 1, 0.0, unext_ref[0])
    ext_rows = ROW_TILE + 2 * POOL_HALO
    runs[:, ext_rows:ext_rows + POOL_HALO, :] = jnp.zeros((2, POOL_HALO, POOL_GROUP_WIDTH), F32)
    pos = j * ROW_TILE + lax.broadcasted_iota(I32, (ROW_TILE, 1), 0)
    mixed = []
    for gi, w in enumerate(POOL_WINDOWS):
        half = w // 2
        sl = slice(gi * POOL_GROUP_WIDTH, (gi + 1) * POOL_GROUP_WIDTH)
        runs[0, 0:POOL_HALO, :] = prev[:, sl]
        runs[0, POOL_HALO:POOL_HALO + ROW_TILE, :] = u[:, sl]
        runs[0, POOL_HALO + ROW_TILE:ext_rows, :] = nxt[:, sl]
        src, span = 0, 1
        while 2 * span < w:
            runs[1 - src, 0:ext_rows, :] = runs[src, 0:ext_rows, :] + runs[src, span:span + ext_rows, :]
            src, span = 1 - src, 2 * span
        lo = POOL_HALO - half
        acc = runs[src, lo:lo + ROW_TILE, :] + runs[src, lo + half:lo + half + ROW_TILE, :]
        cnt = (jnp.minimum(pos + half, SEQ) - jnp.maximum(pos - half, 0)).astype(F32)
        pooled = acc / cnt - u[:, sl]
        mixed.append(_dot(pooled.astype(BF16), pw_ref[gi]) * ps_ref[:, sl])
    y_pool = _dot(jnp.concatenate(mixed, axis=1).astype(BF16), wpp_ref[...])

    y = (gate_ref[:, 0:D_MODEL] * y_attn.astype(BF16)
         + gate_ref[:, D_MODEL:GATE_WIDTH] * y_pool.astype(BF16))
    h = h_ref[...] + _dot(y, wo_ref[...])
    out_ref[...] = h
    _route_rows(h, gffn_ref, wr_ref, xn_ref, ri_ref, rw_ref, cnt_ref)


def _mix_out(h, attn, u, gates, pool_w, pool_scale, w_proj_attn, w_proj_pool, w_out, g_ffn, wr_split):
    halo_blocks = ROW_TILE // POOL_HALO
    u3 = u.reshape(TOKENS // POOL_HALO, POOL_HALO, POOL_WIDTH)
    last = TOKENS // POOL_HALO - 1
    prev_spec = pl.BlockSpec((1, POOL_HALO, POOL_WIDTH),
                             lambda i: (jnp.maximum(i * halo_blocks - 1, 0), 0, 0))
    next_spec = pl.BlockSpec((1, POOL_HALO, POOL_WIDTH),
                             lambda i: (jnp.minimum((i + 1) * halo_blocks, last), 0, 0))
    weights = (pool_w, pool_scale, w_proj_attn, w_proj_pool, w_out, g_ffn, wr_split)
    return pl.pallas_call(
        _mixout_kernel,
        grid=(N_TILES,),
        in_specs=[_row_spec(D_MODEL), _row_spec(GROUP_WIDTH),
                  _row_spec(POOL_WIDTH), prev_spec, next_spec, _row_spec(GATE_WIDTH)]
                 + [_full_spec(w) for w in weights],
        out_specs=[_row_spec(D_MODEL), _row_spec(D_MODEL), _row_spec(ROUTER_LANES),
                   _row_spec(ROUTER_LANES),
                   pl.BlockSpec((1, SUBLANES, ROUTER_LANES), lambda i: (i, 0, 0))],
        out_shape=[jax.ShapeDtypeStruct((TOKENS, D_MODEL), F32),
                   jax.ShapeDtypeStruct((TOKENS, D_MODEL), BF16),
                   jax.ShapeDtypeStruct((TOKENS, ROUTER_LANES), I32),
                   jax.ShapeDtypeStruct((TOKENS, ROUTER_LANES), F32),
                   jax.ShapeDtypeStruct((N_TILES, SUBLANES, ROUTER_LANES), F32)],
        scratch_shapes=[pltpu.VMEM((2, ROW_TILE + 3 * POOL_HALO, POOL_GROUP_WIDTH), F32)],
        compiler_params=_cparams("parallel"),
        name="mixout",
    )(h, attn, u, u3, u3, gates, *weights)


def _route_rows(h, g_ref, wr_ref, xn_ref, ri_ref, rw_ref, cnt_ref):
    xn = _rms(h, g_ref[...])
    hi = xn.astype(BF16)
    xn_ref[...] = hi
    lo = (xn - hi.astype(F32)).astype(BF16)
    both = _dot(hi, wr_ref[...])
    lg = both[:, 0:ROUTER_LANES] + (both[:, ROUTER_LANES:] + _dot(lo, wr_ref[:, 0:ROUTER_LANES]))

    lane = lax.broadcasted_iota(I32, (ROW_TILE, ROUTER_LANES), 1)
    lanef = lane.astype(F32)
    low = jnp.float32(-3.0e38)
    far = jnp.float32(ROUTER_LANES)
    first = lambda hit: jnp.min(jnp.where(hit, lanef, far), axis=-1, keepdims=True)

    is_group = lane < N_EXPERT_GROUPS
    gl = jnp.where(is_group, lg, low)
    gmax = jnp.max(gl, axis=-1, keepdims=True)
    gidx = first(gl == gmax).astype(I32)
    gden = jnp.sum(jnp.where(is_group, jnp.exp(gl - gmax), 0.0), axis=-1, keepdims=True)
    g_p = 1.0 / gden

    in_group = ((lane >= EXPERT_LANE0) & (lane < EXPERT_LANE0 + N_EXPERTS)
                & ((lane - EXPERT_LANE0) // EXPERTS_PER_GROUP == gidx))
    el = jnp.where(in_group, lg, low)
    t1 = jnp.max(el, axis=-1, keepdims=True)
    l1 = first(in_group & (el == t1))
    rest = in_group & (lanef != l1)
    el2 = jnp.where(rest, lg, low)
    t2 = jnp.max(el2, axis=-1, keepdims=True)
    l2 = first(rest & (el2 == t2))
    e2 = jnp.exp(t2 - t1)
    w1 = g_p * (1.0 / (1.0 + e2))
    w2 = g_p * (e2 / (1.0 + e2))

    hit1 = lanef == l1
    hit2 = lanef == l2
    onehot = (hit1 | hit2).astype(BF16)
    ri = lax.broadcasted_iota(I32, (ROW_TILE, ROW_TILE), 0)
    ci = lax.broadcasted_iota(I32, (ROW_TILE, ROW_TILE), 1)
    before = (ci < ri).astype(BF16)
    seen = _dot(before, onehot)
    r1 = jnp.sum(jnp.where(hit1, seen, 0.0), axis=-1, keepdims=True)
    r2 = jnp.sum(jnp.where(hit2, seen, 0.0), axis=-1, keepdims=True)

    packed = jnp.zeros((ROW_TILE, ROUTER_LANES), F32)
    for k, v in enumerate((l1, l2, r1, r2)):
        packed = jnp.where(lane == k, v, packed)
    ri_ref[...] = packed.astype(I32)
    rw_ref[...] = jnp.where(lane == 0, w1, jnp.where(lane == 1, w2, 0.0))
    counts = jnp.sum(onehot.astype(F32), axis=0, keepdims=True)
    cnt_ref[0] = jnp.broadcast_to(counts, (SUBLANES, ROUTER_LANES))


def _slots(ri_ref, off_ref):
    ri = ri_ref[...]
    lane = lax.broadcasted_iota(I32, (ROW_TILE, ROUTER_LANES), 1)
    off = off_ref[0, 0:1, :]
    pick = lambda k: jnp.sum(jnp.where(lane == ri[:, k:k + 1], off, 0.0), axis=-1, keepdims=True)
    return (pick(0) + ri[:, 2:3].astype(F32), pick(1) + ri[:, 3:4].astype(F32))


def _as_rows(cols):
    eye = (lax.broadcasted_iota(I32, (ROW_TILE, ROW_TILE), 0)
           == lax.broadcasted_iota(I32, (ROW_TILE, ROW_TILE), 1))
    return [jnp.sum(jnp.where(eye, c, 0.0), axis=0, keepdims=True) for c in cols]


def _run_copy(tile, e, loff_s, c8_s, dst_s, buf_ref, slot, hbm_ref, sem, to_hbm):
    k = tile * N_EXPERTS + e
    n = pl.multiple_of(c8_s[k], RUN_ALIGN)
    vm = buf_ref.at[slot, pl.ds(pl.multiple_of(loff_s[k], RUN_ALIGN), n)]
    hb = hbm_ref.at[pl.ds(pl.multiple_of(dst_s[k], RUN_ALIGN), n)]
    src, dst = (vm, hb) if to_hbm else (hb, vm)
    return n, pltpu.make_async_copy(src, dst, sem.at[slot])


def _loop(lo, hi, fn, unroll=1):
    def body(e, carry):
        fn(e)
        return carry
    lax.fori_loop(lo, hi, body, 0, unroll=unroll)


def _wait_rows(n, buf_ref, slot, hbm_ref, sem, to_hbm):
    vm = buf_ref.at[slot, pl.ds(0, n)]
    hb = hbm_ref.at[pl.ds(0, n)]
    src, dst = (vm, hb) if to_hbm else (hb, vm)
    pltpu.make_async_copy(src, dst, sem.at[slot]).wait()


def _start(n, cp):
    @pl.when(n > 0)
    def _():
        cp.start()


def _wait(n, cp):
    @pl.when(n > 0)
    def _():
        cp.wait()


def _dispatch_kernel(loff_s, c8_s, dst_s, nslot_s, zdst_s, zcnt_s, nact_s,
                     xn_ref, ri_ref, rw_ref, off_ref, xpad_hbm, sorted_buf, zero_buf, sem, zsem):
    i = pl.program_id(0)
    slot = i % 2
    last = pl.num_programs(0) - 1

    def zero_copy(e):
        n = pl.multiple_of(zcnt_s[e], RUN_ALIGN)
        dst = xpad_hbm.at[pl.ds(pl.multiple_of(zdst_s[e], RUN_ALIGN), n)]
        return n, pltpu.make_async_copy(zero_buf.at[pl.ds(0, n)], dst, zsem)

    def tail_copy(b):
        dst = xpad_hbm.at[pl.ds(pl.multiple_of(b * MOE_BLOCK, MOE_BLOCK), MOE_BLOCK)]
        return pltpu.make_async_copy(zero_buf, dst, zsem)

    @pl.when(i == 0)
    def _():
        zero_buf[...] = jnp.zeros_like(zero_buf)
        _loop(0, N_EXPERTS, lambda e: _start(*zero_copy(e)))
        _loop(nact_s[0], N_MOE_BLOCKS, lambda b: tail_copy(b).start())

    s1, s2 = _slots(ri_ref, off_ref)
    rw = rw_ref[...]
    s1_row, s2_row, w1_row, w2_row = _as_rows([s1, s2, rw[:, 0:1], rw[:, 1:2]])
    xn = xn_ref[...]
    for first, size in SLOT_CHUNKS:
        @pl.when(first < nslot_s[i])
        def _():
            rows = slice(first, first + size)
            lane = lax.broadcasted_iota(I32, (size, LANES), 1)
            srow = (first + lax.broadcasted_iota(I32, (size, ROW_TILE), 0)).astype(F32)
            hit1 = srow == s1_row
            hit2 = srow == s2_row
            xs = _dot((hit1 | hit2).astype(BF16), xn)
            sorted_buf[slot, rows, 0:HALF] = _pack_halves(xs[:, 0:HALF], xs[:, HALF:D_MODEL])
            ws = jnp.sum(jnp.where(hit1, w1_row, 0.0) + jnp.where(hit2, w2_row, 0.0),
                         axis=-1, keepdims=True)
            sorted_buf[slot, rows, HALF:X_WORDS] = jnp.where(lane == 0, _bits(ws), jnp.uint32(0))

    copy = lambda tile, sl: (lambda e: _run_copy(tile, e, loff_s, c8_s, dst_s, sorted_buf, sl,
                                                 xpad_hbm, sem, True))
    mine = copy(i, slot)
    _loop(0, N_EXPERTS, lambda e: _start(*mine(e)), unroll=RUN_UNROLL)
    tile_rows = lambda t: pl.multiple_of(nslot_s[t], RUN_ALIGN)

    @pl.when(i > 0)
    def _():
        _wait_rows(tile_rows(i - 1), sorted_buf, 1 - slot, xpad_hbm, sem, True)

    @pl.when(i == last)
    def _():
        _wait_rows(tile_rows(i), sorted_buf, slot, xpad_hbm, sem, True)
        _loop(0, N_EXPERTS, lambda e: _wait(*zero_copy(e)))
        _loop(nact_s[0], N_MOE_BLOCKS, lambda b: tail_copy(b).wait())


def _dispatch(tables, xn, route_i, route_w):
    tile_row = pl.BlockSpec((1, SUBLANES, ROUTER_LANES), lambda i, *_: (i, 0, 0))
    grid_spec = pltpu.PrefetchScalarGridSpec(
        num_scalar_prefetch=7,
        grid=(N_TILES,),
        in_specs=[_row_spec(D_MODEL), _row_spec(ROUTER_LANES), _row_spec(ROUTER_LANES), tile_row],
        out_specs=pl.BlockSpec(memory_space=pl.ANY),
        scratch_shapes=[pltpu.VMEM((2, MAX_SLOTS, X_WORDS), U32),
                        pltpu.VMEM((MOE_BLOCK, X_WORDS), U32),
                        pltpu.SemaphoreType.DMA((2,)),
                        pltpu.SemaphoreType.DMA(())],
    )
    return pl.pallas_call(
        _dispatch_kernel,
        grid_spec=grid_spec,
        out_shape=jax.ShapeDtypeStruct((PAD_ROWS, X_WORDS), U32),
        compiler_params=_cparams("arbitrary"),
        name="dispatch",
    )(tables["loff"], tables["c8"], tables["dst"], tables["nslot"], tables["zdst"], tables["zcnt"],
      tables["n_active"], xn, route_i, route_w, tables["offrow"])


def _combine_kernel(loff_s, c8_s, dst_s, nslot_s,
                    h_ref, ri_ref, off_ref, g_ref, ypad_hbm, out_ref,
                    ybuf, sem, *, final_norm):
    i = pl.program_id(0)
    slot = i % 2
    last = pl.num_programs(0) - 1
    fetch = lambda tile, sl: (lambda e: _run_copy(tile, e, loff_s, c8_s, dst_s, ybuf, sl,
                                                  ypad_hbm, sem, False))

    @pl.when(i == 0)
    def _():
        ybuf[...] = jnp.zeros_like(ybuf)
        first = fetch(i, slot)
        _loop(0, N_EXPERTS, lambda e: _start(*first(e)), unroll=RUN_UNROLL)

    @pl.when(i < last)
    def _():
        nxt = fetch(i + 1, 1 - slot)
        _loop(0, N_EXPERTS, lambda e: _start(*nxt(e)), unroll=RUN_UNROLL)

    _wait_rows(pl.multiple_of(nslot_s[i], RUN_ALIGN), ybuf, slot, ypad_hbm, sem, False)

    s1, s2 = _slots(ri_ref, off_ref)

    def gathered(first, size):
        scol = (first + lax.broadcasted_iota(I32, (ROW_TILE, size), 1)).astype(F32)
        pick = ((scol == s1) | (scol == s2)).astype(BF16)
        y_hi, y_lo = _unpack_halves(ybuf[slot, first:first + size, :])
        return jnp.concatenate([_dot(pick, y_hi), _dot(pick, y_lo)], axis=1)

    finish = (lambda v: _rms(v, g_ref[...])) if final_norm else (lambda v: v)
    h = h_ref[...] + gathered(0, COMBINE_BASE)
    spill = nslot_s[i] > COMBINE_BASE

    @pl.when(jnp.logical_not(spill))
    def _():
        out_ref[...] = finish(h)

    @pl.when(spill)
    def _():
        out_ref[...] = finish(h + gathered(COMBINE_BASE, MAX_SLOTS - COMBINE_BASE))


def _combine(tables, h, route_i, y_pad, g, final_norm):
    tile_row = pl.BlockSpec((1, SUBLANES, ROUTER_LANES), lambda i, *_: (i, 0, 0))
    grid_spec = pltpu.PrefetchScalarGridSpec(
        num_scalar_prefetch=4,
        grid=(N_TILES,),
        in_specs=[_row_spec(D_MODEL), _row_spec(ROUTER_LANES), tile_row, _full_spec(g),
                  pl.BlockSpec(memory_space=pl.ANY)],
        out_specs=_row_spec(D_MODEL),
        scratch_shapes=[pltpu.VMEM((2, MAX_SLOTS, HALF), U32),
                        pltpu.SemaphoreType.DMA((2,))],
    )
    return pl.pallas_call(
        functools.partial(_combine_kernel, final_norm=final_norm),
        grid_spec=grid_spec,
        out_shape=jax.ShapeDtypeStruct((TOKENS, D_MODEL), F32),
        compiler_params=_cparams("arbitrary"),
        name="combine",
    )(tables["loff"], tables["c8"], tables["dst"], tables["nslot"],
      h, route_i, tables["offrow"], g, y_pad)


def _expert_kernel(be_ref, slot_ref, next_ref, nact_ref, x_ref, wg_hbm, wu_hbm, wd_hbm, y_hbm,
                   wg_f, wu_f, wg_b, wu_b, wd_s, ybuf, zero_buf, sem, ysem, zsem, *, layer):
    i = pl.program_id(0)
    n_active = nact_ref[0]
    staged = ((wg_hbm, wg_f, wg_b), (wu_hbm, wu_f, wu_b))

    def fetch(e, slot):
        copies = [pltpu.make_async_copy(hbm.at[layer, e], f32.at[slot], sem.at[slot, k])
                  for k, (hbm, f32, _) in enumerate(staged)]
        return copies + [pltpu.make_async_copy(wd_hbm.at[e], wd_s.at[slot], sem.at[slot, len(staged)])]

    def block_rows(b):
        return y_hbm.at[pl.ds(pl.multiple_of(b * MOE_BLOCK, MOE_BLOCK), MOE_BLOCK)]

    def put(b):
        return pltpu.make_async_copy(ybuf.at[b % 2], block_rows(b), ysem.at[b % 2])

    def tail_copy(b):
        return pltpu.make_async_copy(zero_buf, block_rows(b), zsem)

    @pl.when(i == 0)
    def _():
        zero_buf[...] = jnp.zeros_like(zero_buf)
        _loop(n_active, N_MOE_BLOCKS, lambda b: tail_copy(b).start())

    @pl.when(i == pl.num_programs(0) - 1)
    def _():
        _loop(n_active, N_MOE_BLOCKS, lambda b: tail_copy(b).wait())

    @pl.when(i < n_active)
    def _():
        e = be_ref[i]
        slot = slot_ref[i]

        def request(ahead, block):
            nxt = next_ref[(ahead - 1) * N_MOE_BLOCKS + block]

            @pl.when(nxt >= 0)
            def _():
                for cp in fetch(nxt, (slot + ahead) % W_STAGES):
                    cp.start()

        @pl.when((i == 0) | (e != be_ref[jnp.maximum(i - 1, 0)]))
        def _():
            @pl.when(i == 0)
            def _():
                for cp in fetch(e, slot):
                    cp.start()
                for ahead in range(1, W_STAGES - 1):
                    request(ahead, i)
            for cp in fetch(e, slot):
                cp.wait()
            request(W_STAGES - 1, i)
            for _, f32, b16 in staged:
                b16[...] = f32[slot].astype(BF16)

        x_hi, x_lo = _unpack_halves(x_ref[:, 0:HALF])
        row_w = lax.bitcast_convert_type(x_ref[:, HALF:HALF + 1], F32)
        gate = _dot(x_hi, wg_b[0:HALF, :]) + _dot(x_lo, wg_b[HALF:D_MODEL, :])
        up = _dot(x_hi, wu_b[0:HALF, :]) + _dot(x_lo, wu_b[HALF:D_MODEL, :])
        hmid = (jax.nn.silu(gate) * up).astype(BF16)
        y = (_dot(hmid, wd_s[slot]) * row_w).astype(BF16).astype(F32)
        ybuf[i % 2] = _pack_halves(y[:, 0:HALF], y[:, HALF:D_MODEL])
        put(i).start()

        @pl.when(i > 0)
        def _():
            put(i - 1).wait()

        @pl.when(i == n_active - 1)
        def _():
            put(i).wait()


def _experts(layer, tables, x_pad, w_gate, w_up, w_down):
    hbm = pl.BlockSpec(memory_space=pl.ANY)
    up_shape, down_shape = (D_MODEL, D_EXPERT), (D_EXPERT, D_MODEL)
    grid_spec = pltpu.PrefetchScalarGridSpec(
        num_scalar_prefetch=4,
        grid=(N_MOE_BLOCKS,),
        in_specs=[pl.BlockSpec((MOE_BLOCK, X_WORDS),
                               lambda i, be, sl, nx, na: (jnp.minimum(i, na[0] - 1), 0)),
                  hbm, hbm, hbm],
        out_specs=hbm,
        scratch_shapes=[pltpu.VMEM((W_STAGES,) + up_shape, F32),
                        pltpu.VMEM((W_STAGES,) + up_shape, F32),
                        pltpu.VMEM(up_shape, BF16), pltpu.VMEM(up_shape, BF16),
                        pltpu.VMEM((W_STAGES,) + down_shape, BF16),
                        pltpu.VMEM((2, MOE_BLOCK, HALF), U32),
                        pltpu.VMEM((MOE_BLOCK, HALF), U32),
                        pltpu.SemaphoreType.DMA((W_STAGES, 3)),
                        pltpu.SemaphoreType.DMA((2,)),
                        pltpu.SemaphoreType.DMA(())],
    )
    return pl.pallas_call(
        functools.partial(_expert_kernel, layer=layer),
        grid_spec=grid_spec,
        out_shape=jax.ShapeDtypeStruct((PAD_ROWS, HALF), U32),
        compiler_params=_cparams("arbitrary"),
        name="experts",
    )(tables["blk_expert"], tables["blk_slot"], tables["blk_next"], tables["n_active"],
      x_pad, w_gate, w_up, w_down)


def _routing_tables(tile_counts):
    cnt = tile_counts[:, 0, EXPERT_LANE0:EXPERT_LANE0 + N_EXPERTS].astype(I32)
    c8 = (cnt + RUN_ALIGN - 1) // RUN_ALIGN * RUN_ALIGN
    loff = jnp.cumsum(c8, axis=1) - c8
    nslot = jnp.sum(c8, axis=1)
    tot = jnp.sum(c8, axis=0)
    padded = (tot + MOE_BLOCK - 1) // MOE_BLOCK * MOE_BLOCK
    end = jnp.cumsum(padded)
    base = end - padded
    dst = base[None, :] + jnp.cumsum(c8, axis=0) - c8
    blk_start = jnp.arange(N_MOE_BLOCKS, dtype=I32) * MOE_BLOCK
    blk_expert = jnp.minimum(jnp.sum((end[None, :] <= blk_start[:, None]).astype(I32), axis=1),
                             N_EXPERTS - 1).astype(I32)
    offrow = jnp.zeros((N_TILES, ROUTER_LANES), F32).at[:, EXPERT_LANE0:EXPERT_LANE0 + N_EXPERTS].set(
        loff.astype(F32))
    offrow = jnp.broadcast_to(offrow[:, None, :], (N_TILES, SUBLANES, ROUTER_LANES))
    experts = jnp.arange(N_EXPERTS, dtype=I32)
    present = padded > 0
    ordinal = jnp.cumsum(present.astype(I32)) - 1
    at_or_after = lax.cummin(jnp.where(present, experts, N_EXPERTS), reverse=True)
    after = jnp.concatenate([at_or_after[1:], jnp.full((1,), N_EXPERTS, I32)])
    hop = jnp.concatenate([after, jnp.full((1,), N_EXPERTS, I32)])
    pick = (blk_expert[:, None] == experts[None, :]).astype(I32)
    blk_slot = jnp.sum(pick * (ordinal % W_STAGES)[None, :], axis=1).astype(I32)
    ahead, blk_next = experts, []
    for _ in range(W_STAGES - 1):
        ahead = jnp.sum((ahead[:, None] == jnp.arange(N_EXPERTS + 1)[None, :]) * hop[None, :], axis=1)
        per_block = jnp.sum(pick * ahead[None, :], axis=1)
        blk_next.append(jnp.where(per_block >= N_EXPERTS, -1, per_block))
    blk_next = jnp.concatenate(blk_next).astype(I32)
    return {
        "blk_slot": blk_slot, "blk_next": blk_next,
        "loff": loff.reshape(-1).astype(I32), "c8": c8.reshape(-1).astype(I32),
        "dst": dst.reshape(-1).astype(I32), "nslot": nslot.astype(I32),
        "zdst": (base + tot).astype(I32), "zcnt": (padded - tot).astype(I32),
        "offrow": offrow, "blk_expert": blk_expert,
        "n_active": (end[-1:] // MOE_BLOCK).astype(I32),
    }


def _split_router_weights(w_router_group, w_router_expert):
    w_e = w_router_expert.transpose(1, 0, 2).reshape(D_MODEL, N_EXPERTS)
    w = jnp.concatenate([w_router_group, w_e], axis=1)
    w = jnp.pad(w, ((0, 0), (0, ROUTER_LANES - w.shape[1])))
    hi = w.astype(BF16)
    lo = (w - hi.astype(F32)).astype(BF16)
    return jnp.concatenate([hi, lo], axis=1)


def _input_weights(w):
    cols = []
    for g in range(N_GROUPS):
        for part in range(3):
            c0 = part * ATTN_WIDTH + g * GROUP_WIDTH
            cols.append(w[:, c0:c0 + GROUP_WIDTH])
    cols.append(w[:, QKV_WIDTH:])
    return jnp.concatenate(cols, axis=1).astype(BF16)


def kernel(x, rel_bias, norm_mix_g, w_in, pool_w, pool_scale, w_proj_attn, w_proj_pool, w_out,
           norm_ffn_g, w_router_group, w_router_expert, w_gate_e, w_up_e, w_down_e, norm_final_g):
    h = x.reshape(TOKENS, D_MODEL)
    for l in range(DEPTH):
        qkv, u, gates, w_down_b = _project(l, h, norm_mix_g[l][None], _input_weights(w_in[l]), w_down_e)
        h, xn, route_i, route_w, tile_counts = _mix_out(
            h, _attention(qkv, rel_bias), u, gates, pool_w[l].astype(BF16), pool_scale[l][None],
            w_proj_attn[l].astype(BF16), w_proj_pool[l].astype(BF16), w_out[l].astype(BF16),
            norm_ffn_g[l][None], _split_router_weights(w_router_group[l], w_router_expert[l]))
        tables = _routing_tables(tile_counts)
        x_pad = _dispatch(tables, xn, route_i, route_w)
        y_pad = _experts(l, tables, x_pad, w_gate_e, w_up_e, w_down_b)
        h = _combine(tables, h, route_i, y_pad, norm_final_g[None], l == DEPTH - 1)
    return h.reshape(BATCH, SEQ, D_MODEL)
```

```python
import functools
import math

import jax
import jax.numpy as jnp
from jax import lax
from jax.experimental import pallas as pl
from jax.experimental.pallas import tpu as pltpu

F32 = jnp.float32
BF16 = jnp.bfloat16
I32 = jnp.int32
U32 = jnp.uint32

D_MODEL = 1024
BATCH = 8
SEQ = 2048
TOKENS = BATCH * SEQ
DEPTH = 2

HEAD_DIM = 64
HEADS_PER_GROUP = 4
GROUP_WIDTH = HEADS_PER_GROUP * HEAD_DIM
DILATION_PATTERNS = ((128, 1), (512, 4), (2048, 16))
N_GROUPS = len(DILATION_PATTERNS)
N_ATTN_HEADS = N_GROUPS * HEADS_PER_GROUP
ATTN_WIDTH = N_ATTN_HEADS * HEAD_DIM
QKV_WIDTH = 3 * ATTN_WIDTH
GROUP_QKV = 3 * GROUP_WIDTH
N_SIDE = 64
assert all(w // (2 * d) == N_SIDE for w, d in DILATION_PATTERNS)
POOL_WINDOWS = (2, 4, 8, 16)
POOL_GROUP_WIDTH = 128
POOL_WIDTH = len(POOL_WINDOWS) * POOL_GROUP_WIDTH
POOL_HALO = max(POOL_WINDOWS) // 2
N_BRANCHES = 2
GATE_WIDTH = N_BRANCHES * D_MODEL
IN_WIDTH = QKV_WIDTH + POOL_WIDTH + GATE_WIDTH
N_BUCKETS = 32
MAX_DISTANCE = 1024
N_EXPERT_GROUPS = 8
EXPERTS_PER_GROUP = 8
N_EXPERTS = N_EXPERT_GROUPS * EXPERTS_PER_GROUP
TOP_K = 2
D_EXPERT = 512
N_ASSIGN = TOKENS * TOP_K
EPS = 1e-6
NEG_INF = -1e30
LOG2_E = math.log2(math.e)
Q_SCALE = HEAD_DIM ** -0.5 * LOG2_E

LANES = 128
SUBLANES = 8
ROW_TILE = 512
N_TILES = TOKENS // ROW_TILE
TILES_PER_SEQ = SEQ // ROW_TILE
Q_BLOCK = 128
K_BLOCK = Q_BLOCK + 2 * N_SIDE
PAD_WAYS = 4
BLOCKS_IN_FLIGHT = 4
ROUTER_LANES = 128
EXPERT_LANE0 = N_EXPERT_GROUPS
VMEM_LIMIT = 56 * 1024 * 1024

RUN_ALIGN = SUBLANES
MOE_BLOCK = 336
HALF = D_MODEL // 2
X_WORDS = HALF + LANES
SLOT_CHUNK = 256
RUN_UNROLL = 4
W_STAGES = 2
MAX_SLOTS = -(-(TOP_K * ROW_TILE + N_EXPERTS * (RUN_ALIGN - 1)) // SLOT_CHUNK) * SLOT_CHUNK
SLOT_CHUNKS = (tuple((s, ROW_TILE) for s in range(0, TOP_K * ROW_TILE, ROW_TILE))
               + tuple((s, SLOT_CHUNK) for s in range(TOP_K * ROW_TILE, MAX_SLOTS, SLOT_CHUNK)))
COMBINE_BASE = MAX_SLOTS - SLOT_CHUNK
N_MOE_BLOCKS = -(-(N_ASSIGN + N_TILES * N_EXPERTS * (RUN_ALIGN - 1)
                   + N_EXPERTS * (MOE_BLOCK - RUN_ALIGN)) // MOE_BLOCK)
PAD_ROWS = N_MOE_BLOCKS * MOE_BLOCK
HIGH_HALF = 0xFFFF0000


def _cparams(*sem):
    return pltpu.CompilerParams(dimension_semantics=sem, vmem_limit_bytes=VMEM_LIMIT)


def _rms(h, g):
    r = lax.rsqrt(jnp.mean(h * h, axis=-1, keepdims=True) + EPS)
    return (h * r) * g


def _dot(a, b):
    return jnp.dot(a, b, preferred_element_type=F32)


def _row_spec(width):
    return pl.BlockSpec((ROW_TILE, width), lambda i, *_: (i, 0))


def _full_spec(a):
    return pl.BlockSpec(a.shape, lambda i, *_: (0,) * a.ndim)


def _bits(x):
    return lax.bitcast_convert_type(x, U32)


def _pack_halves(a, b):
    return (_bits(a) & jnp.uint32(HIGH_HALF)) | (_bits(b) >> 16)


def _unpack_halves(words):
    hi = lax.bitcast_convert_type(words & jnp.uint32(HIGH_HALF), F32)
    lo = lax.bitcast_convert_type(words << 16, F32)
    return hi.astype(BF16), lo.astype(BF16)


def _proj_kernel(h_ref, g_ref, w_ref, wd_ref, q0_ref, q1_ref, q2_ref, u_ref, gate_ref, wdb_ref, slabs):
    wdb_ref[...] = wd_ref[...].astype(BF16)
    xn = _rms(h_ref[...], g_ref[...]).astype(BF16)
    n_slabs = GROUP_QKV // LANES
    for g, out_ref in enumerate((q0_ref, q1_ref, q2_ref)):
        dilation = DILATION_PATTERNS[g][1]
        res = _dot(xn, w_ref[:, g * GROUP_QKV:(g + 1) * GROUP_QKV])
        res = jnp.concatenate([res[:, 0:GROUP_WIDTH] * Q_SCALE, res[:, GROUP_WIDTH:]], axis=1)
        if dilation == 1:
            out_ref[0, 0] = res.astype(BF16)
            continue
        for s in range(n_slabs):
            slabs[s] = res[:, s * LANES:(s + 1) * LANES]
        n = ROW_TILE // dilation
        for r in range(dilation):
            rows = [slabs[s, pl.ds(r, n, stride=dilation), :] for s in range(n_slabs)]
            out_ref[0, r] = jnp.concatenate(rows, axis=1).astype(BF16)
    u_ref[...] = _dot(xn, w_ref[:, QKV_WIDTH:QKV_WIDTH + POOL_WIDTH])
    gates = _dot(xn, w_ref[:, QKV_WIDTH + POOL_WIDTH:IN_WIDTH])
    gate_ref[...] = jax.nn.sigmoid(gates).astype(BF16)


def _project(layer, h, g, w_bf16, w_down):
    qkv_shapes, qkv_specs = [], []
    for _, d in DILATION_PATTERNS:
        qkv_shapes.append(jax.ShapeDtypeStruct((BATCH, d, SEQ // d, GROUP_QKV), BF16))
        qkv_specs.append(pl.BlockSpec((1, d, ROW_TILE // d, GROUP_QKV),
                                      lambda i: (i // TILES_PER_SEQ, 0, i % TILES_PER_SEQ, 0)))
    layer_rows = N_EXPERTS * D_EXPERT
    step_rows = layer_rows // N_TILES
    res = pl.pallas_call(
        _proj_kernel,
        grid=(N_TILES,),
        in_specs=[_row_spec(D_MODEL), _full_spec(g), _full_spec(w_bf16),
                  pl.BlockSpec((step_rows, D_MODEL), lambda i: (layer * N_TILES + i, 0))],
        out_specs=qkv_specs + [_row_spec(POOL_WIDTH), _row_spec(GATE_WIDTH),
                               pl.BlockSpec((step_rows, D_MODEL), lambda i: (i, 0))],
        out_shape=qkv_shapes + [jax.ShapeDtypeStruct((TOKENS, POOL_WIDTH), F32),
                                jax.ShapeDtypeStruct((TOKENS, GATE_WIDTH), BF16),
                                jax.ShapeDtypeStruct((layer_rows, D_MODEL), BF16)],
        scratch_shapes=[pltpu.VMEM((GROUP_QKV // LANES, ROW_TILE, LANES), F32)],
        compiler_params=_cparams("parallel"),
        name="proj",
    )(h, g, w_bf16, w_down.reshape(DEPTH * layer_rows, D_MODEL))
    w_down_b = res[N_GROUPS + 2].reshape(N_EXPERTS, D_EXPERT, D_MODEL)
    return res[:N_GROUPS], res[N_GROUPS], res[N_GROUPS + 1], w_down_b


def _attn_kernel(q0_ref, q1_ref, q2_ref, b0_ref, b1_ref, b2_ref, o_ref, pads, num, den, top):
    zpad = jnp.zeros((N_SIDE, GROUP_WIDTH), BF16)
    groups = ((q2_ref, b2_ref, 2), (q1_ref, b1_ref, 1), (q0_ref, b0_ref, 0))
    for order, (qkv_ref, bias_ref, g) in enumerate(groups):
        dilation = DILATION_PATTERNS[g][1]
        sub_len = SEQ // dilation
        for way in range(PAD_WAYS):
            for kv in range(2):
                pads[way, kv, 0:N_SIDE, :] = zpad
                pads[way, kv, N_SIDE + sub_len:2 * N_SIDE + sub_len, :] = zpad
        _attn_group(qkv_ref, bias_ref, pads, num, den, top, dilation=dilation, first=order == 0)

    def finish(c, carry):
        rows = pl.ds(pl.multiple_of(c * ROW_TILE, ROW_TILE), ROW_TILE)
        merged = [num[half, rows, :] / den[half, rows, :] for half in range(GROUP_WIDTH // LANES)]
        o_ref[0, rows, :] = jnp.concatenate(merged, axis=1).astype(BF16)
        return carry

    lax.fori_loop(0, SEQ // ROW_TILE, finish, 0)


def _attn_group(qkv_ref, bias_ref, pads, num, den, top, *, dilation, first):
    sub_len = SEQ // dilation
    n_blocks = sub_len // Q_BLOCK
    head_of_lane = lax.broadcasted_iota(I32, (1, GROUP_WIDTH), 1) // HEAD_DIM

    def by_head(cols):
        out = cols[HEADS_PER_GROUP - 1]
        for h in range(HEADS_PER_GROUP - 2, -1, -1):
            out = jnp.where(head_of_lane == h, cols[h], out)
        return out

    def load_keys(r, way):
        pads[way, 0, N_SIDE:N_SIDE + sub_len, :] = qkv_ref[0, r, :, GROUP_WIDTH:2 * GROUP_WIDTH]
        pads[way, 1, N_SIDE:N_SIDE + sub_len, :] = qkv_ref[0, r, :, 2 * GROUP_WIDTH:3 * GROUP_WIDTH]

    def block(r, way, i):
        static = isinstance(i, int)
        r0 = i * Q_BLOCK if static else pl.multiple_of(i * Q_BLOCK, Q_BLOCK)
        qb = qkv_ref[0, r, pl.ds(r0, Q_BLOCK), 0:GROUP_WIDTH]
        kw = pads[way, 0, pl.ds(r0, K_BLOCK), :]
        vw = pads[way, 1, pl.ds(r0, K_BLOCK), :]
        is_first, is_last = i == 0, i == n_blocks - 1
        edge = (int(is_first) + 2 * int(is_last) if static
                else is_first.astype(I32) + 2 * is_last.astype(I32))
        zero = jnp.zeros_like(qb)
        q_heads = jnp.concatenate(
            [jnp.where(head_of_lane == h, qb, zero) for h in range(HEADS_PER_GROUP)], axis=0)
        s = lax.dot_general(q_heads, kw, (((1,), (1,)), ((), ())), preferred_element_type=F32)
        s = s.reshape(HEADS_PER_GROUP, Q_BLOCK, K_BLOCK) + bias_ref[edge]
        m = jnp.max(s, axis=-1, keepdims=True)
        p = jnp.exp2(s - m)
        psum = jnp.sum(p, axis=-1, keepdims=True)
        pb = p.astype(BF16)
        p_cat = jnp.concatenate([pb[h] for h in range(HEADS_PER_GROUP)], axis=1)
        zv = jnp.zeros_like(vw)
        v_heads = jnp.concatenate(
            [jnp.where(head_of_lane == h, vw, zv) for h in range(HEADS_PER_GROUP)], axis=0)
        o = _dot(p_cat, v_heads)
        lse = m + jnp.log2(psum)
        o = o / by_head([psum[h] for h in range(HEADS_PER_GROUP)])
        lse_lanes = by_head([lse[h] for h in range(HEADS_PER_GROUP)])
        if dilation == 1:
            rows = pl.ds(r0, Q_BLOCK)
        else:
            rows = pl.ds(r + dilation * r0, Q_BLOCK, stride=dilation)
        for half in range(GROUP_WIDTH // LANES):
            lanes = slice(half * LANES, (half + 1) * LANES)
            if first:
                num[half, rows, :] = o[:, lanes]
                den[half, rows, :] = jnp.ones((Q_BLOCK, LANES), F32)
                top[half, rows, :] = lse_lanes[:, lanes]
            else:
                old = top[half, rows, :]
                new = jnp.maximum(old, lse_lanes[:, lanes])
                keep = jnp.exp2(old - new)
                add = jnp.exp2(lse_lanes[:, lanes] - new)
                num[half, rows, :] = keep * num[half, rows, :] + add * o[:, lanes]
                den[half, rows, :] = keep * den[half, rows, :] + add
                top[half, rows, :] = new

    def blocks_of(r, way):
        if n_blocks == 1:
            block(r, way, 0)
        else:
            assert n_blocks % BLOCKS_IN_FLIGHT == 0

            def several(j, carry):
                for k in range(BLOCKS_IN_FLIGHT):
                    block(r, way, BLOCKS_IN_FLIGHT * j + k)
                return carry
            lax.fori_loop(0, n_blocks // BLOCKS_IN_FLIGHT, several, 0)

    if dilation == 1:
        load_keys(0, 0)
        blocks_of(0, 0)
    else:
        ways = min(PAD_WAYS, dilation, max(1, 2 * BLOCKS_IN_FLIGHT // n_blocks))

        def residues(j, carry):
            for way in range(ways):
                load_keys(ways * j + way, way)
            for way in range(ways):
                blocks_of(ways * j + way, way)
            return carry
        lax.fori_loop(0, dilation // ways, residues, 0)


def _t5_bucket(rel):
    nb = N_BUCKETS // 2
    ret = jnp.where(rel > 0, nb, 0)
    n = jnp.abs(rel)
    max_exact = nb // 2
    nf = jnp.maximum(n, max_exact).astype(F32)
    large = max_exact + (jnp.log(nf / max_exact) / math.log(MAX_DISTANCE / max_exact)
                         * (nb - max_exact)).astype(I32)
    large = jnp.minimum(large, nb - 1)
    return ret + jnp.where(n < max_exact, n, large)


def _band_bias(rel_bias, group, dilation):
    qi = jnp.arange(Q_BLOCK)[:, None]
    ki = jnp.arange(K_BLOCK)[None, :]
    bucket = _t5_bucket((ki - N_SIDE - qi) * dilation)
    tab = rel_bias[:, group * HEADS_PER_GROUP:(group + 1) * HEADS_PER_GROUP]
    onehot = (bucket[:, :, None] == jnp.arange(N_BUCKETS)[None, None, :]).astype(F32)
    bias = jnp.einsum('qkb,bh->hqk', onehot, tab, precision=lax.Precision.HIGHEST).astype(F32)
    bias = bias * LOG2_E
    band = jnp.abs(ki - N_SIDE - qi) <= N_SIDE
    variants = []
    for edge in range(4):
        ok = band
        if edge & 1:
            ok = ok & (ki >= N_SIDE)
        if edge & 2:
            ok = ok & (ki < Q_BLOCK + N_SIDE)
        variants.append(jnp.where(ok[None], bias, NEG_INF))
    return jnp.stack(variants)


def _attention(qkv, rel_bias):
    biases = [_band_bias(rel_bias, g, d) for g, (_, d) in enumerate(DILATION_PATTERNS)]
    slab = pltpu.VMEM((GROUP_WIDTH // LANES, SEQ, LANES), F32)
    pads = pltpu.VMEM((PAD_WAYS, 2, SEQ + 2 * N_SIDE, GROUP_WIDTH), BF16)
    out = pl.pallas_call(
        _attn_kernel,
        grid=(BATCH,),
        in_specs=[pl.BlockSpec((1,) + a.shape[1:], lambda b: (b, 0, 0, 0)) for a in qkv]
                 + [_full_spec(b) for b in biases],
        out_specs=pl.BlockSpec((1, SEQ, GROUP_WIDTH), lambda b: (b, 0, 0)),
        out_shape=jax.ShapeDtypeStruct((BATCH, SEQ, GROUP_WIDTH), BF16),
        scratch_shapes=[pads, slab, slab, slab],
        compiler_params=_cparams("parallel"),
        name="attn",
    )(*qkv, *biases)
    return out.reshape(TOKENS, GROUP_WIDTH)


def _mixout_kernel(h_ref, attn_ref, u_ref, uprev_ref, unext_ref, gate_ref,
                   pw_ref, ps_ref, wpa_ref, wpp_ref, wo_ref, gffn_ref, wr_ref,
                   out_ref, xn_ref, ri_ref, rw_ref, cnt_ref, runs):
    j = pl.program_id(0) % TILES_PER_SEQ
    y_attn = _dot(attn_ref[...], wpa_ref[...])

    u = u_ref[...]
    prev = jnp.where(j == 0, 0.0, uprev_ref[0])
    nxt = jnp.where(j == TILES_PER_SEQ - 1, 0.0, unext_ref[0])
    ext_rows = ROW_TILE + 2 * POOL_HALO
    runs[:, ext_rows:ext_rows + POOL_HALO, :] = jnp.zeros((2, POOL_HALO, POOL_GROUP_WIDTH), F32)
    pos = j * ROW_TILE + lax.broadcasted_iota(I32, (ROW_TILE, 1), 0)
    mixed = []
    for gi, w in enumerate(POOL_WINDOWS):
        half = w // 2
        sl = slice(gi * POOL_GROUP_WIDTH, (gi + 1) * POOL_GROUP_WIDTH)
        runs[0, 0:POOL_HALO, :] = prev[:, sl]
        runs[0, POOL_HALO:POOL_HALO + ROW_TILE, :] = u[:, sl]
        runs[0, POOL_HALO + ROW_TILE:ext_rows, :] = nxt[:, sl]
        src, span = 0, 1
        while 2 * span < w:
            runs[1 - src, 0:ext_rows, :] = runs[src, 0:ext_rows, :] + runs[src, span:span + ext_rows, :]
            src, span = 1 - src, 2 * span
        lo = POOL_HALO - half
        acc = runs[src, lo:lo + ROW_TILE, :] + runs[src, lo + half:lo + half + ROW_TILE, :]
        cnt = (jnp.minimum(pos + half, SEQ) - jnp.maximum(pos - half, 0)).astype(F32)
        pooled = acc / cnt - u[:, sl]
        mixed.append(_dot(pooled.astype(BF16), pw_ref[gi]) * ps_ref[:, sl])
    y_pool = _dot(jnp.concatenate(mixed, axis=1).astype(BF16), wpp_ref[...])

    y = (gate_ref[:, 0:D_MODEL] * y_attn.astype(BF16)
         + gate_ref[:, D_MODEL:GATE_WIDTH] * y_pool.astype(BF16))
    h = h_ref[...] + _dot(y, wo_ref[...])
    out_ref[...] = h
    _route_rows(h, gffn_ref, wr_ref, xn_ref, ri_ref, rw_ref, cnt_ref)


def _mix_out(h, attn, u, gates, pool_w, pool_scale, w_proj_attn, w_proj_pool, w_out, g_ffn, wr_split):
    halo_blocks = ROW_TILE // POOL_HALO
    u3 = u.reshape(TOKENS // POOL_HALO, POOL_HALO, POOL_WIDTH)
    last = TOKENS // POOL_HALO - 1
    prev_spec = pl.BlockSpec((1, POOL_HALO, POOL_WIDTH),
                             lambda i: (jnp.maximum(i * halo_blocks - 1, 0), 0, 0))
    next_spec = pl.BlockSpec((1, POOL_HALO, POOL_WIDTH),
                             lambda i: (jnp.minimum((i + 1) * halo_blocks, last), 0, 0))
    weights = (pool_w, pool_scale, w_proj_attn, w_proj_pool, w_out, g_ffn, wr_split)
    return pl.pallas_call(
        _mixout_kernel,
        grid=(N_TILES,),
        in_specs=[_row_spec(D_MODEL), _row_spec(GROUP_WIDTH),
                  _row_spec(POOL_WIDTH), prev_spec, next_spec, _row_spec(GATE_WIDTH)]
                 + [_full_spec(w) for w in weights],
        out_specs=[_row_spec(D_MODEL), _row_spec(D_MODEL), _row_spec(ROUTER_LANES),
                   _row_spec(ROUTER_LANES),
                   pl.BlockSpec((1, SUBLANES, ROUTER_LANES), lambda i: (i, 0, 0))],
        out_shape=[jax.ShapeDtypeStruct((TOKENS, D_MODEL), F32),
                   jax.ShapeDtypeStruct((TOKENS, D_MODEL), BF16),
                   jax.ShapeDtypeStruct((TOKENS, ROUTER_LANES), I32),
                   jax.ShapeDtypeStruct((TOKENS, ROUTER_LANES), F32),
                   jax.ShapeDtypeStruct((N_TILES, SUBLANES, ROUTER_LANES), F32)],
        scratch_shapes=[pltpu.VMEM((2, ROW_TILE + 3 * POOL_HALO, POOL_GROUP_WIDTH), F32)],
        compiler_params=_cparams("parallel"),
        name="mixout",
    )(h, attn, u, u3, u3, gates, *weights)


def _route_rows(h, g_ref, wr_ref, xn_ref, ri_ref, rw_ref, cnt_ref):
    xn = _rms(h, g_ref[...])
    hi = xn.astype(BF16)
    xn_ref[...] = hi
    lo = (xn - hi.astype(F32)).astype(BF16)
    both = _dot(hi, wr_ref[...])
    lg = both[:, 0:ROUTER_LANES] + (both[:, ROUTER_LANES:] + _dot(lo, wr_ref[:, 0:ROUTER_LANES]))

    lane = lax.broadcasted_iota(I32, (ROW_TILE, ROUTER_LANES), 1)
    lanef = lane.astype(F32)
    low = jnp.float32(-3.0e38)
    far = jnp.float32(ROUTER_LANES)
    first = lambda hit: jnp.min(jnp.where(hit, lanef, far), axis=-1, keepdims=True)

    is_group = lane < N_EXPERT_GROUPS
    gl = jnp.where(is_group, lg, low)
    gmax = jnp.max(gl, axis=-1, keepdims=True)
    gidx = first(gl == gmax).astype(I32)
    gden = jnp.sum(jnp.where(is_group, jnp.exp(gl - gmax), 0.0), axis=-1, keepdims=True)
    g_p = 1.0 / gden

    in_group = ((lane >= EXPERT_LANE0) & (lane < EXPERT_LANE0 + N_EXPERTS)
                & ((lane - EXPERT_LANE0) // EXPERTS_PER_GROUP == gidx))
    el = jnp.where(in_group, lg, low)
    t1 = jnp.max(el, axis=-1, keepdims=True)
    l1 = first(in_group & (el == t1))
    rest = in_group & (lanef != l1)
    el2 = jnp.where(rest, lg, low)
    t2 = jnp.max(el2, axis=-1, keepdims=True)
    l2 = first(rest & (el2 == t2))
    e2 = jnp.exp(t2 - t1)
    w1 = g_p * (1.0 / (1.0 + e2))
    w2 = g_p * (e2 / (1.0 + e2))

    hit1 = lanef == l1
    hit2 = lanef == l2
    onehot = (hit1 | hit2).astype(BF16)
    ri = lax.broadcasted_iota(I32, (ROW_TILE, ROW_TILE), 0)
    ci = lax.broadcasted_iota(I32, (ROW_TILE, ROW_TILE), 1)
    before = (ci < ri).astype(BF16)
    seen = _dot(before, onehot)
    r1 = jnp.sum(jnp.where(hit1, seen, 0.0), axis=-1, keepdims=True)
    r2 = jnp.sum(jnp.where(hit2, seen, 0.0), axis=-1, keepdims=True)

    packed = jnp.zeros((ROW_TILE, ROUTER_LANES), F32)
    for k, v in enumerate((l1, l2, r1, r2)):
        packed = jnp.where(lane == k, v, packed)
    ri_ref[...] = packed.astype(I32)
    rw_ref[...] = jnp.where(lane == 0, w1, jnp.where(lane == 1, w2, 0.0))
    counts = jnp.sum(onehot.astype(F32), axis=0, keepdims=True)
    cnt_ref[0] = jnp.broadcast_to(counts, (SUBLANES, ROUTER_LANES))


def _slots(ri_ref, off_ref):
    ri = ri_ref[...]
    lane = lax.broadcasted_iota(I32, (ROW_TILE, ROUTER_LANES), 1)
    off = off_ref[0, 0:1, :]
    pick = lambda k: jnp.sum(jnp.where(lane == ri[:, k:k + 1], off, 0.0), axis=-1, keepdims=True)
    return (pick(0) + ri[:, 2:3].astype(F32), pick(1) + ri[:, 3:4].astype(F32))


def _as_rows(cols):
    eye = (lax.broadcasted_iota(I32, (ROW_TILE, ROW_TILE), 0)
           == lax.broadcasted_iota(I32, (ROW_TILE, ROW_TILE), 1))
    return [jnp.sum(jnp.where(eye, c, 0.0), axis=0, keepdims=True) for c in cols]


def _run_copy(tile, e, loff_s, c8_s, dst_s, buf_ref, slot, hbm_ref, sem, to_hbm):
    k = tile * N_EXPERTS + e
    n = pl.multiple_of(c8_s[k], RUN_ALIGN)
    vm = buf_ref.at[slot, pl.ds(pl.multiple_of(loff_s[k], RUN_ALIGN), n)]
    hb = hbm_ref.at[pl.ds(pl.multiple_of(dst_s[k], RUN_ALIGN), n)]
    src, dst = (vm, hb) if to_hbm else (hb, vm)
    return n, pltpu.make_async_copy(src, dst, sem.at[slot])


def _loop(lo, hi, fn, unroll=1):
    def body(e, carry):
        fn(e)
        return carry
    lax.fori_loop(lo, hi, body, 0, unroll=unroll)


def _wait_rows(n, buf_ref, slot, hbm_ref, sem, to_hbm):
    vm = buf_ref.at[slot, pl.ds(0, n)]
    hb = hbm_ref.at[pl.ds(0, n)]
    src, dst = (vm, hb) if to_hbm else (hb, vm)
    pltpu.make_async_copy(src, dst, sem.at[slot]).wait()


def _start(n, cp):
    @pl.when(n > 0)
    def _():
        cp.start()


def _wait(n, cp):
    @pl.when(n > 0)
    def _():
        cp.wait()


def _dispatch_kernel(loff_s, c8_s, dst_s, nslot_s, zdst_s, zcnt_s, nact_s,
                     xn_ref, ri_ref, rw_ref, off_ref, xpad_hbm, sorted_buf, zero_buf, sem, zsem):
    i = pl.program_id(0)
    slot = i % 2
    last = pl.num_programs(0) - 1

    def zero_copy(e):
        n = pl.multiple_of(zcnt_s[e], RUN_ALIGN)
        dst = xpad_hbm.at[pl.ds(pl.multiple_of(zdst_s[e], RUN_ALIGN), n)]
        return n, pltpu.make_async_copy(zero_buf.at[pl.ds(0, n)], dst, zsem)

    def tail_copy(b):
        dst = xpad_hbm.at[pl.ds(pl.multiple_of(b * MOE_BLOCK, MOE_BLOCK), MOE_BLOCK)]
        return pltpu.make_async_copy(zero_buf, dst, zsem)

    @pl.when(i == 0)
    def _():
        zero_buf[...] = jnp.zeros_like(zero_buf)
        _loop(0, N_EXPERTS, lambda e: _start(*zero_copy(e)))
        _loop(nact_s[0], N_MOE_BLOCKS, lambda b: tail_copy(b).start())

    s1, s2 = _slots(ri_ref, off_ref)
    rw = rw_ref[...]
    s1_row, s2_row, w1_row, w2_row = _as_rows([s1, s2, rw[:, 0:1], rw[:, 1:2]])
    xn = xn_ref[...]
    for first, size in SLOT_CHUNKS:
        @pl.when(first < nslot_s[i])
        def _():
            rows = slice(first, first + size)
            lane = lax.broadcasted_iota(I32, (size, LANES), 1)
            srow = (first + lax.broadcasted_iota(I32, (size, ROW_TILE), 0)).astype(F32)
            hit1 = srow == s1_row
            hit2 = srow == s2_row
            xs = _dot((hit1 | hit2).astype(BF16), xn)
            sorted_buf[slot, rows, 0:HALF] = _pack_halves(xs[:, 0:HALF], xs[:, HALF:D_MODEL])
            ws = jnp.sum(jnp.where(hit1, w1_row, 0.0) + jnp.where(hit2, w2_row, 0.0),
                         axis=-1, keepdims=True)
            sorted_buf[slot, rows, HALF:X_WORDS] = jnp.where(lane == 0, _bits(ws), jnp.uint32(0))

    copy = lambda tile, sl: (lambda e: _run_copy(tile, e, loff_s, c8_s, dst_s, sorted_buf, sl,
                                                 xpad_hbm, sem, True))
    mine = copy(i, slot)
    _loop(0, N_EXPERTS, lambda e: _start(*mine(e)), unroll=RUN_UNROLL)
    tile_rows = lambda t: pl.multiple_of(nslot_s[t], RUN_ALIGN)

    @pl.when(i > 0)
    def _():
        _wait_rows(tile_rows(i - 1), sorted_buf, 1 - slot, xpad_hbm, sem, True)

    @pl.when(i == last)
    def _():
        _wait_rows(tile_rows(i), sorted_buf, slot, xpad_hbm, sem, True)
        _loop(0, N_EXPERTS, lambda e: _wait(*zero_copy(e)))
        _loop(nact_s[0], N_MOE_BLOCKS, lambda b: tail_copy(b).wait())


def _dispatch(tables, xn, route_i, route_w):
    tile_row = pl.BlockSpec((1, SUBLANES, ROUTER_LANES), lambda i, *_: (i, 0, 0))
    grid_spec = pltpu.PrefetchScalarGridSpec(
        num_scalar_prefetch=7,
        grid=(N_TILES,),
        in_specs=[_row_spec(D_MODEL), _row_spec(ROUTER_LANES), _row_spec(ROUTER_LANES), tile_row],
        out_specs=pl.BlockSpec(memory_space=pl.ANY),
        scratch_shapes=[pltpu.VMEM((2, MAX_SLOTS, X_WORDS), U32),
                        pltpu.VMEM((MOE_BLOCK, X_WORDS), U32),
                        pltpu.SemaphoreType.DMA((2,)),
                        pltpu.SemaphoreType.DMA(())],
    )
    return pl.pallas_call(
        _dispatch_kernel,
        grid_spec=grid_spec,
        out_shape=jax.ShapeDtypeStruct((PAD_ROWS, X_WORDS), U32),
        compiler_params=_cparams("arbitrary"),
        name="dispatch",
    )(tables["loff"], tables["c8"], tables["dst"], tables["nslot"], tables["zdst"], tables["zcnt"],
      tables["n_active"], xn, route_i, route_w, tables["offrow"])


def _combine_kernel(loff_s, c8_s, dst_s, nslot_s,
                    h_ref, ri_ref, off_ref, g_ref, ypad_hbm, out_ref,
                    ybuf, sem, *, final_norm):
    i = pl.program_id(0)
    slot = i % 2
    last = pl.num_programs(0) - 1
    fetch = lambda tile, sl: (lambda e: _run_copy(tile, e, loff_s, c8_s, dst_s, ybuf, sl,
                                                  ypad_hbm, sem, False))

    @pl.when(i == 0)
    def _():
        ybuf[...] = jnp.zeros_like(ybuf)
        first = fetch(i, slot)
        _loop(0, N_EXPERTS, lambda e: _start(*first(e)), unroll=RUN_UNROLL)

    @pl.when(i < last)
    def _():
        nxt = fetch(i + 1, 1 - slot)
        _loop(0, N_EXPERTS, lambda e: _start(*nxt(e)), unroll=RUN_UNROLL)

    _wait_rows(pl.multiple_of(nslot_s[i], RUN_ALIGN), ybuf, slot, ypad_hbm, sem, False)

    s1, s2 = _slots(ri_ref, off_ref)

    def gathered(first, size):
        scol = (first + lax.broadcasted_iota(I32, (ROW_TILE, size), 1)).astype(F32)
        pick = ((scol == s1) | (scol == s2)).astype(BF16)
        y_hi, y_lo = _unpack_halves(ybuf[slot, first:first + size, :])
        return jnp.concatenate([_dot(pick, y_hi), _dot(pick, y_lo)], axis=1)

    finish = (lambda v: _rms(v, g_ref[...])) if final_norm else (lambda v: v)
    h = h_ref[...] + gathered(0, COMBINE_BASE)
    spill = nslot_s[i] > COMBINE_BASE

    @pl.when(jnp.logical_not(spill))
    def _():
        out_ref[...] = finish(h)

    @pl.when(spill)
    def _():
        out_ref[...] = finish(h + gathered(COMBINE_BASE, MAX_SLOTS - COMBINE_BASE))


def _combine(tables, h, route_i, y_pad, g, final_norm):
    tile_row = pl.BlockSpec((1, SUBLANES, ROUTER_LANES), lambda i, *_: (i, 0, 0))
    grid_spec = pltpu.PrefetchScalarGridSpec(
        num_scalar_prefetch=4,
        grid=(N_TILES,),
        in_specs=[_row_spec(D_MODEL), _row_spec(ROUTER_LANES), tile_row, _full_spec(g),
                  pl.BlockSpec(memory_space=pl.ANY)],
        out_specs=_row_spec(D_MODEL),
        scratch_shapes=[pltpu.VMEM((2, MAX_SLOTS, HALF), U32),
                        pltpu.SemaphoreType.DMA((2,))],
    )
    return pl.pallas_call(
        functools.partial(_combine_kernel, final_norm=final_norm),
        grid_spec=grid_spec,
        out_shape=jax.ShapeDtypeStruct((TOKENS, D_MODEL), F32),
        compiler_params=_cparams("arbitrary"),
        name="combine",
    )(tables["loff"], tables["c8"], tables["dst"], tables["nslot"],
      h, route_i, tables["offrow"], g, y_pad)


def _expert_kernel(be_ref, slot_ref, next_ref, nact_ref, x_ref, wg_hbm, wu_hbm, wd_hbm, y_hbm,
                   wg_f, wu_f, wg_b, wu_b, wd_s, ybuf, zero_buf, sem, ysem, zsem, *, layer):
    i = pl.program_id(0)
    n_active = nact_ref[0]
    staged = ((wg_hbm, wg_f, wg_b), (wu_hbm, wu_f, wu_b))

    def fetch(e, slot):
        copies = [pltpu.make_async_copy(hbm.at[layer, e], f32.at[slot], sem.at[slot, k])
                  for k, (hbm, f32, _) in enumerate(staged)]
        return copies + [pltpu.make_async_copy(wd_hbm.at[e], wd_s.at[slot], sem.at[slot, len(staged)])]

    def block_rows(b):
        return y_hbm.at[pl.ds(pl.multiple_of(b * MOE_BLOCK, MOE_BLOCK), MOE_BLOCK)]

    def put(b):
        return pltpu.make_async_copy(ybuf.at[b % 2], block_rows(b), ysem.at[b % 2])

    def tail_copy(b):
        return pltpu.make_async_copy(zero_buf, block_rows(b), zsem)

    @pl.when(i == 0)
    def _():
        zero_buf[...] = jnp.zeros_like(zero_buf)
        _loop(n_active, N_MOE_BLOCKS, lambda b: tail_copy(b).start())

    @pl.when(i == pl.num_programs(0) - 1)
    def _():
        _loop(n_active, N_MOE_BLOCKS, lambda b: tail_copy(b).wait())

    @pl.when(i < n_active)
    def _():
        e = be_ref[i]
        slot = slot_ref[i]

        def request(ahead, block):
            nxt = next_ref[(ahead - 1) * N_MOE_BLOCKS + block]

            @pl.when(nxt >= 0)
            def _():
                for cp in fetch(nxt, (slot + ahead) % W_STAGES):
                    cp.start()

        @pl.when((i == 0) | (e != be_ref[jnp.maximum(i - 1, 0)]))
        def _():
            @pl.when(i == 0)
            def _():
                for cp in fetch(e, slot):
                    cp.start()
                for ahead in range(1, W_STAGES - 1):
                    request(ahead, i)
            for cp in fetch(e, slot):
                cp.wait()
            request(W_STAGES - 1, i)
            for _, f32, b16 in staged:
                b16[...] = f32[slot].astype(BF16)

        x_hi, x_lo = _unpack_halves(x_ref[:, 0:HALF])
        row_w = lax.bitcast_convert_type(x_ref[:, HALF:HALF + 1], F32)
        gate = _dot(x_hi, wg_b[0:HALF, :]) + _dot(x_lo, wg_b[HALF:D_MODEL, :])
        up = _dot(x_hi, wu_b[0:HALF, :]) + _dot(x_lo, wu_b[HALF:D_MODEL, :])
        hmid = (jax.nn.silu(gate) * up).astype(BF16)
        y = (_dot(hmid, wd_s[slot]) * row_w).astype(BF16).astype(F32)
        ybuf[i % 2] = _pack_halves(y[:, 0:HALF], y[:, HALF:D_MODEL])
        put(i).start()

        @pl.when(i > 0)
        def _():
            put(i - 1).wait()

        @pl.when(i == n_active - 1)
        def _():
            put(i).wait()


def _experts(layer, tables, x_pad, w_gate, w_up, w_down):
    hbm = pl.BlockSpec(memory_space=pl.ANY)
    up_shape, down_shape = (D_MODEL, D_EXPERT), (D_EXPERT, D_MODEL)
    grid_spec = pltpu.PrefetchScalarGridSpec(
        num_scalar_prefetch=4,
        grid=(N_MOE_BLOCKS,),
        in_specs=[pl.BlockSpec((MOE_BLOCK, X_WORDS),
                               lambda i, be, sl, nx, na: (jnp.minimum(i, na[0] - 1), 0)),
                  hbm, hbm, hbm],
        out_specs=hbm,
        scratch_shapes=[pltpu.VMEM((W_STAGES,) + up_shape, F32),
                        pltpu.VMEM((W_STAGES,) + up_shape, F32),
                        pltpu.VMEM(up_shape, BF16), pltpu.VMEM(up_shape, BF16),
                        pltpu.VMEM((W_STAGES,) + down_shape, BF16),
                        pltpu.VMEM((2, MOE_BLOCK, HALF), U32),
                        pltpu.VMEM((MOE_BLOCK, HALF), U32),
                        pltpu.SemaphoreType.DMA((W_STAGES, 3)),
                        pltpu.SemaphoreType.DMA((2,)),
                        pltpu.SemaphoreType.DMA(())],
    )
    return pl.pallas_call(
        functools.partial(_expert_kernel, layer=layer),
        grid_spec=grid_spec,
        out_shape=jax.ShapeDtypeStruct((PAD_ROWS, HALF), U32),
        compiler_params=_cparams("arbitrary"),
        name="experts",
    )(tables["blk_expert"], tables["blk_slot"], tables["blk_next"], tables["n_active"],
      x_pad, w_gate, w_up, w_down)


def _routing_tables(tile_counts):
    cnt = tile_counts[:, 0, EXPERT_LANE0:EXPERT_LANE0 + N_EXPERTS].astype(I32)
    c8 = (cnt + RUN_ALIGN - 1) // RUN_ALIGN * RUN_ALIGN
    loff = jnp.cumsum(c8, axis=1) - c8
    nslot = jnp.sum(c8, axis=1)
    tot = jnp.sum(c8, axis=0)
    padded = (tot + MOE_BLOCK - 1) // MOE_BLOCK * MOE_BLOCK
    end = jnp.cumsum(padded)
    base = end - padded
    dst = base[None, :] + jnp.cumsum(c8, axis=0) - c8
    blk_start = jnp.arange(N_MOE_BLOCKS, dtype=I32) * MOE_BLOCK
    blk_expert = jnp.minimum(jnp.sum((end[None, :] <= blk_start[:, None]).astype(I32), axis=1),
                             N_EXPERTS - 1).astype(I32)
    offrow = jnp.zeros((N_TILES, ROUTER_LANES), F32).at[:, EXPERT_LANE0:EXPERT_LANE0 + N_EXPERTS].set(
        loff.astype(F32))
    offrow = jnp.broadcast_to(offrow[:, None, :], (N_TILES, SUBLANES, ROUTER_LANES))
    experts = jnp.arange(N_EXPERTS, dtype=I32)
    present = padded > 0
    ordinal = jnp.cumsum(present.astype(I32)) - 1
    at_or_after = lax.cummin(jnp.where(present, experts, N_EXPERTS), reverse=True)
    after = jnp.concatenate([at_or_after[1:], jnp.full((1,), N_EXPERTS, I32)])
    hop = jnp.concatenate([after, jnp.full((1,), N_EXPERTS, I32)])
    pick = (blk_expert[:, None] == experts[None, :]).astype(I32)
    blk_slot = jnp.sum(pick * (ordinal % W_STAGES)[None, :], axis=1).astype(I32)
    ahead, blk_next = experts, []
    for _ in range(W_STAGES - 1):
        ahead = jnp.sum((ahead[:, None] == jnp.arange(N_EXPERTS + 1)[None, :]) * hop[None, :], axis=1)
        per_block = jnp.sum(pick * ahead[None, :], axis=1)
        blk_next.append(jnp.where(per_block >= N_EXPERTS, -1, per_block))
    blk_next = jnp.concatenate(blk_next).astype(I32)
    return {
        "blk_slot": blk_slot, "blk_next": blk_next,
        "loff": loff.reshape(-1).astype(I32), "c8": c8.reshape(-1).astype(I32),
        "dst": dst.reshape(-1).astype(I32), "nslot": nslot.astype(I32),
        "zdst": (base + tot).astype(I32), "zcnt": (padded - tot).astype(I32),
        "offrow": offrow, "blk_expert": blk_expert,
        "n_active": (end[-1:] // MOE_BLOCK).astype(I32),
    }


def _split_router_weights(w_router_group, w_router_expert):
    w_e = w_router_expert.transpose(1, 0, 2).reshape(D_MODEL, N_EXPERTS)
    w = jnp.concatenate([w_router_group, w_e], axis=1)
    w = jnp.pad(w, ((0, 0), (0, ROUTER_LANES - w.shape[1])))
    hi = w.astype(BF16)
    lo = (w - hi.astype(F32)).astype(BF16)
    return jnp.concatenate([hi, lo], axis=1)


def _input_weights(w):
    cols = []
    for g in range(N_GROUPS):
        for part in range(3):
            c0 = part * ATTN_WIDTH + g * GROUP_WIDTH
            cols.append(w[:, c0:c0 + GROUP_WIDTH])
    cols.append(w[:, QKV_WIDTH:])
    return jnp.concatenate(cols, axis=1).astype(BF16)


def kernel(x, rel_bias, norm_mix_g, w_in, pool_w, pool_scale, w_proj_attn, w_proj_pool, w_out,
           norm_ffn_g, w_router_group, w_router_expert, w_gate_e, w_up_e, w_down_e, norm_final_g):
    h = x.reshape(TOKENS, D_MODEL)
    for l in range(DEPTH):
        qkv, u, gates, w_down_b = _project(l, h, norm_mix_g[l][None], _input_weights(w_in[l]), w_down_e)
        h, xn, route_i, route_w, tile_counts = _mix_out(
            h, _attention(qkv, rel_bias), u, gates, pool_w[l].astype(BF16), pool_scale[l][None],
            w_proj_attn[l].astype(BF16), w_proj_pool[l].astype(BF16), w_out[l].astype(BF16),
            norm_ffn_g[l][None], _split_router_weights(w_router_group[l], w_router_expert[l]))
        tables = _routing_tables(tile_counts)
        x_pad = _dispatch(tables, xn, route_i, route_w)
        y_pad = _experts(l, tables, x_pad, w_gate_e, w_up_e, w_down_b)
        h = _combine(tables, h, route_i, y_pad, norm_final_g[None], l == DEPTH - 1)
    return h.reshape(BATCH, SEQ, D_MODEL)
```

```python
import functools
import math

import jax
import jax.numpy as jnp
from jax import lax
from jax.experimental import pallas as pl
from jax.experimental.pallas import tpu as pltpu

F32 = jnp.float32
BF16 = jnp.bfloat16
I32 = jnp.int32
U32 = jnp.uint32

D_MODEL = 1024
BATCH = 8
SEQ = 2048
TOKENS = BATCH * SEQ
DEPTH = 2

HEAD_DIM = 64
HEADS_PER_GROUP = 4
GROUP_WIDTH = HEADS_PER_GROUP * HEAD_DIM
DILATION_PATTERNS = ((128, 1), (512, 4), (2048, 16))
N_GROUPS = len(DILATION_PATTERNS)
N_ATTN_HEADS = N_GROUPS * HEADS_PER_GROUP
ATTN_WIDTH = N_ATTN_HEADS * HEAD_DIM
QKV_WIDTH = 3 * ATTN_WIDTH
GROUP_QKV = 3 * GROUP_WIDTH
N_SIDE = 64
assert all(w // (2 * d) == N_SIDE for w, d in DILATION_PATTERNS)
POOL_WINDOWS = (2, 4, 8, 16)
POOL_GROUP_WIDTH = 128
POOL_WIDTH = len(POOL_WINDOWS) * POOL_GROUP_WIDTH
POOL_HALO = max(POOL_WINDOWS) // 2
N_BRANCHES = 2
GATE_WIDTH = N_BRANCHES * D_MODEL
IN_WIDTH = QKV_WIDTH + POOL_WIDTH + GATE_WIDTH
N_BUCKETS = 32
MAX_DISTANCE = 1024
N_EXPERT_GROUPS = 8
EXPERTS_PER_GROUP = 8
N_EXPERTS = N_EXPERT_GROUPS * EXPERTS_PER_GROUP
TOP_K = 2
D_EXPERT = 512
N_ASSIGN = TOKENS * TOP_K
EPS = 1e-6
NEG_INF = -1e30
LOG2_E = math.log2(math.e)
Q_SCALE = HEAD_DIM ** -0.5 * LOG2_E

LANES = 128
SUBLANES = 8
ROW_TILE = 512
N_TILES = TOKENS // ROW_TILE
TILES_PER_SEQ = SEQ // ROW_TILE
Q_BLOCK = 128
K_BLOCK = Q_BLOCK + 2 * N_SIDE
PAD_WAYS = 4
BLOCKS_IN_FLIGHT = 4
ROUTER_LANES = 128
EXPERT_LANE0 = N_EXPERT_GROUPS
VMEM_LIMIT = 56 * 1024 * 1024

RUN_ALIGN = SUBLANES
MOE_BLOCK = 336
HALF = D_MODEL // 2
X_WORDS = HALF + LANES
SLOT_CHUNK = 256
RUN_UNROLL = 4
W_STAGES = 2
MAX_SLOTS = -(-(TOP_K * ROW_TILE + N_EXPERTS * (RUN_ALIGN - 1)) // SLOT_CHUNK) * SLOT_CHUNK
SLOT_CHUNKS = (tuple((s, ROW_TILE) for s in range(0, TOP_K * ROW_TILE, ROW_TILE))
               + tuple((s, SLOT_CHUNK) for s in range(TOP_K * ROW_TILE, MAX_SLOTS, SLOT_CHUNK)))
COMBINE_BASE = MAX_SLOTS - SLOT_CHUNK
N_MOE_BLOCKS = -(-(N_ASSIGN + N_TILES * N_EXPERTS * (RUN_ALIGN - 1)
                   + N_EXPERTS * (MOE_BLOCK - RUN_ALIGN)) // MOE_BLOCK)
PAD_ROWS = N_MOE_BLOCKS * MOE_BLOCK
HIGH_HALF = 0xFFFF0000


def _cparams(*sem):
    return pltpu.CompilerParams(dimension_semantics=sem, vmem_limit_bytes=VMEM_LIMIT)


def _rms(h, g):
    r = lax.rsqrt(jnp.mean(h * h, axis=-1, keepdims=True) + EPS)
    return (h * r) * g


def _dot(a, b):
    return jnp.dot(a, b, preferred_element_type=F32)


def _row_spec(width):
    return pl.BlockSpec((ROW_TILE, width), lambda i, *_: (i, 0))


def _full_spec(a):
    return pl.BlockSpec(a.shape, lambda i, *_: (0,) * a.ndim)


def _bits(x):
    return lax.bitcast_convert_type(x, U32)


def _pack_halves(a, b):
    return (_bits(a) & jnp.uint32(HIGH_HALF)) | (_bits(b) >> 16)


def _unpack_halves(words):
    hi = lax.bitcast_convert_type(words & jnp.uint32(HIGH_HALF), F32)
    lo = lax.bitcast_convert_type(words << 16, F32)
    return hi.astype(BF16), lo.astype(BF16)


def _proj_kernel(h_ref, g_ref, w_ref, wd_ref, q0_ref, q1_ref, q2_ref, u_ref, gate_ref, wdb_ref, slabs):
    wdb_ref[...] = wd_ref[...].astype(BF16)
    xn = _rms(h_ref[...], g_ref[...]).astype(BF16)
    n_slabs = GROUP_QKV // LANES
    for g, out_ref in enumerate((q0_ref, q1_ref, q2_ref)):
        dilation = DILATION_PATTERNS[g][1]
        res = _dot(xn, w_ref[:, g * GROUP_QKV:(g + 1) * GROUP_QKV])
        res = jnp.concatenate([res[:, 0:GROUP_WIDTH] * Q_SCALE, res[:, GROUP_WIDTH:]], axis=1)
        if dilation == 1:
            out_ref[0, 0] = res.astype(BF16)
            continue
        for s in range(n_slabs):
            slabs[s] = res[:, s * LANES:(s + 1) * LANES]
        n = ROW_TILE // dilation
        for r in range(dilation):
            rows = [slabs[s, pl.ds(r, n, stride=dilation), :] for s in range(n_slabs)]
            out_ref[0, r] = jnp.concatenate(rows, axis=1).astype(BF16)
    u_ref[...] = _dot(xn, w_ref[:, QKV_WIDTH:QKV_WIDTH + POOL_WIDTH])
    gates = _dot(xn, w_ref[:, QKV_WIDTH + POOL_WIDTH:IN_WIDTH])
    gate_ref[...] = jax.nn.sigmoid(gates).astype(BF16)


def _project(layer, h, g, w_bf16, w_down):
    qkv_shapes, qkv_specs = [], []
    for _, d in DILATION_PATTERNS:
        qkv_shapes.append(jax.ShapeDtypeStruct((BATCH, d, SEQ // d, GROUP_QKV), BF16))
        qkv_specs.append(pl.BlockSpec((1, d, ROW_TILE // d, GROUP_QKV),
                                      lambda i: (i // TILES_PER_SEQ, 0, i % TILES_PER_SEQ, 0)))
    layer_rows = N_EXPERTS * D_EXPERT
    step_rows = layer_rows // N_TILES
    res = pl.pallas_call(
        _proj_kernel,
        grid=(N_TILES,),
        in_specs=[_row_spec(D_MODEL), _full_spec(g), _full_spec(w_bf16),
                  pl.BlockSpec((step_rows, D_MODEL), lambda i: (layer * N_TILES + i, 0))],
        out_specs=qkv_specs + [_row_spec(POOL_WIDTH), _row_spec(GATE_WIDTH),
                               pl.BlockSpec((step_rows, D_MODEL), lambda i: (i, 0))],
        out_shape=qkv_shapes + [jax.ShapeDtypeStruct((TOKENS, POOL_WIDTH), F32),
                                jax.ShapeDtypeStruct((TOKENS, GATE_WIDTH), BF16),
                                jax.ShapeDtypeStruct((layer_rows, D_MODEL), BF16)],
        scratch_shapes=[pltpu.VMEM((GROUP_QKV // LANES, ROW_TILE, LANES), F32)],
        compiler_params=_cparams("parallel"),
        name="proj",
    )(h, g, w_bf16, w_down.reshape(DEPTH * layer_rows, D_MODEL))
    w_down_b = res[N_GROUPS + 2].reshape(N_EXPERTS, D_EXPERT, D_MODEL)
    return res[:N_GROUPS], res[N_GROUPS], res[N_GROUPS + 1], w_down_b


def _attn_kernel(q0_ref, q1_ref, q2_ref, b0_ref, b1_ref, b2_ref, o_ref, pads, num, den, top):
    zpad = jnp.zeros((N_SIDE, GROUP_WIDTH), BF16)
    groups = ((q2_ref, b2_ref, 2), (q1_ref, b1_ref, 1), (q0_ref, b0_ref, 0))
    for order, (qkv_ref, bias_ref, g) in enumerate(groups):
        dilation = DILATION_PATTERNS[g][1]
        sub_len = SEQ // dilation
        for way in range(PAD_WAYS):
            for kv in range(2):
                pads[way, kv, 0:N_SIDE, :] = zpad
                pads[way, kv, N_SIDE + sub_len:2 * N_SIDE + sub_len, :] = zpad
        _attn_group(qkv_ref, bias_ref, pads, num, den, top, dilation=dilation, first=order == 0)

    def finish(c, carry):
        rows = pl.ds(pl.multiple_of(c * ROW_TILE, ROW_TILE), ROW_TILE)
        merged = [num[half, rows, :] / den[half, rows, :] for half in range(GROUP_WIDTH // LANES)]
        o_ref[0, rows, :] = jnp.concatenate(merged, axis=1).astype(BF16)
        return carry

    lax.fori_loop(0, SEQ // ROW_TILE, finish, 0)


def _attn_group(qkv_ref, bias_ref, pads, num, den, top, *, dilation, first):
    sub_len = SEQ // dilation
    n_blocks = sub_len // Q_BLOCK
    head_of_lane = lax.broadcasted_iota(I32, (1, GROUP_WIDTH), 1) // HEAD_DIM

    def by_head(cols):
        out = cols[HEADS_PER_GROUP - 1]
        for h in range(HEADS_PER_GROUP - 2, -1, -1):
            out = jnp.where(head_of_lane == h, cols[h], out)
        return out

    def load_keys(r, way):
        pads[way, 0, N_SIDE:N_SIDE + sub_len, :] = qkv_ref[0, r, :, GROUP_WIDTH:2 * GROUP_WIDTH]
        pads[way, 1, N_SIDE:N_SIDE + sub_len, :] = qkv_ref[0, r, :, 2 * GROUP_WIDTH:3 * GROUP_WIDTH]

    def block(r, way, i):
        static = isinstance(i, int)
        r0 = i * Q_BLOCK if static else pl.multiple_of(i * Q_BLOCK, Q_BLOCK)
        qb = qkv_ref[0, r, pl.ds(r0, Q_BLOCK), 0:GROUP_WIDTH]
        kw = pads[way, 0, pl.ds(r0, K_BLOCK), :]
        vw = pads[way, 1, pl.ds(r0, K_BLOCK), :]
        is_first, is_last = i == 0, i == n_blocks - 1
        edge = (int(is_first) + 2 * int(is_last) if static
                else is_first.astype(I32) + 2 * is_last.astype(I32))
        zero = jnp.zeros_like(qb)
        q_heads = jnp.concatenate(
            [jnp.where(head_of_lane == h, qb, zero) for h in range(HEADS_PER_GROUP)], axis=0)
        s = lax.dot_general(q_heads, kw, (((1,), (1,)), ((), ())), preferred_element_type=F32)
        s = s.reshape(HEADS_PER_GROUP, Q_BLOCK, K_BLOCK) + bias_ref[edge]
        m = jnp.max(s, axis=-1, keepdims=True)
        p = jnp.exp2(s - m)
        psum = jnp.sum(p, axis=-1, keepdims=True)
        pb = p.astype(BF16)
        p_cat = jnp.concatenate([pb[h] for h in range(HEADS_PER_GROUP)], axis=1)
        zv = jnp.zeros_like(vw)
        v_heads = jnp.concatenate(
            [jnp.where(head_of_lane == h, vw, zv) for h in range(HEADS_PER_GROUP)], axis=0)
        o = _dot(p_cat, v_heads)
        lse = m + jnp.log2(psum)
        o = o / by_head([psum[h] for h in range(HEADS_PER_GROUP)])
        lse_lanes = by_head([lse[h] for h in range(HEADS_PER_GROUP)])
        if dilation == 1:
            rows = pl.ds(r0, Q_BLOCK)
        else:
            rows = pl.ds(r + dilation * r0, Q_BLOCK, stride=dilation)
        for half in range(GROUP_WIDTH // LANES):
            lanes = slice(half * LANES, (half + 1) * LANES)
            if first:
                num[half, rows, :] = o[:, lanes]
                den[half, rows, :] = jnp.ones((Q_BLOCK, LANES), F32)
                top[half, rows, :] = lse_lanes[:, lanes]
            else:
                old = top[half, rows, :]
                new = jnp.maximum(old, lse_lanes[:, lanes])
                keep = jnp.exp2(old - new)
                add = jnp.exp2(lse_lanes[:, lanes] - new)
                num[half, rows, :] = keep * num[half, rows, :] + add * o[:, lanes]
                den[half, rows, :] = keep * den[half, rows, :] + add
                top[half, rows, :] = new

    def blocks_of(r, way):
        if n_blocks == 1:
            block(r, way, 0)
        else:
            assert n_blocks % BLOCKS_IN_FLIGHT == 0

            def several(j, carry):
                for k in range(BLOCKS_IN_FLIGHT):
                    block(r, way, BLOCKS_IN_FLIGHT * j + k)
                return carry
            lax.fori_loop(0, n_blocks // BLOCKS_IN_FLIGHT, several, 0)

    if dilation == 1:
        load_keys(0, 0)
        blocks_of(0, 0)
    else:
        ways = min(PAD_WAYS, dilation, max(1, 2 * BLOCKS_IN_FLIGHT // n_blocks))

        def residues(j, carry):
            for way in range(ways):
                load_keys(ways * j + way, way)
            for way in range(ways):
                blocks_of(ways * j + way, way)
            return carry
        lax.fori_loop(0, dilation // ways, residues, 0)


def _t5_bucket(rel):
    nb = N_BUCKETS // 2
    ret = jnp.where(rel > 0, nb, 0)
    n = jnp.abs(rel)
    max_exact = nb // 2
    nf = jnp.maximum(n, max_exact).astype(F32)
    large = max_exact + (jnp.log(nf / max_exact) / math.log(MAX_DISTANCE / max_exact)
                         * (nb - max_exact)).astype(I32)
    large = jnp.minimum(large, nb - 1)
    return ret + jnp.where(n < max_exact, n, large)


def _band_bias(rel_bias, group, dilation):
    qi = jnp.arange(Q_BLOCK)[:, None]
    ki = jnp.arange(K_BLOCK)[None, :]
    bucket = _t5_bucket((ki - N_SIDE - qi) * dilation)
    tab = rel_bias[:, group * HEADS_PER_GROUP:(group + 1) * HEADS_PER_GROUP]
    onehot = (bucket[:, :, None] == jnp.arange(N_BUCKETS)[None, None, :]).astype(F32)
    bias = jnp.einsum('qkb,bh->hqk', onehot, tab, precision=lax.Precision.HIGHEST).astype(F32)
    bias = bias * LOG2_E
    edge = jnp.arange(4)[:, None, None]
    ok = ((jnp.abs(ki - N_SIDE - qi) <= N_SIDE)[None]
          & (((edge & 1) == 0) | (ki >= N_SIDE)[None])
          & (((edge & 2) == 0) | (ki < Q_BLOCK + N_SIDE)[None]))
    return jnp.where(ok[:, None], bias[None], NEG_INF)


def _attention(qkv, biases):
    slab = pltpu.VMEM((GROUP_WIDTH // LANES, SEQ, LANES), F32)
    pads = pltpu.VMEM((PAD_WAYS, 2, SEQ + 2 * N_SIDE, GROUP_WIDTH), BF16)
    out = pl.pallas_call(
        _attn_kernel,
        grid=(BATCH,),
        in_specs=[pl.BlockSpec((1,) + a.shape[1:], lambda b: (b, 0, 0, 0)) for a in qkv]
                 + [_full_spec(b) for b in biases],
        out_specs=pl.BlockSpec((1, SEQ, GROUP_WIDTH), lambda b: (b, 0, 0)),
        out_shape=jax.ShapeDtypeStruct((BATCH, SEQ, GROUP_WIDTH), BF16),
        scratch_shapes=[pads, slab, slab, slab],
        compiler_params=_cparams("parallel"),
        name="attn",
    )(*qkv, *biases)
    return out.reshape(TOKENS, GROUP_WIDTH)


def _mixout_kernel(h_ref, attn_ref, u_ref, uprev_ref, unext_ref, gate_ref,
                   pw_ref, ps_ref, wpa_ref, wpp_ref, wo_ref, gffn_ref, wr_ref,
                   out_ref, xn_ref, ri_ref, rw_ref, cnt_ref, runs):
    j = pl.program_id(0) % TILES_PER_SEQ
    y_attn = _dot(attn_ref[...], wpa_ref[...])

    u = u_ref[...]
    prev = jnp.where(j == 0, 0.0, uprev_ref[0])
    nxt = jnp.where(j == TILES_PER_SEQ - 1, 0.0, unext_ref[0])
    ext_rows = ROW_TILE + 2 * POOL_HALO
    runs[:, ext_rows:ext_rows + POOL_HALO, :] = jnp.zeros((2, POOL_HALO, POOL_GROUP_WIDTH), F32)
    pos = j * ROW_TILE + lax.broadcasted_iota(I32, (ROW_TILE, 1), 0)
    mixed = []
    for gi, w in enumerate(POOL_WINDOWS):
        half = w // 2
        sl = slice(gi * POOL_GROUP_WIDTH, (gi + 1) * POOL_GROUP_WIDTH)
        runs[0, 0:POOL_HALO, :] = prev[:, sl]
        runs[0, POOL_HALO:POOL_HALO + ROW_TILE, :] = u[:, sl]
        runs[0, POOL_HALO + ROW_TILE:ext_rows, :] = nxt[:, sl]
        src, span = 0, 1
        while 2 * span < w:
            runs[1 - src, 0:ext_rows, :] = runs[src, 0:ext_rows, :] + runs[src, span:span + ext_rows, :]
            src, span = 1 - src, 2 * span
        lo = POOL_HALO - half
        acc = runs[src, lo:lo + ROW_TILE, :] + runs[src, lo + half:lo + half + ROW_TILE, :]
        cnt = (jnp.minimum(pos + half, SEQ) - jnp.maximum(pos - half, 0)).astype(F32)
        pooled = acc / cnt - u[:, sl]
        mixed.append(_dot(pooled.astype(BF16), pw_ref[gi]) * ps_ref[:, sl])
    y_pool = _dot(jnp.concatenate(mixed, axis=1).astype(BF16), wpp_ref[...])

    y = (gate_ref[:, 0:D_MODEL] * y_attn.astype(BF16)
         + gate_ref[:, D_MODEL:GATE_WIDTH] * y_pool.astype(BF16))
    h = h_ref[...] + _dot(y, wo_ref[...])
    out_ref[...] = h
    _route_rows(h, gffn_ref, wr_ref, xn_ref, ri_ref, rw_ref, cnt_ref)


def _mix_out(h, attn, u, gates, pool_w, pool_scale, w_proj_attn, w_proj_pool, w_out, g_ffn, wr_split):
    halo_blocks = ROW_TILE // POOL_HALO
    u3 = u.reshape(TOKENS // POOL_HALO, POOL_HALO, POOL_WIDTH)
    last = TOKENS // POOL_HALO - 1
    prev_spec = pl.BlockSpec((1, POOL_HALO, POOL_WIDTH),
                             lambda i: (jnp.maximum(i * halo_blocks - 1, 0), 0, 0))
    next_spec = pl.BlockSpec((1, POOL_HALO, POOL_WIDTH),
                             lambda i: (jnp.minimum((i + 1) * halo_blocks, last), 0, 0))
    weights = (pool_w, pool_scale, w_proj_attn, w_proj_pool, w_out, g_ffn, wr_split)
    return pl.pallas_call(
        _mixout_kernel,
        grid=(N_TILES,),
        in_specs=[_row_spec(D_MODEL), _row_spec(GROUP_WIDTH),
                  _row_spec(POOL_WIDTH), prev_spec, next_spec, _row_spec(GATE_WIDTH)]
                 + [_full_spec(w) for w in weights],
        out_specs=[_row_spec(D_MODEL), _row_spec(D_MODEL), _row_spec(ROUTER_LANES),
                   _row_spec(ROUTER_LANES),
                   pl.BlockSpec((1, SUBLANES, ROUTER_LANES), lambda i: (i, 0, 0))],
        out_shape=[jax.ShapeDtypeStruct((TOKENS, D_MODEL), F32),
                   jax.ShapeDtypeStruct((TOKENS, D_MODEL), BF16),
                   jax.ShapeDtypeStruct((TOKENS, ROUTER_LANES), I32),
                   jax.ShapeDtypeStruct((TOKENS, ROUTER_LANES), F32),
                   jax.ShapeDtypeStruct((N_TILES, SUBLANES, ROUTER_LANES), F32)],
        scratch_shapes=[pltpu.VMEM((2, ROW_TILE + 3 * POOL_HALO, POOL_GROUP_WIDTH), F32)],
        compiler_params=_cparams("parallel"),
        name="mixout",
    )(h, attn, u, u3, u3, gates, *weights)


def _route_rows(h, g_ref, wr_ref, xn_ref, ri_ref, rw_ref, cnt_ref):
    xn = _rms(h, g_ref[...])
    hi = xn.astype(BF16)
    xn_ref[...] = hi
    lo = (xn - hi.astype(F32)).astype(BF16)
    both = _dot(hi, wr_ref[...])
    lg = both[:, 0:ROUTER_LANES] + (both[:, ROUTER_LANES:] + _dot(lo, wr_ref[:, 0:ROUTER_LANES]))

    lane = lax.broadcasted_iota(I32, (ROW_TILE, ROUTER_LANES), 1)
    lanef = lane.astype(F32)
    low = jnp.float32(-3.0e38)
    far = jnp.float32(ROUTER_LANES)
    first = lambda hit: jnp.min(jnp.where(hit, lanef, far), axis=-1, keepdims=True)

    is_group = lane < N_EXPERT_GROUPS
    gl = jnp.where(is_group, lg, low)
    gmax = jnp.max(gl, axis=-1, keepdims=True)
    gidx = first(gl == gmax).astype(I32)
    gden = jnp.sum(jnp.where(is_group, jnp.exp(gl - gmax), 0.0), axis=-1, keepdims=True)
    g_p = 1.0 / gden

    in_group = ((lane >= EXPERT_LANE0) & (lane < EXPERT_LANE0 + N_EXPERTS)
                & ((lane - EXPERT_LANE0) // EXPERTS_PER_GROUP == gidx))
    el = jnp.where(in_group, lg, low)
    t1 = jnp.max(el, axis=-1, keepdims=True)
    l1 = first(in_group & (el == t1))
    rest = in_group & (lanef != l1)
    el2 = jnp.where(rest, lg, low)
    t2 = jnp.max(el2, axis=-1, keepdims=True)
    l2 = first(rest & (el2 == t2))
    e2 = jnp.exp(t2 - t1)
    w1 = g_p * (1.0 / (1.0 + e2))
    w2 = g_p * (e2 / (1.0 + e2))

    hit1 = lanef == l1
    hit2 = lanef == l2
    onehot = (hit1 | hit2).astype(BF16)
    ri = lax.broadcasted_iota(I32, (ROW_TILE, ROW_TILE), 0)
    ci = lax.broadcasted_iota(I32, (ROW_TILE, ROW_TILE), 1)
    before = (ci < ri).astype(BF16)
    seen = _dot(before, onehot)
    r1 = jnp.sum(jnp.where(hit1, seen, 0.0), axis=-1, keepdims=True)
    r2 = jnp.sum(jnp.where(hit2, seen, 0.0), axis=-1, keepdims=True)

    packed = jnp.zeros((ROW_TILE, ROUTER_LANES), F32)
    for k, v in enumerate((l1, l2, r1, r2)):
        packed = jnp.where(lane == k, v, packed)
    ri_ref[...] = packed.astype(I32)
    rw_ref[...] = jnp.where(lane == 0, w1, jnp.where(lane == 1, w2, 0.0))
    counts = jnp.sum(onehot.astype(F32), axis=0, keepdims=True)
    cnt_ref[0] = jnp.broadcast_to(counts, (SUBLANES, ROUTER_LANES))


def _slots(ri_ref, off_ref):
    ri = ri_ref[...]
    lane = lax.broadcasted_iota(I32, (ROW_TILE, ROUTER_LANES), 1)
    off = off_ref[0, 0:1, :]
    pick = lambda k: jnp.sum(jnp.where(lane == ri[:, k:k + 1], off, 0.0), axis=-1, keepdims=True)
    return (pick(0) + ri[:, 2:3].astype(F32), pick(1) + ri[:, 3:4].astype(F32))


def _as_rows(cols):
    eye = (lax.broadcasted_iota(I32, (ROW_TILE, ROW_TILE), 0)
           == lax.broadcasted_iota(I32, (ROW_TILE, ROW_TILE), 1))
    return [jnp.sum(jnp.where(eye, c, 0.0), axis=0, keepdims=True) for c in cols]


def _run_copy(tile, e, loff_s, c8_s, dst_s, buf_ref, slot, hbm_ref, sem, to_hbm):
    k = tile * N_EXPERTS + e
    n = pl.multiple_of(c8_s[k], RUN_ALIGN)
    vm = buf_ref.at[slot, pl.ds(pl.multiple_of(loff_s[k], RUN_ALIGN), n)]
    hb = hbm_ref.at[pl.ds(pl.multiple_of(dst_s[k], RUN_ALIGN), n)]
    src, dst = (vm, hb) if to_hbm else (hb, vm)
    return n, pltpu.make_async_copy(src, dst, sem.at[slot])


def _loop(lo, hi, fn, unroll=1):
    def body(e, carry):
        fn(e)
        return carry
    lax.fori_loop(lo, hi, body, 0, unroll=unroll)


def _wait_rows(n, buf_ref, slot, hbm_ref, sem, to_hbm):
    vm = buf_ref.at[slot, pl.ds(0, n)]
    hb = hbm_ref.at[pl.ds(0, n)]
    src, dst = (vm, hb) if to_hbm else (hb, vm)
    pltpu.make_async_copy(src, dst, sem.at[slot]).wait()


def _start(n, cp):
    @pl.when(n > 0)
    def _():
        cp.start()


def _wait(n, cp):
    @pl.when(n > 0)
    def _():
        cp.wait()


def _dispatch_kernel(loff_s, c8_s, dst_s, nslot_s, zdst_s, zcnt_s, nact_s,
                     xn_ref, ri_ref, rw_ref, off_ref, xpad_hbm, sorted_buf, zero_buf, sem, zsem):
    i = pl.program_id(0)
    slot = i % 2
    last = pl.num_programs(0) - 1

    def zero_copy(e):
        n = pl.multiple_of(zcnt_s[e], RUN_ALIGN)
        dst = xpad_hbm.at[pl.ds(pl.multiple_of(zdst_s[e], RUN_ALIGN), n)]
        return n, pltpu.make_async_copy(zero_buf.at[pl.ds(0, n)], dst, zsem)

    def tail_copy(b):
        dst = xpad_hbm.at[pl.ds(pl.multiple_of(b * MOE_BLOCK, MOE_BLOCK), MOE_BLOCK)]
        return pltpu.make_async_copy(zero_buf, dst, zsem)

    @pl.when(i == 0)
    def _():
        zero_buf[...] = jnp.zeros_like(zero_buf)
        _loop(0, N_EXPERTS, lambda e: _start(*zero_copy(e)))
        _loop(nact_s[0], N_MOE_BLOCKS, lambda b: tail_copy(b).start())

    s1, s2 = _slots(ri_ref, off_ref)
    rw = rw_ref[...]
    s1_row, s2_row, w1_row, w2_row = _as_rows([s1, s2, rw[:, 0:1], rw[:, 1:2]])
    xn = xn_ref[...]
    for first, size in SLOT_CHUNKS:
        @pl.when(first < nslot_s[i])
        def _():
            rows = slice(first, first + size)
            lane = lax.broadcasted_iota(I32, (size, LANES), 1)
            srow = (first + lax.broadcasted_iota(I32, (size, ROW_TILE), 0)).astype(F32)
            hit1 = srow == s1_row
            hit2 = srow == s2_row
            xs = _dot((hit1 | hit2).astype(BF16), xn)
            sorted_buf[slot, rows, 0:HALF] = _pack_halves(xs[:, 0:HALF], xs[:, HALF:D_MODEL])
            ws = jnp.sum(jnp.where(hit1, w1_row, 0.0) + jnp.where(hit2, w2_row, 0.0),
                         axis=-1, keepdims=True)
            sorted_buf[slot, rows, HALF:X_WORDS] = jnp.where(lane == 0, _bits(ws), jnp.uint32(0))

    copy = lambda tile, sl: (lambda e: _run_copy(tile, e, loff_s, c8_s, dst_s, sorted_buf, sl,
                                                 xpad_hbm, sem, True))
    mine = copy(i, slot)
    _loop(0, N_EXPERTS, lambda e: _start(*mine(e)), unroll=RUN_UNROLL)
    tile_rows = lambda t: pl.multiple_of(nslot_s[t], RUN_ALIGN)

    @pl.when(i > 0)
    def _():
        _wait_rows(tile_rows(i - 1), sorted_buf, 1 - slot, xpad_hbm, sem, True)

    @pl.when(i == last)
    def _():
        _wait_rows(tile_rows(i), sorted_buf, slot, xpad_hbm, sem, True)
        _loop(0, N_EXPERTS, lambda e: _wait(*zero_copy(e)))
        _loop(nact_s[0], N_MOE_BLOCKS, lambda b: tail_copy(b).wait())


def _dispatch(tables, xn, route_i, route_w):
    tile_row = pl.BlockSpec((1, SUBLANES, ROUTER_LANES), lambda i, *_: (i, 0, 0))
    grid_spec = pltpu.PrefetchScalarGridSpec(
        num_scalar_prefetch=7,
        grid=(N_TILES,),
        in_specs=[_row_spec(D_MODEL), _row_spec(ROUTER_LANES), _row_spec(ROUTER_LANES), tile_row],
        out_specs=pl.BlockSpec(memory_space=pl.ANY),
        scratch_shapes=[pltpu.VMEM((2, MAX_SLOTS, X_WORDS), U32),
                        pltpu.VMEM((MOE_BLOCK, X_WORDS), U32),
                        pltpu.SemaphoreType.DMA((2,)),
                        pltpu.SemaphoreType.DMA(())],
    )
    return pl.pallas_call(
        _dispatch_kernel,
        grid_spec=grid_spec,
        out_shape=jax.ShapeDtypeStruct((PAD_ROWS, X_WORDS), U32),
        compiler_params=_cparams("arbitrary"),
        name="dispatch",
    )(tables["loff"], tables["c8"], tables["dst"], tables["nslot"], tables["zdst"], tables["zcnt"],
      tables["n_active"], xn, route_i, route_w, tables["offrow"])


def _combine_kernel(loff_s, c8_s, dst_s, nslot_s,
                    h_ref, ri_ref, off_ref, g_ref, ypad_hbm, out_ref,
                    ybuf, sem, *, final_norm):
    i = pl.program_id(0)
    slot = i % 2
    last = pl.num_programs(0) - 1
    fetch = lambda tile, sl: (lambda e: _run_copy(tile, e, loff_s, c8_s, dst_s, ybuf, sl,
                                                  ypad_hbm, sem, False))

    @pl.when(i == 0)
    def _():
        ybuf[...] = jnp.zeros_like(ybuf)
        first = fetch(i, slot)
        _loop(0, N_EXPERTS, lambda e: _start(*first(e)), unroll=RUN_UNROLL)

    @pl.when(i < last)
    def _():
        nxt = fetch(i + 1, 1 - slot)
        _loop(0, N_EXPERTS, lambda e: _start(*nxt(e)), unroll=RUN_UNROLL)

    _wait_rows(pl.multiple_of(nslot_s[i], RUN_ALIGN), ybuf, slot, ypad_hbm, sem, False)

    s1, s2 = _slots(ri_ref, off_ref)

    def gathered(first, size):
        scol = (first + lax.broadcasted_iota(I32, (ROW_TILE, size), 1)).astype(F32)
        pick = ((scol == s1) | (scol == s2)).astype(BF16)
        y_hi, y_lo = _unpack_halves(ybuf[slot, first:first + size, :])
        return jnp.concatenate([_dot(pick, y_hi), _dot(pick, y_lo)], axis=1)

    finish = (lambda v: _rms(v, g_ref[...])) if final_norm else (lambda v: v)
    h = h_ref[...] + gathered(0, COMBINE_BASE)
    spill = nslot_s[i] > COMBINE_BASE

    @pl.when(jnp.logical_not(spill))
    def _():
        out_ref[...] = finish(h)

    @pl.when(spill)
    def _():
        out_ref[...] = finish(h + gathered(COMBINE_BASE, MAX_SLOTS - COMBINE_BASE))


def _combine(tables, h, route_i, y_pad, g, final_norm):
    tile_row = pl.BlockSpec((1, SUBLANES, ROUTER_LANES), lambda i, *_: (i, 0, 0))
    grid_spec = pltpu.PrefetchScalarGridSpec(
        num_scalar_prefetch=4,
        grid=(N_TILES,),
        in_specs=[_row_spec(D_MODEL), _row_spec(ROUTER_LANES), tile_row, _full_spec(g),
                  pl.BlockSpec(memory_space=pl.ANY)],
        out_specs=_row_spec(D_MODEL),
        scratch_shapes=[pltpu.VMEM((2, MAX_SLOTS, HALF), U32),
                        pltpu.SemaphoreType.DMA((2,))],
    )
    return pl.pallas_call(
        functools.partial(_combine_kernel, final_norm=final_norm),
        grid_spec=grid_spec,
        out_shape=jax.ShapeDtypeStruct((TOKENS, D_MODEL), F32),
        compiler_params=_cparams("arbitrary"),
        name="combine",
    )(tables["loff"], tables["c8"], tables["dst"], tables["nslot"],
      h, route_i, tables["offrow"], g, y_pad)


def _expert_kernel(be_ref, slot_ref, next_ref, nact_ref, x_ref, wg_hbm, wu_hbm, wd_hbm, y_hbm,
                   wg_f, wu_f, wg_b, wu_b, wd_s, ybuf, zero_buf, sem, ysem, zsem, *, layer):
    i = pl.program_id(0)
    n_active = nact_ref[0]
    staged = ((wg_hbm, wg_f, wg_b), (wu_hbm, wu_f, wu_b))

    def fetch(e, slot):
        copies = [pltpu.make_async_copy(hbm.at[layer, e], f32.at[slot], sem.at[slot, k])
                  for k, (hbm, f32, _) in enumerate(staged)]
        return copies + [pltpu.make_async_copy(wd_hbm.at[e], wd_s.at[slot], sem.at[slot, len(staged)])]

    def block_rows(b):
        return y_hbm.at[pl.ds(pl.multiple_of(b * MOE_BLOCK, MOE_BLOCK), MOE_BLOCK)]

    def put(b):
        return pltpu.make_async_copy(ybuf.at[b % 2], block_rows(b), ysem.at[b % 2])

    def tail_copy(b):
        return pltpu.make_async_copy(zero_buf, block_rows(b), zsem)

    @pl.when(i == 0)
    def _():
        zero_buf[...] = jnp.zeros_like(zero_buf)
        _loop(n_active, N_MOE_BLOCKS, lambda b: tail_copy(b).start())

    @pl.when(i == pl.num_programs(0) - 1)
    def _():
        _loop(n_active, N_MOE_BLOCKS, lambda b: tail_copy(b).wait())

    @pl.when(i < n_active)
    def _():
        e = be_ref[i]
        slot = slot_ref[i]

        def request(ahead, block):
            nxt = next_ref[(ahead - 1) * N_MOE_BLOCKS + block]

            @pl.when(nxt >= 0)
            def _():
                for cp in fetch(nxt, (slot + ahead) % W_STAGES):
                    cp.start()

        @pl.when((i == 0) | (e != be_ref[jnp.maximum(i - 1, 0)]))
        def _():
            @pl.when(i == 0)
            def _():
                for cp in fetch(e, slot):
                    cp.start()
                for ahead in range(1, W_STAGES - 1):
                    request(ahead, i)
            for cp in fetch(e, slot):
                cp.wait()
            request(W_STAGES - 1, i)
            for _, f32, b16 in staged:
                b16[...] = f32[slot].astype(BF16)

        x_hi, x_lo = _unpack_halves(x_ref[:, 0:HALF])
        row_w = lax.bitcast_convert_type(x_ref[:, HALF:HALF + 1], F32)
        gate = _dot(x_hi, wg_b[0:HALF, :]) + _dot(x_lo, wg_b[HALF:D_MODEL, :])
        up = _dot(x_hi, wu_b[0:HALF, :]) + _dot(x_lo, wu_b[HALF:D_MODEL, :])
        hmid = (jax.nn.silu(gate) * up).astype(BF16)
        y = (_dot(hmid, wd_s[slot]) * row_w).astype(BF16).astype(F32)
        ybuf[i % 2] = _pack_halves(y[:, 0:HALF], y[:, HALF:D_MODEL])
        put(i).start()

        @pl.when(i > 0)
        def _():
            put(i - 1).wait()

        @pl.when(i == n_active - 1)
        def _():
            put(i).wait()


def _experts(layer, tables, x_pad, w_gate, w_up, w_down):
    hbm = pl.BlockSpec(memory_space=pl.ANY)
    up_shape, down_shape = (D_MODEL, D_EXPERT), (D_EXPERT, D_MODEL)
    grid_spec = pltpu.PrefetchScalarGridSpec(
        num_scalar_prefetch=4,
        grid=(N_MOE_BLOCKS,),
        in_specs=[pl.BlockSpec((MOE_BLOCK, X_WORDS),
                               lambda i, be, sl, nx, na: (jnp.minimum(i, na[0] - 1), 0)),
                  hbm, hbm, hbm],
        out_specs=hbm,
        scratch_shapes=[pltpu.VMEM((W_STAGES,) + up_shape, F32),
                        pltpu.VMEM((W_STAGES,) + up_shape, F32),
                        pltpu.VMEM(up_shape, BF16), pltpu.VMEM(up_shape, BF16),
                        pltpu.VMEM((W_STAGES,) + down_shape, BF16),
                        pltpu.VMEM((2, MOE_BLOCK, HALF), U32),
                        pltpu.VMEM((MOE_BLOCK, HALF), U32),
                        pltpu.SemaphoreType.DMA((W_STAGES, 3)),
                        pltpu.SemaphoreType.DMA((2,)),
                        pltpu.SemaphoreType.DMA(())],
    )
    return pl.pallas_call(
        functools.partial(_expert_kernel, layer=layer),
        grid_spec=grid_spec,
        out_shape=jax.ShapeDtypeStruct((PAD_ROWS, HALF), U32),
        compiler_params=_cparams("arbitrary"),
        name="experts",
    )(tables["blk_expert"], tables["blk_slot"], tables["blk_next"], tables["n_active"],
      x_pad, w_gate, w_up, w_down)


def _routing_tables(tile_counts):
    cnt = tile_counts[:, 0, EXPERT_LANE0:EXPERT_LANE0 + N_EXPERTS].astype(I32)
    c8 = (cnt + RUN_ALIGN - 1) // RUN_ALIGN * RUN_ALIGN
    loff = jnp.cumsum(c8, axis=1) - c8
    nslot = jnp.sum(c8, axis=1)
    tot = jnp.sum(c8, axis=0)
    padded = (tot + MOE_BLOCK - 1) // MOE_BLOCK * MOE_BLOCK
    end = jnp.cumsum(padded)
    base = end - padded
    dst = base[None, :] + jnp.cumsum(c8, axis=0) - c8
    blk_start = jnp.arange(N_MOE_BLOCKS, dtype=I32) * MOE_BLOCK
    blk_expert = jnp.minimum(jnp.sum((end[None, :] <= blk_start[:, None]).astype(I32), axis=1),
                             N_EXPERTS - 1).astype(I32)
    offrow = jnp.pad(loff.astype(F32),
                     ((0, 0), (EXPERT_LANE0, ROUTER_LANES - EXPERT_LANE0 - N_EXPERTS)))
    offrow = jnp.broadcast_to(offrow[:, None, :], (N_TILES, SUBLANES, ROUTER_LANES))
    experts = jnp.arange(N_EXPERTS, dtype=I32)
    present = padded > 0
    ordinal = jnp.cumsum(present.astype(I32)) - 1
    at_or_after = lax.cummin(jnp.where(present, experts, N_EXPERTS), reverse=True)
    after = jnp.concatenate([at_or_after[1:], jnp.full((1,), N_EXPERTS, I32)])
    hop = jnp.concatenate([after, jnp.full((1,), N_EXPERTS, I32)])
    pick = (blk_expert[:, None] == experts[None, :]).astype(I32)
    blk_slot = jnp.sum(pick * (ordinal % W_STAGES)[None, :], axis=1).astype(I32)
    ahead, blk_next = experts, []
    for _ in range(W_STAGES - 1):
        ahead = jnp.sum((ahead[:, None] == jnp.arange(N_EXPERTS + 1)[None, :]) * hop[None, :], axis=1)
        per_block = jnp.sum(pick * ahead[None, :], axis=1)
        blk_next.append(jnp.where(per_block >= N_EXPERTS, -1, per_block))
    blk_next = jnp.concatenate(blk_next).astype(I32)
    return {
        "blk_slot": blk_slot, "blk_next": blk_next,
        "loff": loff.reshape(-1).astype(I32), "c8": c8.reshape(-1).astype(I32),
        "dst": dst.reshape(-1).astype(I32), "nslot": nslot.astype(I32),
        "zdst": (base + tot).astype(I32), "zcnt": (padded - tot).astype(I32),
        "offrow": offrow, "blk_expert": blk_expert,
        "n_active": (end[-1:] // MOE_BLOCK).astype(I32),
    }


def _split_router_weights(w_router_group, w_router_expert):
    w_e = w_router_expert.transpose(1, 0, 2).reshape(D_MODEL, N_EXPERTS)
    w = jnp.concatenate([w_router_group, w_e], axis=1)
    w = jnp.pad(w, ((0, 0), (0, ROUTER_LANES - w.shape[1])))
    hi = w.astype(BF16)
    lo = (w - hi.astype(F32)).astype(BF16)
    return jnp.concatenate([hi, lo], axis=1)


def _input_weights(w):
    cols = []
    for g in range(N_GROUPS):
        for part in range(3):
            c0 = part * ATTN_WIDTH + g * GROUP_WIDTH
            cols.append(w[:, c0:c0 + GROUP_WIDTH])
    cols.append(w[:, QKV_WIDTH:])
    return jnp.concatenate(cols, axis=1).astype(BF16)


def kernel(x, rel_bias, norm_mix_g, w_in, pool_w, pool_scale, w_proj_attn, w_proj_pool, w_out,
           norm_ffn_g, w_router_group, w_router_expert, w_gate_e, w_up_e, w_down_e, norm_final_g):
    h = x.reshape(TOKENS, D_MODEL)
    biases = [_band_bias(rel_bias, g, d) for g, (_, d) in enumerate(DILATION_PATTERNS)]
    for l in range(DEPTH):
        qkv, u, gates, w_down_b = _project(l, h, norm_mix_g[l][None], _input_weights(w_in[l]), w_down_e)
        h, xn, route_i, route_w, tile_counts = _mix_out(
            h, _attention(qkv, biases), u, gates, pool_w[l].astype(BF16), pool_scale[l][None],
            w_proj_attn[l].astype(BF16), w_proj_pool[l].astype(BF16), w_out[l].astype(BF16),
            norm_ffn_g[l][None], _split_router_weights(w_router_group[l], w_router_expert[l]))
        tables = _routing_tables(tile_counts)
        x_pad = _dispatch(tables, xn, route_i, route_w)
        y_pad = _experts(l, tables, x_pad, w_gate_e, w_up_e, w_down_b)
        h = _combine(tables, h, route_i, y_pad, norm_final_g[None], l == DEPTH - 1)
    return h.reshape(BATCH, SEQ, D_MODEL)
```

```python
import functools
import math

import jax
import jax.numpy as jnp
from jax import lax
from jax.experimental import pallas as pl
from jax.experimental.pallas import tpu as pltpu

F32 = jnp.float32
BF16 = jnp.bfloat16
I32 = jnp.int32
U32 = jnp.uint32

D_MODEL = 1024
BATCH = 8
SEQ = 2048
TOKENS = BATCH * SEQ
DEPTH = 2

HEAD_DIM = 64
HEADS_PER_GROUP = 4
GROUP_WIDTH = HEADS_PER_GROUP * HEAD_DIM
DILATION_PATTERNS = ((128, 1), (512, 4), (2048, 16))
N_GROUPS = len(DILATION_PATTERNS)
N_ATTN_HEADS = N_GROUPS * HEADS_PER_GROUP
ATTN_WIDTH = N_ATTN_HEADS * HEAD_DIM
QKV_WIDTH = 3 * ATTN_WIDTH
GROUP_QKV = 3 * GROUP_WIDTH
N_SIDE = 64
assert all(w // (2 * d) == N_SIDE for w, d in DILATION_PATTERNS)
POOL_WINDOWS = (2, 4, 8, 16)
POOL_GROUP_WIDTH = 128
POOL_WIDTH = len(POOL_WINDOWS) * POOL_GROUP_WIDTH
POOL_HALO = max(POOL_WINDOWS) // 2
N_BRANCHES = 2
GATE_WIDTH = N_BRANCHES * D_MODEL
IN_WIDTH = QKV_WIDTH + POOL_WIDTH + GATE_WIDTH
N_BUCKETS = 32
MAX_DISTANCE = 1024
N_EXPERT_GROUPS = 8
EXPERTS_PER_GROUP = 8
N_EXPERTS = N_EXPERT_GROUPS * EXPERTS_PER_GROUP
TOP_K = 2
D_EXPERT = 512
N_ASSIGN = TOKENS * TOP_K
EPS = 1e-6
NEG_INF = -1e30
LOG2_E = math.log2(math.e)
Q_SCALE = HEAD_DIM ** -0.5 * LOG2_E

LANES = 128
SUBLANES = 8
ROW_TILE = 512
N_TILES = TOKENS // ROW_TILE
TILES_PER_SEQ = SEQ // ROW_TILE
Q_BLOCK = 128
K_BLOCK = Q_BLOCK + 2 * N_SIDE
PAD_WAYS = 4
BLOCKS_IN_FLIGHT = 4
ROUTER_LANES = 128
EXPERT_LANE0 = N_EXPERT_GROUPS
VMEM_LIMIT = 56 * 1024 * 1024

RUN_ALIGN = SUBLANES
MOE_BLOCK = 336
HALF = D_MODEL // 2
X_WORDS = HALF + LANES
SLOT_CHUNK = 256
RUN_UNROLL = 4
W_STAGES = 2
MAX_SLOTS = -(-(TOP_K * ROW_TILE + N_EXPERTS * (RUN_ALIGN - 1)) // SLOT_CHUNK) * SLOT_CHUNK
SLOT_CHUNKS = (tuple((s, ROW_TILE) for s in range(0, TOP_K * ROW_TILE, ROW_TILE))
               + tuple((s, SLOT_CHUNK) for s in range(TOP_K * ROW_TILE, MAX_SLOTS, SLOT_CHUNK)))
COMBINE_BASE = MAX_SLOTS - SLOT_CHUNK
N_MOE_BLOCKS = -(-(N_ASSIGN + N_TILES * N_EXPERTS * (RUN_ALIGN - 1)
                   + N_EXPERTS * (MOE_BLOCK - RUN_ALIGN)) // MOE_BLOCK)
PAD_ROWS = N_MOE_BLOCKS * MOE_BLOCK
HIGH_HALF = 0xFFFF0000


def _cparams(*sem):
    return pltpu.CompilerParams(dimension_semantics=sem, vmem_limit_bytes=VMEM_LIMIT)


def _rms(h, g):
    r = lax.rsqrt(jnp.mean(h * h, axis=-1, keepdims=True) + EPS)
    return (h * r) * g


def _dot(a, b):
    return jnp.dot(a, b, preferred_element_type=F32)


def _row_spec(width):
    return pl.BlockSpec((ROW_TILE, width), lambda i, *_: (i, 0))


def _full_spec(a):
    return pl.BlockSpec(a.shape, lambda i, *_: (0,) * a.ndim)


def _bits(x):
    return lax.bitcast_convert_type(x, U32)


def _pack_halves(a, b):
    return (_bits(a) & jnp.uint32(HIGH_HALF)) | (_bits(b) >> 16)


def _unpack_halves(words):
    hi = lax.bitcast_convert_type(words & jnp.uint32(HIGH_HALF), F32)
    lo = lax.bitcast_convert_type(words << 16, F32)
    return hi.astype(BF16), lo.astype(BF16)


def _proj_kernel(h_ref, g_ref, w_ref, wd_ref, q0_ref, q1_ref, q2_ref, u_ref, gate_ref, wdb_ref, slabs):
    wdb_ref[...] = wd_ref[...].astype(BF16)
    xn = _rms(h_ref[...], g_ref[...]).astype(BF16)
    n_slabs = GROUP_QKV // LANES
    for g, out_ref in enumerate((q0_ref, q1_ref, q2_ref)):
        dilation = DILATION_PATTERNS[g][1]
        res = _dot(xn, w_ref[:, g * GROUP_QKV:(g + 1) * GROUP_QKV])
        res = jnp.concatenate([res[:, 0:GROUP_WIDTH] * Q_SCALE, res[:, GROUP_WIDTH:]], axis=1)
        if dilation == 1:
            out_ref[0, 0] = res.astype(BF16)
            continue
        for s in range(n_slabs):
            slabs[s] = res[:, s * LANES:(s + 1) * LANES]
        n = ROW_TILE // dilation
        for r in range(dilation):
            rows = [slabs[s, pl.ds(r, n, stride=dilation), :] for s in range(n_slabs)]
            out_ref[0, r] = jnp.concatenate(rows, axis=1).astype(BF16)
    u_ref[...] = _dot(xn, w_ref[:, QKV_WIDTH:QKV_WIDTH + POOL_WIDTH])
    gates = _dot(xn, w_ref[:, QKV_WIDTH + POOL_WIDTH:IN_WIDTH])
    gate_ref[...] = jax.nn.sigmoid(gates).astype(BF16)


def _project(layer, h, g, w_bf16, w_down):
    qkv_shapes, qkv_specs = [], []
    for _, d in DILATION_PATTERNS:
        qkv_shapes.append(jax.ShapeDtypeStruct((BATCH, d, SEQ // d, GROUP_QKV), BF16))
        qkv_specs.append(pl.BlockSpec((1, d, ROW_TILE // d, GROUP_QKV),
                                      lambda i: (i // TILES_PER_SEQ, 0, i % TILES_PER_SEQ, 0)))
    layer_rows = N_EXPERTS * D_EXPERT
    step_rows = layer_rows // N_TILES
    res = pl.pallas_call(
        _proj_kernel,
        grid=(N_TILES,),
        in_specs=[_row_spec(D_MODEL), _full_spec(g), _full_spec(w_bf16),
                  pl.BlockSpec((step_rows, D_MODEL), lambda i: (layer * N_TILES + i, 0))],
        out_specs=qkv_specs + [_row_spec(POOL_WIDTH), _row_spec(GATE_WIDTH),
                               pl.BlockSpec((step_rows, D_MODEL), lambda i: (i, 0))],
        out_shape=qkv_shapes + [jax.ShapeDtypeStruct((TOKENS, POOL_WIDTH), F32),
                                jax.ShapeDtypeStruct((TOKENS, GATE_WIDTH), BF16),
                                jax.ShapeDtypeStruct((layer_rows, D_MODEL), BF16)],
        scratch_shapes=[pltpu.VMEM((GROUP_QKV // LANES, ROW_TILE, LANES), F32)],
        compiler_params=_cparams("parallel"),
        name="proj",
    )(h, g, w_bf16, w_down.reshape(DEPTH * layer_rows, D_MODEL))
    w_down_b = res[N_GROUPS + 2].reshape(N_EXPERTS, D_EXPERT, D_MODEL)
    return res[:N_GROUPS], res[N_GROUPS], res[N_GROUPS + 1], w_down_b


def _attn_kernel(q0_ref, q1_ref, q2_ref, b0_ref, b1_ref, b2_ref, o_ref, pads, num, den, top):
    zpad = jnp.zeros((N_SIDE, GROUP_WIDTH), BF16)
    groups = ((q2_ref, b2_ref, 2), (q1_ref, b1_ref, 1), (q0_ref, b0_ref, 0))
    for order, (qkv_ref, bias_ref, g) in enumerate(groups):
        dilation = DILATION_PATTERNS[g][1]
        sub_len = SEQ // dilation
        for way in range(PAD_WAYS):
            for kv in range(2):
                pads[way, kv, 0:N_SIDE, :] = zpad
                pads[way, kv, N_SIDE + sub_len:2 * N_SIDE + sub_len, :] = zpad
        _attn_group(qkv_ref, bias_ref, pads, num, den, top, dilation=dilation, first=order == 0)

    def finish(c, carry):
        rows = pl.ds(pl.multiple_of(c * ROW_TILE, ROW_TILE), ROW_TILE)
        merged = [num[half, rows, :] / den[half, rows, :] for half in range(GROUP_WIDTH // LANES)]
        o_ref[0, rows, :] = jnp.concatenate(merged, axis=1).astype(BF16)
        return carry

    lax.fori_loop(0, SEQ // ROW_TILE, finish, 0)


def _attn_group(qkv_ref, bias_ref, pads, num, den, top, *, dilation, first):
    sub_len = SEQ // dilation
    n_blocks = sub_len // Q_BLOCK
    head_of_lane = lax.broadcasted_iota(I32, (1, GROUP_WIDTH), 1) // HEAD_DIM

    def by_head(cols):
        out = cols[HEADS_PER_GROUP - 1]
        for h in range(HEADS_PER_GROUP - 2, -1, -1):
            out = jnp.where(head_of_lane == h, cols[h], out)
        return out

    def load_keys(r, way):
        pads[way, 0, N_SIDE:N_SIDE + sub_len, :] = qkv_ref[0, r, :, GROUP_WIDTH:2 * GROUP_WIDTH]
        pads[way, 1, N_SIDE:N_SIDE + sub_len, :] = qkv_ref[0, r, :, 2 * GROUP_WIDTH:3 * GROUP_WIDTH]

    def block(r, way, i):
        static = isinstance(i, int)
        r0 = i * Q_BLOCK if static else pl.multiple_of(i * Q_BLOCK, Q_BLOCK)
        qb = qkv_ref[0, r, pl.ds(r0, Q_BLOCK), 0:GROUP_WIDTH]
        kw = pads[way, 0, pl.ds(r0, K_BLOCK), :]
        vw = pads[way, 1, pl.ds(r0, K_BLOCK), :]
        is_first, is_last = i == 0, i == n_blocks - 1
        edge = (int(is_first) + 2 * int(is_last) if static
                else is_first.astype(I32) + 2 * is_last.astype(I32))
        zero = jnp.zeros_like(qb)
        q_heads = jnp.concatenate(
            [jnp.where(head_of_lane == h, qb, zero) for h in range(HEADS_PER_GROUP)], axis=0)
        s = lax.dot_general(q_heads, kw, (((1,), (1,)), ((), ())), preferred_element_type=F32)
        s = s.reshape(HEADS_PER_GROUP, Q_BLOCK, K_BLOCK) + bias_ref[edge]
        m = jnp.max(s, axis=-1, keepdims=True)
        p = jnp.exp2(s - m)
        psum = jnp.sum(p, axis=-1, keepdims=True)
        pb = p.astype(BF16)
        p_cat = jnp.concatenate([pb[h] for h in range(HEADS_PER_GROUP)], axis=1)
        zv = jnp.zeros_like(vw)
        v_heads = jnp.concatenate(
            [jnp.where(head_of_lane == h, vw, zv) for h in range(HEADS_PER_GROUP)], axis=0)
        o = _dot(p_cat, v_heads)
        lse = m + jnp.log2(psum)
        o = o / by_head([psum[h] for h in range(HEADS_PER_GROUP)])
        lse_lanes = by_head([lse[h] for h in range(HEADS_PER_GROUP)])
        if dilation == 1:
            rows = pl.ds(r0, Q_BLOCK)
        else:
            rows = pl.ds(r + dilation * r0, Q_BLOCK, stride=dilation)
        for half in range(GROUP_WIDTH // LANES):
            lanes = slice(half * LANES, (half + 1) * LANES)
            if first:
                num[half, rows, :] = o[:, lanes]
                den[half, rows, :] = jnp.ones((Q_BLOCK, LANES), F32)
                top[half, rows, :] = lse_lanes[:, lanes]
            else:
                old = top[half, rows, :]
                new = jnp.maximum(old, lse_lanes[:, lanes])
                keep = jnp.exp2(old - new)
                add = jnp.exp2(lse_lanes[:, lanes] - new)
                num[half, rows, :] = keep * num[half, rows, :] + add * o[:, lanes]
                den[half, rows, :] = keep * den[half, rows, :] + add
                top[half, rows, :] = new

    def blocks_of(r, way):
        if n_blocks == 1:
            block(r, way, 0)
        else:
            assert n_blocks % BLOCKS_IN_FLIGHT == 0

            def several(j, carry):
                for k in range(BLOCKS_IN_FLIGHT):
                    block(r, way, BLOCKS_IN_FLIGHT * j + k)
                return carry
            lax.fori_loop(0, n_blocks // BLOCKS_IN_FLIGHT, several, 0)

    if dilation == 1:
        load_keys(0, 0)
        blocks_of(0, 0)
    else:
        ways = min(PAD_WAYS, dilation, max(1, 2 * BLOCKS_IN_FLIGHT // n_blocks))

        def residues(j, carry):
            for way in range(ways):
                load_keys(ways * j + way, way)
            for way in range(ways):
                blocks_of(ways * j + way, way)
            return carry
        lax.fori_loop(0, dilation // ways, residues, 0)


def _t5_bucket(rel):
    nb = N_BUCKETS // 2
    ret = jnp.where(rel > 0, nb, 0)
    n = jnp.abs(rel)
    max_exact = nb // 2
    nf = jnp.maximum(n, max_exact).astype(F32)
    large = max_exact + (jnp.log(nf / max_exact) / math.log(MAX_DISTANCE / max_exact)
                         * (nb - max_exact)).astype(I32)
    large = jnp.minimum(large, nb - 1)
    return ret + jnp.where(n < max_exact, n, large)


def _band_bias(rel_bias, group, dilation):
    qi = jnp.arange(Q_BLOCK)[:, None]
    ki = jnp.arange(K_BLOCK)[None, :]
    bucket = _t5_bucket((ki - N_SIDE - qi) * dilation)
    tab = rel_bias[:, group * HEADS_PER_GROUP:(group + 1) * HEADS_PER_GROUP]
    onehot = (bucket[:, :, None] == jnp.arange(N_BUCKETS)[None, None, :]).astype(F32)
    bias = jnp.einsum('qkb,bh->hqk', onehot, tab, precision=lax.Precision.HIGHEST).astype(F32)
    bias = bias * LOG2_E
    edge = jnp.arange(4)[:, None, None]
    ok = ((jnp.abs(ki - N_SIDE - qi) <= N_SIDE)[None]
          & (((edge & 1) == 0) | (ki >= N_SIDE)[None])
          & (((edge & 2) == 0) | (ki < Q_BLOCK + N_SIDE)[None]))
    return jnp.where(ok[:, None], bias[None], NEG_INF)


def _attention(qkv, biases):
    slab = pltpu.VMEM((GROUP_WIDTH // LANES, SEQ, LANES), F32)
    pads = pltpu.VMEM((PAD_WAYS, 2, SEQ + 2 * N_SIDE, GROUP_WIDTH), BF16)
    out = pl.pallas_call(
        _attn_kernel,
        grid=(BATCH,),
        in_specs=[pl.BlockSpec((1,) + a.shape[1:], lambda b: (b, 0, 0, 0)) for a in qkv]
                 + [_full_spec(b) for b in biases],
        out_specs=pl.BlockSpec((1, SEQ, GROUP_WIDTH), lambda b: (b, 0, 0)),
        out_shape=jax.ShapeDtypeStruct((BATCH, SEQ, GROUP_WIDTH), BF16),
        scratch_shapes=[pads, slab, slab, slab],
        compiler_params=_cparams("parallel"),
        name="attn",
    )(*qkv, *biases)
    return out.reshape(TOKENS, GROUP_WIDTH)


def _mixout_kernel(h_ref, attn_ref, u_ref, uprev_ref, unext_ref, gate_ref,
                   pw_ref, ps_ref, wpa_ref, wpp_ref, wo_ref, gffn_ref, wr_ref,
                   out_ref, xn_ref, ri_ref, rw_ref, cnt_ref, runs):
    j = pl.program_id(0) % TILES_PER_SEQ
    y_attn = _dot(attn_ref[...], wpa_ref[...])

    u = u_ref[...]
    prev = jnp.where(j == 0, 0.0, uprev_ref[0])
    nxt = jnp.where(j == TILES_PER_SEQ - 1, 0.0, unext_ref[0])
    ext_rows = ROW_TILE + 2 * POOL_HALO
    runs[:, ext_rows:ext_rows + POOL_HALO, :] = jnp.zeros((2, POOL_HALO, POOL_GROUP_WIDTH), F32)
    pos = j * ROW_TILE + lax.broadcasted_iota(I32, (ROW_TILE, 1), 0)
    mixed = []
    for gi, w in enumerate(POOL_WINDOWS):
        half = w // 2
        sl = slice(gi * POOL_GROUP_WIDTH, (gi + 1) * POOL_GROUP_WIDTH)
        runs[0, 0:POOL_HALO, :] = prev[:, sl]
        runs[0, POOL_HALO:POOL_HALO + ROW_TILE, :] = u[:, sl]
        runs[0, POOL_HALO + ROW_TILE:ext_rows, :] = nxt[:, sl]
        src, span = 0, 1
        while 2 * span < w:
            runs[1 - src, 0:ext_rows, :] = runs[src, 0:ext_rows, :] + runs[src, span:span + ext_rows, :]
            src, span = 1 - src, 2 * span
        lo = POOL_HALO - half
        acc = runs[src, lo:lo + ROW_TILE, :] + runs[src, lo + half:lo + half + ROW_TILE, :]
        cnt = (jnp.minimum(pos + half, SEQ) - jnp.maximum(pos - half, 0)).astype(F32)
        pooled = acc / cnt - u[:, sl]
        mixed.append(_dot(pooled.astype(BF16), pw_ref[gi]) * ps_ref[:, sl])
    y_pool = _dot(jnp.concatenate(mixed, axis=1).astype(BF16), wpp_ref[...])

    y = (gate_ref[:, 0:D_MODEL] * y_attn.astype(BF16)
         + gate_ref[:, D_MODEL:GATE_WIDTH] * y_pool.astype(BF16))
    h = h_ref[...] + _dot(y, wo_ref[...])
    out_ref[...] = h
    _route_rows(h, gffn_ref, wr_ref, xn_ref, ri_ref, rw_ref, cnt_ref)


def _mix_out(h, attn, u, gates, pool_w, pool_scale, w_proj_attn, w_proj_pool, w_out, g_ffn, wr_split):
    halo_blocks = ROW_TILE // POOL_HALO
    u3 = u.reshape(TOKENS // POOL_HALO, POOL_HALO, POOL_WIDTH)
    last = TOKENS // POOL_HALO - 1
    prev_spec = pl.BlockSpec((1, POOL_HALO, POOL_WIDTH),
                             lambda i: (jnp.maximum(i * halo_blocks - 1, 0), 0, 0))
    next_spec = pl.BlockSpec((1, POOL_HALO, POOL_WIDTH),
                             lambda i: (jnp.minimum((i + 1) * halo_blocks, last), 0, 0))
    weights = (pool_w, pool_scale, w_proj_attn, w_proj_pool, w_out, g_ffn, wr_split)
    return pl.pallas_call(
        _mixout_kernel,
        grid=(N_TILES,),
        in_specs=[_row_spec(D_MODEL), _row_spec(GROUP_WIDTH),
                  _row_spec(POOL_WIDTH), prev_spec, next_spec, _row_spec(GATE_WIDTH)]
                 + [_full_spec(w) for w in weights],
        out_specs=[_row_spec(D_MODEL), _row_spec(D_MODEL), _row_spec(ROUTER_LANES),
                   _row_spec(ROUTER_LANES),
                   pl.BlockSpec((1, SUBLANES, ROUTER_LANES), lambda i: (i, 0, 0))],
        out_shape=[jax.ShapeDtypeStruct((TOKENS, D_MODEL), F32),
                   jax.ShapeDtypeStruct((TOKENS, D_MODEL), BF16),
                   jax.ShapeDtypeStruct((TOKENS, ROUTER_LANES), I32),
                   jax.ShapeDtypeStruct((TOKENS, ROUTER_LANES), F32),
                   jax.ShapeDtypeStruct((N_TILES, SUBLANES, ROUTER_LANES), F32)],
        scratch_shapes=[pltpu.VMEM((2, ROW_TILE + 3 * POOL_HALO, POOL_GROUP_WIDTH), F32)],
        compiler_params=_cparams("parallel"),
        name="mixout",
    )(h, attn, u, u3, u3, gates, *weights)


def _route_rows(h, g_ref, wr_ref, xn_ref, ri_ref, rw_ref, cnt_ref):
    xn = _rms(h, g_ref[...])
    hi = xn.astype(BF16)
    xn_ref[...] = hi
    lo = (xn - hi.astype(F32)).astype(BF16)
    both = _dot(hi, wr_ref[...])
    lg = both[:, 0:ROUTER_LANES] + (both[:, ROUTER_LANES:] + _dot(lo, wr_ref[:, 0:ROUTER_LANES]))

    lane = lax.broadcasted_iota(I32, (ROW_TILE, ROUTER_LANES), 1)
    lanef = lane.astype(F32)
    low = jnp.float32(-3.0e38)
    far = jnp.float32(ROUTER_LANES)
    first = lambda hit: jnp.min(jnp.where(hit, lanef, far), axis=-1, keepdims=True)

    is_group = lane < N_EXPERT_GROUPS
    gl = jnp.where(is_group, lg, low)
    gmax = jnp.max(gl, axis=-1, keepdims=True)
    gidx = first(gl == gmax).astype(I32)
    gden = jnp.sum(jnp.where(is_group, jnp.exp(gl - gmax), 0.0), axis=-1, keepdims=True)
    g_p = 1.0 / gden

    in_group = ((lane >= EXPERT_LANE0) & (lane < EXPERT_LANE0 + N_EXPERTS)
                & ((lane - EXPERT_LANE0) // EXPERTS_PER_GROUP == gidx))
    el = jnp.where(in_group, lg, low)
    t1 = jnp.max(el, axis=-1, keepdims=True)
    l1 = first(in_group & (el == t1))
    rest = in_group & (lanef != l1)
    el2 = jnp.where(rest, lg, low)
    t2 = jnp.max(el2, axis=-1, keepdims=True)
    l2 = first(rest & (el2 == t2))
    e2 = jnp.exp(t2 - t1)
    w1 = g_p * (1.0 / (1.0 + e2))
    w2 = g_p * (e2 / (1.0 + e2))

    hit1 = lanef == l1
    hit2 = lanef == l2
    onehot = (hit1 | hit2).astype(BF16)
    ri = lax.broadcasted_iota(I32, (ROW_TILE, ROW_TILE), 0)
    ci = lax.broadcasted_iota(I32, (ROW_TILE, ROW_TILE), 1)
    before = (ci < ri).astype(BF16)
    seen = _dot(before, onehot)
    r1 = jnp.sum(jnp.where(hit1, seen, 0.0), axis=-1, keepdims=True)
    r2 = jnp.sum(jnp.where(hit2, seen, 0.0), axis=-1, keepdims=True)

    packed = jnp.zeros((ROW_TILE, ROUTER_LANES), F32)
    for k, v in enumerate((l1, l2, r1, r2)):
        packed = jnp.where(lane == k, v, packed)
    ri_ref[...] = packed.astype(I32)
    rw_ref[...] = jnp.where(lane == 0, w1, jnp.where(lane == 1, w2, 0.0))
    counts = jnp.sum(onehot.astype(F32), axis=0, keepdims=True)
    cnt_ref[0] = jnp.broadcast_to(counts, (SUBLANES, ROUTER_LANES))


def _slots(ri_ref, off_ref):
    ri = ri_ref[...]
    lane = lax.broadcasted_iota(I32, (ROW_TILE, ROUTER_LANES), 1)
    off = off_ref[0, 0:1, :]
    pick = lambda k: jnp.sum(jnp.where(lane == ri[:, k:k + 1], off, 0.0), axis=-1, keepdims=True)
    return (pick(0) + ri[:, 2:3].astype(F32), pick(1) + ri[:, 3:4].astype(F32))


def _as_rows(cols):
    eye = (lax.broadcasted_iota(I32, (ROW_TILE, ROW_TILE), 0)
           == lax.broadcasted_iota(I32, (ROW_TILE, ROW_TILE), 1))
    return [jnp.sum(jnp.where(eye, c, 0.0), axis=0, keepdims=True) for c in cols]


def _run_copy(tile, e, loff_s, c8_s, dst_s, buf_ref, slot, hbm_ref, sem, to_hbm):
    k = tile * N_EXPERTS + e
    n = pl.multiple_of(c8_s[k], RUN_ALIGN)
    vm = buf_ref.at[slot, pl.ds(pl.multiple_of(loff_s[k], RUN_ALIGN), n)]
    hb = hbm_ref.at[pl.ds(pl.multiple_of(dst_s[k], RUN_ALIGN), n)]
    src, dst = (vm, hb) if to_hbm else (hb, vm)
    return n, pltpu.make_async_copy(src, dst, sem.at[slot])


def _loop(lo, hi, fn, unroll=1):
    def body(e, carry):
        fn(e)
        return carry
    lax.fori_loop(lo, hi, body, 0, unroll=unroll)


def _wait_rows(n, buf_ref, slot, hbm_ref, sem, to_hbm):
    vm = buf_ref.at[slot, pl.ds(0, n)]
    hb = hbm_ref.at[pl.ds(0, n)]
    src, dst = (vm, hb) if to_hbm else (hb, vm)
    pltpu.make_async_copy(src, dst, sem.at[slot]).wait()


def _start(n, cp):
    @pl.when(n > 0)
    def _():
        cp.start()


def _wait(n, cp):
    @pl.when(n > 0)
    def _():
        cp.wait()


def _dispatch_kernel(loff_s, c8_s, dst_s, nslot_s, zdst_s, zcnt_s, nact_s,
                     xn_ref, ri_ref, rw_ref, off_ref, *rest, recycled):
    xpad_hbm, sorted_buf, zero_buf, sem, zsem = rest[1:] if recycled else rest
    i = pl.program_id(0)
    slot = i % 2
    last = pl.num_programs(0) - 1

    def zero_copy(e):
        n = pl.multiple_of(zcnt_s[e], RUN_ALIGN)
        dst = xpad_hbm.at[pl.ds(pl.multiple_of(zdst_s[e], RUN_ALIGN), n)]
        return n, pltpu.make_async_copy(zero_buf.at[pl.ds(0, n)], dst, zsem)

    def tail_copy(b):
        dst = xpad_hbm.at[pl.ds(pl.multiple_of(b * MOE_BLOCK, MOE_BLOCK), MOE_BLOCK)]
        return pltpu.make_async_copy(zero_buf, dst, zsem)

    @pl.when(i == 0)
    def _():
        zero_buf[...] = jnp.zeros_like(zero_buf)
        _loop(0, N_EXPERTS, lambda e: _start(*zero_copy(e)))
        if not recycled:
            _loop(nact_s[0], N_MOE_BLOCKS, lambda b: tail_copy(b).start())

    s1, s2 = _slots(ri_ref, off_ref)
    rw = rw_ref[...]
    s1_row, s2_row, w1_row, w2_row = _as_rows([s1, s2, rw[:, 0:1], rw[:, 1:2]])
    xn = xn_ref[...]
    for first, size in SLOT_CHUNKS:
        @pl.when(first < nslot_s[i])
        def _():
            rows = slice(first, first + size)
            lane = lax.broadcasted_iota(I32, (size, LANES), 1)
            srow = (first + lax.broadcasted_iota(I32, (size, ROW_TILE), 0)).astype(F32)
            hit1 = srow == s1_row
            hit2 = srow == s2_row
            xs = _dot((hit1 | hit2).astype(BF16), xn)
            sorted_buf[slot, rows, 0:HALF] = _pack_halves(xs[:, 0:HALF], xs[:, HALF:D_MODEL])
            ws = jnp.sum(jnp.where(hit1, w1_row, 0.0) + jnp.where(hit2, w2_row, 0.0),
                         axis=-1, keepdims=True)
            sorted_buf[slot, rows, HALF:X_WORDS] = jnp.where(lane == 0, _bits(ws), jnp.uint32(0))

    copy = lambda tile, sl: (lambda e: _run_copy(tile, e, loff_s, c8_s, dst_s, sorted_buf, sl,
                                                 xpad_hbm, sem, True))
    mine = copy(i, slot)
    _loop(0, N_EXPERTS, lambda e: _start(*mine(e)), unroll=RUN_UNROLL)
    tile_rows = lambda t: pl.multiple_of(nslot_s[t], RUN_ALIGN)

    @pl.when(i > 0)
    def _():
        _wait_rows(tile_rows(i - 1), sorted_buf, 1 - slot, xpad_hbm, sem, True)

    @pl.when(i == last)
    def _():
        _wait_rows(tile_rows(i), sorted_buf, slot, xpad_hbm, sem, True)
        _loop(0, N_EXPERTS, lambda e: _wait(*zero_copy(e)))
        if not recycled:
            _loop(nact_s[0], N_MOE_BLOCKS, lambda b: tail_copy(b).wait())


def _dispatch(tables, xn, route_i, route_w, old_rows=None):
    tile_row = pl.BlockSpec((1, SUBLANES, ROUTER_LANES), lambda i, *_: (i, 0, 0))
    hbm = pl.BlockSpec(memory_space=pl.ANY)
    recycled = old_rows is not None
    scalars = (tables["loff"], tables["c8"], tables["dst"], tables["nslot"], tables["zdst"],
               tables["zcnt"], tables["n_active"])
    operands = (xn, route_i, route_w, tables["offrow"]) + ((old_rows,) if recycled else ())
    grid_spec = pltpu.PrefetchScalarGridSpec(
        num_scalar_prefetch=len(scalars),
        grid=(N_TILES,),
        in_specs=[_row_spec(D_MODEL), _row_spec(ROUTER_LANES), _row_spec(ROUTER_LANES), tile_row]
                 + [hbm] * recycled,
        out_specs=hbm,
        scratch_shapes=[pltpu.VMEM((2, MAX_SLOTS, X_WORDS), U32),
                        pltpu.VMEM((MOE_BLOCK, X_WORDS), U32),
                        pltpu.SemaphoreType.DMA((2,)),
                        pltpu.SemaphoreType.DMA(())],
    )
    return pl.pallas_call(
        functools.partial(_dispatch_kernel, recycled=recycled),
        grid_spec=grid_spec,
        out_shape=jax.ShapeDtypeStruct((PAD_ROWS, X_WORDS), U32),
        input_output_aliases={len(scalars) + len(operands) - 1: 0} if recycled else {},
        compiler_params=_cparams("arbitrary"),
        name="dispatch",
    )(*scalars, *operands)


def _combine_kernel(loff_s, c8_s, dst_s, nslot_s,
                    h_ref, ri_ref, off_ref, g_ref, ypad_hbm, out_ref,
                    ybuf, sem, *, final_norm):
    i = pl.program_id(0)
    slot = i % 2
    last = pl.num_programs(0) - 1
    fetch = lambda tile, sl: (lambda e: _run_copy(tile, e, loff_s, c8_s, dst_s, ybuf, sl,
                                                  ypad_hbm, sem, False))

    @pl.when(i == 0)
    def _():
        ybuf[...] = jnp.zeros_like(ybuf)
        first = fetch(i, slot)
        _loop(0, N_EXPERTS, lambda e: _start(*first(e)), unroll=RUN_UNROLL)

    @pl.when(i < last)
    def _():
        nxt = fetch(i + 1, 1 - slot)
        _loop(0, N_EXPERTS, lambda e: _start(*nxt(e)), unroll=RUN_UNROLL)

    _wait_rows(pl.multiple_of(nslot_s[i], RUN_ALIGN), ybuf, slot, ypad_hbm, sem, False)

    s1, s2 = _slots(ri_ref, off_ref)

    def gathered(first, size):
        scol = (first + lax.broadcasted_iota(I32, (ROW_TILE, size), 1)).astype(F32)
        pick = ((scol == s1) | (scol == s2)).astype(BF16)
        y_hi, y_lo = _unpack_halves(ybuf[slot, first:first + size, :])
        return jnp.concatenate([_dot(pick, y_hi), _dot(pick, y_lo)], axis=1)

    finish = (lambda v: _rms(v, g_ref[...])) if final_norm else (lambda v: v)
    h = h_ref[...] + gathered(0, COMBINE_BASE)
    spill = nslot_s[i] > COMBINE_BASE

    @pl.when(jnp.logical_not(spill))
    def _():
        out_ref[...] = finish(h)

    @pl.when(spill)
    def _():
        out_ref[...] = finish(h + gathered(COMBINE_BASE, MAX_SLOTS - COMBINE_BASE))


def _combine(tables, h, route_i, y_pad, g, final_norm):
    tile_row = pl.BlockSpec((1, SUBLANES, ROUTER_LANES), lambda i, *_: (i, 0, 0))
    grid_spec = pltpu.PrefetchScalarGridSpec(
        num_scalar_prefetch=4,
        grid=(N_TILES,),
        in_specs=[_row_spec(D_MODEL), _row_spec(ROUTER_LANES), tile_row, _full_spec(g),
                  pl.BlockSpec(memory_space=pl.ANY)],
        out_specs=_row_spec(D_MODEL),
        scratch_shapes=[pltpu.VMEM((2, MAX_SLOTS, HALF), U32),
                        pltpu.SemaphoreType.DMA((2,))],
    )
    return pl.pallas_call(
        functools.partial(_combine_kernel, final_norm=final_norm),
        grid_spec=grid_spec,
        out_shape=jax.ShapeDtypeStruct((TOKENS, D_MODEL), F32),
        compiler_params=_cparams("arbitrary"),
        name="combine",
    )(tables["loff"], tables["c8"], tables["dst"], tables["nslot"],
      h, route_i, tables["offrow"], g, y_pad)


def _expert_kernel(be_ref, slot_ref, next_ref, nact_ref, x_ref, wg_hbm, wu_hbm, wd_hbm, *rest,
                   layer, recycled):
    (y_hbm, wg_f, wu_f, wg_b, wu_b, wd_s, ybuf, zero_buf, sem, ysem, zsem) = (
        rest[1:] if recycled else rest)
    i = pl.program_id(0)
    n_active = nact_ref[0]
    staged = ((wg_hbm, wg_f, wg_b), (wu_hbm, wu_f, wu_b))

    def fetch(e, slot):
        copies = [pltpu.make_async_copy(hbm.at[layer, e], f32.at[slot], sem.at[slot, k])
                  for k, (hbm, f32, _) in enumerate(staged)]
        return copies + [pltpu.make_async_copy(wd_hbm.at[e], wd_s.at[slot], sem.at[slot, len(staged)])]

    def block_rows(b):
        return y_hbm.at[pl.ds(pl.multiple_of(b * MOE_BLOCK, MOE_BLOCK), MOE_BLOCK)]

    def put(b):
        return pltpu.make_async_copy(ybuf.at[b % 2], block_rows(b), ysem.at[b % 2])

    def tail_copy(b):
        return pltpu.make_async_copy(zero_buf, block_rows(b), zsem)

    if not recycled:
        @pl.when(i == 0)
        def _():
            zero_buf[...] = jnp.zeros_like(zero_buf)
            _loop(n_active, N_MOE_BLOCKS, lambda b: tail_copy(b).start())

        @pl.when(i == pl.num_programs(0) - 1)
        def _():
            _loop(n_active, N_MOE_BLOCKS, lambda b: tail_copy(b).wait())

    @pl.when(i < n_active)
    def _():
        e = be_ref[i]
        slot = slot_ref[i]

        def request(ahead, block):
            nxt = next_ref[(ahead - 1) * N_MOE_BLOCKS + block]

            @pl.when(nxt >= 0)
            def _():
                for cp in fetch(nxt, (slot + ahead) % W_STAGES):
                    cp.start()

        @pl.when((i == 0) | (e != be_ref[jnp.maximum(i - 1, 0)]))
        def _():
            @pl.when(i == 0)
            def _():
                for cp in fetch(e, slot):
                    cp.start()
                for ahead in range(1, W_STAGES - 1):
                    request(ahead, i)
            for cp in fetch(e, slot):
                cp.wait()
            request(W_STAGES - 1, i)
            for _, f32, b16 in staged:
                b16[...] = f32[slot].astype(BF16)

        x_hi, x_lo = _unpack_halves(x_ref[:, 0:HALF])
        row_w = lax.bitcast_convert_type(x_ref[:, HALF:HALF + 1], F32)
        gate = _dot(x_hi, wg_b[0:HALF, :]) + _dot(x_lo, wg_b[HALF:D_MODEL, :])
        up = _dot(x_hi, wu_b[0:HALF, :]) + _dot(x_lo, wu_b[HALF:D_MODEL, :])
        hmid = (jax.nn.silu(gate) * up).astype(BF16)
        y = (_dot(hmid, wd_s[slot]) * row_w).astype(BF16).astype(F32)
        ybuf[i % 2] = _pack_halves(y[:, 0:HALF], y[:, HALF:D_MODEL])
        put(i).start()

        @pl.when(i > 0)
        def _():
            put(i - 1).wait()

        @pl.when(i == n_active - 1)
        def _():
            put(i).wait()


def _experts(layer, tables, x_pad, w_gate, w_up, w_down, old_rows=None):
    hbm = pl.BlockSpec(memory_space=pl.ANY)
    up_shape, down_shape = (D_MODEL, D_EXPERT), (D_EXPERT, D_MODEL)
    recycled = old_rows is not None
    scalars = (tables["blk_expert"], tables["blk_slot"], tables["blk_next"], tables["n_active"])
    operands = (x_pad, w_gate, w_up, w_down) + ((old_rows,) if recycled else ())
    grid_spec = pltpu.PrefetchScalarGridSpec(
        num_scalar_prefetch=len(scalars),
        grid=(N_MOE_BLOCKS,),
        in_specs=[pl.BlockSpec((MOE_BLOCK, X_WORDS),
                               lambda i, be, sl, nx, na: (jnp.minimum(i, na[0] - 1), 0)),
                  hbm, hbm, hbm] + [hbm] * recycled,
        out_specs=hbm,
        scratch_shapes=[pltpu.VMEM((W_STAGES,) + up_shape, F32),
                        pltpu.VMEM((W_STAGES,) + up_shape, F32),
                        pltpu.VMEM(up_shape, BF16), pltpu.VMEM(up_shape, BF16),
                        pltpu.VMEM((W_STAGES,) + down_shape, BF16),
                        pltpu.VMEM((2, MOE_BLOCK, HALF), U32),
                        pltpu.VMEM((MOE_BLOCK, HALF), U32),
                        pltpu.SemaphoreType.DMA((W_STAGES, 3)),
                        pltpu.SemaphoreType.DMA((2,)),
                        pltpu.SemaphoreType.DMA(())],
    )
    return pl.pallas_call(
        functools.partial(_expert_kernel, layer=layer, recycled=recycled),
        grid_spec=grid_spec,
        out_shape=jax.ShapeDtypeStruct((PAD_ROWS, HALF), U32),
        input_output_aliases={len(scalars) + len(operands) - 1: 0} if recycled else {},
        compiler_params=_cparams("arbitrary"),
        name="experts",
    )(*scalars, *operands)


def _routing_tables(tile_counts):
    cnt = tile_counts[:, 0, EXPERT_LANE0:EXPERT_LANE0 + N_EXPERTS].astype(I32)
    c8 = (cnt + RUN_ALIGN - 1) // RUN_ALIGN * RUN_ALIGN
    loff = jnp.cumsum(c8, axis=1) - c8
    nslot = jnp.sum(c8, axis=1)
    tot = jnp.sum(c8, axis=0)
    padded = (tot + MOE_BLOCK - 1) // MOE_BLOCK * MOE_BLOCK
    end = jnp.cumsum(padded)
    base = end - padded
    dst = base[None, :] + jnp.cumsum(c8, axis=0) - c8
    blk_start = jnp.arange(N_MOE_BLOCKS, dtype=I32) * MOE_BLOCK
    blk_expert = jnp.minimum(jnp.sum((end[None, :] <= blk_start[:, None]).astype(I32), axis=1),
                             N_EXPERTS - 1).astype(I32)
    offrow = jnp.pad(loff.astype(F32),
                     ((0, 0), (EXPERT_LANE0, ROUTER_LANES - EXPERT_LANE0 - N_EXPERTS)))
    offrow = jnp.broadcast_to(offrow[:, None, :], (N_TILES, SUBLANES, ROUTER_LANES))
    experts = jnp.arange(N_EXPERTS, dtype=I32)
    present = padded > 0
    ordinal = jnp.cumsum(present.astype(I32)) - 1
    at_or_after = lax.cummin(jnp.where(present, experts, N_EXPERTS), reverse=True)
    after = jnp.concatenate([at_or_after[1:], jnp.full((1,), N_EXPERTS, I32)])
    hop = jnp.concatenate([after, jnp.full((1,), N_EXPERTS, I32)])
    pick = (blk_expert[:, None] == experts[None, :]).astype(I32)
    blk_slot = jnp.sum(pick * (ordinal % W_STAGES)[None, :], axis=1).astype(I32)
    ahead, blk_next = experts, []
    for _ in range(W_STAGES - 1):
        ahead = jnp.sum((ahead[:, None] == jnp.arange(N_EXPERTS + 1)[None, :]) * hop[None, :], axis=1)
        per_block = jnp.sum(pick * ahead[None, :], axis=1)
        blk_next.append(jnp.where(per_block >= N_EXPERTS, -1, per_block))
    blk_next = jnp.concatenate(blk_next).astype(I32)
    return {
        "blk_slot": blk_slot, "blk_next": blk_next,
        "loff": loff.reshape(-1).astype(I32), "c8": c8.reshape(-1).astype(I32),
        "dst": dst.reshape(-1).astype(I32), "nslot": nslot.astype(I32),
        "zdst": (base + tot).astype(I32), "zcnt": (padded - tot).astype(I32),
        "offrow": offrow, "blk_expert": blk_expert,
        "n_active": (end[-1:] // MOE_BLOCK).astype(I32),
    }


def _split_router_weights(w_router_group, w_router_expert):
    w_e = w_router_expert.transpose(1, 0, 2).reshape(D_MODEL, N_EXPERTS)
    w = jnp.concatenate([w_router_group, w_e], axis=1)
    w = jnp.pad(w, ((0, 0), (0, ROUTER_LANES - w.shape[1])))
    hi = w.astype(BF16)
    lo = (w - hi.astype(F32)).astype(BF16)
    return jnp.concatenate([hi, lo], axis=1)


def _input_weights(w):
    cols = []
    for g in range(N_GROUPS):
        for part in range(3):
            c0 = part * ATTN_WIDTH + g * GROUP_WIDTH
            cols.append(w[:, c0:c0 + GROUP_WIDTH])
    cols.append(w[:, QKV_WIDTH:])
    return jnp.concatenate(cols, axis=1).astype(BF16)


def kernel(x, rel_bias, norm_mix_g, w_in, pool_w, pool_scale, w_proj_attn, w_proj_pool, w_out,
           norm_ffn_g, w_router_group, w_router_expert, w_gate_e, w_up_e, w_down_e, norm_final_g):
    h = x.reshape(TOKENS, D_MODEL)
    biases = [_band_bias(rel_bias, g, d) for g, (_, d) in enumerate(DILATION_PATTERNS)]
    x_pad = y_pad = None
    for l in range(DEPTH):
        qkv, u, gates, w_down_b = _project(l, h, norm_mix_g[l][None], _input_weights(w_in[l]), w_down_e)
        h, xn, route_i, route_w, tile_counts = _mix_out(
            h, _attention(qkv, biases), u, gates, pool_w[l].astype(BF16), pool_scale[l][None],
            w_proj_attn[l].astype(BF16), w_proj_pool[l].astype(BF16), w_out[l].astype(BF16),
            norm_ffn_g[l][None], _split_router_weights(w_router_group[l], w_router_expert[l]))
        tables = _routing_tables(tile_counts)
        x_pad = _dispatch(tables, xn, route_i, route_w, x_pad)
        y_pad = _experts(l, tables, x_pad, w_gate_e, w_up_e, w_down_b, y_pad)
        h = _combine(tables, h, route_i, y_pad, norm_final_g[None], l == DEPTH - 1)
    return h.reshape(BATCH, SEQ, D_MODEL)
```

```python
import functools
import math

import jax
import jax.numpy as jnp
from jax import lax
from jax.experimental import pallas as pl
from jax.experimental.pallas import tpu as pltpu

F32 = jnp.float32
BF16 = jnp.bfloat16
I32 = jnp.int32
U32 = jnp.uint32

D_MODEL = 1024
BATCH = 8
SEQ = 2048
TOKENS = BATCH * SEQ
DEPTH = 2

HEAD_DIM = 64
HEADS_PER_GROUP = 4
GROUP_WIDTH = HEADS_PER_GROUP * HEAD_DIM
DILATION_PATTERNS = ((128, 1), (512, 4), (2048, 16))
N_GROUPS = len(DILATION_PATTERNS)
N_ATTN_HEADS = N_GROUPS * HEADS_PER_GROUP
ATTN_WIDTH = N_ATTN_HEADS * HEAD_DIM
QKV_WIDTH = 3 * ATTN_WIDTH
GROUP_QKV = 3 * GROUP_WIDTH
N_SIDE = 64
assert all(w // (2 * d) == N_SIDE for w, d in DILATION_PATTERNS)
POOL_WINDOWS = (2, 4, 8, 16)
POOL_GROUP_WIDTH = 128
POOL_WIDTH = len(POOL_WINDOWS) * POOL_GROUP_WIDTH
POOL_HALO = max(POOL_WINDOWS) // 2
N_BRANCHES = 2
GATE_WIDTH = N_BRANCHES * D_MODEL
IN_WIDTH = QKV_WIDTH + POOL_WIDTH + GATE_WIDTH
N_BUCKETS = 32
MAX_DISTANCE = 1024
N_EXPERT_GROUPS = 8
EXPERTS_PER_GROUP = 8
N_EXPERTS = N_EXPERT_GROUPS * EXPERTS_PER_GROUP
TOP_K = 2
D_EXPERT = 512
N_ASSIGN = TOKENS * TOP_K
EPS = 1e-6
NEG_INF = -1e30
LOG2_E = math.log2(math.e)
Q_SCALE = HEAD_DIM ** -0.5 * LOG2_E

LANES = 128
SUBLANES = 8
ROW_TILE = 512
N_TILES = TOKENS // ROW_TILE
TILES_PER_SEQ = SEQ // ROW_TILE
Q_BLOCK = 128
K_BLOCK = Q_BLOCK + 2 * N_SIDE
PAD_WAYS = 4
BLOCKS_IN_FLIGHT = 4
ROUTER_LANES = 128
EXPERT_LANE0 = N_EXPERT_GROUPS
VMEM_LIMIT = 56 * 1024 * 1024

RUN_ALIGN = SUBLANES
MOE_BLOCK = 336
HALF = D_MODEL // 2
X_WORDS = HALF + LANES
SLOT_CHUNK = 256
RUN_UNROLL = 4
W_STAGES = 2
MAX_SLOTS = -(-(TOP_K * ROW_TILE + N_EXPERTS * (RUN_ALIGN - 1)) // SLOT_CHUNK) * SLOT_CHUNK
SLOT_CHUNKS = (tuple((s, ROW_TILE) for s in range(0, TOP_K * ROW_TILE, ROW_TILE))
               + tuple((s, SLOT_CHUNK) for s in range(TOP_K * ROW_TILE, MAX_SLOTS, SLOT_CHUNK)))
COMBINE_BASE = MAX_SLOTS - SLOT_CHUNK
N_MOE_BLOCKS = -(-(N_ASSIGN + N_TILES * N_EXPERTS * (RUN_ALIGN - 1)
                   + N_EXPERTS * (MOE_BLOCK - RUN_ALIGN)) // MOE_BLOCK)
PAD_ROWS = N_MOE_BLOCKS * MOE_BLOCK
HIGH_HALF = 0xFFFF0000


def _cparams(*sem):
    return pltpu.CompilerParams(dimension_semantics=sem, vmem_limit_bytes=VMEM_LIMIT)


def _rms(h, g):
    r = lax.rsqrt(jnp.mean(h * h, axis=-1, keepdims=True) + EPS)
    return (h * r) * g


def _dot(a, b):
    return jnp.dot(a, b, preferred_element_type=F32)


def _row_spec(width):
    return pl.BlockSpec((ROW_TILE, width), lambda i, *_: (i, 0))


def _full_spec(a):
    return pl.BlockSpec(a.shape, lambda i, *_: (0,) * a.ndim)


def _bits(x):
    return lax.bitcast_convert_type(x, U32)


def _pack_halves(a, b):
    return (_bits(a) & jnp.uint32(HIGH_HALF)) | (_bits(b) >> 16)


def _unpack_halves(words):
    hi = lax.bitcast_convert_type(words & jnp.uint32(HIGH_HALF), F32)
    lo = lax.bitcast_convert_type(words << 16, F32)
    return hi.astype(BF16), lo.astype(BF16)


def _proj_kernel(h_ref, g_ref, w_ref, wd_ref, q0_ref, q1_ref, q2_ref, u_ref, gate_ref, wdb_ref, slabs):
    wdb_ref[...] = wd_ref[...].astype(BF16)
    xn = _rms(h_ref[...], g_ref[...]).astype(BF16)
    n_slabs = GROUP_QKV // LANES
    for g, out_ref in enumerate((q0_ref, q1_ref, q2_ref)):
        dilation = DILATION_PATTERNS[g][1]
        q, k, v = (_dot(xn, w_ref[:, part * ATTN_WIDTH + g * GROUP_WIDTH:
                                  part * ATTN_WIDTH + (g + 1) * GROUP_WIDTH]) for part in range(3))
        res = jnp.concatenate([q * Q_SCALE, k, v], axis=1)
        if dilation == 1:
            out_ref[0, 0] = res.astype(BF16)
            continue
        for s in range(n_slabs):
            slabs[s] = res[:, s * LANES:(s + 1) * LANES]
        n = ROW_TILE // dilation
        for r in range(dilation):
            rows = [slabs[s, pl.ds(r, n, stride=dilation), :] for s in range(n_slabs)]
            out_ref[0, r] = jnp.concatenate(rows, axis=1).astype(BF16)
    u_ref[...] = _dot(xn, w_ref[:, QKV_WIDTH:QKV_WIDTH + POOL_WIDTH])
    gates = _dot(xn, w_ref[:, QKV_WIDTH + POOL_WIDTH:IN_WIDTH])
    gate_ref[...] = jax.nn.sigmoid(gates).astype(BF16)


def _project(layer, h, g, w_bf16, w_down):
    qkv_shapes, qkv_specs = [], []
    for _, d in DILATION_PATTERNS:
        qkv_shapes.append(jax.ShapeDtypeStruct((BATCH, d, SEQ // d, GROUP_QKV), BF16))
        qkv_specs.append(pl.BlockSpec((1, d, ROW_TILE // d, GROUP_QKV),
                                      lambda i: (i // TILES_PER_SEQ, 0, i % TILES_PER_SEQ, 0)))
    layer_rows = N_EXPERTS * D_EXPERT
    step_rows = layer_rows // N_TILES
    res = pl.pallas_call(
        _proj_kernel,
        grid=(N_TILES,),
        in_specs=[_row_spec(D_MODEL), _full_spec(g), _full_spec(w_bf16),
                  pl.BlockSpec((step_rows, D_MODEL), lambda i: (layer * N_TILES + i, 0))],
        out_specs=qkv_specs + [_row_spec(POOL_WIDTH), _row_spec(GATE_WIDTH),
                               pl.BlockSpec((step_rows, D_MODEL), lambda i: (i, 0))],
        out_shape=qkv_shapes + [jax.ShapeDtypeStruct((TOKENS, POOL_WIDTH), F32),
                                jax.ShapeDtypeStruct((TOKENS, GATE_WIDTH), BF16),
                                jax.ShapeDtypeStruct((layer_rows, D_MODEL), BF16)],
        scratch_shapes=[pltpu.VMEM((GROUP_QKV // LANES, ROW_TILE, LANES), F32)],
        compiler_params=_cparams("parallel"),
        name="proj",
    )(h, g, w_bf16, w_down.reshape(DEPTH * layer_rows, D_MODEL))
    w_down_b = res[N_GROUPS + 2].reshape(N_EXPERTS, D_EXPERT, D_MODEL)
    return res[:N_GROUPS], res[N_GROUPS], res[N_GROUPS + 1], w_down_b


def _attn_kernel(q0_ref, q1_ref, q2_ref, b0_ref, b1_ref, b2_ref, o_ref, pads, num, den, top):
    zpad = jnp.zeros((N_SIDE, GROUP_WIDTH), BF16)
    groups = ((q2_ref, b2_ref, 2), (q1_ref, b1_ref, 1), (q0_ref, b0_ref, 0))
    for order, (qkv_ref, bias_ref, g) in enumerate(groups):
        dilation = DILATION_PATTERNS[g][1]
        sub_len = SEQ // dilation
        for way in range(PAD_WAYS):
            for kv in range(2):
                pads[way, kv, 0:N_SIDE, :] = zpad
                pads[way, kv, N_SIDE + sub_len:2 * N_SIDE + sub_len, :] = zpad
        _attn_group(qkv_ref, bias_ref, pads, num, den, top, dilation=dilation, first=order == 0)

    def finish(c, carry):
        rows = pl.ds(pl.multiple_of(c * ROW_TILE, ROW_TILE), ROW_TILE)
        merged = [num[half, rows, :] / den[half, rows, :] for half in range(GROUP_WIDTH // LANES)]
        o_ref[0, rows, :] = jnp.concatenate(merged, axis=1).astype(BF16)
        return carry

    lax.fori_loop(0, SEQ // ROW_TILE, finish, 0)


def _attn_group(qkv_ref, bias_ref, pads, num, den, top, *, dilation, first):
    sub_len = SEQ // dilation
    n_blocks = sub_len // Q_BLOCK
    head_of_lane = lax.broadcasted_iota(I32, (1, GROUP_WIDTH), 1) // HEAD_DIM

    def by_head(cols):
        out = cols[HEADS_PER_GROUP - 1]
        for h in range(HEADS_PER_GROUP - 2, -1, -1):
            out = jnp.where(head_of_lane == h, cols[h], out)
        return out

    def load_keys(r, way):
        pads[way, 0, N_SIDE:N_SIDE + sub_len, :] = qkv_ref[0, r, :, GROUP_WIDTH:2 * GROUP_WIDTH]
        pads[way, 1, N_SIDE:N_SIDE + sub_len, :] = qkv_ref[0, r, :, 2 * GROUP_WIDTH:3 * GROUP_WIDTH]

    def block(r, way, i):
        static = isinstance(i, int)
        r0 = i * Q_BLOCK if static else pl.multiple_of(i * Q_BLOCK, Q_BLOCK)
        qb = qkv_ref[0, r, pl.ds(r0, Q_BLOCK), 0:GROUP_WIDTH]
        kw = pads[way, 0, pl.ds(r0, K_BLOCK), :]
        vw = pads[way, 1, pl.ds(r0, K_BLOCK), :]
        is_first, is_last = i == 0, i == n_blocks - 1
        edge = (int(is_first) + 2 * int(is_last) if static
                else is_first.astype(I32) + 2 * is_last.astype(I32))
        zero = jnp.zeros_like(qb)
        q_heads = jnp.concatenate(
            [jnp.where(head_of_lane == h, qb, zero) for h in range(HEADS_PER_GROUP)], axis=0)
        s = lax.dot_general(q_heads, kw, (((1,), (1,)), ((), ())), preferred_element_type=F32)
        s = s.reshape(HEADS_PER_GROUP, Q_BLOCK, K_BLOCK) + bias_ref[edge]
        m = jnp.max(s, axis=-1, keepdims=True)
        p = jnp.exp2(s - m)
        psum = jnp.sum(p, axis=-1, keepdims=True)
        pb = p.astype(BF16)
        p_cat = jnp.concatenate([pb[h] for h in range(HEADS_PER_GROUP)], axis=1)
        zv = jnp.zeros_like(vw)
        v_heads = jnp.concatenate(
            [jnp.where(head_of_lane == h, vw, zv) for h in range(HEADS_PER_GROUP)], axis=0)
        o = _dot(p_cat, v_heads)
        lse = m + jnp.log2(psum)
        o = o / by_head([psum[h] for h in range(HEADS_PER_GROUP)])
        lse_lanes = by_head([lse[h] for h in range(HEADS_PER_GROUP)])
        if dilation == 1:
            rows = pl.ds(r0, Q_BLOCK)
        else:
            rows = pl.ds(r + dilation * r0, Q_BLOCK, stride=dilation)
        for half in range(GROUP_WIDTH // LANES):
            lanes = slice(half * LANES, (half + 1) * LANES)
            if first:
                num[half, rows, :] = o[:, lanes]
                den[half, rows, :] = jnp.ones((Q_BLOCK, LANES), F32)
                top[half, rows, :] = lse_lanes[:, lanes]
            else:
                old = top[half, rows, :]
                new = jnp.maximum(old, lse_lanes[:, lanes])
                keep = jnp.exp2(old - new)
                add = jnp.exp2(lse_lanes[:, lanes] - new)
                num[half, rows, :] = keep * num[half, rows, :] + add * o[:, lanes]
                den[half, rows, :] = keep * den[half, rows, :] + add
                top[half, rows, :] = new

    flight = min(n_blocks, BLOCKS_IN_FLIGHT * (2 if dilation == 1 else 1))
    assert n_blocks % flight == 0

    def blocks_of(r, way):
        if n_blocks == 1:
            block(r, way, 0)
        else:
            def several(j, carry):
                for k in range(flight):
                    block(r, way, flight * j + k)
                return carry
            lax.fori_loop(0, n_blocks // flight, several, 0)

    if dilation == 1:
        load_keys(0, 0)
        blocks_of(0, 0)
    else:
        ways = min(PAD_WAYS, dilation, max(1, 2 * BLOCKS_IN_FLIGHT // n_blocks))

        def residues(j, carry):
            for way in range(ways):
                load_keys(ways * j + way, way)
            for way in range(ways):
                blocks_of(ways * j + way, way)
            return carry
        lax.fori_loop(0, dilation // ways, residues, 0)


def _t5_bucket(rel):
    nb = N_BUCKETS // 2
    ret = jnp.where(rel > 0, nb, 0)
    n = jnp.abs(rel)
    max_exact = nb // 2
    nf = jnp.maximum(n, max_exact).astype(F32)
    large = max_exact + (jnp.log(nf / max_exact) / math.log(MAX_DISTANCE / max_exact)
                         * (nb - max_exact)).astype(I32)
    large = jnp.minimum(large, nb - 1)
    return ret + jnp.where(n < max_exact, n, large)


def _band_bias(rel_bias, group, dilation):
    qi = jnp.arange(Q_BLOCK)[:, None]
    ki = jnp.arange(K_BLOCK)[None, :]
    bucket = _t5_bucket((ki - N_SIDE - qi) * dilation)
    tab = rel_bias[:, group * HEADS_PER_GROUP:(group + 1) * HEADS_PER_GROUP]
    onehot = (bucket[:, :, None] == jnp.arange(N_BUCKETS)[None, None, :]).astype(F32)
    bias = jnp.einsum('qkb,bh->hqk', onehot, tab, precision=lax.Precision.HIGHEST).astype(F32)
    bias = bias * LOG2_E
    edge = jnp.arange(4)[:, None, None]
    ok = ((jnp.abs(ki - N_SIDE - qi) <= N_SIDE)[None]
          & (((edge & 1) == 0) | (ki >= N_SIDE)[None])
          & (((edge & 2) == 0) | (ki < Q_BLOCK + N_SIDE)[None]))
    return jnp.where(ok[:, None], bias[None], NEG_INF)


def _attention(qkv, biases):
    slab = pltpu.VMEM((GROUP_WIDTH // LANES, SEQ, LANES), F32)
    pads = pltpu.VMEM((PAD_WAYS, 2, SEQ + 2 * N_SIDE, GROUP_WIDTH), BF16)
    out = pl.pallas_call(
        _attn_kernel,
        grid=(BATCH,),
        in_specs=[pl.BlockSpec((1,) + a.shape[1:], lambda b: (b, 0, 0, 0)) for a in qkv]
                 + [_full_spec(b) for b in biases],
        out_specs=pl.BlockSpec((1, SEQ, GROUP_WIDTH), lambda b: (b, 0, 0)),
        out_shape=jax.ShapeDtypeStruct((BATCH, SEQ, GROUP_WIDTH), BF16),
        scratch_shapes=[pads, slab, slab, slab],
        compiler_params=_cparams("parallel"),
        name="attn",
    )(*qkv, *biases)
    return out.reshape(TOKENS, GROUP_WIDTH)


def _mixout_kernel(h_ref, attn_ref, u_ref, uprev_ref, unext_ref, gate_ref,
                   pw_ref, ps_ref, wpa_ref, wpp_ref, wo_ref, gffn_ref, wr_ref,
                   out_ref, xn_ref, ri_ref, rw_ref, cnt_ref, runs):
    j = pl.program_id(0) % TILES_PER_SEQ
    y_attn = _dot(attn_ref[...], wpa_ref[...])

    u = u_ref[...]
    prev = jnp.where(j == 0, 0.0, uprev_ref[0])
    nxt = jnp.where(j == TILES_PER_SEQ - 1, 0.0, unext_ref[0])
    ext_rows = ROW_TILE + 2 * POOL_HALO
    runs[:, ext_rows:ext_rows + POOL_HALO, :] = jnp.zeros((2, POOL_HALO, POOL_GROUP_WIDTH), F32)
    pos = j * ROW_TILE + lax.broadcasted_iota(I32, (ROW_TILE, 1), 0)
    mixed = []
    for gi, w in enumerate(POOL_WINDOWS):
        half = w // 2
        sl = slice(gi * POOL_GROUP_WIDTH, (gi + 1) * POOL_GROUP_WIDTH)
        runs[0, 0:POOL_HALO, :] = prev[:, sl]
        runs[0, POOL_HALO:POOL_HALO + ROW_TILE, :] = u[:, sl]
        runs[0, POOL_HALO + ROW_TILE:ext_rows, :] = nxt[:, sl]
        src, span = 0, 1
        while 2 * span < w:
            runs[1 - src, 0:ext_rows, :] = runs[src, 0:ext_rows, :] + runs[src, span:span + ext_rows, :]
            src, span = 1 - src, 2 * span
        lo = POOL_HALO - half
        acc = runs[src, lo:lo + ROW_TILE, :] + runs[src, lo + half:lo + half + ROW_TILE, :]
        cnt = (jnp.minimum(pos + half, SEQ) - jnp.maximum(pos - half, 0)).astype(F32)
        pooled = acc / cnt - u[:, sl]
        mixed.append(_dot(pooled.astype(BF16), pw_ref[gi]) * ps_ref[:, sl])
    y_pool = _dot(jnp.concatenate(mixed, axis=1).astype(BF16), wpp_ref[...])

    y = (gate_ref[:, 0:D_MODEL] * y_attn.astype(BF16)
         + gate_ref[:, D_MODEL:GATE_WIDTH] * y_pool.astype(BF16))
    h = h_ref[...] + _dot(y, wo_ref[...])
    out_ref[...] = h
    _route_rows(h, gffn_ref, wr_ref, xn_ref, ri_ref, rw_ref, cnt_ref)


def _mix_out(h, attn, u, gates, pool_w, pool_scale, w_proj_attn, w_proj_pool, w_out, g_ffn, wr_split):
    halo_blocks = ROW_TILE // POOL_HALO
    u3 = u.reshape(TOKENS // POOL_HALO, POOL_HALO, POOL_WIDTH)
    last = TOKENS // POOL_HALO - 1
    prev_spec = pl.BlockSpec((1, POOL_HALO, POOL_WIDTH),
                             lambda i: (jnp.maximum(i * halo_blocks - 1, 0), 0, 0))
    next_spec = pl.BlockSpec((1, POOL_HALO, POOL_WIDTH),
                             lambda i: (jnp.minimum((i + 1) * halo_blocks, last), 0, 0))
    weights = (pool_w, pool_scale, w_proj_attn, w_proj_pool, w_out, g_ffn, wr_split)
    return pl.pallas_call(
        _mixout_kernel,
        grid=(N_TILES,),
        in_specs=[_row_spec(D_MODEL), _row_spec(GROUP_WIDTH),
                  _row_spec(POOL_WIDTH), prev_spec, next_spec, _row_spec(GATE_WIDTH)]
                 + [_full_spec(w) for w in weights],
        out_specs=[_row_spec(D_MODEL), _row_spec(D_MODEL), _row_spec(ROUTER_LANES),
                   _row_spec(ROUTER_LANES),
                   pl.BlockSpec((1, SUBLANES, ROUTER_LANES), lambda i: (i, 0, 0))],
        out_shape=[jax.ShapeDtypeStruct((TOKENS, D_MODEL), F32),
                   jax.ShapeDtypeStruct((TOKENS, D_MODEL), BF16),
                   jax.ShapeDtypeStruct((TOKENS, ROUTER_LANES), I32),
                   jax.ShapeDtypeStruct((TOKENS, ROUTER_LANES), F32),
                   jax.ShapeDtypeStruct((N_TILES, SUBLANES, ROUTER_LANES), F32)],
        scratch_shapes=[pltpu.VMEM((2, ROW_TILE + 3 * POOL_HALO, POOL_GROUP_WIDTH), F32)],
        compiler_params=_cparams("parallel"),
        name="mixout",
    )(h, attn, u, u3, u3, gates, *weights)


def _route_rows(h, g_ref, wr_ref, xn_ref, ri_ref, rw_ref, cnt_ref):
    xn = _rms(h, g_ref[...])
    hi = xn.astype(BF16)
    xn_ref[...] = hi
    lo = (xn - hi.astype(F32)).astype(BF16)
    both = _dot(hi, wr_ref[...])
    lg = both[:, 0:ROUTER_LANES] + (both[:, ROUTER_LANES:] + _dot(lo, wr_ref[:, 0:ROUTER_LANES]))

    lane = lax.broadcasted_iota(I32, (ROW_TILE, ROUTER_LANES), 1)
    lanef = lane.astype(F32)
    low = jnp.float32(-3.0e38)
    far = jnp.float32(ROUTER_LANES)
    first = lambda hit: jnp.min(jnp.where(hit, lanef, far), axis=-1, keepdims=True)

    is_group = lane < N_EXPERT_GROUPS
    gl = jnp.where(is_group, lg, low)
    gmax = jnp.max(gl, axis=-1, keepdims=True)
    gidx = first(gl == gmax).astype(I32)
    gden = jnp.sum(jnp.where(is_group, jnp.exp(gl - gmax), 0.0), axis=-1, keepdims=True)
    g_p = 1.0 / gden

    in_group = ((lane >= EXPERT_LANE0) & (lane < EXPERT_LANE0 + N_EXPERTS)
                & ((lane - EXPERT_LANE0) // EXPERTS_PER_GROUP == gidx))
    el = jnp.where(in_group, lg, low)
    t1 = jnp.max(el, axis=-1, keepdims=True)
    l1 = first(in_group & (el == t1))
    rest = in_group & (lanef != l1)
    el2 = jnp.where(rest, lg, low)
    t2 = jnp.max(el2, axis=-1, keepdims=True)
    l2 = first(rest & (el2 == t2))
    e2 = jnp.exp(t2 - t1)
    w1 = g_p * (1.0 / (1.0 + e2))
    w2 = g_p * (e2 / (1.0 + e2))

    hit1 = lanef == l1
    hit2 = lanef == l2
    onehot = (hit1 | hit2).astype(BF16)
    ri = lax.broadcasted_iota(I32, (ROW_TILE, ROW_TILE), 0)
    ci = lax.broadcasted_iota(I32, (ROW_TILE, ROW_TILE), 1)
    before = (ci < ri).astype(BF16)
    seen = _dot(before, onehot)
    r1 = jnp.sum(jnp.where(hit1, seen, 0.0), axis=-1, keepdims=True)
    r2 = jnp.sum(jnp.where(hit2, seen, 0.0), axis=-1, keepdims=True)

    packed = jnp.zeros((ROW_TILE, ROUTER_LANES), F32)
    for k, v in enumerate((l1, l2, r1, r2)):
        packed = jnp.where(lane == k, v, packed)
    ri_ref[...] = packed.astype(I32)
    rw_ref[...] = jnp.where(lane == 0, w1, jnp.where(lane == 1, w2, 0.0))
    counts = jnp.sum(onehot.astype(F32), axis=0, keepdims=True)
    cnt_ref[0] = jnp.broadcast_to(counts, (SUBLANES, ROUTER_LANES))


def _slots(ri_ref, off_ref):
    ri = ri_ref[...]
    lane = lax.broadcasted_iota(I32, (ROW_TILE, ROUTER_LANES), 1)
    off = off_ref[0, 0:1, :]
    pick = lambda k: jnp.sum(jnp.where(lane == ri[:, k:k + 1], off, 0.0), axis=-1, keepdims=True)
    return (pick(0) + ri[:, 2:3].astype(F32), pick(1) + ri[:, 3:4].astype(F32))


def _as_rows(cols):
    eye = (lax.broadcasted_iota(I32, (ROW_TILE, ROW_TILE), 0)
           == lax.broadcasted_iota(I32, (ROW_TILE, ROW_TILE), 1))
    return [jnp.sum(jnp.where(eye, c, 0.0), axis=0, keepdims=True) for c in cols]


def _run_copy(tile, e, loff_s, c8_s, dst_s, buf_ref, slot, hbm_ref, sem, to_hbm):
    k = tile * N_EXPERTS + e
    n = pl.multiple_of(c8_s[k], RUN_ALIGN)
    vm = buf_ref.at[slot, pl.ds(pl.multiple_of(loff_s[k], RUN_ALIGN), n)]
    hb = hbm_ref.at[pl.ds(pl.multiple_of(dst_s[k], RUN_ALIGN), n)]
    src, dst = (vm, hb) if to_hbm else (hb, vm)
    return n, pltpu.make_async_copy(src, dst, sem.at[slot])


def _loop(lo, hi, fn, unroll=1):
    def body(e, carry):
        fn(e)
        return carry
    lax.fori_loop(lo, hi, body, 0, unroll=unroll)


def _wait_rows(n, buf_ref, slot, hbm_ref, sem, to_hbm):
    vm = buf_ref.at[slot, pl.ds(0, n)]
    hb = hbm_ref.at[pl.ds(0, n)]
    src, dst = (vm, hb) if to_hbm else (hb, vm)
    pltpu.make_async_copy(src, dst, sem.at[slot]).wait()


def _start(n, cp):
    @pl.when(n > 0)
    def _():
        cp.start()


def _wait(n, cp):
    @pl.when(n > 0)
    def _():
        cp.wait()


def _dispatch_kernel(loff_s, c8_s, dst_s, nslot_s, zdst_s, zcnt_s, nact_s,
                     xn_ref, ri_ref, rw_ref, off_ref, *rest, recycled):
    xpad_hbm, sorted_buf, zero_buf, sem, zsem = rest[1:] if recycled else rest
    i = pl.program_id(0)
    slot = i % 2
    last = pl.num_programs(0) - 1

    def zero_copy(e):
        n = pl.multiple_of(zcnt_s[e], RUN_ALIGN)
        dst = xpad_hbm.at[pl.ds(pl.multiple_of(zdst_s[e], RUN_ALIGN), n)]
        return n, pltpu.make_async_copy(zero_buf.at[pl.ds(0, n)], dst, zsem)

    def tail_copy(b):
        dst = xpad_hbm.at[pl.ds(pl.multiple_of(b * MOE_BLOCK, MOE_BLOCK), MOE_BLOCK)]
        return pltpu.make_async_copy(zero_buf, dst, zsem)

    @pl.when(i == 0)
    def _():
        zero_buf[...] = jnp.zeros_like(zero_buf)
        _loop(0, N_EXPERTS, lambda e: _start(*zero_copy(e)))
        if not recycled:
            _loop(nact_s[0], N_MOE_BLOCKS, lambda b: tail_copy(b).start())

    s1, s2 = _slots(ri_ref, off_ref)
    rw = rw_ref[...]
    s1_row, s2_row, w1_row, w2_row = _as_rows([s1, s2, rw[:, 0:1], rw[:, 1:2]])
    xn = xn_ref[...]
    for first, size in SLOT_CHUNKS:
        @pl.when(first < nslot_s[i])
        def _():
            rows = slice(first, first + size)
            lane = lax.broadcasted_iota(I32, (size, LANES), 1)
            srow = (first + lax.broadcasted_iota(I32, (size, ROW_TILE), 0)).astype(F32)
            hit1 = srow == s1_row
            hit2 = srow == s2_row
            xs = _dot((hit1 | hit2).astype(BF16), xn)
            sorted_buf[slot, rows, 0:HALF] = _pack_halves(xs[:, 0:HALF], xs[:, HALF:D_MODEL])
            ws = jnp.sum(jnp.where(hit1, w1_row, 0.0) + jnp.where(hit2, w2_row, 0.0),
                         axis=-1, keepdims=True)
            sorted_buf[slot, rows, HALF:X_WORDS] = jnp.where(lane == 0, _bits(ws), jnp.uint32(0))

    copy = lambda tile, sl: (lambda e: _run_copy(tile, e, loff_s, c8_s, dst_s, sorted_buf, sl,
                                                 xpad_hbm, sem, True))
    mine = copy(i, slot)
    _loop(0, N_EXPERTS, lambda e: _start(*mine(e)), unroll=RUN_UNROLL)
    tile_rows = lambda t: pl.multiple_of(nslot_s[t], RUN_ALIGN)

    @pl.when(i > 0)
    def _():
        _wait_rows(tile_rows(i - 1), sorted_buf, 1 - slot, xpad_hbm, sem, True)

    @pl.when(i == last)
    def _():
        _wait_rows(tile_rows(i), sorted_buf, slot, xpad_hbm, sem, True)
        _loop(0, N_EXPERTS, lambda e: _wait(*zero_copy(e)))
        if not recycled:
            _loop(nact_s[0], N_MOE_BLOCKS, lambda b: tail_copy(b).wait())


def _dispatch(tables, xn, route_i, route_w, old_rows=None):
    tile_row = pl.BlockSpec((1, SUBLANES, ROUTER_LANES), lambda i, *_: (i, 0, 0))
    hbm = pl.BlockSpec(memory_space=pl.ANY)
    recycled = old_rows is not None
    scalars = (tables["loff"], tables["c8"], tables["dst"], tables["nslot"], tables["zdst"],
               tables["zcnt"], tables["n_active"])
    operands = (xn, route_i, route_w, tables["offrow"]) + ((old_rows,) if recycled else ())
    grid_spec = pltpu.PrefetchScalarGridSpec(
        num_scalar_prefetch=len(scalars),
        grid=(N_TILES,),
        in_specs=[_row_spec(D_MODEL), _row_spec(ROUTER_LANES), _row_spec(ROUTER_LANES), tile_row]
                 + [hbm] * recycled,
        out_specs=hbm,
        scratch_shapes=[pltpu.VMEM((2, MAX_SLOTS, X_WORDS), U32),
                        pltpu.VMEM((MOE_BLOCK, X_WORDS), U32),
                        pltpu.SemaphoreType.DMA((2,)),
                        pltpu.SemaphoreType.DMA(())],
    )
    return pl.pallas_call(
        functools.partial(_dispatch_kernel, recycled=recycled),
        grid_spec=grid_spec,
        out_shape=jax.ShapeDtypeStruct((PAD_ROWS, X_WORDS), U32),
        input_output_aliases={len(scalars) + len(operands) - 1: 0} if recycled else {},
        compiler_params=_cparams("arbitrary"),
        name="dispatch",
    )(*scalars, *operands)


def _combine_kernel(loff_s, c8_s, dst_s, nslot_s,
                    h_ref, ri_ref, off_ref, g_ref, ypad_hbm, out_ref,
                    ybuf, sem, *, final_norm):
    i = pl.program_id(0)
    slot = i % 2
    last = pl.num_programs(0) - 1
    fetch = lambda tile, sl: (lambda e: _run_copy(tile, e, loff_s, c8_s, dst_s, ybuf, sl,
                                                  ypad_hbm, sem, False))

    @pl.when(i == 0)
    def _():
        ybuf[...] = jnp.zeros_like(ybuf)
        first = fetch(i, slot)
        _loop(0, N_EXPERTS, lambda e: _start(*first(e)), unroll=RUN_UNROLL)

    @pl.when(i < last)
    def _():
        nxt = fetch(i + 1, 1 - slot)
        _loop(0, N_EXPERTS, lambda e: _start(*nxt(e)), unroll=RUN_UNROLL)

    _wait_rows(pl.multiple_of(nslot_s[i], RUN_ALIGN), ybuf, slot, ypad_hbm, sem, False)

    s1, s2 = _slots(ri_ref, off_ref)

    def gathered(first, size):
        scol = (first + lax.broadcasted_iota(I32, (ROW_TILE, size), 1)).astype(F32)
        pick = ((scol == s1) | (scol == s2)).astype(BF16)
        y_hi, y_lo = _unpack_halves(ybuf[slot, first:first + size, :])
        return jnp.concatenate([_dot(pick, y_hi), _dot(pick, y_lo)], axis=1)

    finish = (lambda v: _rms(v, g_ref[...])) if final_norm else (lambda v: v)
    h = h_ref[...] + gathered(0, COMBINE_BASE)
    spill = nslot_s[i] > COMBINE_BASE

    @pl.when(jnp.logical_not(spill))
    def _():
        out_ref[...] = finish(h)

    @pl.when(spill)
    def _():
        out_ref[...] = finish(h + gathered(COMBINE_BASE, MAX_SLOTS - COMBINE_BASE))


def _combine(tables, h, route_i, y_pad, g, final_norm):
    tile_row = pl.BlockSpec((1, SUBLANES, ROUTER_LANES), lambda i, *_: (i, 0, 0))
    grid_spec = pltpu.PrefetchScalarGridSpec(
        num_scalar_prefetch=4,
        grid=(N_TILES,),
        in_specs=[_row_spec(D_MODEL), _row_spec(ROUTER_LANES), tile_row, _full_spec(g),
                  pl.BlockSpec(memory_space=pl.ANY)],
        out_specs=_row_spec(D_MODEL),
        scratch_shapes=[pltpu.VMEM((2, MAX_SLOTS, HALF), U32),
                        pltpu.SemaphoreType.DMA((2,))],
    )
    return pl.pallas_call(
        functools.partial(_combine_kernel, final_norm=final_norm),
        grid_spec=grid_spec,
        out_shape=jax.ShapeDtypeStruct((TOKENS, D_MODEL), F32),
        compiler_params=_cparams("arbitrary"),
        name="combine",
    )(tables["loff"], tables["c8"], tables["dst"], tables["nslot"],
      h, route_i, tables["offrow"], g, y_pad)


def _expert_kernel(be_ref, slot_ref, next_ref, nact_ref, x_ref, wg_hbm, wu_hbm, wd_hbm, *rest,
                   layer, recycled):
    (y_hbm, wg_f, wu_f, wg_b, wu_b, wd_s, ybuf, zero_buf, sem, ysem, zsem) = (
        rest[1:] if recycled else rest)
    i = pl.program_id(0)
    n_active = nact_ref[0]
    staged = ((wg_hbm, wg_f, wg_b), (wu_hbm, wu_f, wu_b))

    def fetch(e, slot):
        copies = [pltpu.make_async_copy(hbm.at[layer, e], f32.at[slot], sem.at[slot, k])
                  for k, (hbm, f32, _) in enumerate(staged)]
        return copies + [pltpu.make_async_copy(wd_hbm.at[e], wd_s.at[slot], sem.at[slot, len(staged)])]

    def block_rows(b):
        return y_hbm.at[pl.ds(pl.multiple_of(b * MOE_BLOCK, MOE_BLOCK), MOE_BLOCK)]

    def put(b):
        return pltpu.make_async_copy(ybuf.at[b % 2], block_rows(b), ysem.at[b % 2])

    def tail_copy(b):
        return pltpu.make_async_copy(zero_buf, block_rows(b), zsem)

    if not recycled:
        @pl.when(i == 0)
        def _():
            zero_buf[...] = jnp.zeros_like(zero_buf)
            _loop(n_active, N_MOE_BLOCKS, lambda b: tail_copy(b).start())

        @pl.when(i == pl.num_programs(0) - 1)
        def _():
            _loop(n_active, N_MOE_BLOCKS, lambda b: tail_copy(b).wait())

    @pl.when(i < n_active)
    def _():
        e = be_ref[i]
        slot = slot_ref[i]

        def request(ahead, block):
            nxt = next_ref[(ahead - 1) * N_MOE_BLOCKS + block]

            @pl.when(nxt >= 0)
            def _():
                for cp in fetch(nxt, (slot + ahead) % W_STAGES):
                    cp.start()

        @pl.when((i == 0) | (e != be_ref[jnp.maximum(i - 1, 0)]))
        def _():
            @pl.when(i == 0)
            def _():
                for cp in fetch(e, slot):
                    cp.start()
                for ahead in range(1, W_STAGES - 1):
                    request(ahead, i)
            for cp in fetch(e, slot):
                cp.wait()
            request(W_STAGES - 1, i)
            for _, f32, b16 in staged:
                b16[...] = f32[slot].astype(BF16)

        x_hi, x_lo = _unpack_halves(x_ref[:, 0:HALF])
        row_w = lax.bitcast_convert_type(x_ref[:, HALF:HALF + 1], F32)
        gate = _dot(x_hi, wg_b[0:HALF, :]) + _dot(x_lo, wg_b[HALF:D_MODEL, :])
        up = _dot(x_hi, wu_b[0:HALF, :]) + _dot(x_lo, wu_b[HALF:D_MODEL, :])
        hmid = (jax.nn.silu(gate) * up).astype(BF16)
        y = (_dot(hmid, wd_s[slot]) * row_w).astype(BF16).astype(F32)
        ybuf[i % 2] = _pack_halves(y[:, 0:HALF], y[:, HALF:D_MODEL])
        put(i).start()

        @pl.when(i > 0)
        def _():
            put(i - 1).wait()

        @pl.when(i == n_active - 1)
        def _():
            put(i).wait()


def _experts(layer, tables, x_pad, w_gate, w_up, w_down, old_rows=None):
    hbm = pl.BlockSpec(memory_space=pl.ANY)
    up_shape, down_shape = (D_MODEL, D_EXPERT), (D_EXPERT, D_MODEL)
    recycled = old_rows is not None
    scalars = (tables["blk_expert"], tables["blk_slot"], tables["blk_next"], tables["n_active"])
    operands = (x_pad, w_gate, w_up, w_down) + ((old_rows,) if recycled else ())
    grid_spec = pltpu.PrefetchScalarGridSpec(
        num_scalar_prefetch=len(scalars),
        grid=(N_MOE_BLOCKS,),
        in_specs=[pl.BlockSpec((MOE_BLOCK, X_WORDS),
                               lambda i, be, sl, nx, na: (jnp.minimum(i, na[0] - 1), 0)),
                  hbm, hbm, hbm] + [hbm] * recycled,
        out_specs=hbm,
        scratch_shapes=[pltpu.VMEM((W_STAGES,) + up_shape, F32),
                        pltpu.VMEM((W_STAGES,) + up_shape, F32),
                        pltpu.VMEM(up_shape, BF16), pltpu.VMEM(up_shape, BF16),
                        pltpu.VMEM((W_STAGES,) + down_shape, BF16),
                        pltpu.VMEM((2, MOE_BLOCK, HALF), U32),
                        pltpu.VMEM((MOE_BLOCK, HALF), U32),
                        pltpu.SemaphoreType.DMA((W_STAGES, 3)),
                        pltpu.SemaphoreType.DMA((2,)),
                        pltpu.SemaphoreType.DMA(())],
    )
    return pl.pallas_call(
        functools.partial(_expert_kernel, layer=layer, recycled=recycled),
        grid_spec=grid_spec,
        out_shape=jax.ShapeDtypeStruct((PAD_ROWS, HALF), U32),
        input_output_aliases={len(scalars) + len(operands) - 1: 0} if recycled else {},
        compiler_params=_cparams("arbitrary"),
        name="experts",
    )(*scalars, *operands)


def _routing_tables(tile_counts):
    cnt = tile_counts[:, 0, EXPERT_LANE0:EXPERT_LANE0 + N_EXPERTS].astype(I32)
    c8 = (cnt + RUN_ALIGN - 1) // RUN_ALIGN * RUN_ALIGN
    loff = jnp.cumsum(c8, axis=1) - c8
    nslot = jnp.sum(c8, axis=1)
    tot = jnp.sum(c8, axis=0)
    padded = (tot + MOE_BLOCK - 1) // MOE_BLOCK * MOE_BLOCK
    end = jnp.cumsum(padded)
    base = end - padded
    dst = base[None, :] + jnp.cumsum(c8, axis=0) - c8
    blk_start = jnp.arange(N_MOE_BLOCKS, dtype=I32) * MOE_BLOCK
    blk_expert = jnp.minimum(jnp.sum((end[None, :] <= blk_start[:, None]).astype(I32), axis=1),
                             N_EXPERTS - 1).astype(I32)
    offrow = jnp.pad(loff.astype(F32),
                     ((0, 0), (EXPERT_LANE0, ROUTER_LANES - EXPERT_LANE0 - N_EXPERTS)))
    offrow = jnp.broadcast_to(offrow[:, None, :], (N_TILES, SUBLANES, ROUTER_LANES))
    experts = jnp.arange(N_EXPERTS, dtype=I32)
    present = padded > 0
    ordinal = jnp.cumsum(present.astype(I32)) - 1
    at_or_after = lax.cummin(jnp.where(present, experts, N_EXPERTS), reverse=True)
    after = jnp.concatenate([at_or_after[1:], jnp.full((1,), N_EXPERTS, I32)])
    hop = jnp.concatenate([after, jnp.full((1,), N_EXPERTS, I32)])
    pick = (blk_expert[:, None] == experts[None, :]).astype(I32)
    blk_slot = jnp.sum(pick * (ordinal % W_STAGES)[None, :], axis=1).astype(I32)
    ahead, blk_next = experts, []
    for _ in range(W_STAGES - 1):
        ahead = jnp.sum((ahead[:, None] == jnp.arange(N_EXPERTS + 1)[None, :]) * hop[None, :], axis=1)
        per_block = jnp.sum(pick * ahead[None, :], axis=1)
        blk_next.append(jnp.where(per_block >= N_EXPERTS, -1, per_block))
    blk_next = jnp.concatenate(blk_next).astype(I32)
    return {
        "blk_slot": blk_slot, "blk_next": blk_next,
        "loff": loff.reshape(-1).astype(I32), "c8": c8.reshape(-1).astype(I32),
        "dst": dst.reshape(-1).astype(I32), "nslot": nslot.astype(I32),
        "zdst": (base + tot).astype(I32), "zcnt": (padded - tot).astype(I32),
        "offrow": offrow, "blk_expert": blk_expert,
        "n_active": (end[-1:] // MOE_BLOCK).astype(I32),
    }


def _split_router_weights(w_router_group, w_router_expert):
    w_e = w_router_expert.transpose(1, 0, 2).reshape(D_MODEL, N_EXPERTS)
    w = jnp.concatenate([w_router_group, w_e], axis=1)
    w = jnp.pad(w, ((0, 0), (0, ROUTER_LANES - w.shape[1])))
    hi = w.astype(BF16)
    lo = (w - hi.astype(F32)).astype(BF16)
    return jnp.concatenate([hi, lo], axis=1)


def kernel(x, rel_bias, norm_mix_g, w_in, pool_w, pool_scale, w_proj_attn, w_proj_pool, w_out,
           norm_ffn_g, w_router_group, w_router_expert, w_gate_e, w_up_e, w_down_e, norm_final_g):
    h = x.reshape(TOKENS, D_MODEL)
    biases = [_band_bias(rel_bias, g, d) for g, (_, d) in enumerate(DILATION_PATTERNS)]
    x_pad = y_pad = None
    for l in range(DEPTH):
        qkv, u, gates, w_down_b = _project(l, h, norm_mix_g[l][None], w_in[l].astype(BF16), w_down_e)
        h, xn, route_i, route_w, tile_counts = _mix_out(
            h, _attention(qkv, biases), u, gates, pool_w[l].astype(BF16), pool_scale[l][None],
            w_proj_attn[l].astype(BF16), w_proj_pool[l].astype(BF16), w_out[l].astype(BF16),
            norm_ffn_g[l][None], _split_router_weights(w_router_group[l], w_router_expert[l]))
        tables = _routing_tables(tile_counts)
        x_pad = _dispatch(tables, xn, route_i, route_w, x_pad)
        y_pad = _experts(l, tables, x_pad, w_gate_e, w_up_e, w_down_b, y_pad)
        h = _combine(tables, h, route_i, y_pad, norm_final_g[None], l == DEPTH - 1)
    return h.reshape(BATCH, SEQ, D_MODEL)
```

```python
import functools
import math

import jax
import jax.numpy as jnp
from jax import lax
from jax.experimental import pallas as pl
from jax.experimental.pallas import tpu as pltpu

F32 = jnp.float32
BF16 = jnp.bfloat16
I32 = jnp.int32
U32 = jnp.uint32

D_MODEL = 1024
BATCH = 8
SEQ = 2048
TOKENS = BATCH * SEQ
DEPTH = 2

HEAD_DIM = 64
HEADS_PER_GROUP = 4
GROUP_WIDTH = HEADS_PER_GROUP * HEAD_DIM
DILATION_PATTERNS = ((128, 1), (512, 4), (2048, 16))
N_GROUPS = len(DILATION_PATTERNS)
N_ATTN_HEADS = N_GROUPS * HEADS_PER_GROUP
ATTN_WIDTH = N_ATTN_HEADS * HEAD_DIM
QKV_WIDTH = 3 * ATTN_WIDTH
GROUP_QKV = 3 * GROUP_WIDTH
N_SIDE = 64
assert all(w // (2 * d) == N_SIDE for w, d in DILATION_PATTERNS)
POOL_WINDOWS = (2, 4, 8, 16)
POOL_GROUP_WIDTH = 128
POOL_WIDTH = len(POOL_WINDOWS) * POOL_GROUP_WIDTH
POOL_HALO = max(POOL_WINDOWS) // 2
N_BRANCHES = 2
GATE_WIDTH = N_BRANCHES * D_MODEL
IN_WIDTH = QKV_WIDTH + POOL_WIDTH + GATE_WIDTH
N_BUCKETS = 32
MAX_DISTANCE = 1024
N_EXPERT_GROUPS = 8
EXPERTS_PER_GROUP = 8
N_EXPERTS = N_EXPERT_GROUPS * EXPERTS_PER_GROUP
TOP_K = 2
D_EXPERT = 512
N_ASSIGN = TOKENS * TOP_K
EPS = 1e-6
NEG_INF = -1e30
LOG2_E = math.log2(math.e)
Q_SCALE = HEAD_DIM ** -0.5 * LOG2_E

LANES = 128
SUBLANES = 8
ROW_TILE = 512
N_TILES = TOKENS // ROW_TILE
TILES_PER_SEQ = SEQ // ROW_TILE
Q_BLOCK = 128
K_BLOCK = Q_BLOCK + 2 * N_SIDE
BLOCKS_IN_FLIGHT = 4
ROUTER_LANES = 128
EXPERT_LANE0 = N_EXPERT_GROUPS
VMEM_LIMIT = 56 * 1024 * 1024

RUN_ALIGN = SUBLANES
MOE_BLOCK = 336
HALF = D_MODEL // 2
X_WORDS = HALF + LANES
SLOT_CHUNK = 256
RUN_UNROLL = 4
W_STAGES = 2
MAX_SLOTS = -(-(TOP_K * ROW_TILE + N_EXPERTS * (RUN_ALIGN - 1)) // SLOT_CHUNK) * SLOT_CHUNK
SLOT_CHUNKS = (tuple((s, ROW_TILE) for s in range(0, TOP_K * ROW_TILE, ROW_TILE))
               + tuple((s, SLOT_CHUNK) for s in range(TOP_K * ROW_TILE, MAX_SLOTS, SLOT_CHUNK)))
COMBINE_BASE = MAX_SLOTS - SLOT_CHUNK
N_MOE_BLOCKS = -(-(N_ASSIGN + N_TILES * N_EXPERTS * (RUN_ALIGN - 1)
                   + N_EXPERTS * (MOE_BLOCK - RUN_ALIGN)) // MOE_BLOCK)
PAD_ROWS = N_MOE_BLOCKS * MOE_BLOCK
HIGH_HALF = 0xFFFF0000


def _cparams(*sem):
    return pltpu.CompilerParams(dimension_semantics=sem, vmem_limit_bytes=VMEM_LIMIT)


def _rms(h, g):
    r = lax.rsqrt(jnp.mean(h * h, axis=-1, keepdims=True) + EPS)
    return (h * r) * g


def _dot(a, b):
    return jnp.dot(a, b, preferred_element_type=F32)


def _row_spec(width):
    return pl.BlockSpec((ROW_TILE, width), lambda i, *_: (i, 0))


def _full_spec(a):
    return pl.BlockSpec(a.shape, lambda i, *_: (0,) * a.ndim)


def _bits(x):
    return lax.bitcast_convert_type(x, U32)


def _pack_halves(a, b):
    return (_bits(a) & jnp.uint32(HIGH_HALF)) | (_bits(b) >> 16)


def _unpack_halves(words):
    hi = lax.bitcast_convert_type(words & jnp.uint32(HIGH_HALF), F32)
    lo = lax.bitcast_convert_type(words << 16, F32)
    return hi.astype(BF16), lo.astype(BF16)


def _proj_kernel(h_ref, g_ref, w_ref, wd_ref, q0_ref, q1_ref, q2_ref, u_ref, gate_ref, wdb_ref, slabs):
    wdb_ref[...] = wd_ref[...].astype(BF16)
    xn = _rms(h_ref[...], g_ref[...]).astype(BF16)
    n_slabs = GROUP_QKV // LANES
    for g, out_ref in enumerate((q0_ref, q1_ref, q2_ref)):
        dilation = DILATION_PATTERNS[g][1]
        q, k, v = (_dot(xn, w_ref[:, part * ATTN_WIDTH + g * GROUP_WIDTH:
                                  part * ATTN_WIDTH + (g + 1) * GROUP_WIDTH]) for part in range(3))
        res = jnp.concatenate([q * Q_SCALE, k, v], axis=1)
        if dilation == 1:
            out_ref[0, 0] = res.astype(BF16)
            continue
        for s in range(n_slabs):
            slabs[s] = res[:, s * LANES:(s + 1) * LANES]
        n = ROW_TILE // dilation
        for r in range(dilation):
            rows = [slabs[s, pl.ds(r, n, stride=dilation), :] for s in range(n_slabs)]
            out_ref[0, r] = jnp.concatenate(rows, axis=1).astype(BF16)
    u_ref[...] = _dot(xn, w_ref[:, QKV_WIDTH:QKV_WIDTH + POOL_WIDTH])
    gates = _dot(xn, w_ref[:, QKV_WIDTH + POOL_WIDTH:IN_WIDTH])
    gate_ref[...] = jax.nn.sigmoid(gates).astype(BF16)


def _project(layer, h, g, w_bf16, w_down):
    qkv_shapes, qkv_specs = [], []
    for _, d in DILATION_PATTERNS:
        qkv_shapes.append(jax.ShapeDtypeStruct((BATCH, d, SEQ // d, GROUP_QKV), BF16))
        qkv_specs.append(pl.BlockSpec((1, d, ROW_TILE // d, GROUP_QKV),
                                      lambda i: (i // TILES_PER_SEQ, 0, i % TILES_PER_SEQ, 0)))
    layer_rows = N_EXPERTS * D_EXPERT
    step_rows = layer_rows // N_TILES
    res = pl.pallas_call(
        _proj_kernel,
        grid=(N_TILES,),
        in_specs=[_row_spec(D_MODEL), _full_spec(g), _full_spec(w_bf16),
                  pl.BlockSpec((step_rows, D_MODEL), lambda i: (layer * N_TILES + i, 0))],
        out_specs=qkv_specs + [_row_spec(POOL_WIDTH), _row_spec(GATE_WIDTH),
                               pl.BlockSpec((step_rows, D_MODEL), lambda i: (i, 0))],
        out_shape=qkv_shapes + [jax.ShapeDtypeStruct((TOKENS, POOL_WIDTH), F32),
                                jax.ShapeDtypeStruct((TOKENS, GATE_WIDTH), BF16),
                                jax.ShapeDtypeStruct((layer_rows, D_MODEL), BF16)],
        scratch_shapes=[pltpu.VMEM((GROUP_QKV // LANES, ROW_TILE, LANES), F32)],
        compiler_params=_cparams("parallel"),
        name="proj",
    )(h, g, w_bf16, w_down.reshape(DEPTH * layer_rows, D_MODEL))
    w_down_b = res[N_GROUPS + 2].reshape(N_EXPERTS, D_EXPERT, D_MODEL)
    return res[:N_GROUPS], res[N_GROUPS], res[N_GROUPS + 1], w_down_b


def _attn_kernel(q0_ref, q1_ref, q2_ref, b0_ref, b1_ref, b2_ref, o_ref, pads, num, den, top):
    groups = ((q2_ref, b2_ref, 2), (q1_ref, b1_ref, 1), (q0_ref, b0_ref, 0))
    for order, (qkv_ref, bias_ref, g) in enumerate(groups):
        _attn_group(qkv_ref, bias_ref, pads, num, den, top,
                    dilation=DILATION_PATTERNS[g][1], first=order == 0)

    def finish(c, carry):
        rows = pl.ds(pl.multiple_of(c * ROW_TILE, ROW_TILE), ROW_TILE)
        merged = [num[half, rows, :] / den[half, rows, :] for half in range(GROUP_WIDTH // LANES)]
        o_ref[0, rows, :] = jnp.concatenate(merged, axis=1).astype(BF16)
        return carry

    lax.fori_loop(0, SEQ // ROW_TILE, finish, 0)


def _attn_group(qkv_ref, bias_ref, pads, num, den, top, *, dilation, first):
    sub_len = SEQ // dilation
    n_blocks = sub_len // Q_BLOCK
    head_of_lane = lax.broadcasted_iota(I32, (1, GROUP_WIDTH), 1) // HEAD_DIM

    def by_head(cols):
        out = cols[HEADS_PER_GROUP - 1]
        for h in range(HEADS_PER_GROUP - 2, -1, -1):
            out = jnp.where(head_of_lane == h, cols[h], out)
        return out

    window = sub_len + 2 * N_SIDE
    assert window % Q_BLOCK == 0
    ways = min(dilation, max(1, 2 * BLOCKS_IN_FLIGHT // n_blocks), pads.shape[1] // window)
    zpad = jnp.zeros((N_SIDE, GROUP_WIDTH), BF16)
    for way in range(ways):
        for kv in range(2):
            pads[kv, way * window:way * window + N_SIDE, :] = zpad
            pads[kv, (way + 1) * window - N_SIDE:(way + 1) * window, :] = zpad

    def load_keys(r, way):
        rows = slice(way * window + N_SIDE, way * window + N_SIDE + sub_len)
        pads[0, rows, :] = qkv_ref[0, r, :, GROUP_WIDTH:2 * GROUP_WIDTH]
        pads[1, rows, :] = qkv_ref[0, r, :, 2 * GROUP_WIDTH:3 * GROUP_WIDTH]

    def block(r, way, i):
        static = isinstance(i, int)
        r0 = i * Q_BLOCK if static else pl.multiple_of(i * Q_BLOCK, Q_BLOCK)
        qb = qkv_ref[0, r, pl.ds(r0, Q_BLOCK), 0:GROUP_WIDTH]
        k0 = way * window + r0
        k0 = k0 if static else pl.multiple_of(k0, Q_BLOCK)
        kw = pads[0, pl.ds(k0, K_BLOCK), :]
        vw = pads[1, pl.ds(k0, K_BLOCK), :]
        is_first, is_last = i == 0, i == n_blocks - 1
        edge = (int(is_first) + 2 * int(is_last) if static
                else is_first.astype(I32) + 2 * is_last.astype(I32))
        zero = jnp.zeros_like(qb)
        q_heads = jnp.concatenate(
            [jnp.where(head_of_lane == h, qb, zero) for h in range(HEADS_PER_GROUP)], axis=0)
        s = lax.dot_general(q_heads, kw, (((1,), (1,)), ((), ())), preferred_element_type=F32)
        s = s.reshape(HEADS_PER_GROUP, Q_BLOCK, K_BLOCK) + bias_ref[edge]
        m = jnp.max(s, axis=-1, keepdims=True)
        p = jnp.exp2(s - m)
        psum = jnp.sum(p, axis=-1, keepdims=True)
        pb = p.astype(BF16)
        p_cat = jnp.concatenate([pb[h] for h in range(HEADS_PER_GROUP)], axis=1)
        zv = jnp.zeros_like(vw)
        v_heads = jnp.concatenate(
            [jnp.where(head_of_lane == h, vw, zv) for h in range(HEADS_PER_GROUP)], axis=0)
        o = _dot(p_cat, v_heads)
        lse = m + jnp.log2(psum)
        o = o / by_head([psum[h] for h in range(HEADS_PER_GROUP)])
        lse_lanes = by_head([lse[h] for h in range(HEADS_PER_GROUP)])
        if dilation == 1:
            rows = pl.ds(r0, Q_BLOCK)
        else:
            rows = pl.ds(r + dilation * r0, Q_BLOCK, stride=dilation)
        for half in range(GROUP_WIDTH // LANES):
            lanes = slice(half * LANES, (half + 1) * LANES)
            if first:
                num[half, rows, :] = o[:, lanes]
                den[half, rows, :] = jnp.ones((Q_BLOCK, LANES), F32)
                top[half, rows, :] = lse_lanes[:, lanes]
            else:
                old = top[half, rows, :]
                new = jnp.maximum(old, lse_lanes[:, lanes])
                keep = jnp.exp2(old - new)
                add = jnp.exp2(lse_lanes[:, lanes] - new)
                num[half, rows, :] = keep * num[half, rows, :] + add * o[:, lanes]
                den[half, rows, :] = keep * den[half, rows, :] + add
                top[half, rows, :] = new

    flight = min(n_blocks, BLOCKS_IN_FLIGHT * (2 if dilation == 1 else 1))
    assert n_blocks % flight == 0

    def blocks_of(r, way):
        if n_blocks == 1:
            block(r, way, 0)
        else:
            def several(j, carry):
                for k in range(flight):
                    block(r, way, flight * j + k)
                return carry
            lax.fori_loop(0, n_blocks // flight, several, 0)

    if dilation == 1:
        load_keys(0, 0)
        blocks_of(0, 0)
    else:
        def residues(j, carry):
            for way in range(ways):
                load_keys(ways * j + way, way)
            for way in range(ways):
                blocks_of(ways * j + way, way)
            return carry
        lax.fori_loop(0, dilation // ways, residues, 0)


def _t5_bucket(rel):
    nb = N_BUCKETS // 2
    ret = jnp.where(rel > 0, nb, 0)
    n = jnp.abs(rel)
    max_exact = nb // 2
    nf = jnp.maximum(n, max_exact).astype(F32)
    large = max_exact + (jnp.log(nf / max_exact) / math.log(MAX_DISTANCE / max_exact)
                         * (nb - max_exact)).astype(I32)
    large = jnp.minimum(large, nb - 1)
    return ret + jnp.where(n < max_exact, n, large)


def _band_bias(rel_bias, group, dilation):
    qi = jnp.arange(Q_BLOCK)[:, None]
    ki = jnp.arange(K_BLOCK)[None, :]
    bucket = _t5_bucket((ki - N_SIDE - qi) * dilation)
    tab = rel_bias[:, group * HEADS_PER_GROUP:(group + 1) * HEADS_PER_GROUP]
    onehot = (bucket[:, :, None] == jnp.arange(N_BUCKETS)[None, None, :]).astype(F32)
    bias = jnp.einsum('qkb,bh->hqk', onehot, tab, precision=lax.Precision.HIGHEST).astype(F32)
    bias = bias * LOG2_E
    edge = jnp.arange(4)[:, None, None]
    ok = ((jnp.abs(ki - N_SIDE - qi) <= N_SIDE)[None]
          & (((edge & 1) == 0) | (ki >= N_SIDE)[None])
          & (((edge & 2) == 0) | (ki < Q_BLOCK + N_SIDE)[None]))
    return jnp.where(ok[:, None], bias[None], NEG_INF)


def _attention(qkv, biases):
    slab = pltpu.VMEM((GROUP_WIDTH // LANES, SEQ, LANES), F32)
    pads = pltpu.VMEM((2, SEQ + 2 * N_SIDE, GROUP_WIDTH), BF16)
    out = pl.pallas_call(
        _attn_kernel,
        grid=(BATCH,),
        in_specs=[pl.BlockSpec((1,) + a.shape[1:], lambda b: (b, 0, 0, 0)) for a in qkv]
                 + [_full_spec(b) for b in biases],
        out_specs=pl.BlockSpec((1, SEQ, GROUP_WIDTH), lambda b: (b, 0, 0)),
        out_shape=jax.ShapeDtypeStruct((BATCH, SEQ, GROUP_WIDTH), BF16),
        scratch_shapes=[pads, slab, slab, slab],
        compiler_params=_cparams("parallel"),
        name="attn",
    )(*qkv, *biases)
    return out.reshape(TOKENS, GROUP_WIDTH)


def _mixout_kernel(h_ref, attn_ref, u_ref, uprev_ref, unext_ref, gate_ref,
                   pw_ref, ps_ref, wpa_ref, wpp_ref, wo_ref, gffn_ref, wr_ref,
                   out_ref, xn_ref, ri_ref, rw_ref, cnt_ref, runs):
    j = pl.program_id(0) % TILES_PER_SEQ
    y_attn = _dot(attn_ref[...], wpa_ref[...])

    u = u_ref[...]
    prev = jnp.where(j == 0, 0.0, uprev_ref[0])
    nxt = jnp.where(j == TILES_PER_SEQ - 1, 0.0, unext_ref[0])
    ext_rows = ROW_TILE + 2 * POOL_HALO
    runs[:, ext_rows:ext_rows + POOL_HALO, :] = jnp.zeros((2, POOL_HALO, POOL_GROUP_WIDTH), F32)
    pos = j * ROW_TILE + lax.broadcasted_iota(I32, (ROW_TILE, 1), 0)
    mixed = []
    for gi, w in enumerate(POOL_WINDOWS):
        half = w // 2
        sl = slice(gi * POOL_GROUP_WIDTH, (gi + 1) * POOL_GROUP_WIDTH)
        runs[0, 0:POOL_HALO, :] = prev[:, sl]
        runs[0, POOL_HALO:POOL_HALO + ROW_TILE, :] = u[:, sl]
        runs[0, POOL_HALO + ROW_TILE:ext_rows, :] = nxt[:, sl]
        src, span = 0, 1
        while 2 * span < w:
            runs[1 - src, 0:ext_rows, :] = runs[src, 0:ext_rows, :] + runs[src, span:span + ext_rows, :]
            src, span = 1 - src, 2 * span
        lo = POOL_HALO - half
        acc = runs[src, lo:lo + ROW_TILE, :] + runs[src, lo + half:lo + half + ROW_TILE, :]
        cnt = (jnp.minimum(pos + half, SEQ) - jnp.maximum(pos - half, 0)).astype(F32)
        pooled = acc / cnt - u[:, sl]
        mixed.append(_dot(pooled.astype(BF16), pw_ref[gi]) * ps_ref[:, sl])
    y_pool = _dot(jnp.concatenate(mixed, axis=1).astype(BF16), wpp_ref[...])

    y = (gate_ref[:, 0:D_MODEL] * y_attn.astype(BF16)
         + gate_ref[:, D_MODEL:GATE_WIDTH] * y_pool.astype(BF16))
    h = h_ref[...] + _dot(y, wo_ref[...])
    out_ref[...] = h
    _route_rows(h, gffn_ref, wr_ref, xn_ref, ri_ref, rw_ref, cnt_ref)


def _mix_out(h, attn, u, gates, pool_w, pool_scale, w_proj_attn, w_proj_pool, w_out, g_ffn, wr_split):
    halo_blocks = ROW_TILE // POOL_HALO
    u3 = u.reshape(TOKENS // POOL_HALO, POOL_HALO, POOL_WIDTH)
    last = TOKENS // POOL_HALO - 1
    prev_spec = pl.BlockSpec((1, POOL_HALO, POOL_WIDTH),
                             lambda i: (jnp.maximum(i * halo_blocks - 1, 0), 0, 0))
    next_spec = pl.BlockSpec((1, POOL_HALO, POOL_WIDTH),
                             lambda i: (jnp.minimum((i + 1) * halo_blocks, last), 0, 0))
    weights = (pool_w, pool_scale, w_proj_attn, w_proj_pool, w_out, g_ffn, wr_split)
    return pl.pallas_call(
        _mixout_kernel,
        grid=(N_TILES,),
        in_specs=[_row_spec(D_MODEL), _row_spec(GROUP_WIDTH),
                  _row_spec(POOL_WIDTH), prev_spec, next_spec, _row_spec(GATE_WIDTH)]
                 + [_full_spec(w) for w in weights],
        out_specs=[_row_spec(D_MODEL), _row_spec(D_MODEL), _row_spec(ROUTER_LANES),
                   _row_spec(ROUTER_LANES),
                   pl.BlockSpec((1, SUBLANES, ROUTER_LANES), lambda i: (i, 0, 0))],
        out_shape=[jax.ShapeDtypeStruct((TOKENS, D_MODEL), F32),
                   jax.ShapeDtypeStruct((TOKENS, D_MODEL), BF16),
                   jax.ShapeDtypeStruct((TOKENS, ROUTER_LANES), I32),
                   jax.ShapeDtypeStruct((TOKENS, ROUTER_LANES), F32),
                   jax.ShapeDtypeStruct((N_TILES, SUBLANES, ROUTER_LANES), F32)],
        scratch_shapes=[pltpu.VMEM((2, ROW_TILE + 3 * POOL_HALO, POOL_GROUP_WIDTH), F32)],
        compiler_params=_cparams("parallel"),
        name="mixout",
    )(h, attn, u, u3, u3, gates, *weights)


def _route_rows(h, g_ref, wr_ref, xn_ref, ri_ref, rw_ref, cnt_ref):
    xn = _rms(h, g_ref[...])
    hi = xn.astype(BF16)
    xn_ref[...] = hi
    lo = (xn - hi.astype(F32)).astype(BF16)
    both = _dot(hi, wr_ref[...])
    lg = both[:, 0:ROUTER_LANES] + (both[:, ROUTER_LANES:] + _dot(lo, wr_ref[:, 0:ROUTER_LANES]))

    lane = lax.broadcasted_iota(I32, (ROW_TILE, ROUTER_LANES), 1)
    lanef = lane.astype(F32)
    low = jnp.float32(-3.0e38)
    far = jnp.float32(ROUTER_LANES)
    first = lambda hit: jnp.min(jnp.where(hit, lanef, far), axis=-1, keepdims=True)

    is_group = lane < N_EXPERT_GROUPS
    gl = jnp.where(is_group, lg, low)
    gmax = jnp.max(gl, axis=-1, keepdims=True)
    gidx = first(gl == gmax).astype(I32)
    gden = jnp.sum(jnp.where(is_group, jnp.exp(gl - gmax), 0.0), axis=-1, keepdims=True)
    g_p = 1.0 / gden

    in_group = ((lane >= EXPERT_LANE0) & (lane < EXPERT_LANE0 + N_EXPERTS)
                & ((lane - EXPERT_LANE0) // EXPERTS_PER_GROUP == gidx))
    el = jnp.where(in_group, lg, low)
    t1 = jnp.max(el, axis=-1, keepdims=True)
    l1 = first(in_group & (el == t1))
    rest = in_group & (lanef != l1)
    el2 = jnp.where(rest, lg, low)
    t2 = jnp.max(el2, axis=-1, keepdims=True)
    l2 = first(rest & (el2 == t2))
    e2 = jnp.exp(t2 - t1)
    w1 = g_p * (1.0 / (1.0 + e2))
    w2 = g_p * (e2 / (1.0 + e2))

    hit1 = lanef == l1
    hit2 = lanef == l2
    onehot = (hit1 | hit2).astype(BF16)
    ri = lax.broadcasted_iota(I32, (ROW_TILE, ROW_TILE), 0)
    ci = lax.broadcasted_iota(I32, (ROW_TILE, ROW_TILE), 1)
    before = (ci < ri).astype(BF16)
    seen = _dot(before, onehot)
    r1 = jnp.sum(jnp.where(hit1, seen, 0.0), axis=-1, keepdims=True)
    r2 = jnp.sum(jnp.where(hit2, seen, 0.0), axis=-1, keepdims=True)

    packed = jnp.zeros((ROW_TILE, ROUTER_LANES), F32)
    for k, v in enumerate((l1, l2, r1, r2)):
        packed = jnp.where(lane == k, v, packed)
    ri_ref[...] = packed.astype(I32)
    rw_ref[...] = jnp.where(lane == 0, w1, jnp.where(lane == 1, w2, 0.0))
    counts = jnp.sum(onehot.astype(F32), axis=0, keepdims=True)
    cnt_ref[0] = jnp.broadcast_to(counts, (SUBLANES, ROUTER_LANES))


def _slots(ri_ref, off_ref):
    ri = ri_ref[...]
    lane = lax.broadcasted_iota(I32, (ROW_TILE, ROUTER_LANES), 1)
    off = off_ref[0, 0:1, :]
    pick = lambda k: jnp.sum(jnp.where(lane == ri[:, k:k + 1], off, 0.0), axis=-1, keepdims=True)
    return (pick(0) + ri[:, 2:3].astype(F32), pick(1) + ri[:, 3:4].astype(F32))


def _as_rows(cols):
    eye = (lax.broadcasted_iota(I32, (ROW_TILE, ROW_TILE), 0)
           == lax.broadcasted_iota(I32, (ROW_TILE, ROW_TILE), 1))
    return [jnp.sum(jnp.where(eye, c, 0.0), axis=0, keepdims=True) for c in cols]


def _run_copy(tile, e, loff_s, c8_s, dst_s, buf_ref, slot, hbm_ref, sem, to_hbm):
    k = tile * N_EXPERTS + e
    n = pl.multiple_of(c8_s[k], RUN_ALIGN)
    vm = buf_ref.at[slot, pl.ds(pl.multiple_of(loff_s[k], RUN_ALIGN), n)]
    hb = hbm_ref.at[pl.ds(pl.multiple_of(dst_s[k], RUN_ALIGN), n)]
    src, dst = (vm, hb) if to_hbm else (hb, vm)
    return n, pltpu.make_async_copy(src, dst, sem.at[slot])


def _loop(lo, hi, fn, unroll=1):
    def body(e, carry):
        fn(e)
        return carry
    lax.fori_loop(lo, hi, body, 0, unroll=unroll)


def _wait_rows(n, buf_ref, slot, hbm_ref, sem, to_hbm):
    vm = buf_ref.at[slot, pl.ds(0, n)]
    hb = hbm_ref.at[pl.ds(0, n)]
    src, dst = (vm, hb) if to_hbm else (hb, vm)
    pltpu.make_async_copy(src, dst, sem.at[slot]).wait()


def _start(n, cp):
    @pl.when(n > 0)
    def _():
        cp.start()


def _wait(n, cp):
    @pl.when(n > 0)
    def _():
        cp.wait()


def _dispatch_kernel(loff_s, c8_s, dst_s, nslot_s, zdst_s, zcnt_s, nact_s,
                     xn_ref, ri_ref, rw_ref, off_ref, *rest, recycled):
    xpad_hbm, sorted_buf, zero_buf, sem, zsem = rest[1:] if recycled else rest
    i = pl.program_id(0)
    slot = i % 2
    last = pl.num_programs(0) - 1

    def zero_copy(e):
        n = pl.multiple_of(zcnt_s[e], RUN_ALIGN)
        dst = xpad_hbm.at[pl.ds(pl.multiple_of(zdst_s[e], RUN_ALIGN), n)]
        return n, pltpu.make_async_copy(zero_buf.at[pl.ds(0, n)], dst, zsem)

    def tail_copy(b):
        dst = xpad_hbm.at[pl.ds(pl.multiple_of(b * MOE_BLOCK, MOE_BLOCK), MOE_BLOCK)]
        return pltpu.make_async_copy(zero_buf, dst, zsem)

    @pl.when(i == 0)
    def _():
        zero_buf[...] = jnp.zeros_like(zero_buf)
        _loop(0, N_EXPERTS, lambda e: _start(*zero_copy(e)))
        if not recycled:
            _loop(nact_s[0], N_MOE_BLOCKS, lambda b: tail_copy(b).start())

    s1, s2 = _slots(ri_ref, off_ref)
    rw = rw_ref[...]
    s1_row, s2_row, w1_row, w2_row = _as_rows([s1, s2, rw[:, 0:1], rw[:, 1:2]])
    xn = xn_ref[...]
    for first, size in SLOT_CHUNKS:
        @pl.when(first < nslot_s[i])
        def _():
            rows = slice(first, first + size)
            lane = lax.broadcasted_iota(I32, (size, LANES), 1)
            srow = (first + lax.broadcasted_iota(I32, (size, ROW_TILE), 0)).astype(F32)
            hit1 = srow == s1_row
            hit2 = srow == s2_row
            xs = _dot((hit1 | hit2).astype(BF16), xn)
            sorted_buf[slot, rows, 0:HALF] = _pack_halves(xs[:, 0:HALF], xs[:, HALF:D_MODEL])
            ws = jnp.sum(jnp.where(hit1, w1_row, 0.0) + jnp.where(hit2, w2_row, 0.0),
                         axis=-1, keepdims=True)
            sorted_buf[slot, rows, HALF:X_WORDS] = jnp.where(lane == 0, _bits(ws), jnp.uint32(0))

    copy = lambda tile, sl: (lambda e: _run_copy(tile, e, loff_s, c8_s, dst_s, sorted_buf, sl,
                                                 xpad_hbm, sem, True))
    mine = copy(i, slot)
    _loop(0, N_EXPERTS, lambda e: _start(*mine(e)), unroll=RUN_UNROLL)
    tile_rows = lambda t: pl.multiple_of(nslot_s[t], RUN_ALIGN)

    @pl.when(i > 0)
    def _():
        _wait_rows(tile_rows(i - 1), sorted_buf, 1 - slot, xpad_hbm, sem, True)

    @pl.when(i == last)
    def _():
        _wait_rows(tile_rows(i), sorted_buf, slot, xpad_hbm, sem, True)
        _loop(0, N_EXPERTS, lambda e: _wait(*zero_copy(e)))
        if not recycled:
            _loop(nact_s[0], N_MOE_BLOCKS, lambda b: tail_copy(b).wait())


def _dispatch(tables, xn, route_i, route_w, old_rows=None):
    tile_row = pl.BlockSpec((1, SUBLANES, ROUTER_LANES), lambda i, *_: (i, 0, 0))
    hbm = pl.BlockSpec(memory_space=pl.ANY)
    recycled = old_rows is not None
    scalars = (tables["loff"], tables["c8"], tables["dst"], tables["nslot"], tables["zdst"],
               tables["zcnt"], tables["n_active"])
    operands = (xn, route_i, route_w, tables["offrow"]) + ((old_rows,) if recycled else ())
    grid_spec = pltpu.PrefetchScalarGridSpec(
        num_scalar_prefetch=len(scalars),
        grid=(N_TILES,),
        in_specs=[_row_spec(D_MODEL), _row_spec(ROUTER_LANES), _row_spec(ROUTER_LANES), tile_row]
                 + [hbm] * recycled,
        out_specs=hbm,
        scratch_shapes=[pltpu.VMEM((2, MAX_SLOTS, X_WORDS), U32),
                        pltpu.VMEM((MOE_BLOCK, X_WORDS), U32),
                        pltpu.SemaphoreType.DMA((2,)),
                        pltpu.SemaphoreType.DMA(())],
    )
    return pl.pallas_call(
        functools.partial(_dispatch_kernel, recycled=recycled),
        grid_spec=grid_spec,
        out_shape=jax.ShapeDtypeStruct((PAD_ROWS, X_WORDS), U32),
        input_output_aliases={len(scalars) + len(operands) - 1: 0} if recycled else {},
        compiler_params=_cparams("arbitrary"),
        name="dispatch",
    )(*scalars, *operands)


def _combine_kernel(loff_s, c8_s, dst_s, nslot_s,
                    h_ref, ri_ref, off_ref, g_ref, ypad_hbm, out_ref,
                    ybuf, sem, *, final_norm):
    i = pl.program_id(0)
    slot = i % 2
    last = pl.num_programs(0) - 1
    fetch = lambda tile, sl: (lambda e: _run_copy(tile, e, loff_s, c8_s, dst_s, ybuf, sl,
                                                  ypad_hbm, sem, False))

    @pl.when(i == 0)
    def _():
        ybuf[...] = jnp.zeros_like(ybuf)
        first = fetch(i, slot)
        _loop(0, N_EXPERTS, lambda e: _start(*first(e)), unroll=RUN_UNROLL)

    @pl.when(i < last)
    def _():
        nxt = fetch(i + 1, 1 - slot)
        _loop(0, N_EXPERTS, lambda e: _start(*nxt(e)), unroll=RUN_UNROLL)

    _wait_rows(pl.multiple_of(nslot_s[i], RUN_ALIGN), ybuf, slot, ypad_hbm, sem, False)

    s1, s2 = _slots(ri_ref, off_ref)

    def gathered(first, size):
        scol = (first + lax.broadcasted_iota(I32, (ROW_TILE, size), 1)).astype(F32)
        pick = ((scol == s1) | (scol == s2)).astype(BF16)
        y_hi, y_lo = _unpack_halves(ybuf[slot, first:first + size, :])
        return jnp.concatenate([_dot(pick, y_hi), _dot(pick, y_lo)], axis=1)

    finish = (lambda v: _rms(v, g_ref[...])) if final_norm else (lambda v: v)
    h = h_ref[...] + gathered(0, COMBINE_BASE)
    spill = nslot_s[i] > COMBINE_BASE

    @pl.when(jnp.logical_not(spill))
    def _():
        out_ref[...] = finish(h)

    @pl.when(spill)
    def _():
        out_ref[...] = finish(h + gathered(COMBINE_BASE, MAX_SLOTS - COMBINE_BASE))


def _combine(tables, h, route_i, y_pad, g, final_norm):
    tile_row = pl.BlockSpec((1, SUBLANES, ROUTER_LANES), lambda i, *_: (i, 0, 0))
    grid_spec = pltpu.PrefetchScalarGridSpec(
        num_scalar_prefetch=4,
        grid=(N_TILES,),
        in_specs=[_row_spec(D_MODEL), _row_spec(ROUTER_LANES), tile_row, _full_spec(g),
                  pl.BlockSpec(memory_space=pl.ANY)],
        out_specs=_row_spec(D_MODEL),
        scratch_shapes=[pltpu.VMEM((2, MAX_SLOTS, HALF), U32),
                        pltpu.SemaphoreType.DMA((2,))],
    )
    return pl.pallas_call(
        functools.partial(_combine_kernel, final_norm=final_norm),
        grid_spec=grid_spec,
        out_shape=jax.ShapeDtypeStruct((TOKENS, D_MODEL), F32),
        compiler_params=_cparams("arbitrary"),
        name="combine",
    )(tables["loff"], tables["c8"], tables["dst"], tables["nslot"],
      h, route_i, tables["offrow"], g, y_pad)


def _expert_kernel(be_ref, slot_ref, next_ref, nact_ref, x_ref, wg_hbm, wu_hbm, wd_hbm, *rest,
                   layer, recycled):
    (y_hbm, wg_f, wu_f, wg_b, wu_b, wd_s, ybuf, zero_buf, sem, ysem, zsem) = (
        rest[1:] if recycled else rest)
    i = pl.program_id(0)
    n_active = nact_ref[0]
    staged = ((wg_hbm, wg_f, wg_b), (wu_hbm, wu_f, wu_b))

    def fetch(e, slot):
        copies = [pltpu.make_async_copy(hbm.at[layer, e], f32.at[slot], sem.at[slot, k])
                  for k, (hbm, f32, _) in enumerate(staged)]
        return copies + [pltpu.make_async_copy(wd_hbm.at[e], wd_s.at[slot], sem.at[slot, len(staged)])]

    def block_rows(b):
        return y_hbm.at[pl.ds(pl.multiple_of(b * MOE_BLOCK, MOE_BLOCK), MOE_BLOCK)]

    def put(b):
        return pltpu.make_async_copy(ybuf.at[b % 2], block_rows(b), ysem.at[b % 2])

    def tail_copy(b):
        return pltpu.make_async_copy(zero_buf, block_rows(b), zsem)

    if not recycled:
        @pl.when(i == 0)
        def _():
            zero_buf[...] = jnp.zeros_like(zero_buf)
            _loop(n_active, N_MOE_BLOCKS, lambda b: tail_copy(b).start())

        @pl.when(i == pl.num_programs(0) - 1)
        def _():
            _loop(n_active, N_MOE_BLOCKS, lambda b: tail_copy(b).wait())

    @pl.when(i < n_active)
    def _():
        e = be_ref[i]
        slot = slot_ref[i]

        def request(ahead, block):
            nxt = next_ref[(ahead - 1) * N_MOE_BLOCKS + block]

            @pl.when(nxt >= 0)
            def _():
                for cp in fetch(nxt, (slot + ahead) % W_STAGES):
                    cp.start()

        @pl.when((i == 0) | (e != be_ref[jnp.maximum(i - 1, 0)]))
        def _():
            @pl.when(i == 0)
            def _():
                for cp in fetch(e, slot):
                    cp.start()
                for ahead in range(1, W_STAGES - 1):
                    request(ahead, i)
            for cp in fetch(e, slot):
                cp.wait()
            request(W_STAGES - 1, i)
            for _, f32, b16 in staged:
                b16[...] = f32[slot].astype(BF16)

        x_hi, x_lo = _unpack_halves(x_ref[:, 0:HALF])
        row_w = lax.bitcast_convert_type(x_ref[:, HALF:HALF + 1], F32)
        gate = _dot(x_hi, wg_b[0:HALF, :]) + _dot(x_lo, wg_b[HALF:D_MODEL, :])
        up = _dot(x_hi, wu_b[0:HALF, :]) + _dot(x_lo, wu_b[HALF:D_MODEL, :])
        hmid = (jax.nn.silu(gate) * up).astype(BF16)
        y = (_dot(hmid, wd_s[slot]) * row_w).astype(BF16).astype(F32)
        ybuf[i % 2] = _pack_halves(y[:, 0:HALF], y[:, HALF:D_MODEL])
        put(i).start()

        @pl.when(i > 0)
        def _():
            put(i - 1).wait()

        @pl.when(i == n_active - 1)
        def _():
            put(i).wait()


def _experts(layer, tables, x_pad, w_gate, w_up, w_down, old_rows=None):
    hbm = pl.BlockSpec(memory_space=pl.ANY)
    up_shape, down_shape = (D_MODEL, D_EXPERT), (D_EXPERT, D_MODEL)
    recycled = old_rows is not None
    scalars = (tables["blk_expert"], tables["blk_slot"], tables["blk_next"], tables["n_active"])
    operands = (x_pad, w_gate, w_up, w_down) + ((old_rows,) if recycled else ())
    grid_spec = pltpu.PrefetchScalarGridSpec(
        num_scalar_prefetch=len(scalars),
        grid=(N_MOE_BLOCKS,),
        in_specs=[pl.BlockSpec((MOE_BLOCK, X_WORDS),
                               lambda i, be, sl, nx, na: (jnp.minimum(i, na[0] - 1), 0)),
                  hbm, hbm, hbm] + [hbm] * recycled,
        out_specs=hbm,
        scratch_shapes=[pltpu.VMEM((W_STAGES,) + up_shape, F32),
                        pltpu.VMEM((W_STAGES,) + up_shape, F32),
                        pltpu.VMEM(up_shape, BF16), pltpu.VMEM(up_shape, BF16),
                        pltpu.VMEM((W_STAGES,) + down_shape, BF16),
                        pltpu.VMEM((2, MOE_BLOCK, HALF), U32),
                        pltpu.VMEM((MOE_BLOCK, HALF), U32),
                        pltpu.SemaphoreType.DMA((W_STAGES, 3)),
                        pltpu.SemaphoreType.DMA((2,)),
                        pltpu.SemaphoreType.DMA(())],
    )
    return pl.pallas_call(
        functools.partial(_expert_kernel, layer=layer, recycled=recycled),
        grid_spec=grid_spec,
        out_shape=jax.ShapeDtypeStruct((PAD_ROWS, HALF), U32),
        input_output_aliases={len(scalars) + len(operands) - 1: 0} if recycled else {},
        compiler_params=_cparams("arbitrary"),
        name="experts",
    )(*scalars, *operands)


def _routing_tables(tile_counts):
    cnt = tile_counts[:, 0, EXPERT_LANE0:EXPERT_LANE0 + N_EXPERTS].astype(I32)
    c8 = (cnt + RUN_ALIGN - 1) // RUN_ALIGN * RUN_ALIGN
    loff = jnp.cumsum(c8, axis=1) - c8
    nslot = jnp.sum(c8, axis=1)
    tot = jnp.sum(c8, axis=0)
    padded = (tot + MOE_BLOCK - 1) // MOE_BLOCK * MOE_BLOCK
    end = jnp.cumsum(padded)
    base = end - padded
    dst = base[None, :] + jnp.cumsum(c8, axis=0) - c8
    blk_start = jnp.arange(N_MOE_BLOCKS, dtype=I32) * MOE_BLOCK
    blk_expert = jnp.minimum(jnp.sum((end[None, :] <= blk_start[:, None]).astype(I32), axis=1),
                             N_EXPERTS - 1).astype(I32)
    offrow = jnp.pad(loff.astype(F32),
                     ((0, 0), (EXPERT_LANE0, ROUTER_LANES - EXPERT_LANE0 - N_EXPERTS)))
    offrow = jnp.broadcast_to(offrow[:, None, :], (N_TILES, SUBLANES, ROUTER_LANES))
    experts = jnp.arange(N_EXPERTS, dtype=I32)
    present = padded > 0
    ordinal = jnp.cumsum(present.astype(I32)) - 1
    at_or_after = lax.cummin(jnp.where(present, experts, N_EXPERTS), reverse=True)
    after = jnp.concatenate([at_or_after[1:], jnp.full((1,), N_EXPERTS, I32)])
    hop = jnp.concatenate([after, jnp.full((1,), N_EXPERTS, I32)])
    pick = (blk_expert[:, None] == experts[None, :]).astype(I32)
    blk_slot = jnp.sum(pick * (ordinal % W_STAGES)[None, :], axis=1).astype(I32)
    ahead, blk_next = experts, []
    for _ in range(W_STAGES - 1):
        ahead = jnp.sum((ahead[:, None] == jnp.arange(N_EXPERTS + 1)[None, :]) * hop[None, :], axis=1)
        per_block = jnp.sum(pick * ahead[None, :], axis=1)
        blk_next.append(jnp.where(per_block >= N_EXPERTS, -1, per_block))
    blk_next = jnp.concatenate(blk_next).astype(I32)
    return {
        "blk_slot": blk_slot, "blk_next": blk_next,
        "loff": loff.reshape(-1).astype(I32), "c8": c8.reshape(-1).astype(I32),
        "dst": dst.reshape(-1).astype(I32), "nslot": nslot.astype(I32),
        "zdst": (base + tot).astype(I32), "zcnt": (padded - tot).astype(I32),
        "offrow": offrow, "blk_expert": blk_expert,
        "n_active": (end[-1:] // MOE_BLOCK).astype(I32),
    }


def _split_router_weights(w_router_group, w_router_expert):
    w_e = w_router_expert.transpose(1, 0, 2).reshape(D_MODEL, N_EXPERTS)
    w = jnp.concatenate([w_router_group, w_e], axis=1)
    w = jnp.pad(w, ((0, 0), (0, ROUTER_LANES - w.shape[1])))
    hi = w.astype(BF16)
    lo = (w - hi.astype(F32)).astype(BF16)
    return jnp.concatenate([hi, lo], axis=1)


def kernel(x, rel_bias, norm_mix_g, w_in, pool_w, pool_scale, w_proj_attn, w_proj_pool, w_out,
           norm_ffn_g, w_router_group, w_router_expert, w_gate_e, w_up_e, w_down_e, norm_final_g):
    h = x.reshape(TOKENS, D_MODEL)
    biases = [_band_bias(rel_bias, g, d) for g, (_, d) in enumerate(DILATION_PATTERNS)]
    x_pad = y_pad = None
    for l in range(DEPTH):
        qkv, u, gates, w_down_b = _project(l, h, norm_mix_g[l][None], w_in[l].astype(BF16), w_down_e)
        h, xn, route_i, route_w, tile_counts = _mix_out(
            h, _attention(qkv, biases), u, gates, pool_w[l].astype(BF16), pool_scale[l][None],
            w_proj_attn[l].astype(BF16), w_proj_pool[l].astype(BF16), w_out[l].astype(BF16),
            norm_ffn_g[l][None], _split_router_weights(w_router_group[l], w_router_expert[l]))
        tables = _routing_tables(tile_counts)
        x_pad = _dispatch(tables, xn, route_i, route_w, x_pad)
        y_pad = _experts(l, tables, x_pad, w_gate_e, w_up_e, w_down_b, y_pad)
        h = _combine(tables, h, route_i, y_pad, norm_final_g[None], l == DEPTH - 1)
    return h.reshape(BATCH, SEQ, D_MODEL)
```

```python
import functools
import math

import jax
import jax.numpy as jnp
from jax import lax
from jax.experimental import pallas as pl
from jax.experimental.pallas import tpu as pltpu

F32 = jnp.float32
BF16 = jnp.bfloat16
I32 = jnp.int32
U32 = jnp.uint32

D_MODEL = 1024
BATCH = 8
SEQ = 2048
TOKENS = BATCH * SEQ
DEPTH = 2

HEAD_DIM = 64
HEADS_PER_GROUP = 4
GROUP_WIDTH = HEADS_PER_GROUP * HEAD_DIM
DILATION_PATTERNS = ((128, 1), (512, 4), (2048, 16))
N_GROUPS = len(DILATION_PATTERNS)
N_ATTN_HEADS = N_GROUPS * HEADS_PER_GROUP
ATTN_WIDTH = N_ATTN_HEADS * HEAD_DIM
QKV_WIDTH = 3 * ATTN_WIDTH
GROUP_QKV = 3 * GROUP_WIDTH
N_SIDE = 64
assert all(w // (2 * d) == N_SIDE for w, d in DILATION_PATTERNS)
POOL_WINDOWS = (2, 4, 8, 16)
POOL_GROUP_WIDTH = 128
POOL_WIDTH = len(POOL_WINDOWS) * POOL_GROUP_WIDTH
POOL_HALO = max(POOL_WINDOWS) // 2
N_BRANCHES = 2
GATE_WIDTH = N_BRANCHES * D_MODEL
IN_WIDTH = QKV_WIDTH + POOL_WIDTH + GATE_WIDTH
N_BUCKETS = 32
MAX_DISTANCE = 1024
N_EXPERT_GROUPS = 8
EXPERTS_PER_GROUP = 8
N_EXPERTS = N_EXPERT_GROUPS * EXPERTS_PER_GROUP
TOP_K = 2
D_EXPERT = 512
N_ASSIGN = TOKENS * TOP_K
EPS = 1e-6
NEG_INF = -1e30
LOG2_E = math.log2(math.e)
Q_SCALE = HEAD_DIM ** -0.5 * LOG2_E

LANES = 128
SUBLANES = 8
ROW_TILE = 512
N_TILES = TOKENS // ROW_TILE
TILES_PER_SEQ = SEQ // ROW_TILE
Q_BLOCK = 128
K_BLOCK = Q_BLOCK + 2 * N_SIDE
BLOCKS_IN_FLIGHT = 4
ROUTER_LANES = 128
EXPERT_LANE0 = N_EXPERT_GROUPS
VMEM_LIMIT = 56 * 1024 * 1024

RUN_ALIGN = SUBLANES
MOE_BLOCK = 336
HALF = D_MODEL // 2
X_WORDS = HALF + LANES
SLOT_CHUNK = 256
RUN_UNROLL = 4
W_STAGES = 2
WEIGHT_DMA_PRIORITY = 1
MAX_SLOTS = -(-(TOP_K * ROW_TILE + N_EXPERTS * (RUN_ALIGN - 1)) // SLOT_CHUNK) * SLOT_CHUNK
SLOT_CHUNKS = (tuple((s, ROW_TILE) for s in range(0, TOP_K * ROW_TILE, ROW_TILE))
               + tuple((s, SLOT_CHUNK) for s in range(TOP_K * ROW_TILE, MAX_SLOTS, SLOT_CHUNK)))
COMBINE_BASE = MAX_SLOTS - SLOT_CHUNK
N_MOE_BLOCKS = -(-(N_ASSIGN + N_TILES * N_EXPERTS * (RUN_ALIGN - 1)
                   + N_EXPERTS * (MOE_BLOCK - RUN_ALIGN)) // MOE_BLOCK)
PAD_ROWS = N_MOE_BLOCKS * MOE_BLOCK
HIGH_HALF = 0xFFFF0000


def _cparams(*sem):
    return pltpu.CompilerParams(dimension_semantics=sem, vmem_limit_bytes=VMEM_LIMIT)


def _rms(h, g):
    r = lax.rsqrt(jnp.mean(h * h, axis=-1, keepdims=True) + EPS)
    return (h * r) * g


def _dot(a, b):
    return jnp.dot(a, b, preferred_element_type=F32)


def _row_spec(width):
    return pl.BlockSpec((ROW_TILE, width), lambda i, *_: (i, 0))


def _full_spec(a):
    return pl.BlockSpec(a.shape, lambda i, *_: (0,) * a.ndim)


def _bits(x):
    return lax.bitcast_convert_type(x, U32)


def _pack_halves(a, b):
    return (_bits(a) & jnp.uint32(HIGH_HALF)) | (_bits(b) >> 16)


def _unpack_halves(words):
    hi = lax.bitcast_convert_type(words & jnp.uint32(HIGH_HALF), F32)
    lo = lax.bitcast_convert_type(words << 16, F32)
    return hi.astype(BF16), lo.astype(BF16)


def _proj_kernel(h_ref, g_ref, w_ref, wd_ref, q0_ref, q1_ref, q2_ref, u_ref, gate_ref, wdb_ref, slabs):
    wdb_ref[...] = wd_ref[...].astype(BF16)
    xn = _rms(h_ref[...], g_ref[...]).astype(BF16)
    n_slabs = GROUP_QKV // LANES
    for g, out_ref in enumerate((q0_ref, q1_ref, q2_ref)):
        dilation = DILATION_PATTERNS[g][1]
        q, k, v = (_dot(xn, w_ref[:, part * ATTN_WIDTH + g * GROUP_WIDTH:
                                  part * ATTN_WIDTH + (g + 1) * GROUP_WIDTH]) for part in range(3))
        res = jnp.concatenate([q * Q_SCALE, k, v], axis=1)
        if dilation == 1:
            out_ref[0, 0] = res.astype(BF16)
            continue
        for s in range(n_slabs):
            slabs[s] = res[:, s * LANES:(s + 1) * LANES]
        n = ROW_TILE // dilation
        for r in range(dilation):
            rows = [slabs[s, pl.ds(r, n, stride=dilation), :] for s in range(n_slabs)]
            out_ref[0, r] = jnp.concatenate(rows, axis=1).astype(BF16)
    u_ref[...] = _dot(xn, w_ref[:, QKV_WIDTH:QKV_WIDTH + POOL_WIDTH])
    gates = _dot(xn, w_ref[:, QKV_WIDTH + POOL_WIDTH:IN_WIDTH])
    gate_ref[...] = jax.nn.sigmoid(gates).astype(BF16)


def _project(layer, h, g, w_bf16, w_down):
    qkv_shapes, qkv_specs = [], []
    for _, d in DILATION_PATTERNS:
        qkv_shapes.append(jax.ShapeDtypeStruct((BATCH, d, SEQ // d, GROUP_QKV), BF16))
        qkv_specs.append(pl.BlockSpec((1, d, ROW_TILE // d, GROUP_QKV),
                                      lambda i: (i // TILES_PER_SEQ, 0, i % TILES_PER_SEQ, 0)))
    layer_rows = N_EXPERTS * D_EXPERT
    step_rows = layer_rows // N_TILES
    res = pl.pallas_call(
        _proj_kernel,
        grid=(N_TILES,),
        in_specs=[_row_spec(D_MODEL), _full_spec(g), _full_spec(w_bf16),
                  pl.BlockSpec((step_rows, D_MODEL), lambda i: (layer * N_TILES + i, 0))],
        out_specs=qkv_specs + [_row_spec(POOL_WIDTH), _row_spec(GATE_WIDTH),
                               pl.BlockSpec((step_rows, D_MODEL), lambda i: (i, 0))],
        out_shape=qkv_shapes + [jax.ShapeDtypeStruct((TOKENS, POOL_WIDTH), F32),
                                jax.ShapeDtypeStruct((TOKENS, GATE_WIDTH), BF16),
                                jax.ShapeDtypeStruct((layer_rows, D_MODEL), BF16)],
        scratch_shapes=[pltpu.VMEM((GROUP_QKV // LANES, ROW_TILE, LANES), F32)],
        compiler_params=_cparams("parallel"),
        name="proj",
    )(h, g, w_bf16, w_down.reshape(DEPTH * layer_rows, D_MODEL))
    w_down_b = res[N_GROUPS + 2].reshape(N_EXPERTS, D_EXPERT, D_MODEL)
    return res[:N_GROUPS], res[N_GROUPS], res[N_GROUPS + 1], w_down_b


def _attn_kernel(q0_ref, q1_ref, q2_ref, b0_ref, b1_ref, b2_ref, o_ref, pads, num, den, top):
    groups = ((q2_ref, b2_ref, 2), (q1_ref, b1_ref, 1), (q0_ref, b0_ref, 0))
    for order, (qkv_ref, bias_ref, g) in enumerate(groups):
        _attn_group(qkv_ref, bias_ref, pads, num, den, top,
                    dilation=DILATION_PATTERNS[g][1], first=order == 0)

    def finish(c, carry):
        rows = pl.ds(pl.multiple_of(c * ROW_TILE, ROW_TILE), ROW_TILE)
        merged = [num[half, rows, :] / den[half, rows, :] for half in range(GROUP_WIDTH // LANES)]
        o_ref[0, rows, :] = jnp.concatenate(merged, axis=1).astype(BF16)
        return carry

    lax.fori_loop(0, SEQ // ROW_TILE, finish, 0)


def _attn_group(qkv_ref, bias_ref, pads, num, den, top, *, dilation, first):
    sub_len = SEQ // dilation
    n_blocks = sub_len // Q_BLOCK
    head_of_lane = lax.broadcasted_iota(I32, (1, GROUP_WIDTH), 1) // HEAD_DIM

    def by_head(cols):
        out = cols[HEADS_PER_GROUP - 1]
        for h in range(HEADS_PER_GROUP - 2, -1, -1):
            out = jnp.where(head_of_lane == h, cols[h], out)
        return out

    window = sub_len + 2 * N_SIDE
    assert window % Q_BLOCK == 0
    ways = min(dilation, max(1, 2 * BLOCKS_IN_FLIGHT // n_blocks), pads.shape[1] // window)
    zpad = jnp.zeros((N_SIDE, GROUP_WIDTH), BF16)
    for way in range(ways):
        for kv in range(2):
            pads[kv, way * window:way * window + N_SIDE, :] = zpad
            pads[kv, (way + 1) * window - N_SIDE:(way + 1) * window, :] = zpad

    def load_keys(r, way):
        rows = slice(way * window + N_SIDE, way * window + N_SIDE + sub_len)
        pads[0, rows, :] = qkv_ref[0, r, :, GROUP_WIDTH:2 * GROUP_WIDTH]
        pads[1, rows, :] = qkv_ref[0, r, :, 2 * GROUP_WIDTH:3 * GROUP_WIDTH]

    def block(r, way, i):
        static = isinstance(i, int)
        r0 = i * Q_BLOCK if static else pl.multiple_of(i * Q_BLOCK, Q_BLOCK)
        qb = qkv_ref[0, r, pl.ds(r0, Q_BLOCK), 0:GROUP_WIDTH]
        k0 = way * window + r0
        k0 = k0 if static else pl.multiple_of(k0, Q_BLOCK)
        kw = pads[0, pl.ds(k0, K_BLOCK), :]
        vw = pads[1, pl.ds(k0, K_BLOCK), :]
        is_first, is_last = i == 0, i == n_blocks - 1
        edge = (int(is_first) + 2 * int(is_last) if static
                else is_first.astype(I32) + 2 * is_last.astype(I32))
        zero = jnp.zeros_like(qb)
        q_heads = jnp.concatenate(
            [jnp.where(head_of_lane == h, qb, zero) for h in range(HEADS_PER_GROUP)], axis=0)
        s = lax.dot_general(q_heads, kw, (((1,), (1,)), ((), ())), preferred_element_type=F32)
        s = s.reshape(HEADS_PER_GROUP, Q_BLOCK, K_BLOCK) + bias_ref[edge]
        m = jnp.max(s, axis=-1, keepdims=True)
        p = jnp.exp2(s - m)
        psum = jnp.sum(p, axis=-1, keepdims=True)
        pb = p.astype(BF16)
        p_cat = jnp.concatenate([pb[h] for h in range(HEADS_PER_GROUP)], axis=1)
        zv = jnp.zeros_like(vw)
        v_heads = jnp.concatenate(
            [jnp.where(head_of_lane == h, vw, zv) for h in range(HEADS_PER_GROUP)], axis=0)
        o = _dot(p_cat, v_heads)
        lse = m + jnp.log2(psum)
        o = o / by_head([psum[h] for h in range(HEADS_PER_GROUP)])
        lse_lanes = by_head([lse[h] for h in range(HEADS_PER_GROUP)])
        if dilation == 1:
            rows = pl.ds(r0, Q_BLOCK)
        else:
            rows = pl.ds(r + dilation * r0, Q_BLOCK, stride=dilation)
        for half in range(GROUP_WIDTH // LANES):
            lanes = slice(half * LANES, (half + 1) * LANES)
            if first:
                num[half, rows, :] = o[:, lanes]
                den[half, rows, :] = jnp.ones((Q_BLOCK, LANES), F32)
                top[half, rows, :] = lse_lanes[:, lanes]
            else:
                old = top[half, rows, :]
                new = jnp.maximum(old, lse_lanes[:, lanes])
                keep = jnp.exp2(old - new)
                add = jnp.exp2(lse_lanes[:, lanes] - new)
                num[half, rows, :] = keep * num[half, rows, :] + add * o[:, lanes]
                den[half, rows, :] = keep * den[half, rows, :] + add
                top[half, rows, :] = new

    flight = min(n_blocks, BLOCKS_IN_FLIGHT * (2 if dilation == 1 else 1))
    assert n_blocks % flight == 0

    def blocks_of(r, way):
        if n_blocks == 1:
            block(r, way, 0)
        else:
            def several(j, carry):
                for k in range(flight):
                    block(r, way, flight * j + k)
                return carry
            lax.fori_loop(0, n_blocks // flight, several, 0)

    if dilation == 1:
        load_keys(0, 0)
        blocks_of(0, 0)
    else:
        def residues(j, carry):
            for way in range(ways):
                load_keys(ways * j + way, way)
            for way in range(ways):
                blocks_of(ways * j + way, way)
            return carry
        lax.fori_loop(0, dilation // ways, residues, 0)


def _t5_bucket(rel):
    nb = N_BUCKETS // 2
    ret = jnp.where(rel > 0, nb, 0)
    n = jnp.abs(rel)
    max_exact = nb // 2
    nf = jnp.maximum(n, max_exact).astype(F32)
    large = max_exact + (jnp.log(nf / max_exact) / math.log(MAX_DISTANCE / max_exact)
                         * (nb - max_exact)).astype(I32)
    large = jnp.minimum(large, nb - 1)
    return ret + jnp.where(n < max_exact, n, large)


def _band_bias(rel_bias, group, dilation):
    qi = jnp.arange(Q_BLOCK)[:, None]
    ki = jnp.arange(K_BLOCK)[None, :]
    bucket = _t5_bucket((ki - N_SIDE - qi) * dilation)
    tab = rel_bias[:, group * HEADS_PER_GROUP:(group + 1) * HEADS_PER_GROUP]
    onehot = (bucket[:, :, None] == jnp.arange(N_BUCKETS)[None, None, :]).astype(F32)
    bias = jnp.einsum('qkb,bh->hqk', onehot, tab, precision=lax.Precision.HIGHEST).astype(F32)
    bias = bias * LOG2_E
    edge = jnp.arange(4)[:, None, None]
    ok = ((jnp.abs(ki - N_SIDE - qi) <= N_SIDE)[None]
          & (((edge & 1) == 0) | (ki >= N_SIDE)[None])
          & (((edge & 2) == 0) | (ki < Q_BLOCK + N_SIDE)[None]))
    return jnp.where(ok[:, None], bias[None], NEG_INF)


def _attention(qkv, biases):
    slab = pltpu.VMEM((GROUP_WIDTH // LANES, SEQ, LANES), F32)
    pads = pltpu.VMEM((2, SEQ + 2 * N_SIDE, GROUP_WIDTH), BF16)
    out = pl.pallas_call(
        _attn_kernel,
        grid=(BATCH,),
        in_specs=[pl.BlockSpec((1,) + a.shape[1:], lambda b: (b, 0, 0, 0)) for a in qkv]
                 + [_full_spec(b) for b in biases],
        out_specs=pl.BlockSpec((1, SEQ, GROUP_WIDTH), lambda b: (b, 0, 0)),
        out_shape=jax.ShapeDtypeStruct((BATCH, SEQ, GROUP_WIDTH), BF16),
        scratch_shapes=[pads, slab, slab, slab],
        compiler_params=_cparams("parallel"),
        name="attn",
    )(*qkv, *biases)
    return out.reshape(TOKENS, GROUP_WIDTH)


def _mixout_kernel(h_ref, attn_ref, u_ref, uprev_ref, unext_ref, gate_ref,
                   pw_ref, ps_ref, wpa_ref, wpp_ref, wo_ref, gffn_ref, wr_ref,
                   out_ref, xn_ref, ri_ref, rw_ref, cnt_ref, runs):
    j = pl.program_id(0) % TILES_PER_SEQ
    y_attn = _dot(attn_ref[...], wpa_ref[...])

    u = u_ref[...]
    prev = jnp.where(j == 0, 0.0, uprev_ref[0])
    nxt = jnp.where(j == TILES_PER_SEQ - 1, 0.0, unext_ref[0])
    ext_rows = ROW_TILE + 2 * POOL_HALO
    runs[:, ext_rows:ext_rows + POOL_HALO, :] = jnp.zeros((2, POOL_HALO, POOL_GROUP_WIDTH), F32)
    pos = j * ROW_TILE + lax.broadcasted_iota(I32, (ROW_TILE, 1), 0)
    mixed = []
    for gi, w in enumerate(POOL_WINDOWS):
        half = w // 2
        sl = slice(gi * POOL_GROUP_WIDTH, (gi + 1) * POOL_GROUP_WIDTH)
        runs[0, 0:POOL_HALO, :] = prev[:, sl]
        runs[0, POOL_HALO:POOL_HALO + ROW_TILE, :] = u[:, sl]
        runs[0, POOL_HALO + ROW_TILE:ext_rows, :] = nxt[:, sl]
        src, span = 0, 1
        while 2 * span < w:
            runs[1 - src, 0:ext_rows, :] = runs[src, 0:ext_rows, :] + runs[src, span:span + ext_rows, :]
            src, span = 1 - src, 2 * span
        lo = POOL_HALO - half
        acc = runs[src, lo:lo + ROW_TILE, :] + runs[src, lo + half:lo + half + ROW_TILE, :]
        cnt = (jnp.minimum(pos + half, SEQ) - jnp.maximum(pos - half, 0)).astype(F32)
        pooled = acc / cnt - u[:, sl]
        mixed.append(_dot(pooled.astype(BF16), pw_ref[gi]) * ps_ref[:, sl])
    y_pool = _dot(jnp.concatenate(mixed, axis=1).astype(BF16), wpp_ref[...])

    y = (gate_ref[:, 0:D_MODEL] * y_attn.astype(BF16)
         + gate_ref[:, D_MODEL:GATE_WIDTH] * y_pool.astype(BF16))
    h = h_ref[...] + _dot(y, wo_ref[...])
    out_ref[...] = h
    _route_rows(h, gffn_ref, wr_ref, xn_ref, ri_ref, rw_ref, cnt_ref)


def _mix_out(h, attn, u, gates, pool_w, pool_scale, w_proj_attn, w_proj_pool, w_out, g_ffn, wr_split):
    halo_blocks = ROW_TILE // POOL_HALO
    u3 = u.reshape(TOKENS // POOL_HALO, POOL_HALO, POOL_WIDTH)
    last = TOKENS // POOL_HALO - 1
    prev_spec = pl.BlockSpec((1, POOL_HALO, POOL_WIDTH),
                             lambda i: (jnp.maximum(i * halo_blocks - 1, 0), 0, 0))
    next_spec = pl.BlockSpec((1, POOL_HALO, POOL_WIDTH),
                             lambda i: (jnp.minimum((i + 1) * halo_blocks, last), 0, 0))
    weights = (pool_w, pool_scale, w_proj_attn, w_proj_pool, w_out, g_ffn, wr_split)
    return pl.pallas_call(
        _mixout_kernel,
        grid=(N_TILES,),
        in_specs=[_row_spec(D_MODEL), _row_spec(GROUP_WIDTH),
                  _row_spec(POOL_WIDTH), prev_spec, next_spec, _row_spec(GATE_WIDTH)]
                 + [_full_spec(w) for w in weights],
        out_specs=[_row_spec(D_MODEL), _row_spec(D_MODEL), _row_spec(ROUTER_LANES),
                   _row_spec(ROUTER_LANES),
                   pl.BlockSpec((1, SUBLANES, ROUTER_LANES), lambda i: (i, 0, 0))],
        out_shape=[jax.ShapeDtypeStruct((TOKENS, D_MODEL), F32),
                   jax.ShapeDtypeStruct((TOKENS, D_MODEL), BF16),
                   jax.ShapeDtypeStruct((TOKENS, ROUTER_LANES), I32),
                   jax.ShapeDtypeStruct((TOKENS, ROUTER_LANES), F32),
                   jax.ShapeDtypeStruct((N_TILES, SUBLANES, ROUTER_LANES), F32)],
        scratch_shapes=[pltpu.VMEM((2, ROW_TILE + 3 * POOL_HALO, POOL_GROUP_WIDTH), F32)],
        compiler_params=_cparams("parallel"),
        name="mixout",
    )(h, attn, u, u3, u3, gates, *weights)


def _route_rows(h, g_ref, wr_ref, xn_ref, ri_ref, rw_ref, cnt_ref):
    xn = _rms(h, g_ref[...])
    hi = xn.astype(BF16)
    xn_ref[...] = hi
    lo = (xn - hi.astype(F32)).astype(BF16)
    both = _dot(hi, wr_ref[...])
    lg = both[:, 0:ROUTER_LANES] + (both[:, ROUTER_LANES:] + _dot(lo, wr_ref[:, 0:ROUTER_LANES]))

    lane = lax.broadcasted_iota(I32, (ROW_TILE, ROUTER_LANES), 1)
    lanef = lane.astype(F32)
    low = jnp.float32(-3.0e38)
    far = jnp.float32(ROUTER_LANES)
    first = lambda hit: jnp.min(jnp.where(hit, lanef, far), axis=-1, keepdims=True)

    is_group = lane < N_EXPERT_GROUPS
    gl = jnp.where(is_group, lg, low)
    gmax = jnp.max(gl, axis=-1, keepdims=True)
    gidx = first(gl == gmax).astype(I32)
    gden = jnp.sum(jnp.where(is_group, jnp.exp(gl - gmax), 0.0), axis=-1, keepdims=True)
    g_p = 1.0 / gden

    in_group = ((lane >= EXPERT_LANE0) & (lane < EXPERT_LANE0 + N_EXPERTS)
                & ((lane - EXPERT_LANE0) // EXPERTS_PER_GROUP == gidx))
    el = jnp.where(in_group, lg, low)
    t1 = jnp.max(el, axis=-1, keepdims=True)
    l1 = first(in_group & (el == t1))
    rest = in_group & (lanef != l1)
    el2 = jnp.where(rest, lg, low)
    t2 = jnp.max(el2, axis=-1, keepdims=True)
    l2 = first(rest & (el2 == t2))
    e2 = jnp.exp(t2 - t1)
    w1 = g_p * (1.0 / (1.0 + e2))
    w2 = g_p * (e2 / (1.0 + e2))

    hit1 = lanef == l1
    hit2 = lanef == l2
    onehot = (hit1 | hit2).astype(BF16)
    ri = lax.broadcasted_iota(I32, (ROW_TILE, ROW_TILE), 0)
    ci = lax.broadcasted_iota(I32, (ROW_TILE, ROW_TILE), 1)
    before = (ci < ri).astype(BF16)
    seen = _dot(before, onehot)
    r1 = jnp.sum(jnp.where(hit1, seen, 0.0), axis=-1, keepdims=True)
    r2 = jnp.sum(jnp.where(hit2, seen, 0.0), axis=-1, keepdims=True)

    packed = jnp.zeros((ROW_TILE, ROUTER_LANES), F32)
    for k, v in enumerate((l1, l2, r1, r2)):
        packed = jnp.where(lane == k, v, packed)
    ri_ref[...] = packed.astype(I32)
    rw_ref[...] = jnp.where(lane == 0, w1, jnp.where(lane == 1, w2, 0.0))
    counts = jnp.sum(onehot.astype(F32), axis=0, keepdims=True)
    cnt_ref[0] = jnp.broadcast_to(counts, (SUBLANES, ROUTER_LANES))


def _slots(ri_ref, off_ref):
    ri = ri_ref[...]
    lane = lax.broadcasted_iota(I32, (ROW_TILE, ROUTER_LANES), 1)
    off = off_ref[0, 0:1, :]
    pick = lambda k: jnp.sum(jnp.where(lane == ri[:, k:k + 1], off, 0.0), axis=-1, keepdims=True)
    return (pick(0) + ri[:, 2:3].astype(F32), pick(1) + ri[:, 3:4].astype(F32))


def _as_rows(cols):
    eye = (lax.broadcasted_iota(I32, (ROW_TILE, ROW_TILE), 0)
           == lax.broadcasted_iota(I32, (ROW_TILE, ROW_TILE), 1))
    return [jnp.sum(jnp.where(eye, c, 0.0), axis=0, keepdims=True) for c in cols]


def _run_copy(tile, e, loff_s, c8_s, dst_s, buf_ref, slot, hbm_ref, sem, to_hbm):
    k = tile * N_EXPERTS + e
    n = pl.multiple_of(c8_s[k], RUN_ALIGN)
    vm = buf_ref.at[slot, pl.ds(pl.multiple_of(loff_s[k], RUN_ALIGN), n)]
    hb = hbm_ref.at[pl.ds(pl.multiple_of(dst_s[k], RUN_ALIGN), n)]
    src, dst = (vm, hb) if to_hbm else (hb, vm)
    return n, pltpu.make_async_copy(src, dst, sem.at[slot])


def _loop(lo, hi, fn, unroll=1):
    def body(e, carry):
        fn(e)
        return carry
    lax.fori_loop(lo, hi, body, 0, unroll=unroll)


def _wait_rows(n, buf_ref, slot, hbm_ref, sem, to_hbm):
    vm = buf_ref.at[slot, pl.ds(0, n)]
    hb = hbm_ref.at[pl.ds(0, n)]
    src, dst = (vm, hb) if to_hbm else (hb, vm)
    pltpu.make_async_copy(src, dst, sem.at[slot]).wait()


def _start(n, cp):
    @pl.when(n > 0)
    def _():
        cp.start()


def _wait(n, cp):
    @pl.when(n > 0)
    def _():
        cp.wait()


def _dispatch_kernel(loff_s, c8_s, dst_s, nslot_s, zdst_s, zcnt_s, nact_s,
                     xn_ref, ri_ref, rw_ref, off_ref, *rest, recycled):
    xpad_hbm, sorted_buf, zero_buf, sem, zsem = rest[1:] if recycled else rest
    i = pl.program_id(0)
    slot = i % 2
    last = pl.num_programs(0) - 1

    def zero_copy(e):
        n = pl.multiple_of(zcnt_s[e], RUN_ALIGN)
        dst = xpad_hbm.at[pl.ds(pl.multiple_of(zdst_s[e], RUN_ALIGN), n)]
        return n, pltpu.make_async_copy(zero_buf.at[pl.ds(0, n)], dst, zsem)

    def tail_copy(b):
        dst = xpad_hbm.at[pl.ds(pl.multiple_of(b * MOE_BLOCK, MOE_BLOCK), MOE_BLOCK)]
        return pltpu.make_async_copy(zero_buf, dst, zsem)

    @pl.when(i == 0)
    def _():
        zero_buf[...] = jnp.zeros_like(zero_buf)
        _loop(0, N_EXPERTS, lambda e: _start(*zero_copy(e)))
        if not recycled:
            _loop(nact_s[0], N_MOE_BLOCKS, lambda b: tail_copy(b).start())

    s1, s2 = _slots(ri_ref, off_ref)
    rw = rw_ref[...]
    s1_row, s2_row, w1_row, w2_row = _as_rows([s1, s2, rw[:, 0:1], rw[:, 1:2]])
    xn = xn_ref[...]
    for first, size in SLOT_CHUNKS:
        @pl.when(first < nslot_s[i])
        def _():
            rows = slice(first, first + size)
            lane = lax.broadcasted_iota(I32, (size, LANES), 1)
            srow = (first + lax.broadcasted_iota(I32, (size, ROW_TILE), 0)).astype(F32)
            hit1 = srow == s1_row
            hit2 = srow == s2_row
            xs = _dot((hit1 | hit2).astype(BF16), xn)
            sorted_buf[slot, rows, 0:HALF] = _pack_halves(xs[:, 0:HALF], xs[:, HALF:D_MODEL])
            ws = jnp.sum(jnp.where(hit1, w1_row, 0.0) + jnp.where(hit2, w2_row, 0.0),
                         axis=-1, keepdims=True)
            sorted_buf[slot, rows, HALF:X_WORDS] = jnp.where(lane == 0, _bits(ws), jnp.uint32(0))

    copy = lambda tile, sl: (lambda e: _run_copy(tile, e, loff_s, c8_s, dst_s, sorted_buf, sl,
                                                 xpad_hbm, sem, True))
    mine = copy(i, slot)
    _loop(0, N_EXPERTS, lambda e: _start(*mine(e)), unroll=RUN_UNROLL)
    tile_rows = lambda t: pl.multiple_of(nslot_s[t], RUN_ALIGN)

    @pl.when(i > 0)
    def _():
        _wait_rows(tile_rows(i - 1), sorted_buf, 1 - slot, xpad_hbm, sem, True)

    @pl.when(i == last)
    def _():
        _wait_rows(tile_rows(i), sorted_buf, slot, xpad_hbm, sem, True)
        _loop(0, N_EXPERTS, lambda e: _wait(*zero_copy(e)))
        if not recycled:
            _loop(nact_s[0], N_MOE_BLOCKS, lambda b: tail_copy(b).wait())


def _dispatch(tables, xn, route_i, route_w, old_rows=None):
    tile_row = pl.BlockSpec((1, SUBLANES, ROUTER_LANES), lambda i, *_: (i, 0, 0))
    hbm = pl.BlockSpec(memory_space=pl.ANY)
    recycled = old_rows is not None
    scalars = (tables["loff"], tables["c8"], tables["dst"], tables["nslot"], tables["zdst"],
               tables["zcnt"], tables["n_active"])
    operands = (xn, route_i, route_w, tables["offrow"]) + ((old_rows,) if recycled else ())
    grid_spec = pltpu.PrefetchScalarGridSpec(
        num_scalar_prefetch=len(scalars),
        grid=(N_TILES,),
        in_specs=[_row_spec(D_MODEL), _row_spec(ROUTER_LANES), _row_spec(ROUTER_LANES), tile_row]
                 + [hbm] * recycled,
        out_specs=hbm,
        scratch_shapes=[pltpu.VMEM((2, MAX_SLOTS, X_WORDS), U32),
                        pltpu.VMEM((MOE_BLOCK, X_WORDS), U32),
                        pltpu.SemaphoreType.DMA((2,)),
                        pltpu.SemaphoreType.DMA(())],
    )
    return pl.pallas_call(
        functools.partial(_dispatch_kernel, recycled=recycled),
        grid_spec=grid_spec,
        out_shape=jax.ShapeDtypeStruct((PAD_ROWS, X_WORDS), U32),
        input_output_aliases={len(scalars) + len(operands) - 1: 0} if recycled else {},
        compiler_params=_cparams("arbitrary"),
        name="dispatch",
    )(*scalars, *operands)


def _combine_kernel(loff_s, c8_s, dst_s, nslot_s,
                    h_ref, ri_ref, off_ref, g_ref, ypad_hbm, out_ref,
                    ybuf, sem, *, final_norm):
    i = pl.program_id(0)
    slot = i % 2
    last = pl.num_programs(0) - 1
    fetch = lambda tile, sl: (lambda e: _run_copy(tile, e, loff_s, c8_s, dst_s, ybuf, sl,
                                                  ypad_hbm, sem, False))

    @pl.when(i == 0)
    def _():
        ybuf[...] = jnp.zeros_like(ybuf)
        first = fetch(i, slot)
        _loop(0, N_EXPERTS, lambda e: _start(*first(e)), unroll=RUN_UNROLL)

    @pl.when(i < last)
    def _():
        nxt = fetch(i + 1, 1 - slot)
        _loop(0, N_EXPERTS, lambda e: _start(*nxt(e)), unroll=RUN_UNROLL)

    _wait_rows(pl.multiple_of(nslot_s[i], RUN_ALIGN), ybuf, slot, ypad_hbm, sem, False)

    s1, s2 = _slots(ri_ref, off_ref)

    def gathered(first, size):
        scol = (first + lax.broadcasted_iota(I32, (ROW_TILE, size), 1)).astype(F32)
        pick = ((scol == s1) | (scol == s2)).astype(BF16)
        y_hi, y_lo = _unpack_halves(ybuf[slot, first:first + size, :])
        return jnp.concatenate([_dot(pick, y_hi), _dot(pick, y_lo)], axis=1)

    finish = (lambda v: _rms(v, g_ref[...])) if final_norm else (lambda v: v)
    h = h_ref[...] + gathered(0, COMBINE_BASE)
    spill = nslot_s[i] > COMBINE_BASE

    @pl.when(jnp.logical_not(spill))
    def _():
        out_ref[...] = finish(h)

    @pl.when(spill)
    def _():
        out_ref[...] = finish(h + gathered(COMBINE_BASE, MAX_SLOTS - COMBINE_BASE))


def _combine(tables, h, route_i, y_pad, g, final_norm):
    tile_row = pl.BlockSpec((1, SUBLANES, ROUTER_LANES), lambda i, *_: (i, 0, 0))
    grid_spec = pltpu.PrefetchScalarGridSpec(
        num_scalar_prefetch=4,
        grid=(N_TILES,),
        in_specs=[_row_spec(D_MODEL), _row_spec(ROUTER_LANES), tile_row, _full_spec(g),
                  pl.BlockSpec(memory_space=pl.ANY)],
        out_specs=_row_spec(D_MODEL),
        scratch_shapes=[pltpu.VMEM((2, MAX_SLOTS, HALF), U32),
                        pltpu.SemaphoreType.DMA((2,))],
    )
    return pl.pallas_call(
        functools.partial(_combine_kernel, final_norm=final_norm),
        grid_spec=grid_spec,
        out_shape=jax.ShapeDtypeStruct((TOKENS, D_MODEL), F32),
        compiler_params=_cparams("arbitrary"),
        name="combine",
    )(tables["loff"], tables["c8"], tables["dst"], tables["nslot"],
      h, route_i, tables["offrow"], g, y_pad)


def _expert_kernel(be_ref, slot_ref, next_ref, nact_ref, x_ref, wg_hbm, wu_hbm, wd_hbm, *rest,
                   layer, recycled):
    (y_hbm, wg_f, wu_f, wg_b, wu_b, wd_s, ybuf, zero_buf, sem, ysem, zsem) = (
        rest[1:] if recycled else rest)
    i = pl.program_id(0)
    n_active = nact_ref[0]
    staged = ((wg_hbm, wg_f, wg_b), (wu_hbm, wu_f, wu_b))

    def fetch(e, slot):
        copies = [pltpu.make_async_copy(hbm.at[layer, e], f32.at[slot], sem.at[slot, k])
                  for k, (hbm, f32, _) in enumerate(staged)]
        return copies + [pltpu.make_async_copy(wd_hbm.at[e], wd_s.at[slot], sem.at[slot, len(staged)])]

    def block_rows(b):
        return y_hbm.at[pl.ds(pl.multiple_of(b * MOE_BLOCK, MOE_BLOCK), MOE_BLOCK)]

    def put(b):
        return pltpu.make_async_copy(ybuf.at[b % 2], block_rows(b), ysem.at[b % 2])

    def tail_copy(b):
        return pltpu.make_async_copy(zero_buf, block_rows(b), zsem)

    if not recycled:
        @pl.when(i == 0)
        def _():
            zero_buf[...] = jnp.zeros_like(zero_buf)
            _loop(n_active, N_MOE_BLOCKS, lambda b: tail_copy(b).start())

        @pl.when(i == pl.num_programs(0) - 1)
        def _():
            _loop(n_active, N_MOE_BLOCKS, lambda b: tail_copy(b).wait())

    @pl.when(i < n_active)
    def _():
        e = be_ref[i]
        slot = slot_ref[i]

        def request(ahead, block):
            nxt = next_ref[(ahead - 1) * N_MOE_BLOCKS + block]

            @pl.when(nxt >= 0)
            def _():
                for cp in fetch(nxt, (slot + ahead) % W_STAGES):
                    cp.start(priority=WEIGHT_DMA_PRIORITY)

        @pl.when((i == 0) | (e != be_ref[jnp.maximum(i - 1, 0)]))
        def _():
            @pl.when(i == 0)
            def _():
                for cp in fetch(e, slot):
                    cp.start(priority=WEIGHT_DMA_PRIORITY)
                for ahead in range(1, W_STAGES - 1):
                    request(ahead, i)
            for cp in fetch(e, slot):
                cp.wait()
            request(W_STAGES - 1, i)
            for _, f32, b16 in staged:
                b16[...] = f32[slot].astype(BF16)

        x_hi, x_lo = _unpack_halves(x_ref[:, 0:HALF])
        row_w = lax.bitcast_convert_type(x_ref[:, HALF:HALF + 1], F32)
        gate = _dot(x_hi, wg_b[0:HALF, :]) + _dot(x_lo, wg_b[HALF:D_MODEL, :])
        up = _dot(x_hi, wu_b[0:HALF, :]) + _dot(x_lo, wu_b[HALF:D_MODEL, :])
        hmid = (jax.nn.silu(gate) * up).astype(BF16)
        y = (_dot(hmid, wd_s[slot]) * row_w).astype(BF16).astype(F32)
        ybuf[i % 2] = _pack_halves(y[:, 0:HALF], y[:, HALF:D_MODEL])
        put(i).start()

        @pl.when(i > 0)
        def _():
            put(i - 1).wait()

        @pl.when(i == n_active - 1)
        def _():
            put(i).wait()


def _experts(layer, tables, x_pad, w_gate, w_up, w_down, old_rows=None):
    hbm = pl.BlockSpec(memory_space=pl.ANY)
    up_shape, down_shape = (D_MODEL, D_EXPERT), (D_EXPERT, D_MODEL)
    recycled = old_rows is not None
    scalars = (tables["blk_expert"], tables["blk_slot"], tables["blk_next"], tables["n_active"])
    operands = (x_pad, w_gate, w_up, w_down) + ((old_rows,) if recycled else ())
    grid_spec = pltpu.PrefetchScalarGridSpec(
        num_scalar_prefetch=len(scalars),
        grid=(N_MOE_BLOCKS,),
        in_specs=[pl.BlockSpec((MOE_BLOCK, X_WORDS),
                               lambda i, be, sl, nx, na: (jnp.minimum(i, na[0] - 1), 0)),
                  hbm, hbm, hbm] + [hbm] * recycled,
        out_specs=hbm,
        scratch_shapes=[pltpu.VMEM((W_STAGES,) + up_shape, F32),
                        pltpu.VMEM((W_STAGES,) + up_shape, F32),
                        pltpu.VMEM(up_shape, BF16), pltpu.VMEM(up_shape, BF16),
                        pltpu.VMEM((W_STAGES,) + down_shape, BF16),
                        pltpu.VMEM((2, MOE_BLOCK, HALF), U32),
                        pltpu.VMEM((MOE_BLOCK, HALF), U32),
                        pltpu.SemaphoreType.DMA((W_STAGES, 3)),
                        pltpu.SemaphoreType.DMA((2,)),
                        pltpu.SemaphoreType.DMA(())],
    )
    return pl.pallas_call(
        functools.partial(_expert_kernel, layer=layer, recycled=recycled),
        grid_spec=grid_spec,
        out_shape=jax.ShapeDtypeStruct((PAD_ROWS, HALF), U32),
        input_output_aliases={len(scalars) + len(operands) - 1: 0} if recycled else {},
        compiler_params=_cparams("arbitrary"),
        name="experts",
    )(*scalars, *operands)


def _routing_tables(tile_counts):
    cnt = tile_counts[:, 0, EXPERT_LANE0:EXPERT_LANE0 + N_EXPERTS].astype(I32)
    c8 = (cnt + RUN_ALIGN - 1) // RUN_ALIGN * RUN_ALIGN
    loff = jnp.cumsum(c8, axis=1) - c8
    nslot = jnp.sum(c8, axis=1)
    tot = jnp.sum(c8, axis=0)
    padded = (tot + MOE_BLOCK - 1) // MOE_BLOCK * MOE_BLOCK
    end = jnp.cumsum(padded)
    base = end - padded
    dst = base[None, :] + jnp.cumsum(c8, axis=0) - c8
    blk_start = jnp.arange(N_MOE_BLOCKS, dtype=I32) * MOE_BLOCK
    blk_expert = jnp.minimum(jnp.sum((end[None, :] <= blk_start[:, None]).astype(I32), axis=1),
                             N_EXPERTS - 1).astype(I32)
    offrow = jnp.pad(loff.astype(F32),
                     ((0, 0), (EXPERT_LANE0, ROUTER_LANES - EXPERT_LANE0 - N_EXPERTS)))
    offrow = jnp.broadcast_to(offrow[:, None, :], (N_TILES, SUBLANES, ROUTER_LANES))
    experts = jnp.arange(N_EXPERTS, dtype=I32)
    present = padded > 0
    ordinal = jnp.cumsum(present.astype(I32)) - 1
    at_or_after = lax.cummin(jnp.where(present, experts, N_EXPERTS), reverse=True)
    after = jnp.concatenate([at_or_after[1:], jnp.full((1,), N_EXPERTS, I32)])
    hop = jnp.concatenate([after, jnp.full((1,), N_EXPERTS, I32)])
    pick = (blk_expert[:, None] == experts[None, :]).astype(I32)
    blk_slot = jnp.sum(pick * (ordinal % W_STAGES)[None, :], axis=1).astype(I32)
    ahead, blk_next = experts, []
    for _ in range(W_STAGES - 1):
        ahead = jnp.sum((ahead[:, None] == jnp.arange(N_EXPERTS + 1)[None, :]) * hop[None, :], axis=1)
        per_block = jnp.sum(pick * ahead[None, :], axis=1)
        blk_next.append(jnp.where(per_block >= N_EXPERTS, -1, per_block))
    blk_next = jnp.concatenate(blk_next).astype(I32)
    return {
        "blk_slot": blk_slot, "blk_next": blk_next,
        "loff": loff.reshape(-1).astype(I32), "c8": c8.reshape(-1).astype(I32),
        "dst": dst.reshape(-1).astype(I32), "nslot": nslot.astype(I32),
        "zdst": (base + tot).astype(I32), "zcnt": (padded - tot).astype(I32),
        "offrow": offrow, "blk_expert": blk_expert,
        "n_active": (end[-1:] // MOE_BLOCK).astype(I32),
    }


def _split_router_weights(w_router_group, w_router_expert):
    w_e = w_router_expert.transpose(1, 0, 2).reshape(D_MODEL, N_EXPERTS)
    w = jnp.concatenate([w_router_group, w_e], axis=1)
    w = jnp.pad(w, ((0, 0), (0, ROUTER_LANES - w.shape[1])))
    hi = w.astype(BF16)
    lo = (w - hi.astype(F32)).astype(BF16)
    return jnp.concatenate([hi, lo], axis=1)


def kernel(x, rel_bias, norm_mix_g, w_in, pool_w, pool_scale, w_proj_attn, w_proj_pool, w_out,
           norm_ffn_g, w_router_group, w_router_expert, w_gate_e, w_up_e, w_down_e, norm_final_g):
    h = x.reshape(TOKENS, D_MODEL)
    biases = [_band_bias(rel_bias, g, d) for g, (_, d) in enumerate(DILATION_PATTERNS)]
    x_pad = y_pad = None
    for l in range(DEPTH):
        qkv, u, gates, w_down_b = _project(l, h, norm_mix_g[l][None], w_in[l].astype(BF16), w_down_e)
        h, xn, route_i, route_w, tile_counts = _mix_out(
            h, _attention(qkv, biases), u, gates, pool_w[l].astype(BF16), pool_scale[l][None],
            w_proj_attn[l].astype(BF16), w_proj_pool[l].astype(BF16), w_out[l].astype(BF16),
            norm_ffn_g[l][None], _split_router_weights(w_router_group[l], w_router_expert[l]))
        tables = _routing_tables(tile_counts)
        x_pad = _dispatch(tables, xn, route_i, route_w, x_pad)
        y_pad = _experts(l, tables, x_pad, w_gate_e, w_up_e, w_down_b, y_pad)
        h = _combine(tables, h, route_i, y_pad, norm_final_g[None], l == DEPTH - 1)
    return h.reshape(BATCH, SEQ, D_MODEL)
```
